```python
import math
import jax, jax.numpy as jnp
from jax import lax
import numpy as np

D_MODEL = 2048
BATCH = 4
SEQ = 2048
DEPTH = 4

GRID_W = 64
CTX_LEN = 256
N_MIXERS = 3
POOL_WINDOWS = (2, 4, 8, 16)
POOL_GROUP = D_MODEL // len(POOL_WINDOWS)
HY_ORDER = 2
HY_SHORT = 3
HY_EMB = 33
HY_FILTER = 64
HY_DECAY_TARGET = 1e-2
HY_FAST = 0.3
HY_SLOW = 1.5
MLA_HEADS = D_MODEL // 128
MLA_Q_RANK = 512
MLA_KV_RANK = 512
MLA_NOPE = 128
MLA_ROPE = 64
MLA_V = 128
ROPE_BASE = 10000.0
Q_BLOCK = 128
D_FF = 5632
N_EXPERTS = 8
TOP_K = 2
EXPERT_FF = 2 * D_MODEL
LN_EPS = 1e-5
RMS_EPS = 1e-6
DN_ALPHA = (2 * DEPTH) ** 0.25
DN_BETA = (8 * DEPTH) ** -0.25

kernel_name = 'hybrid_pool_hyena_mla_moe_dit'


def layer_norm(x, g, b):
    xf = x.astype(jnp.float32)
    mu = jnp.mean(xf, axis=-1, keepdims=True)
    var = jnp.mean(jnp.square(xf - mu), axis=-1, keepdims=True)
    return ((xf - mu) * lax.rsqrt(var + LN_EPS)).astype(x.dtype) * g + b


def rms_norm(x, g):
    xf = x.astype(jnp.float32)
    return (xf * lax.rsqrt(jnp.mean(xf * xf, axis=-1, keepdims=True) + RMS_EPS)).astype(x.dtype) * g


def pool_mix(h, w_grp, scale):
    B, L, D = h.shape
    hf = h.astype(jnp.float32)
    cs = jnp.concatenate([jnp.zeros_like(hf[:, :1]), jnp.cumsum(hf, axis=1)], axis=1)
    t = jnp.arange(L)
    parts = []
    for g, w in enumerate(POOL_WINDOWS):
        lo = jnp.maximum(t - w // 2, 0)
        hi = jnp.minimum(t + w // 2, L)
        sl = slice(g * POOL_GROUP, (g + 1) * POOL_GROUP)
        csg = cs[:, :, sl]
        mean = (csg[:, hi] - csg[:, lo]) / (hi - lo).astype(jnp.float32)[None, :, None]
        parts.append(mean - hf[:, :, sl])
    d = jnp.stack(parts, axis=2).astype(h.dtype)
    y = jnp.einsum('blgc,gcd->blgd', d, w_grp).reshape(B, L, D)
    return y * scale


def short_conv(u, w, b):
    L = u.shape[1]
    p = HY_SHORT // 2
    up = jnp.pad(u, ((0, 0), (p, HY_SHORT - 1 - p), (0, 0)))
    y = b
    for j in range(HY_SHORT):
        y = y + up[:, j:j + L] * w[j]
    return y


def hyena_filters(L, f_w1, f_b1, f_w2, f_b2, f_w3, f_b3, f_freq, f_wout):
    D = f_wout.shape[-1] // HY_ORDER
    bands = (HY_EMB - 1) // 2
    t = jnp.linspace(0.0, 1.0, L, dtype=jnp.float32)[:, None]
    wpos = 2.0 * math.pi * jnp.arange(L, dtype=jnp.float32)[:, None] / L
    f = jnp.linspace(1e-4, bands - 1, bands, dtype=jnp.float32)[None, :]
    z = jnp.concatenate([t, jnp.cos(f * wpos), -jnp.sin(f * wpos)], axis=-1).astype(f_w1.dtype)
    a = jnp.sin(f_freq * (z @ f_w1 + f_b1))
    a = jnp.sin(f_freq * (a @ f_w2 + f_b2))
    a = jnp.sin(f_freq * (a @ f_w3 + f_b3))
    k = (a @ f_wout).astype(jnp.float32).reshape(L, HY_ORDER, D)
    dist = jnp.abs(jnp.arange(L) - L // 2).astype(jnp.float32) / max(L // 2, 1)
    deltas = jnp.linspace(math.log(HY_DECAY_TARGET) / HY_SLOW, math.log(HY_DECAY_TARGET) / HY_FAST, D, dtype=jnp.float32)
    k = k * jnp.exp(-dist[:, None] * jnp.abs(deltas)[None, :])[:, None, :]
    return k * lax.rsqrt(jnp.sum(k * k, axis=0, keepdims=True) + 1e-6)


def centred_fftconv(u, k):
    L = u.shape[1]
    uf = jnp.fft.rfft(u, n=2 * L, axis=1)
    kf = jnp.fft.rfft(k, n=2 * L, axis=0)
    y = jnp.fft.irfft(uf * kf[None], n=2 * L, axis=1)
    return y[:, L // 2:L // 2 + L]


def hyena_mix(h, w_in, b_in, conv_w, conv_b, f_w1, f_b1, f_w2, f_b2, f_w3, f_b3, f_freq, f_wout, skip, w_out, b_out):
    B, L, D = h.shape
    u = short_conv(h @ w_in + b_in, conv_w, conv_b)
    v, x1, x2 = jnp.split(u, 3, axis=-1)
    k = hyena_filters(L, f_w1, f_b1, f_w2, f_b2, f_w3, f_b3, f_freq, f_wout)
    z = v.astype(jnp.float32)
    for n, gate in enumerate((x1, x2)):
        z = gate.astype(jnp.float32) * (centred_fftconv(z, k[:, n]) + z * skip[n].astype(jnp.float32))
    return z.astype(h.dtype) @ w_out + b_out


def grid_angles(L):
    rows = L // GRID_W
    row = jnp.repeat(jnp.arange(rows), GRID_W).astype(jnp.float32)
    col = jnp.tile(jnp.arange(GRID_W), rows).astype(jnp.float32)
    n = MLA_ROPE // 4
    inv = ROPE_BASE ** (-jnp.arange(n, dtype=jnp.float32) / n)
    return row[:, None] * inv, col[:, None] * inv


def rope_rotate(x, ang):
    n = x.shape[-1] // 2
    cos, sin = jnp.cos(ang).astype(x.dtype), jnp.sin(ang).astype(x.dtype)
    x1, x2 = x[..., :n], x[..., n:]
    return jnp.concatenate([x1 * cos - x2 * sin, x1 * sin + x2 * cos], axis=-1)


def axial_rope(x, ang_row, ang_col):
    half = x.shape[-1] // 2
    return jnp.concatenate([rope_rotate(x[..., :half], ang_row), rope_rotate(x[..., half:], ang_col)], axis=-1)


def mla_keys_values(h, w_in, kv_norm, w_ukv, angles):
    B, L, _ = h.shape
    a = h @ w_in[:, MLA_Q_RANK:]
    kv = (rms_norm(a[..., :MLA_KV_RANK], kv_norm) @ w_ukv).reshape(B, L, MLA_HEADS, MLA_NOPE + MLA_V)
    k_rope = a[..., MLA_KV_RANK:]
    if angles is not None:
        k_rope = axial_rope(k_rope, angles[0], angles[1])
    return kv[..., :MLA_NOPE], k_rope, kv[..., MLA_NOPE:]


def mla_queries(h, w_in, q_norm, w_uq, angles):
    B, L, _ = h.shape
    q = (rms_norm(h @ w_in[:, :MLA_Q_RANK], q_norm) @ w_uq).reshape(B, L, MLA_HEADS, MLA_NOPE + MLA_ROPE)
    q_rope = q[..., MLA_NOPE:]
    if angles is not None:
        q_rope = axial_rope(q_rope, angles[0][:, None, :], angles[1][:, None, :])
    return q[..., :MLA_NOPE], q_rope


def attend(q_nope, q_rope, k_nope, k_rope, v):
    s = (jnp.einsum('bqhd,bkhd->bhqk', q_nope, k_nope)
         + jnp.einsum('bqhr,bkr->bhqk', q_rope, k_rope)).astype(jnp.float32) * (MLA_NOPE + MLA_ROPE) ** -0.5
    p = jax.nn.softmax(s, axis=-1).astype(v.dtype)
    return jnp.einsum('bhqk,bkhd->bqhd', p, v)


def blocked_attend(q_nope, q_rope, k_nope, k_rope, v):
    B, L, H, _ = q_nope.shape
    nb = L // Q_BLOCK
    def to_blocks(t):
        return t.reshape(B, nb, Q_BLOCK, *t.shape[2:]).swapaxes(0, 1)
    out = lax.map(lambda qs: attend(qs[0], qs[1], k_nope, k_rope, v), (to_blocks(q_nope), to_blocks(q_rope)))
    return out.swapaxes(0, 1).reshape(B, L, H * MLA_V)


def mla_mix(h_lat, h_ctx, w_in, q_norm, kv_norm, w_uq, w_ukv, w_o, ctx_out):
    angles = grid_angles(h_lat.shape[1])
    kn_c, kr_c, v_c = mla_keys_values(h_ctx, w_in, kv_norm, w_ukv, None)
    kn_l, kr_l, v_l = mla_keys_values(h_lat, w_in, kv_norm, w_ukv, angles)
    qn_l, qr_l = mla_queries(h_lat, w_in, q_norm, w_uq, angles)
    y_lat = blocked_attend(qn_l, qr_l,
                           jnp.concatenate([kn_l, kn_c], axis=1),
                           jnp.concatenate([kr_l, kr_c], axis=1),
                           jnp.concatenate([v_l, v_c], axis=1)) @ w_o
    y_ctx = None
    if ctx_out:
        qn_c, qr_c = mla_queries(h_ctx, w_in, q_norm, w_uq, None)
        B, Lc = h_ctx.shape[:2]
        y_ctx = attend(qn_c, qr_c, kn_c, kr_c, v_c).reshape(B, Lc, MLA_HEADS * MLA_V) @ w_o
    return y_lat, y_ctx


def swiglu(h, wg, wu, wd):
    return (jax.nn.silu(h @ wg) * (h @ wu)) @ wd


def moe_swiglu(h, w_router, wg, wu, wd):
    logits = (h @ w_router).astype(jnp.float32)
    top_v, top_i = lax.top_k(logits, TOP_K)
    gates = jax.nn.softmax(top_v, axis=-1)
    comb = jnp.sum(jax.nn.one_hot(top_i, N_EXPERTS, dtype=jnp.float32) * gates[..., None], axis=-2).astype(h.dtype)
    y = jnp.zeros_like(h)
    for e in range(N_EXPERTS):
        y = y + comb[..., e:e + 1] * swiglu(h, wg[e], wu[e], wd[e])
    return y


def setup_inputs(seed: int = 0) -> dict:
    key = jax.random.key(seed)
    ks = iter(jax.random.split(key, 48))
    D = D_MODEL
    n_pool = len(range(0, DEPTH, N_MIXERS))
    n_hy = len(range(1, DEPTH, N_MIXERS))
    n_mla = len(range(2, DEPTH, N_MIXERS))
    n_dense = len(range(0, DEPTH, 2))
    n_moe = len(range(1, DEPTH, 2))

    def nrm(shape, scale):
        return jax.random.normal(next(ks), shape, jnp.float32) * scale

    return {
        'x': nrm((BATCH, SEQ, D), 1.0),
        'c': nrm((BATCH, D), 1.0),
        'ctx': nrm((BATCH, CTX_LEN, D), 1.0),
        'c_ctx': nrm((D,), 1.0),
        'w_mod': nrm((DEPTH, D, 6 * D), 0.5 * D ** -0.5),
        'b_mod': nrm((DEPTH, 6 * D), 0.02),
        'ln_g': 1.0 + nrm((DEPTH, 2, D), 0.02),
        'ln_b': nrm((DEPTH, 2, D), 0.02),
        'pool_w': nrm((n_pool, len(POOL_WINDOWS), POOL_GROUP, POOL_GROUP), DN_BETA * POOL_GROUP ** -0.5),
        'pool_scale': 1.0 + nrm((n_pool, D), 0.1),
        'hy_w_in': nrm((n_hy, D, 3 * D), D ** -0.5),
        'hy_b_in': nrm((n_hy, 3 * D), 0.02),
        'hy_conv_w': nrm((n_hy, HY_SHORT, 3 * D), HY_SHORT ** -0.5),
        'hy_conv_b': nrm((n_hy, 3 * D), 0.02),
        'hy_f_w1': nrm((n_hy, HY_EMB, HY_FILTER), HY_EMB ** -0.5),
        'hy_f_b1': nrm((n_hy, HY_FILTER), 0.1),
        'hy_f_w2': nrm((n_hy, HY_FILTER, HY_FILTER), HY_FILTER ** -0.5),
        'hy_f_b2': nrm((n_hy, HY_FILTER), 0.1),
        'hy_f_w3': nrm((n_hy, HY_FILTER, HY_FILTER), HY_FILTER ** -0.5),
        'hy_f_b3': nrm((n_hy, HY_FILTER), 0.1),
        'hy_f_freq': 1.0 + nrm((n_hy, HY_FILTER), 0.05),
        'hy_f_wout': nrm((n_hy, HY_FILTER, HY_ORDER * D), HY_FILTER ** -0.5),
        'hy_skip': nrm((n_hy, HY_ORDER, D), 1.0),
        'hy_w_out': nrm((n_hy, D, D), DN_BETA * D ** -0.5),
        'hy_b_out': nrm((n_hy, D), 0.02),
        'mla_w_in': nrm((n_mla, D, MLA_Q_RANK + MLA_KV_RANK + MLA_ROPE), D ** -0.5),
        'mla_q_norm': 1.0 + nrm((n_mla, MLA_Q_RANK), 0.02),
        'mla_kv_norm': 1.0 + nrm((n_mla, MLA_KV_RANK), 0.02),
        'mla_w_uq': nrm((n_mla, MLA_Q_RANK, MLA_HEADS * (MLA_NOPE + MLA_ROPE)), MLA_Q_RANK ** -0.5),
        'mla_w_ukv': nrm((n_mla, MLA_KV_RANK, MLA_HEADS * (MLA_NOPE + MLA_V)), MLA_KV_RANK ** -0.5),
        'mla_w_o': nrm((n_mla, MLA_HEADS * MLA_V, D), DN_BETA * (MLA_HEADS * MLA_V) ** -0.5),
        'ffn_w_gate': nrm((n_dense, D, D_FF), D ** -0.5),
        'ffn_w_up': nrm((n_dense, D, D_FF), D ** -0.5),
        'ffn_w_down': nrm((n_dense, D_FF, D), DN_BETA * D_FF ** -0.5),
        'moe_w_router': nrm((n_moe, D, N_EXPERTS), D ** -0.5),
        'moe_w_gate': nrm((n_moe, N_EXPERTS, D, EXPERT_FF), D ** -0.5),
        'moe_w_up': nrm((n_moe, N_EXPERTS, D, EXPERT_FF), D ** -0.5),
        'moe_w_down': nrm((n_moe, N_EXPERTS, EXPERT_FF, D), DN_BETA * EXPERT_FF ** -0.5),
    }


def reference(x, c, ctx, c_ctx, w_mod, b_mod, ln_g, ln_b, pool_w, pool_scale,
              hy_w_in, hy_b_in, hy_conv_w, hy_conv_b, hy_f_w1, hy_f_b1, hy_f_w2, hy_f_b2,
              hy_f_w3, hy_f_b3, hy_f_freq, hy_f_wout, hy_skip, hy_w_out, hy_b_out,
              mla_w_in, mla_q_norm, mla_kv_norm, mla_w_uq, mla_w_ukv, mla_w_o,
              ffn_w_gate, ffn_w_up, ffn_w_down,
              moe_w_router, moe_w_gate, moe_w_up, moe_w_down):
    attn_layers = [i for i in range(DEPTH) if i % N_MIXERS == 2]
    last_read = attn_layers[-1] if attn_layers else -1
    h_lat, h_ctx = x, ctx
    for i in range(DEPTH):
        kind, j = i % N_MIXERS, i // N_MIXERS
        ctx_live = i <= last_read
        ctx_out = i < last_read
        m = [t[:, None, :] for t in jnp.split(jax.nn.silu(c) @ w_mod[i] + b_mod[i], 6, axis=-1)]
        if ctx_live:
            mc = jnp.split(jax.nn.silu(c_ctx) @ w_mod[i] + b_mod[i], 6, axis=-1)
        a_lat = h_lat * (1.0 + m[1]) + m[0]
        a_ctx = h_ctx * (1.0 + mc[1]) + mc[0] if ctx_live else None
        y_ctx = None
        if kind == 0:
            y_lat = pool_mix(a_lat, pool_w[j], pool_scale[j])
            if ctx_out:
                y_ctx = pool_mix(a_ctx, pool_w[j], pool_scale[j])
        elif kind == 1:
            hy = (hy_w_in[j], hy_b_in[j], hy_conv_w[j], hy_conv_b[j], hy_f_w1[j], hy_f_b1[j],
                  hy_f_w2[j], hy_f_b2[j], hy_f_w3[j], hy_f_b3[j], hy_f_freq[j], hy_f_wout[j],
                  hy_skip[j], hy_w_out[j], hy_b_out[j])
            y_lat = hyena_mix(a_lat, *hy)
            if ctx_out:
                y_ctx = hyena_mix(a_ctx, *hy)
        else:
            y_lat, y_ctx = mla_mix(a_lat, a_ctx, mla_w_in[j], mla_q_norm[j], mla_kv_norm[j],
                                   mla_w_uq[j], mla_w_ukv[j], mla_w_o[j], ctx_out)
        h_lat = layer_norm(DN_ALPHA * h_lat + m[2] * y_lat, ln_g[i, 0], ln_b[i, 0])
        if ctx_out:
            h_ctx = layer_norm(DN_ALPHA * h_ctx + mc[2] * y_ctx, ln_g[i, 0], ln_b[i, 0])
        f = i // 2
        if i % 2 == 0:
            ffn = lambda a: swiglu(a, ffn_w_gate[f], ffn_w_up[f], ffn_w_down[f])
        else:
            ffn = lambda a: moe_swiglu(a, moe_w_router[f], moe_w_gate[f], moe_w_up[f], moe_w_down[f])
        h_lat = layer_norm(DN_ALPHA * h_lat + m[5] * ffn(h_lat * (1.0 + m[4]) + m[3]), ln_g[i, 1], ln_b[i, 1])
        if ctx_out:
            h_ctx = layer_norm(DN_ALPHA * h_ctx + mc[5] * ffn(h_ctx * (1.0 + mc[4]) + mc[3]), ln_g[i, 1], ln_b[i, 1])
    return h_lat
```

```python
import functools
import math

import jax
import jax.numpy as jnp
from jax import lax
from jax.experimental import pallas as pl
from jax.experimental.pallas import tpu as pltpu

F32 = jnp.float32
BF16 = jnp.bfloat16

D_MODEL = 2048
BATCH = 4
SEQ = 2048
DEPTH = 4
GRID_W = 64
CTX_LEN = 256
N_MIXERS = 3
POOL_WINDOWS = (2, 4, 8, 16)
POOL_GROUP = D_MODEL // len(POOL_WINDOWS)
HY_ORDER = 2
HY_SHORT = 3
HY_EMB = 33
HY_FILTER = 64
HY_DECAY_TARGET = 1e-2
HY_FAST = 0.3
HY_SLOW = 1.5
MLA_HEADS = D_MODEL // 128
MLA_Q_RANK = 512
MLA_KV_RANK = 512
MLA_NOPE = 128
MLA_ROPE = 64
MLA_V = 128
ROPE_BASE = 10000.0
D_FF = 5632
N_EXPERTS = 8
TOP_K = 2
EXPERT_FF = 2 * D_MODEL
LN_EPS = 1e-5
RMS_EPS = 1e-6
DN_ALPHA = (2 * DEPTH) ** 0.25

N_LAT = BATCH * SEQ
N_CTX = BATCH * CTX_LEN
MOD_ROWS = 8
CTX_MOD_ROW = BATCH
LANE = 128
SUBLANE = 8
VMEM_CAP = 60 * 1024 * 1024
MLA_QK_PAD = 256


def _vmem(nbytes):
    return int(min(VMEM_CAP, max(16 * 1024 * 1024, nbytes * 3 // 2)))


def _nbytes(shape, dtype):
    return math.prod(shape) * jnp.dtype(dtype).itemsize


def _mod_row(i, tm):
    return jnp.minimum((i * tm) // SEQ, CTX_MOD_ROW)


def _mod_spec(which, tm, grid_rank=1, axis=0):
    def index(*idx):
        return (_mod_row(idx[axis], tm) * 6 + which, 0, 0)
    return pl.BlockSpec((None, 1, D_MODEL), index)


def _mod_kernel(c_ref, w_ref, b_ref, o_ref):
    c = c_ref[...]
    a = c / (1.0 + jnp.exp(-c))
    o_ref[...] = jnp.dot(a.astype(BF16), w_ref[...].astype(BF16),
                         preferred_element_type=F32) + b_ref[...]


def modulation_table(c8, w_mod, b_mod):
    tn = 1024
    n = 6 * D_MODEL
    out = pl.pallas_call(
        _mod_kernel,
        grid=(DEPTH, n // tn),
        in_specs=[pl.BlockSpec((MOD_ROWS, D_MODEL), lambda l, j: (0, 0)),
                  pl.BlockSpec((None, D_MODEL, tn), lambda l, j: (l, 0, j)),
                  pl.BlockSpec((None, 1, tn), lambda l, j: (l, 0, j))],
        out_specs=pl.BlockSpec((None, MOD_ROWS, tn), lambda l, j: (l, 0, j)),
        out_shape=jax.ShapeDtypeStruct((DEPTH, MOD_ROWS, n), F32),
        compiler_params=pltpu.CompilerParams(
            dimension_semantics=("arbitrary", "arbitrary"),
            vmem_limit_bytes=_vmem(3 * _nbytes((D_MODEL, tn), F32))),
        name="modulation_table",
    )(c8, w_mod, b_mod.reshape(DEPTH, 1, n))
    return out.reshape(DEPTH, MOD_ROWS * 6, 1, D_MODEL)


def _linear_kernel(xt_ref, te_ref, nv_ref, x_ref, w_ref, *rest, has_bias, cast_w):
    rest = list(rest)
    b_ref = rest.pop(0) if has_bias else None
    o_ref = rest.pop(0)
    wbf_ref = rest.pop(0) if cast_w else w_ref
    i = pl.program_id(1)

    if cast_w:
        prev = te_ref[jnp.maximum(i - 1, 0)]

        @pl.when(jnp.logical_or(i == 0, te_ref[i] != prev))
        def _():
            wbf_ref[...] = w_ref[...].astype(BF16)

    @pl.when(i < nv_ref[0])
    def _():
        acc = jnp.dot(x_ref[...].astype(BF16), wbf_ref[...], preferred_element_type=F32)
        if has_bias:
            acc = acc + b_ref[...]
        o_ref[...] = acc.astype(o_ref.dtype)

    @pl.when(i >= nv_ref[0])
    def _():
        o_ref[...] = jnp.zeros_like(o_ref)


def linear(x, w, *, tm, tn, n_tiles=None, xt=None, te=None, nv=None, bias=None,
           out_dtype=F32, w_col0=0, name="linear"):
    k = x.shape[1]
    if w.ndim == 2:
        w = w[None]
    n = bias.shape[-1] if bias is not None else None
    n_out = n if n is not None else w.shape[2] - w_col0
    return _linear_call(x, w, tm=tm, tn=tn, n_out=n_out, n_tiles=n_tiles, xt=xt, te=te, nv=nv,
                        bias=bias, out_dtype=out_dtype, w_col0=w_col0, name=name)


def _linear_call(x, w, *, tm, tn, n_out, n_tiles, xt, te, nv, bias, out_dtype, w_col0, name):
    k = x.shape[1]
    assert x.shape[0] % tm == 0 and n_out % tn == 0 and w_col0 % tn == 0 and w.shape[1] == k
    if n_tiles is None:
        n_tiles = x.shape[0] // tm
    if xt is None:
        xt = jnp.arange(n_tiles, dtype=jnp.int32)
    if te is None:
        te = jnp.zeros((n_tiles,), jnp.int32)
    if nv is None:
        nv = jnp.full((1,), n_tiles, jnp.int32)
    cast_w = w.dtype != BF16
    has_bias = bias is not None
    cb = w_col0 // tn
    in_specs = [pl.BlockSpec((tm, k), lambda j, i, xt, te, nv: (xt[i], 0)),
                pl.BlockSpec((None, k, tn), lambda j, i, xt, te, nv: (te[i], 0, j + cb))]
    args = [x, w]
    if has_bias:
        bias = bias.reshape(-1, 1, bias.shape[-1])
        in_specs.append(pl.BlockSpec((None, 1, tn), lambda j, i, xt, te, nv: (te[i], 0, j)))
        args.append(bias)
    scratch = [pltpu.VMEM((k, tn), BF16)] if cast_w else []
    est = (2 * _nbytes((tm, k), x.dtype) + 2 * _nbytes((k, tn), w.dtype)
           + (_nbytes((k, tn), BF16) if cast_w else 0) + _nbytes((tm, k), BF16)
           + 2 * _nbytes((tm, tn), out_dtype) + 2 * _nbytes((tm, tn), F32))
    return pl.pallas_call(
        functools.partial(_linear_kernel, has_bias=has_bias, cast_w=cast_w),
        grid_spec=pltpu.PrefetchScalarGridSpec(
            num_scalar_prefetch=3,
            grid=(n_out // tn, n_tiles),
            in_specs=in_specs,
            out_specs=pl.BlockSpec((tm, tn), lambda j, i, xt, te, nv: (i, j)),
            scratch_shapes=scratch),
        out_shape=jax.ShapeDtypeStruct((n_tiles * tm, n_out), out_dtype),
        compiler_params=pltpu.CompilerParams(
            dimension_semantics=("arbitrary", "arbitrary"),
            vmem_limit_bytes=_vmem(est)),
        name=name,
    )(xt, te, nv, *args)


def _swiglu_kernel(te_ref, nv_ref, x_ref, wg_ref, wu_ref, o_ref, wgb_ref, wub_ref):
    i = pl.program_id(1)
    prev = te_ref[jnp.maximum(i - 1, 0)]

    @pl.when(jnp.logical_or(i == 0, te_ref[i] != prev))
    def _():
        wgb_ref[...] = wg_ref[...].astype(BF16)
        wub_ref[...] = wu_ref[...].astype(BF16)

    @pl.when(i < nv_ref[0])
    def _():
        x = x_ref[...]
        g = jnp.dot(x, wgb_ref[...], preferred_element_type=F32)
        u = jnp.dot(x, wub_ref[...], preferred_element_type=F32)
        o_ref[...] = ((g / (1.0 + jnp.exp(-g))) * u).astype(o_ref.dtype)

    @pl.when(i >= nv_ref[0])
    def _():
        o_ref[...] = jnp.zeros_like(o_ref)


def swiglu_up(x, wg, wu, *, tm, tf, te=None, nv=None, name="swiglu_up"):
    m, k = x.shape
    f = wg.shape[2]
    assert m % tm == 0 and f % tf == 0
    n_tiles = m // tm
    if te is None:
        te = jnp.zeros((n_tiles,), jnp.int32)
    if nv is None:
        nv = jnp.full((1,), n_tiles, jnp.int32)
    wspec = pl.BlockSpec((None, k, tf), lambda j, i, te, nv: (te[i], 0, j))
    est = (2 * _nbytes((tm, k), BF16) + 4 * _nbytes((k, tf), F32) + 2 * _nbytes((k, tf), BF16)
           + 2 * _nbytes((tm, tf), BF16) + 3 * _nbytes((tm, tf), F32))
    return pl.pallas_call(
        _swiglu_kernel,
        grid_spec=pltpu.PrefetchScalarGridSpec(
            num_scalar_prefetch=2,
            grid=(f // tf, n_tiles),
            in_specs=[pl.BlockSpec((tm, k), lambda j, i, te, nv: (i, 0)), wspec, wspec],
            out_specs=pl.BlockSpec((tm, tf), lambda j, i, te, nv: (i, j)),
            scratch_shapes=[pltpu.VMEM((k, tf), BF16), pltpu.VMEM((k, tf), BF16)]),
        out_shape=jax.ShapeDtypeStruct((m, f), BF16),
        compiler_params=pltpu.CompilerParams(
            dimension_semantics=("arbitrary", "arbitrary"),
            vmem_limit_bytes=_vmem(est)),
        name=name,
    )(te, nv, x, wg, wu)


def _ln_mod(v, g, b, sc, sh):
    mu = jnp.mean(v, axis=-1, keepdims=True)
    d = v - mu
    var = jnp.mean(d * d, axis=-1, keepdims=True)
    h = d * lax.rsqrt(var + LN_EPS) * g + b
    return h, (h * (1.0 + sc) + sh).astype(BF16)


def _resid_ln_kernel(h_ref, y_ref, gate_ref, g_ref, b_ref, sc_ref, sh_ref, o_ref, a_ref):
    v = DN_ALPHA * h_ref[...] + gate_ref[...] * y_ref[...]
    h, a = _ln_mod(v, g_ref[...], b_ref[...], sc_ref[...], sh_ref[...])
    o_ref[...] = h
    a_ref[...] = a


def resid_ln(h, y, n_rows, mods, gate_idx, ln_g, ln_b, nmods, sc_idx, sh_idx, *, tm=256):
    assert n_rows % tm == 0
    row = pl.BlockSpec((tm, D_MODEL), lambda i: (i, 0))
    vec = pl.BlockSpec((1, D_MODEL), lambda i: (0, 0))
    est = 8 * _nbytes((tm, D_MODEL), F32) + 2 * _nbytes((tm, D_MODEL), BF16)
    return pl.pallas_call(
        _resid_ln_kernel,
        grid=(n_rows // tm,),
        in_specs=[row, row, _mod_spec(gate_idx, tm), vec, vec,
                  _mod_spec(sc_idx, tm), _mod_spec(sh_idx, tm)],
        out_specs=[row, row],
        out_shape=[jax.ShapeDtypeStruct((n_rows, D_MODEL), F32),
                   jax.ShapeDtypeStruct((n_rows, D_MODEL), BF16)],
        compiler_params=pltpu.CompilerParams(
            dimension_semantics=("arbitrary",), vmem_limit_bytes=_vmem(est)),
        name="resid_ln",
    )(h, y, mods, ln_g.reshape(1, D_MODEL), ln_b.reshape(1, D_MODEL), nmods, nmods)


def _modulate_kernel(h_ref, sc_ref, sh_ref, a_ref):
    a_ref[...] = (h_ref[...] * (1.0 + sc_ref[...]) + sh_ref[...]).astype(BF16)


def modulate(h, mods, sc_idx, sh_idx, *, tm=256):
    n_rows = h.shape[0]
    row = pl.BlockSpec((tm, D_MODEL), lambda i: (i, 0))
    return pl.pallas_call(
        _modulate_kernel,
        grid=(n_rows // tm,),
        in_specs=[row, _mod_spec(sc_idx, tm), _mod_spec(sh_idx, tm)],
        out_specs=row,
        out_shape=jax.ShapeDtypeStruct((n_rows, D_MODEL), BF16),
        compiler_params=pltpu.CompilerParams(dimension_semantics=("arbitrary",)),
        name="modulate",
    )(h, mods, mods)


def _seq_pos(i, tl):
    row0 = i * tl
    is_lat = row0 < N_LAT
    pos = jnp.where(is_lat, row0 % SEQ, (row0 - N_LAT) % CTX_LEN)
    seqlen = jnp.where(is_lat, SEQ, CTX_LEN)
    return pos, seqlen


def _halo_specs(tl, width, n_rows, col_index=None):
    per = tl // SUBLANE
    last = n_rows // SUBLANE - 1
    if col_index is None:
        prev = pl.BlockSpec((SUBLANE, width), lambda i: (jnp.maximum(i * per - 1, 0), 0))
        nxt = pl.BlockSpec((SUBLANE, width), lambda i: (jnp.minimum((i + 1) * per, last), 0))
    else:
        prev = pl.BlockSpec((SUBLANE, width),
                            lambda i, j: (jnp.maximum(i * per - 1, 0), col_index(j)))
        nxt = pl.BlockSpec((SUBLANE, width),
                           lambda i, j: (jnp.minimum((i + 1) * per, last), col_index(j)))
    return prev, nxt


def _shift_rows(x, s):
    n = x.shape[0]
    return pltpu.roll(x, s % n, 0)


def _pool_kernel(h_ref, hp_ref, hn_ref, sc_ref, sh_ref, gate_ref, w_ref, ps_ref, g_ref, b_ref,
                 nsc_ref, nsh_ref, o_ref, a_ref, *, tl):
    i = pl.program_id(0)
    pos, seqlen = _seq_pos(i, tl)
    sc = 1.0 + sc_ref[...]
    sh = sh_ref[...]
    h = h_ref[...]
    a = h * sc + sh
    keep_prev = (pos > 0).astype(F32)
    keep_next = (pos + tl < seqlen).astype(F32)
    ext = jnp.concatenate([(hp_ref[...] * sc + sh) * keep_prev, a,
                           (hn_ref[...] * sc + sh) * keep_next], axis=0)
    t = pos + lax.broadcasted_iota(jnp.int32, (tl, 1), 0)
    n_ext = tl + 2 * SUBLANE
    parts = []
    for g, w in enumerate(POOL_WINDOWS):
        cols = slice(g * POOL_GROUP, (g + 1) * POOL_GROUP)
        s = ext[:, cols]
        s = s + _shift_rows(s, 1)
        r = 1
        while 2 * r < w:
            s = _shift_rows(s, n_ext - r) + _shift_rows(s, r)
            r *= 2
        s = s[SUBLANE:SUBLANE + tl]
        cnt = (jnp.minimum(t + w // 2, seqlen) - jnp.maximum(t - w // 2, 0)).astype(F32)
        d = s / cnt - a[:, cols]
        parts.append(jnp.dot(d.astype(BF16), w_ref[g], preferred_element_type=F32))
    y = jnp.concatenate(parts, axis=1) * ps_ref[...]
    v = DN_ALPHA * h + gate_ref[...] * y
    hn, an = _ln_mod(v, g_ref[...], b_ref[...], nsc_ref[...], nsh_ref[...])
    o_ref[...] = hn
    a_ref[...] = an


def pool_layer(h, n_rows, mods, w_grp_bf16, pool_scale, ln_g, ln_b, *, tl):
    row = pl.BlockSpec((tl, D_MODEL), lambda i: (i, 0))
    vec = pl.BlockSpec((1, D_MODEL), lambda i: (0, 0))
    prev, nxt = _halo_specs(tl, D_MODEL, n_rows)
    est = (8 * _nbytes((tl, D_MODEL), F32) + 2 * _nbytes(w_grp_bf16.shape, BF16)
           + 4 * _nbytes((tl, D_MODEL), F32))
    return pl.pallas_call(
        functools.partial(_pool_kernel, tl=tl),
        grid=(n_rows // tl,),
        in_specs=[row, prev, nxt, _mod_spec(1, tl), _mod_spec(0, tl), _mod_spec(2, tl),
                  pl.BlockSpec(w_grp_bf16.shape, lambda i: (0, 0, 0)), vec, vec, vec,
                  _mod_spec(4, tl), _mod_spec(3, tl)],
        out_specs=[row, row],
        out_shape=[jax.ShapeDtypeStruct((n_rows, D_MODEL), F32),
                   jax.ShapeDtypeStruct((n_rows, D_MODEL), BF16)],
        compiler_params=pltpu.CompilerParams(
            dimension_semantics=("arbitrary",), vmem_limit_bytes=_vmem(est)),
        name="pool_layer",
    )(h, h, h, mods, mods, mods, w_grp_bf16, pool_scale.reshape(1, D_MODEL),
      ln_g.reshape(1, D_MODEL), ln_b.reshape(1, D_MODEL), mods, mods)


def _short_conv_kernel(u_ref, up_ref, un_ref, w_ref, b_ref, o_ref, *, tl):
    i = pl.program_id(0)
    pos, seqlen = _seq_pos(i, tl)
    keep_prev = (pos > 0).astype(F32)
    keep_next = (pos + tl < seqlen).astype(F32)
    ext = jnp.concatenate([up_ref[...] * keep_prev, u_ref[...], un_ref[...] * keep_next], axis=0)
    n_ext = tl + 2 * SUBLANE
    before = _shift_rows(ext, 1)[SUBLANE:SUBLANE + tl]
    after = _shift_rows(ext, n_ext - 1)[SUBLANE:SUBLANE + tl]
    w = w_ref[...]
    y = b_ref[...] + before * w[0:1]
    y = y + u_ref[...] * w[1:2]
    y = y + after * w[2:3]
    o_ref[...] = y


def short_conv(u, conv_w, conv_b, *, tl=256, tc=1024):
    n_rows, c = u.shape
    blk = pl.BlockSpec((tl, tc), lambda i, j: (i, j))
    prev, nxt = _halo_specs(tl, tc, n_rows, col_index=lambda j: j)
    return pl.pallas_call(
        functools.partial(_short_conv_kernel, tl=tl),
        grid=(n_rows // tl, c // tc),
        in_specs=[blk, prev, nxt,
                  pl.BlockSpec((HY_SHORT, tc), lambda i, j: (0, j)),
                  pl.BlockSpec((1, tc), lambda i, j: (0, j))],
        out_specs=blk,
        out_shape=jax.ShapeDtypeStruct((n_rows, c), F32),
        compiler_params=pltpu.CompilerParams(dimension_semantics=("arbitrary", "arbitrary")),
        name="short_conv",
    )(u, u, u, conv_w, conv_b.reshape(1, c))


def _split_bf16(x):
    hi = x.astype(BF16)
    lo = (x - hi.astype(F32)).astype(BF16)
    return hi, lo


def _dot3(a, b):
    a_hi, a_lo = _split_bf16(a)
    b_hi, b_lo = _split_bf16(b)
    return (jnp.dot(a_hi, b_hi, preferred_element_type=F32)
            + jnp.dot(a_lo, b_hi, preferred_element_type=F32)
            + jnp.dot(a_hi, b_lo, preferred_element_type=F32))


def _filter_kernel(z_ref, w1_ref, b1_ref, w2_ref, b2_ref, w3_ref, b3_ref, fr_ref, wo_ref,
                   dist_ref, delta_ref, o_ref):
    fr = fr_ref[...]
    a = jnp.sin(fr * (_dot3(z_ref[...], w1_ref[...]) + b1_ref[...]))
    a = jnp.sin(fr * (_dot3(a, w2_ref[...]) + b2_ref[...]))
    a = jnp.sin(fr * (_dot3(a, w3_ref[...]) + b3_ref[...]))
    k = _dot3(a, wo_ref[...])
    k = k * jnp.exp(-dist_ref[...] * delta_ref[...])
    o_ref[...] = k * lax.rsqrt(jnp.sum(k * k, axis=0, keepdims=True) + 1e-6)


def _pad2(x, rows, cols):
    return jnp.pad(x, ((0, rows - x.shape[0]), (0, cols - x.shape[1])))


def hyena_filters(L, f_w1, f_b1, f_w2, f_b2, f_w3, f_b3, f_freq, f_wout, *, tn=512):
    bands = (HY_EMB - 1) // 2
    t = jnp.linspace(0.0, 1.0, L, dtype=F32)[:, None]
    wpos = 2.0 * math.pi * jnp.arange(L, dtype=F32)[:, None] / L
    f = jnp.linspace(1e-4, bands - 1, bands, dtype=F32)[None, :]
    z = jnp.concatenate([t, jnp.cos(f * wpos), -jnp.sin(f * wpos)], axis=-1)
    dist = (jnp.abs(jnp.arange(L) - L // 2).astype(F32) / max(L // 2, 1))[:, None]
    deltas = jnp.linspace(math.log(HY_DECAY_TARGET) / HY_SLOW, math.log(HY_DECAY_TARGET) / HY_FAST,
                          D_MODEL, dtype=F32)
    absdelta = jnp.tile(jnp.abs(deltas), HY_ORDER)[None, :]
    P = LANE
    n = HY_ORDER * D_MODEL
    full = lambda shape: pl.BlockSpec(shape, lambda j: (0, 0))
    args = [_pad2(z, L, P), _pad2(f_w1, P, P), _pad2(f_b1[None], 1, P), _pad2(f_w2, P, P),
            _pad2(f_b2[None], 1, P), _pad2(f_w3, P, P), _pad2(f_b3[None], 1, P),
            _pad2(f_freq[None], 1, P), _pad2(f_wout, P, n), dist, absdelta]
    in_specs = [full((L, P)), full((P, P)), full((1, P)), full((P, P)), full((1, P)), full((P, P)),
                full((1, P)), full((1, P)), pl.BlockSpec((P, tn), lambda j: (0, j)),
                full((L, 1)), pl.BlockSpec((1, tn), lambda j: (0, j))]
    return pl.pallas_call(
        _filter_kernel,
        grid=(n // tn,),
        in_specs=in_specs,
        out_specs=pl.BlockSpec((L, tn), lambda j: (0, j)),
        out_shape=jax.ShapeDtypeStruct((L, n), F32),
        compiler_params=pltpu.CompilerParams(
            dimension_semantics=("arbitrary",),
            vmem_limit_bytes=_vmem(8 * _nbytes((L, tn), F32))),
        name="hyena_filters",
    )(*args)


def dft_tables(L):
    n_fft = 2 * L
    f = jnp.arange(L, dtype=jnp.int32)
    s = jnp.arange(L, dtype=jnp.int32)
    r = ((2 * f[:, None] + 1) * s[None, :]) % (2 * n_fft)
    ang = r.astype(F32) * (math.pi / n_fft)
    fwd = jnp.concatenate([jnp.cos(ang), jnp.sin(ang)], axis=0).astype(BF16)
    n_out = L // 2 + jnp.arange(L, dtype=jnp.int32)
    r2 = (n_out[:, None] * (2 * f[None, :] + 1)) % (2 * n_fft)
    ang2 = r2.astype(F32) * (math.pi / n_fft)
    inv = jnp.concatenate([jnp.cos(ang2), jnp.sin(ang2)], axis=1).astype(BF16)
    return fwd, inv


def _idft_kernel(s_ref, k_ref, inv_ref, gate_ref, z_ref, skip_ref, o_ref, y_ref, *, L):
    i = pl.program_id(2)

    @pl.when(i == 0)
    def _():
        a, b = s_ref[0:L, :], s_ref[L:2 * L, :]
        ka, kb = k_ref[0:L, :], k_ref[L:2 * L, :]
        scale = 1.0 / L
        y_ref[0:L, :] = ((a * ka - b * kb) * scale).astype(BF16)
        y_ref[L:2 * L, :] = ((a * kb + b * ka) * scale).astype(BF16)

    conv = jnp.dot(inv_ref[...], y_ref[...], preferred_element_type=F32)
    z = z_ref[...]
    o_ref[...] = (gate_ref[...] * (conv + z * skip_ref[...])).astype(o_ref.dtype)


def idft_gate(spec, kspec, k_col0, inv, gate, gate_col0, z, z_col0, skip, *, L, nb, tm, tn,
              out_dtype):
    d = D_MODEL
    assert L % tm == 0 and d % tn == 0
    per = L // tm
    kc, gc, zc = k_col0 // tn, gate_col0 // tn, z_col0 // tn
    est = (2 * _nbytes((2 * L, tn), F32) * 2 + _nbytes((2 * L, tn), BF16)
           + 2 * _nbytes((tm, 2 * L), BF16) + 8 * _nbytes((tm, tn), F32) + 4 * _nbytes((L, tn), F32))
    return pl.pallas_call(
        functools.partial(_idft_kernel, L=L),
        grid=(nb, d // tn, per),
        in_specs=[pl.BlockSpec((None, 2 * L, tn), lambda b, j, i: (b, 0, j)),
                  pl.BlockSpec((2 * L, tn), lambda b, j, i: (0, j + kc)),
                  pl.BlockSpec((tm, 2 * L), lambda b, j, i: (i, 0)),
                  pl.BlockSpec((tm, tn), lambda b, j, i: (b * per + i, j + gc)),
                  pl.BlockSpec((tm, tn), lambda b, j, i: (b * per + i, j + zc)),
                  pl.BlockSpec((1, tn), lambda b, j, i: (0, j))],
        out_specs=pl.BlockSpec((tm, tn), lambda b, j, i: (b * per + i, j)),
        out_shape=jax.ShapeDtypeStruct((nb * L, d), out_dtype),
        scratch_shapes=[pltpu.VMEM((2 * L, tn), BF16)],
        compiler_params=pltpu.CompilerParams(
            dimension_semantics=("arbitrary", "arbitrary", "arbitrary"),
            vmem_limit_bytes=_vmem(est)),
        name="idft_gate",
    )(spec, kspec, inv, gate, z, skip.reshape(1, d))


def _dft(fwd, w3d, *, L, tm, tn, w_col0=0, n_out=D_MODEL, name):
    nb = w3d.shape[0]
    per = (2 * L) // tm
    steps = jnp.arange(nb * per, dtype=jnp.int32)
    return _linear_call(fwd, w3d, tm=tm, tn=tn, n_out=n_out, n_tiles=nb * per, xt=steps % per,
                        te=steps // per, nv=None, bias=None, out_dtype=F32, w_col0=w_col0,
                        name=name)


def hyena_stream(u, L, nb, hy, skip):
    d = D_MODEL
    tm = min(512, L)
    tms = min(512, 2 * L)
    k = hyena_filters(L, *hy)
    fwd, inv = dft_tables(L)
    kspec = _dft(fwd, k[None], L=L, tm=tms, tn=512, n_out=HY_ORDER * d, name="dft_filters")
    u3 = u.reshape(nb, L, 3 * d)
    s1 = _dft(fwd, u3, L=L, tm=tms, tn=512, name="dft_fwd").reshape(nb, 2 * L, d)
    z1 = idft_gate(s1, kspec, 0, inv, u, d, u, 0, skip[0], L=L, nb=nb, tm=tm, tn=256,
                   out_dtype=F32)
    s2 = _dft(fwd, z1.reshape(nb, L, d), L=L, tm=tms, tn=512,
              name="dft_fwd").reshape(nb, 2 * L, d)
    return idft_gate(s2, kspec, d, inv, u, 2 * d, z1, 0, skip[1], L=L, nb=nb, tm=tm, tn=256,
                     out_dtype=BF16)


def hyena_mix(a, n_rows, w_in, b_in, conv_w, conv_b, filt, skip, w_out, b_out):
    u = linear(a, w_in, tm=512, tn=1024, bias=b_in, name="hyena_in")
    u = short_conv(u, conv_w, conv_b)
    z = hyena_stream(u[:N_LAT] if n_rows > N_LAT else u, SEQ, BATCH, filt, skip)
    if n_rows > N_LAT:
        z_ctx = hyena_stream(u[N_LAT:], CTX_LEN, BATCH, filt, skip)
        z = jnp.concatenate([z, z_ctx], axis=0)
    return linear(z, w_out, tm=512, tn=1024, bias=b_out, name="hyena_out")


def _attn_kernel(q_ref, k_ref, v_ref, o_ref):
    s = lax.dot_general(q_ref[...], k_ref[...], (((1,), (1,)), ((), ())),
                        preferred_element_type=F32) * (MLA_NOPE + MLA_ROPE) ** -0.5
    m = jnp.max(s, axis=-1, keepdims=True)
    p = jnp.exp(s - m)
    l = jnp.sum(p, axis=-1, keepdims=True)
    o = jnp.dot(p.astype(BF16), v_ref[...], preferred_element_type=F32)
    o_ref[...] = (o / l).astype(o_ref.dtype)


def attention(q, k, v, *, tq=512):
    b, L, _ = q.shape
    lk = k.shape[1]
    est = (2 * _nbytes((tq, MLA_QK_PAD), BF16) + 2 * _nbytes((lk, MLA_QK_PAD), BF16)
           + 2 * _nbytes((lk, MLA_V), BF16) + 4 * _nbytes((tq, lk), F32))
    out = pl.pallas_call(
        _attn_kernel,
        grid=(b, MLA_HEADS, L // tq),
        in_specs=[pl.BlockSpec((None, tq, MLA_QK_PAD), lambda b, h, i: (b, i, h)),
                  pl.BlockSpec((None, lk, MLA_QK_PAD), lambda b, h, i: (b, 0, h)),
                  pl.BlockSpec((None, lk, MLA_V), lambda b, h, i: (b, 0, h))],
        out_specs=pl.BlockSpec((None, tq, MLA_V), lambda b, h, i: (b, i, h)),
        out_shape=jax.ShapeDtypeStruct((b, L, MLA_HEADS * MLA_V), BF16),
        compiler_params=pltpu.CompilerParams(
            dimension_semantics=("arbitrary", "arbitrary", "arbitrary"),
            vmem_limit_bytes=_vmem(est)),
        name="mla_attention",
    )(q, k, v)
    return out.reshape(b * L, MLA_HEADS * MLA_V)


def _rms_norm(x, g):
    return x * lax.rsqrt(jnp.mean(x * x, axis=-1, keepdims=True) + RMS_EPS) * g


def _grid_angles(L):
    rows = L // GRID_W
    row = jnp.repeat(jnp.arange(rows), GRID_W).astype(F32)
    col = jnp.tile(jnp.arange(GRID_W), rows).astype(F32)
    n = MLA_ROPE // 4
    inv = ROPE_BASE ** (-jnp.arange(n, dtype=F32) / n)
    return row[:, None] * inv, col[:, None] * inv


def _rope_rotate(x, ang):
    n = x.shape[-1] // 2
    cos, sin = jnp.cos(ang), jnp.sin(ang)
    x1, x2 = x[..., :n], x[..., n:]
    return jnp.concatenate([x1 * cos - x2 * sin, x1 * sin + x2 * cos], axis=-1)


def _axial_rope(x, ang_row, ang_col):
    half = x.shape[-1] // 2
    return jnp.concatenate([_rope_rotate(x[..., :half], ang_row),
                            _rope_rotate(x[..., half:], ang_col)], axis=-1)


def mla_mix(a, w_in, q_norm, kv_norm, w_uq, w_ukv, w_o):
    n_in = MLA_Q_RANK + MLA_KV_RANK + MLA_ROPE
    n_in_pad = -(-n_in // LANE) * LANE
    proj = linear(a, jnp.pad(w_in, ((0, 0), (0, n_in_pad - n_in))), tm=512, tn=n_in_pad,
                  name="mla_in")
    qc = _rms_norm(proj[:N_LAT, :MLA_Q_RANK], q_norm).astype(BF16)
    kvc = _rms_norm(proj[:, MLA_Q_RANK:MLA_Q_RANK + MLA_KV_RANK], kv_norm).astype(BF16)
    k_rope = proj[:, MLA_Q_RANK + MLA_KV_RANK:n_in]
    q = linear(qc, w_uq, tm=512, tn=1024, name="mla_uq")
    kv = linear(kvc, w_ukv, tm=512, tn=1024, name="mla_ukv")
    ang_row, ang_col = _grid_angles(SEQ)
    q = q.reshape(BATCH, SEQ, MLA_HEADS, MLA_NOPE + MLA_ROPE)
    q_rope = _axial_rope(q[..., MLA_NOPE:], ang_row[:, None, :], ang_col[:, None, :])
    pad = MLA_QK_PAD - MLA_NOPE - MLA_ROPE
    qz = jnp.zeros(q.shape[:3] + (pad,), F32)
    qf = jnp.concatenate([q[..., :MLA_NOPE], q_rope, qz], axis=-1).astype(BF16)
    qf = qf.reshape(BATCH, SEQ, MLA_HEADS * MLA_QK_PAD)

    def keys_values(kv_rows, k_rope_rows, L, angles):
        kv_rows = kv_rows.reshape(BATCH, L, MLA_HEADS, MLA_NOPE + MLA_V)
        kr = k_rope_rows.reshape(BATCH, L, MLA_ROPE)
        if angles is not None:
            kr = _axial_rope(kr, angles[0], angles[1])
        kr = jnp.broadcast_to(kr[:, :, None, :], (BATCH, L, MLA_HEADS, MLA_ROPE))
        kz = jnp.zeros((BATCH, L, MLA_HEADS, pad), F32)
        kf = jnp.concatenate([kv_rows[..., :MLA_NOPE], kr, kz], axis=-1).astype(BF16)
        return (kf.reshape(BATCH, L, MLA_HEADS * MLA_QK_PAD),
                kv_rows[..., MLA_NOPE:].astype(BF16).reshape(BATCH, L, MLA_HEADS * MLA_V))

    k_l, v_l = keys_values(kv[:N_LAT], k_rope[:N_LAT], SEQ, (ang_row, ang_col))
    k_c, v_c = keys_values(kv[N_LAT:], k_rope[N_LAT:], CTX_LEN, None)
    att = attention(qf, jnp.concatenate([k_l, k_c], axis=1), jnp.concatenate([v_l, v_c], axis=1))
    return linear(att, w_o, tm=512, tn=1024, name="mla_out")


def dense_ffn(a, wg, wu, wd):
    p = swiglu_up(a, wg[None], wu[None], tm=512, tf=512, name="ffn_up")
    return linear(p, wd, tm=512, tn=512, name="ffn_down")


def _router_kernel(h_ref, sc_ref, sh_ref, whi_ref, wlo_ref, o_ref):
    a = h_ref[...] * (1.0 + sc_ref[...]) + sh_ref[...]
    a_hi, a_lo = _split_bf16(a)
    w_hi = whi_ref[...]
    o_ref[...] = (jnp.dot(a_hi, w_hi, preferred_element_type=F32)
                  + jnp.dot(a_lo, w_hi, preferred_element_type=F32)
                  + jnp.dot(a_hi, wlo_ref[...], preferred_element_type=F32))


def router_logits(h, n_rows, mods, w_router, *, tm=256):
    w = jnp.pad(w_router, ((0, 0), (0, LANE - N_EXPERTS)))
    w_hi, w_lo = _split_bf16(w)
    row = pl.BlockSpec((tm, D_MODEL), lambda i: (i, 0))
    wspec = pl.BlockSpec((D_MODEL, LANE), lambda i: (0, 0))
    out = pl.pallas_call(
        _router_kernel,
        grid=(n_rows // tm,),
        in_specs=[row, _mod_spec(4, tm), _mod_spec(3, tm), wspec, wspec],
        out_specs=pl.BlockSpec((tm, LANE), lambda i: (i, 0)),
        out_shape=jax.ShapeDtypeStruct((n_rows, LANE), F32),
        compiler_params=pltpu.CompilerParams(dimension_semantics=("arbitrary",)),
        name="moe_router",
    )(h, mods, mods, w_hi, w_lo)
    return out[:, :N_EXPERTS]


def moe_ffn(h, a, n_rows, mods, w_router, wg, wu, wd, *, tm=512):
    logits = router_logits(h, n_rows, mods, w_router)
    top_v, top_i = lax.top_k(logits, TOP_K)
    gates = jax.nn.softmax(top_v, axis=-1)
    n_assign = n_rows * TOP_K
    n_tiles = n_assign // tm + N_EXPERTS
    flat_e = top_i.reshape(n_assign)
    onehot = (flat_e[:, None] == jnp.arange(N_EXPERTS)[None, :]).astype(jnp.int32)
    csum = jnp.cumsum(onehot, axis=0)
    rank = jnp.sum(csum * onehot, axis=1) - 1
    counts = csum[-1]
    padded = ((counts + tm - 1) // tm) * tm
    ends = jnp.cumsum(padded)
    starts = ends - padded
    pos = starts[flat_e] + rank
    src = jnp.zeros((n_tiles * tm,), jnp.int32).at[pos].set(
        jnp.arange(n_assign, dtype=jnp.int32) // TOP_K)
    tile_row0 = jnp.arange(n_tiles, dtype=jnp.int32) * tm
    te = jnp.minimum(jnp.searchsorted(ends, tile_row0, side="right"),
                     N_EXPERTS - 1).astype(jnp.int32)
    nv = (ends[-1] // tm).astype(jnp.int32).reshape(1)
    te = jnp.where(tile_row0 < ends[-1], te, te[jnp.maximum(nv[0] - 1, 0)])
    x_sorted = jnp.take(a[:n_rows], src, axis=0)
    p = swiglu_up(x_sorted, wg, wu, tm=tm, tf=512, te=te, nv=nv, name="moe_up")
    out = linear(p, wd, tm=tm, tn=512, te=te, nv=nv, name="moe_down")
    pos = pos.reshape(n_rows, TOP_K)
    return (gates[:, 0:1] * jnp.take(out, pos[:, 0], axis=0)
            + gates[:, 1:2] * jnp.take(out, pos[:, 1], axis=0))


def kernel(x, c, ctx, c_ctx, w_mod, b_mod, ln_g, ln_b, pool_w, pool_scale, hy_w_in, hy_b_in, hy_conv_w, hy_conv_b, hy_f_w1, hy_f_b1, hy_f_w2, hy_f_b2, hy_f_w3, hy_f_b3, hy_f_freq, hy_f_wout, hy_skip, hy_w_out, hy_b_out, mla_w_in, mla_q_norm, mla_kv_norm, mla_w_uq, mla_w_ukv, mla_w_o, ffn_w_gate, ffn_w_up, ffn_w_down, moe_w_router, moe_w_gate, moe_w_up, moe_w_down):
    attn_layers = [i for i in range(DEPTH) if i % N_MIXERS == 2]
    last_read = attn_layers[-1] if attn_layers else -1

    c8 = jnp.concatenate([c, c_ctx[None], jnp.zeros((MOD_ROWS - BATCH - 1, D_MODEL), F32)], axis=0)
    mods = modulation_table(c8, w_mod, b_mod)

    h = jnp.concatenate([x.reshape(N_LAT, D_MODEL), ctx.reshape(N_CTX, D_MODEL)], axis=0)
    a = None
    for i in range(DEPTH):
        kind, j = i % N_MIXERS, i // N_MIXERS
        ctx_live = i <= last_read
        ctx_out = i < last_read
        n_rows = N_LAT + N_CTX if ctx_out else N_LAT
        m = mods[i]
        if a is None and kind != 0:
            a = modulate(h, m, 1, 0)
        if kind == 0:
            tl = CTX_LEN if n_rows > N_LAT else 512
            h1, a2 = pool_layer(h, n_rows, m, pool_w[j].astype(BF16), pool_scale[j],
                                ln_g[i, 0], ln_b[i, 0], tl=tl)
        else:
            if kind == 1:
                filt = (hy_f_w1[j], hy_f_b1[j], hy_f_w2[j], hy_f_b2[j], hy_f_w3[j], hy_f_b3[j],
                        hy_f_freq[j], hy_f_wout[j])
                a_in = a if a.shape[0] == n_rows else a[:n_rows]
                y = hyena_mix(a_in, n_rows, hy_w_in[j], hy_b_in[j], hy_conv_w[j], hy_conv_b[j],
                              filt, hy_skip[j], hy_w_out[j], hy_b_out[j])
            else:
                assert ctx_live and not ctx_out
                y = mla_mix(a, mla_w_in[j], mla_q_norm[j], mla_kv_norm[j], mla_w_uq[j],
                            mla_w_ukv[j], mla_w_o[j])
            h1, a2 = resid_ln(h, y, n_rows, m, 2, ln_g[i, 0], ln_b[i, 0], m, 4, 3)
        f = i // 2
        if i % 2 == 0:
            y = dense_ffn(a2, ffn_w_gate[f], ffn_w_up[f], ffn_w_down[f])
        else:
            y = moe_ffn(h1, a2, n_rows, m, moe_w_router[f], moe_w_gate[f], moe_w_up[f],
                        moe_w_down[f])
        nm = mods[min(i + 1, DEPTH - 1)]
        h2, a = resid_ln(h1, y, n_rows, m, 5, ln_g[i, 1], ln_b[i, 1], nm, 1, 0)
        if n_rows > N_LAT and not (i + 1 < last_read):
            h_ctx_a = a[N_LAT:]
        h = h2
        if (i + 1) % N_MIXERS == 2 and i + 1 < DEPTH and a.shape[0] == N_LAT:
            a = jnp.concatenate([a, h_ctx_a], axis=0)
        if (i + 1) % N_MIXERS == 0:
            a = None
    return h[:N_LAT].reshape(BATCH, SEQ, D_MODEL)
```

```python
import functools
import math

import jax
import jax.numpy as jnp
from jax import lax
from jax.experimental import pallas as pl
from jax.experimental.pallas import tpu as pltpu

F32 = jnp.float32
BF16 = jnp.bfloat16
U32 = jnp.uint32

D_MODEL = 2048
BATCH = 4
SEQ = 2048
DEPTH = 4
GRID_W = 64
CTX_LEN = 256
N_MIXERS = 3
POOL_WINDOWS = (2, 4, 8, 16)
POOL_GROUP = D_MODEL // len(POOL_WINDOWS)
HY_ORDER = 2
HY_SHORT = 3
HY_EMB = 33
HY_FILTER = 64
HY_DECAY_TARGET = 1e-2
HY_FAST = 0.3
HY_SLOW = 1.5
MLA_HEADS = D_MODEL // 128
MLA_Q_RANK = 512
MLA_KV_RANK = 512
MLA_NOPE = 128
MLA_ROPE = 64
MLA_V = 128
ROPE_BASE = 10000.0
D_FF = 5632
N_EXPERTS = 8
TOP_K = 2
EXPERT_FF = 2 * D_MODEL
LN_EPS = 1e-5
RMS_EPS = 1e-6
DN_ALPHA = (2 * DEPTH) ** 0.25

N_LAT = BATCH * SEQ
N_CTX = BATCH * CTX_LEN
MOD_ROWS = 8
CTX_MOD_ROW = BATCH
LANE = 128
SUBLANE = 8
VMEM_CAP = 60 * 1024 * 1024
MLA_QK_PAD = 256
MLA_IN = MLA_Q_RANK + MLA_KV_RANK + MLA_ROPE
MLA_IN_PAD = MLA_IN + MLA_ROPE


def _vmem(nbytes):
    return int(min(VMEM_CAP, max(16 * 1024 * 1024, nbytes * 3 // 2)))


def _nbytes(shape, dtype):
    return math.prod(shape) * jnp.dtype(dtype).itemsize


def _mod_row(i, tm):
    return jnp.minimum((i * tm) // SEQ, CTX_MOD_ROW)


def _mod_spec(which, tm):
    return pl.BlockSpec((None, 1, D_MODEL), lambda i: (_mod_row(i, tm) * 6 + which, 0, 0))


def _params(sem, est):
    return pltpu.CompilerParams(dimension_semantics=("arbitrary",) * sem,
                                vmem_limit_bytes=_vmem(est))


def _mod_kernel(c_ref, w_ref, b_ref, o_ref):
    c = c_ref[...]
    a = c / (1.0 + jnp.exp(-c))
    o_ref[...] = jnp.dot(a.astype(BF16), w_ref[...].astype(BF16),
                         preferred_element_type=F32) + b_ref[...]


def modulation_table(c8, w_mod, b_mod):
    tn = 1024
    n = 6 * D_MODEL
    out = pl.pallas_call(
        _mod_kernel,
        grid=(DEPTH, n // tn),
        in_specs=[pl.BlockSpec((MOD_ROWS, D_MODEL), lambda l, j: (0, 0)),
                  pl.BlockSpec((None, D_MODEL, tn), lambda l, j: (l, 0, j)),
                  pl.BlockSpec((None, 1, tn), lambda l, j: (l, 0, j))],
        out_specs=pl.BlockSpec((None, MOD_ROWS, tn), lambda l, j: (l, 0, j)),
        out_shape=jax.ShapeDtypeStruct((DEPTH, MOD_ROWS, n), F32),
        compiler_params=_params(2, 3 * _nbytes((D_MODEL, tn), F32)),
        name="modulation_table",
    )(c8, w_mod, b_mod.reshape(DEPTH, 1, n))
    return out.reshape(DEPTH, MOD_ROWS * 6, 1, D_MODEL)


def _pack_halves(a):
    half = a.shape[1] // 2
    hi = lax.bitcast_convert_type(a[:, :half].astype(BF16).astype(F32), U32)
    lo = lax.bitcast_convert_type(a[:, half:].astype(BF16).astype(F32), U32)
    return hi | (lo >> 16)


def _unpack_halves(p):
    hi = lax.bitcast_convert_type(p & jnp.uint32(0xFFFF0000), F32).astype(BF16)
    lo = lax.bitcast_convert_type(p << 16, F32).astype(BF16)
    return hi, lo


def _linear_kernel(xt_ref, te_ref, nv_ref, x_ref, w_ref, *rest, has_bias, cast_w):
    rest = list(rest)
    b_ref = rest.pop(0) if has_bias else None
    o_ref = rest.pop(0)
    wbf_ref = rest.pop(0) if cast_w else w_ref
    i = pl.program_id(1)

    if cast_w:
        prev = te_ref[jnp.maximum(i - 1, 0)]

        @pl.when(jnp.logical_or(i == 0, te_ref[i] != prev))
        def _():
            wbf_ref[...] = w_ref[...].astype(BF16)

    @pl.when(i < nv_ref[0])
    def _():
        acc = jnp.dot(x_ref[...].astype(BF16), wbf_ref[...], preferred_element_type=F32)
        if has_bias:
            acc = acc + b_ref[...]
        o_ref[...] = acc.astype(o_ref.dtype)

    @pl.when(i >= nv_ref[0])
    def _():
        o_ref[...] = jnp.zeros_like(o_ref)


def linear(x, w, *, tm, tn, e=0, n_out=None, n_tiles=None, xt=None, te=None, nv=None, bias=None,
           out_dtype=F32, w_col0=0, name="linear"):
    k = x.shape[1]
    if w.ndim == 2:
        w = w[None]
    if n_out is None:
        n_out = w.shape[2] - w_col0
    assert x.shape[0] % tm == 0 and n_out % tn == 0 and w_col0 % tn == 0 and w.shape[1] == k
    if n_tiles is None:
        n_tiles = x.shape[0] // tm
    if xt is None:
        xt = jnp.arange(n_tiles, dtype=jnp.int32)
    if te is None:
        te = jnp.full((n_tiles,), e, jnp.int32)
    if nv is None:
        nv = jnp.full((1,), n_tiles, jnp.int32)
    cast_w = w.dtype != BF16
    has_bias = bias is not None
    cb = w_col0 // tn
    in_specs = [pl.BlockSpec((tm, k), lambda j, i, xt, te, nv: (xt[i], 0)),
                pl.BlockSpec((None, k, tn), lambda j, i, xt, te, nv: (te[i], 0, j + cb))]
    args = [x, w]
    if has_bias:
        bias = bias.reshape(-1, 1, bias.shape[-1])
        in_specs.append(pl.BlockSpec((None, 1, tn), lambda j, i, xt, te, nv: (te[i], 0, j)))
        args.append(bias)
    scratch = [pltpu.VMEM((k, tn), BF16)] if cast_w else []
    est = (2 * _nbytes((tm, k), x.dtype) + 2 * _nbytes((k, tn), w.dtype)
           + (_nbytes((k, tn), BF16) if cast_w else 0) + _nbytes((tm, k), BF16)
           + 2 * _nbytes((tm, tn), out_dtype) + 2 * _nbytes((tm, tn), F32))
    return pl.pallas_call(
        functools.partial(_linear_kernel, has_bias=has_bias, cast_w=cast_w),
        grid_spec=pltpu.PrefetchScalarGridSpec(
            num_scalar_prefetch=3,
            grid=(n_out // tn, n_tiles),
            in_specs=in_specs,
            out_specs=pl.BlockSpec((tm, tn), lambda j, i, xt, te, nv: (i, j)),
            scratch_shapes=scratch),
        out_shape=jax.ShapeDtypeStruct((n_tiles * tm, n_out), out_dtype),
        compiler_params=_params(2, est),
        name=name,
    )(xt, te, nv, *args)


def _swiglu_kernel(te_ref, nv_ref, x_ref, wg_ref, wu_ref, o_ref, wgb_ref, wub_ref, *, packed):
    i = pl.program_id(1)
    prev = te_ref[jnp.maximum(i - 1, 0)]

    @pl.when(jnp.logical_or(i == 0, te_ref[i] != prev))
    def _():
        wgb_ref[...] = wg_ref[...].astype(BF16)
        wub_ref[...] = wu_ref[...].astype(BF16)

    @pl.when(i < nv_ref[0])
    def _():
        if packed:
            hi, lo = _unpack_halves(x_ref[...])
            half = hi.shape[1]
            g = (jnp.dot(hi, wgb_ref[0:half, :], preferred_element_type=F32)
                 + jnp.dot(lo, wgb_ref[half:2 * half, :], preferred_element_type=F32))
            u = (jnp.dot(hi, wub_ref[0:half, :], preferred_element_type=F32)
                 + jnp.dot(lo, wub_ref[half:2 * half, :], preferred_element_type=F32))
        else:
            x = x_ref[...]
            g = jnp.dot(x, wgb_ref[...], preferred_element_type=F32)
            u = jnp.dot(x, wub_ref[...], preferred_element_type=F32)
        o_ref[...] = ((g / (1.0 + jnp.exp(-g))) * u).astype(o_ref.dtype)

    @pl.when(i >= nv_ref[0])
    def _():
        o_ref[...] = jnp.zeros_like(o_ref)


def swiglu_up(x, wg, wu, *, tm, tf, e=0, te=None, nv=None, name="swiglu_up"):
    m = x.shape[0]
    packed = x.dtype == U32
    k, f = wg.shape[1], wg.shape[2]
    assert m % tm == 0 and f % tf == 0
    n_tiles = m // tm
    if te is None:
        te = jnp.full((n_tiles,), e, jnp.int32)
    if nv is None:
        nv = jnp.full((1,), n_tiles, jnp.int32)
    wspec = pl.BlockSpec((None, k, tf), lambda j, i, te, nv: (te[i], 0, j))
    est = (3 * _nbytes((tm, k), BF16) + 4 * _nbytes((k, tf), F32) + 2 * _nbytes((k, tf), BF16)
           + 2 * _nbytes((tm, tf), BF16) + 3 * _nbytes((tm, tf), F32))
    return pl.pallas_call(
        functools.partial(_swiglu_kernel, packed=packed),
        grid_spec=pltpu.PrefetchScalarGridSpec(
            num_scalar_prefetch=2,
            grid=(f // tf, n_tiles),
            in_specs=[pl.BlockSpec((tm, x.shape[1]), lambda j, i, te, nv: (i, 0)), wspec, wspec],
            out_specs=pl.BlockSpec((tm, tf), lambda j, i, te, nv: (i, j)),
            scratch_shapes=[pltpu.VMEM((k, tf), BF16), pltpu.VMEM((k, tf), BF16)]),
        out_shape=jax.ShapeDtypeStruct((m, f), BF16),
        compiler_params=_params(2, est),
        name=name,
    )(te, nv, x, wg, wu)


def _ln_mod(v, g, b, sc, sh, pack):
    mu = jnp.mean(v, axis=-1, keepdims=True)
    d = v - mu
    var = jnp.mean(d * d, axis=-1, keepdims=True)
    h = d * lax.rsqrt(var + LN_EPS) * g + b
    a = h * (1.0 + sc) + sh
    return h, (_pack_halves(a) if pack else a.astype(BF16))


def _resid_ln_kernel(*refs, n_y, pack):
    h_ref = refs[0]
    y_refs = refs[1:1 + n_y]
    rest = refs[1 + n_y:]
    if n_y == 2:
        g0_ref, g1_ref = rest[0], rest[1]
        rest = rest[2:]
        y = g0_ref[...] * y_refs[0][...] + g1_ref[...] * y_refs[1][...]
    else:
        y = y_refs[0][...]
    gate_ref, g_ref, b_ref, sc_ref, sh_ref, o_ref, a_ref = rest
    v = DN_ALPHA * h_ref[...] + gate_ref[...] * y
    h, a = _ln_mod(v, g_ref[...], b_ref[...], sc_ref[...], sh_ref[...], pack)
    o_ref[...] = h
    a_ref[...] = a


def resid_ln(h, ys, n_rows, mods, gate_idx, ln_g, ln_b, nmods, sc_idx, sh_idx, *, row_gates=None,
             pack=False, tm=256):
    assert n_rows % tm == 0
    row = pl.BlockSpec((tm, D_MODEL), lambda i: (i, 0))
    col = pl.BlockSpec((tm, 1), lambda i: (i, 0))
    vec = pl.BlockSpec((1, D_MODEL), lambda i: (0, 0))
    a_w = D_MODEL // 2 if pack else D_MODEL
    a_dt = U32 if pack else BF16
    est = 10 * _nbytes((tm, D_MODEL), F32)
    in_specs = [row] + [row] * len(ys) + ([col, col] if row_gates else []) + [
        _mod_spec(gate_idx, tm), vec, vec, _mod_spec(sc_idx, tm), _mod_spec(sh_idx, tm)]
    args = [h, *ys, *(row_gates or []), mods, ln_g.reshape(1, D_MODEL), ln_b.reshape(1, D_MODEL),
            nmods, nmods]
    return pl.pallas_call(
        functools.partial(_resid_ln_kernel, n_y=len(ys), pack=pack),
        grid=(n_rows // tm,),
        in_specs=in_specs,
        out_specs=[row, pl.BlockSpec((tm, a_w), lambda i: (i, 0))],
        out_shape=[jax.ShapeDtypeStruct((n_rows, D_MODEL), F32),
                   jax.ShapeDtypeStruct((n_rows, a_w), a_dt)],
        compiler_params=_params(1, est),
        name="resid_ln",
    )(*args)


def _seq_pos(i, tl):
    row0 = i * tl
    is_lat = row0 < N_LAT
    pos = jnp.where(is_lat, row0 % SEQ, (row0 - N_LAT) % CTX_LEN)
    seqlen = jnp.where(is_lat, SEQ, CTX_LEN)
    return pos, seqlen


def _halo_specs(tl, width, n_rows, col_index=None):
    per = tl // SUBLANE
    last = n_rows // SUBLANE - 1
    if col_index is None:
        prev = pl.BlockSpec((SUBLANE, width), lambda i: (jnp.maximum(i * per - 1, 0), 0))
        nxt = pl.BlockSpec((SUBLANE, width), lambda i: (jnp.minimum((i + 1) * per, last), 0))
    else:
        prev = pl.BlockSpec((SUBLANE, width),
                            lambda i, j: (jnp.maximum(i * per - 1, 0), col_index(j)))
        nxt = pl.BlockSpec((SUBLANE, width),
                           lambda i, j: (jnp.minimum((i + 1) * per, last), col_index(j)))
    return prev, nxt


def _shift_rows(x, s):
    n = x.shape[0]
    return pltpu.roll(x, s % n, 0)


def _pool_kernel(h_ref, hp_ref, hn_ref, sc_ref, sh_ref, gate_ref, w_ref, ps_ref, g_ref, b_ref,
                 nsc_ref, nsh_ref, o_ref, a_ref, *, tl, pack):
    i = pl.program_id(0)
    pos, seqlen = _seq_pos(i, tl)
    sc = 1.0 + sc_ref[...]
    sh = sh_ref[...]
    h = h_ref[...]
    a = h * sc + sh
    keep_prev = (pos > 0).astype(F32)
    keep_next = (pos + tl < seqlen).astype(F32)
    ext = jnp.concatenate([(hp_ref[...] * sc + sh) * keep_prev, a,
                           (hn_ref[...] * sc + sh) * keep_next], axis=0)
    t = pos + lax.broadcasted_iota(jnp.int32, (tl, 1), 0)
    n_ext = tl + 2 * SUBLANE
    parts = []
    for g, w in enumerate(POOL_WINDOWS):
        cols = slice(g * POOL_GROUP, (g + 1) * POOL_GROUP)
        s = ext[:, cols]
        s = s + _shift_rows(s, 1)
        r = 1
        while 2 * r < w:
            s = _shift_rows(s, n_ext - r) + _shift_rows(s, r)
            r *= 2
        s = s[SUBLANE:SUBLANE + tl]
        cnt = (jnp.minimum(t + w // 2, seqlen) - jnp.maximum(t - w // 2, 0)).astype(F32)
        d = s / cnt - a[:, cols]
        parts.append(jnp.dot(d.astype(BF16), w_ref[g], preferred_element_type=F32))
    y = jnp.concatenate(parts, axis=1) * ps_ref[...]
    v = DN_ALPHA * h + gate_ref[...] * y
    hn, an = _ln_mod(v, g_ref[...], b_ref[...], nsc_ref[...], nsh_ref[...], pack)
    o_ref[...] = hn
    a_ref[...] = an


def pool_layer(h, n_rows, mods, w_grp_bf16, pool_scale, ln_g, ln_b, *, tl, pack):
    row = pl.BlockSpec((tl, D_MODEL), lambda i: (i, 0))
    vec = pl.BlockSpec((1, D_MODEL), lambda i: (0, 0))
    prev, nxt = _halo_specs(tl, D_MODEL, n_rows)
    a_w = D_MODEL // 2 if pack else D_MODEL
    est = 12 * _nbytes((tl, D_MODEL), F32) + 2 * _nbytes(w_grp_bf16.shape, BF16)
    return pl.pallas_call(
        functools.partial(_pool_kernel, tl=tl, pack=pack),
        grid=(n_rows // tl,),
        in_specs=[row, prev, nxt, _mod_spec(1, tl), _mod_spec(0, tl), _mod_spec(2, tl),
                  pl.BlockSpec(w_grp_bf16.shape, lambda i: (0, 0, 0)), vec, vec, vec,
                  _mod_spec(4, tl), _mod_spec(3, tl)],
        out_specs=[row, pl.BlockSpec((tl, a_w), lambda i: (i, 0))],
        out_shape=[jax.ShapeDtypeStruct((n_rows, D_MODEL), F32),
                   jax.ShapeDtypeStruct((n_rows, a_w), U32 if pack else BF16)],
        compiler_params=_params(1, est),
        name="pool_layer",
    )(h, h, h, mods, mods, mods, w_grp_bf16, pool_scale.reshape(1, D_MODEL),
      ln_g.reshape(1, D_MODEL), ln_b.reshape(1, D_MODEL), mods, mods)


def _short_conv_kernel(u_ref, up_ref, un_ref, w_ref, b_ref, o_ref, *, tl):
    i = pl.program_id(0)
    pos, seqlen = _seq_pos(i, tl)
    keep_prev = (pos > 0).astype(F32)
    keep_next = (pos + tl < seqlen).astype(F32)
    ext = jnp.concatenate([up_ref[...] * keep_prev, u_ref[...], un_ref[...] * keep_next], axis=0)
    n_ext = tl + 2 * SUBLANE
    before = _shift_rows(ext, 1)[SUBLANE:SUBLANE + tl]
    after = _shift_rows(ext, n_ext - 1)[SUBLANE:SUBLANE + tl]
    w = w_ref[...]
    y = b_ref[...] + before * w[0:1]
    y = y + u_ref[...] * w[1:2]
    y = y + after * w[2:3]
    o_ref[...] = y


def short_conv(u, conv_w, conv_b, *, tl=256, tc=2048):
    n_rows, c = u.shape
    blk = pl.BlockSpec((tl, tc), lambda i, j: (i, j))
    prev, nxt = _halo_specs(tl, tc, n_rows, col_index=lambda j: j)
    return pl.pallas_call(
        functools.partial(_short_conv_kernel, tl=tl),
        grid=(n_rows // tl, c // tc),
        in_specs=[blk, prev, nxt,
                  pl.BlockSpec((HY_SHORT, tc), lambda i, j: (0, j)),
                  pl.BlockSpec((1, tc), lambda i, j: (0, j))],
        out_specs=blk,
        out_shape=jax.ShapeDtypeStruct((n_rows, c), F32),
        compiler_params=_params(2, 8 * _nbytes((tl, tc), F32)),
        name="short_conv",
    )(u, u, u, conv_w, conv_b.reshape(1, c))


def _split_bf16(x):
    hi = x.astype(BF16)
    lo = (x - hi.astype(F32)).astype(BF16)
    return hi, lo


def _dot3(a, b):
    a_hi, a_lo = _split_bf16(a)
    b_hi, b_lo = _split_bf16(b)
    return (jnp.dot(a_hi, b_hi, preferred_element_type=F32)
            + jnp.dot(a_lo, b_hi, preferred_element_type=F32)
            + jnp.dot(a_hi, b_lo, preferred_element_type=F32))


def _filter_kernel(z_ref, w1_ref, b1_ref, w2_ref, b2_ref, w3_ref, b3_ref, fr_ref, wo_ref,
                   dist_ref, delta_ref, o_ref, a_ref):
    @pl.when(pl.program_id(0) == 0)
    def _():
        fr = fr_ref[...]
        a = jnp.sin(fr * (_dot3(z_ref[...], w1_ref[...]) + b1_ref[...]))
        a = jnp.sin(fr * (_dot3(a, w2_ref[...]) + b2_ref[...]))
        a_ref[...] = jnp.sin(fr * (_dot3(a, w3_ref[...]) + b3_ref[...]))

    k = _dot3(a_ref[...], wo_ref[...])
    k = k * jnp.exp(-dist_ref[...] * delta_ref[...])
    o_ref[...] = k * lax.rsqrt(jnp.sum(k * k, axis=0, keepdims=True) + 1e-6)


def _pad2(x, rows, cols):
    return jnp.pad(x, ((0, rows - x.shape[0]), (0, cols - x.shape[1])))


def hyena_filters(L, f_w1, f_b1, f_w2, f_b2, f_w3, f_b3, f_freq, f_wout, *, tn=512):
    bands = (HY_EMB - 1) // 2
    t = jnp.linspace(0.0, 1.0, L, dtype=F32)[:, None]
    wpos = 2.0 * math.pi * jnp.arange(L, dtype=F32)[:, None] / L
    f = jnp.linspace(1e-4, bands - 1, bands, dtype=F32)[None, :]
    z = jnp.concatenate([t, jnp.cos(f * wpos), -jnp.sin(f * wpos)], axis=-1)
    dist = (jnp.abs(jnp.arange(L) - L // 2).astype(F32) / max(L // 2, 1))[:, None]
    deltas = jnp.linspace(math.log(HY_DECAY_TARGET) / HY_SLOW, math.log(HY_DECAY_TARGET) / HY_FAST,
                          D_MODEL, dtype=F32)
    absdelta = jnp.tile(jnp.abs(deltas), HY_ORDER)[None, :]
    P = LANE
    n = HY_ORDER * D_MODEL
    full = lambda shape: pl.BlockSpec(shape, lambda j: (0, 0))
    args = [_pad2(z, L, P), _pad2(f_w1, P, P), _pad2(f_b1[None], 1, P), _pad2(f_w2, P, P),
            _pad2(f_b2[None], 1, P), _pad2(f_w3, P, P), _pad2(f_b3[None], 1, P),
            _pad2(f_freq[None], 1, P), _pad2(f_wout, P, n), dist, absdelta]
    in_specs = [full((L, P)), full((P, P)), full((1, P)), full((P, P)), full((1, P)), full((P, P)),
                full((1, P)), full((1, P)), pl.BlockSpec((P, tn), lambda j: (0, j)),
                full((L, 1)), pl.BlockSpec((1, tn), lambda j: (0, j))]
    return pl.pallas_call(
        _filter_kernel,
        grid=(n // tn,),
        in_specs=in_specs,
        out_specs=pl.BlockSpec((L, tn), lambda j: (0, j)),
        out_shape=jax.ShapeDtypeStruct((L, n), F32),
        scratch_shapes=[pltpu.VMEM((L, P), F32)],
        compiler_params=_params(1, 8 * _nbytes((L, tn), F32)),
        name="hyena_filters",
    )(*args)


def dft_tables(L, tm):
    n_fft = 2 * L
    half = tm // 2
    r_idx = jnp.arange(2 * L, dtype=jnp.int32)
    f = (r_idx // tm) * half + (r_idx % half)
    is_sin = (r_idx % tm) >= half
    s = jnp.arange(L, dtype=jnp.int32)
    ang = (((2 * f[:, None] + 1) * s[None, :]) % (2 * n_fft)).astype(F32) * (math.pi / n_fft)
    fwd = jnp.where(is_sin[:, None], jnp.sin(ang), jnp.cos(ang)).astype(BF16)
    n_out = L // 2 + s
    ang2 = ((n_out[:, None] * (2 * f[None, :] + 1)) % (2 * n_fft)).astype(F32) * (math.pi / n_fft)
    inv = jnp.where(is_sin[None, :], jnp.sin(ang2), jnp.cos(ang2)).astype(BF16)
    return fwd, inv


def _dft_mul_kernel(x_ref, w_ref, k_ref, o_ref, wb_ref, *, inv_len):
    @pl.when(pl.program_id(1) == 0)
    def _():
        wb_ref[...] = w_ref[...].astype(BF16)

    s = jnp.dot(x_ref[...], wb_ref[...], preferred_element_type=F32)
    half = s.shape[0] // 2
    a, b = s[:half], s[half:]
    ka, kb = k_ref[0:half, :], k_ref[half:2 * half, :]
    o_ref[0:half, :] = ((a * ka - b * kb) * inv_len).astype(BF16)
    o_ref[half:2 * half, :] = ((a * kb + b * ka) * inv_len).astype(BF16)


def dft_mul(fwd, v, v_col0, kspec, k_col0, *, L, nb, tm, tn):
    d = D_MODEL
    per_b = d // tn
    vc, kc = v_col0 // tn, k_col0 // tn
    est = (2 * _nbytes((tm, L), BF16) + 2 * _nbytes((L, tn), F32) + _nbytes((L, tn), BF16)
           + 2 * _nbytes((tm, tn), F32) + 2 * _nbytes((tm, tn), BF16) + 3 * _nbytes((tm, tn), F32))
    return pl.pallas_call(
        functools.partial(_dft_mul_kernel, inv_len=1.0 / L),
        grid=(nb * per_b, (2 * L) // tm),
        in_specs=[pl.BlockSpec((tm, L), lambda j, i: (i, 0)),
                  pl.BlockSpec((L, tn), lambda j, i: (j // per_b, j % per_b + vc)),
                  pl.BlockSpec((tm, tn), lambda j, i: (i, j % per_b + kc))],
        out_specs=pl.BlockSpec((tm, tn), lambda j, i: (i, j)),
        out_shape=jax.ShapeDtypeStruct((2 * L, nb * d), BF16),
        scratch_shapes=[pltpu.VMEM((L, tn), BF16)],
        compiler_params=_params(2, est),
        name="dft_mul",
    )(fwd, v, kspec)


def _idft_kernel(x_ref, y_ref, gate_ref, z_ref, skip_ref, o_ref):
    conv = jnp.dot(x_ref[...], y_ref[...], preferred_element_type=F32)
    o_ref[...] = (gate_ref[...] * (conv + z_ref[...] * skip_ref[...])).astype(o_ref.dtype)


def idft_gate(inv, y, gate, gate_col0, z, z_col0, skip, *, L, nb, tm, tn, out_dtype):
    d = D_MODEL
    per_b = d // tn
    per = L // tm
    gc, zc = gate_col0 // tn, z_col0 // tn
    est = (2 * _nbytes((tm, 2 * L), BF16) + 2 * _nbytes((2 * L, tn), BF16)
           + 8 * _nbytes((tm, tn), F32))
    return pl.pallas_call(
        _idft_kernel,
        grid=(nb * per_b, per),
        in_specs=[pl.BlockSpec((tm, 2 * L), lambda j, i: (i, 0)),
                  pl.BlockSpec((2 * L, tn), lambda j, i: (0, j)),
                  pl.BlockSpec((tm, tn), lambda j, i: ((j // per_b) * per + i, j % per_b + gc)),
                  pl.BlockSpec((tm, tn), lambda j, i: ((j // per_b) * per + i, j % per_b + zc)),
                  pl.BlockSpec((1, tn), lambda j, i: (0, j % per_b))],
        out_specs=pl.BlockSpec((tm, tn), lambda j, i: ((j // per_b) * per + i, j % per_b)),
        out_shape=jax.ShapeDtypeStruct((nb * L, d), out_dtype),
        compiler_params=_params(2, est),
        name="idft_gate",
    )(inv, y, gate, z, skip.reshape(1, d))


def hyena_stream(u, L, nb, filt, skip):
    d = D_MODEL
    tms = min(1024, 2 * L)
    tmi = min(512, L)
    k = hyena_filters(L, *filt)
    fwd, inv = dft_tables(L, tms)
    kspec = linear(fwd, k, tm=tms, tn=1024, name="dft_filters")
    y1 = dft_mul(fwd, u, 0, kspec, 0, L=L, nb=nb, tm=tms, tn=1024)
    z1 = idft_gate(inv, y1, u, d, u, 0, skip[0], L=L, nb=nb, tm=tmi, tn=1024, out_dtype=F32)
    y2 = dft_mul(fwd, z1, 0, kspec, d, L=L, nb=nb, tm=tms, tn=1024)
    return idft_gate(inv, y2, u, 2 * d, z1, 0, skip[1], L=L, nb=nb, tm=tmi, tn=1024,
                     out_dtype=BF16)


def hyena_mix(a, n_rows, j, w_in, b_in, conv_w, conv_b, filt, skip, w_out, b_out):
    u = linear(a, w_in, e=j, tm=512, tn=1024, bias=b_in, name="hyena_in")
    u = short_conv(u, conv_w, conv_b)
    z = hyena_stream(u[:N_LAT] if n_rows > N_LAT else u, SEQ, BATCH, filt, skip)
    if n_rows > N_LAT:
        z_ctx = hyena_stream(u[N_LAT:], CTX_LEN, BATCH, filt, skip)
        z = jnp.concatenate([z, z_ctx], axis=0)
    return linear(z, w_out, e=j, tm=512, tn=1024, bias=b_out, name="hyena_out")


def _rms(x, g):
    return x * lax.rsqrt(jnp.mean(x * x, axis=-1, keepdims=True) + RMS_EPS) * g


def _mla_q_kernel(p_ref, g_ref, w1_ref, w2_ref, c_ref, s_ref, o_ref, w1b_ref, w2b_ref):
    @pl.when(pl.program_id(1) == 0)
    def _():
        w1b_ref[...] = w1_ref[...].astype(BF16)
        w2b_ref[...] = w2_ref[...].astype(BF16)

    xn = _rms(p_ref[...], g_ref[...]).astype(BF16)
    q1 = jnp.dot(xn, w1b_ref[...], preferred_element_type=F32)
    q2 = jnp.dot(xn, w2b_ref[...], preferred_element_type=F32)
    reps = q1.shape[1] // MLA_QK_PAD
    c = jnp.concatenate([c_ref[...]] * reps, axis=1)
    s = jnp.concatenate([s_ref[...]] * reps, axis=1)
    o_ref[...] = (q1 * c + q2 * s).astype(BF16)


def _mla_kv_kernel(p_ref, r_ref, g_ref, cs_ref, w_ref, o_ref, wb_ref):
    @pl.when(pl.program_id(1) == 0)
    def _():
        wb_ref[...] = w_ref[...].astype(BF16)

    xn = _rms(p_ref[...], g_ref[...]).astype(BF16)
    rr = (r_ref[...] * cs_ref[...]).astype(BF16)
    acc = (jnp.dot(xn, wb_ref[0:MLA_KV_RANK, :], preferred_element_type=F32)
           + jnp.dot(rr, wb_ref[MLA_KV_RANK:MLA_KV_RANK + 2 * MLA_ROPE, :],
                     preferred_element_type=F32))
    o_ref[...] = acc.astype(BF16)


def _attn_kernel(q_ref, kl_ref, kc_ref, vl_ref, vc_ref, o_ref, *, rows):
    c = (MLA_NOPE + MLA_ROPE) ** -0.5 * math.log2(math.e)
    nt = (((1,), (1,)), ((), ()))
    for r in range(q_ref.shape[0] // rows):
        q = q_ref[r * rows:(r + 1) * rows, :]
        s_l = lax.dot_general(q, kl_ref[...], nt, preferred_element_type=F32)
        s_c = lax.dot_general(q, kc_ref[...], nt, preferred_element_type=F32)
        m = jnp.maximum(jnp.max(s_l, axis=-1, keepdims=True), jnp.max(s_c, axis=-1, keepdims=True))
        p_l = jnp.exp2((s_l - m) * c)
        p_c = jnp.exp2((s_c - m) * c)
        l = jnp.sum(p_l, axis=-1, keepdims=True) + jnp.sum(p_c, axis=-1, keepdims=True)
        o = (jnp.dot(p_l.astype(BF16), vl_ref[...], preferred_element_type=F32)
             + jnp.dot(p_c.astype(BF16), vc_ref[...], preferred_element_type=F32))
        o_ref[r * rows:(r + 1) * rows, :] = (o / l).astype(o_ref.dtype)


def _rope_tables():
    rows = SEQ // GRID_W
    row = jnp.repeat(jnp.arange(rows), GRID_W).astype(F32)
    col = jnp.tile(jnp.arange(GRID_W), rows).astype(F32)
    n = MLA_ROPE // 4
    inv = ROPE_BASE ** (-jnp.arange(n, dtype=F32) / n)
    ang_row, ang_col = row[:, None] * inv, col[:, None] * inv
    cos = jnp.concatenate([jnp.cos(ang_row)] * 2 + [jnp.cos(ang_col)] * 2, axis=1)
    sin = jnp.concatenate([jnp.sin(ang_row)] * 2 + [jnp.sin(ang_col)] * 2, axis=1)
    return cos, sin


def _rope_partner(w):
    n = MLA_ROPE // 4
    return jnp.concatenate([-w[..., n:2 * n], w[..., 0:n], -w[..., 3 * n:4 * n],
                            w[..., 2 * n:3 * n]], axis=-1)


def mla_mix(a, j, w_in, q_norm, kv_norm, w_uq, w_ukv, w_o):
    H, NP, R, V, QK = MLA_HEADS, MLA_NOPE, MLA_ROPE, MLA_V, MLA_QK_PAD
    n_all = N_LAT + N_CTX
    cos, sin = _rope_tables()
    w_in_x = jnp.concatenate([w_in, _rope_partner(w_in[:, MLA_IN - R:])], axis=1)
    proj = linear(a, w_in_x, tm=512, tn=MLA_IN_PAD, name="mla_in")
    wq = w_uq.reshape(MLA_Q_RANK, H, NP + R)
    zq = jnp.zeros((MLA_Q_RANK, H, QK - NP - R), F32)
    wq1 = jnp.concatenate([wq, zq], axis=2).reshape(MLA_Q_RANK, H * QK)
    wq2 = jnp.concatenate([jnp.zeros((MLA_Q_RANK, H, NP), F32), _rope_partner(wq[..., NP:]), zq],
                          axis=2).reshape(MLA_Q_RANK, H * QK)
    ones, zeros = jnp.ones((SEQ, NP), F32), jnp.zeros((SEQ, QK - NP - R), F32)
    cq = jnp.concatenate([ones, cos, zeros], axis=1)
    sq = jnp.concatenate([0.0 * ones, sin, zeros], axis=1)
    tm, tn = 512, 1024
    per = SEQ // tm
    q = pl.pallas_call(
        _mla_q_kernel,
        grid=(H * QK // tn, N_LAT // tm),
        in_specs=[pl.BlockSpec((tm, MLA_Q_RANK), lambda jn, i: (i, 0)),
                  pl.BlockSpec((1, MLA_Q_RANK), lambda jn, i: (0, 0)),
                  pl.BlockSpec((MLA_Q_RANK, tn), lambda jn, i: (0, jn)),
                  pl.BlockSpec((MLA_Q_RANK, tn), lambda jn, i: (0, jn)),
                  pl.BlockSpec((tm, QK), lambda jn, i: (i % per, 0)),
                  pl.BlockSpec((tm, QK), lambda jn, i: (i % per, 0))],
        out_specs=pl.BlockSpec((tm, tn), lambda jn, i: (i, jn)),
        out_shape=jax.ShapeDtypeStruct((N_LAT, H * QK), BF16),
        scratch_shapes=[pltpu.VMEM((MLA_Q_RANK, tn), BF16), pltpu.VMEM((MLA_Q_RANK, tn), BF16)],
        compiler_params=_params(2, 6 * _nbytes((MLA_Q_RANK, tn), F32) + 8 * _nbytes((tm, tn), F32)),
        name="mla_q",
    )(proj, q_norm.reshape(1, MLA_Q_RANK), wq1, wq2, cq, sq)
    wkv = w_ukv.reshape(MLA_KV_RANK, H, NP + V)
    wk = jnp.concatenate([wkv[..., :NP], jnp.zeros((MLA_KV_RANK, H, QK - NP), F32)],
                         axis=2).reshape(MLA_KV_RANK, H * QK)
    place = jnp.concatenate([jnp.zeros((R, NP), F32), jnp.eye(R, dtype=F32),
                             jnp.zeros((R, QK - NP - R), F32)], axis=1)
    place = jnp.tile(jnp.concatenate([place, place], axis=0), (1, H))
    w_kv = jnp.concatenate([
        jnp.concatenate([wk, wkv[..., NP:].reshape(MLA_KV_RANK, H * V)], axis=1),
        jnp.concatenate([place, jnp.zeros((2 * R, H * V), F32)], axis=1)], axis=0)
    cs = jnp.concatenate([
        jnp.concatenate([cos, sin], axis=1),
        jnp.concatenate([jnp.ones((CTX_LEN, R), F32), jnp.zeros((CTX_LEN, R), F32)], axis=1)],
        axis=0)
    tm = CTX_LEN
    per = SEQ // tm
    n_kv = H * (QK + V)
    kv = pl.pallas_call(
        _mla_kv_kernel,
        grid=(n_kv // tn, n_all // tm),
        in_specs=[pl.BlockSpec((tm, MLA_KV_RANK), lambda jn, i: (i, MLA_Q_RANK // MLA_KV_RANK)),
                  pl.BlockSpec((tm, 2 * R), lambda jn, i: (i, (MLA_IN - R) // (2 * R))),
                  pl.BlockSpec((1, MLA_KV_RANK), lambda jn, i: (0, 0)),
                  pl.BlockSpec((tm, 2 * R),
                               lambda jn, i: (jnp.where(i * tm < N_LAT, i % per, per), 0)),
                  pl.BlockSpec((MLA_KV_RANK + 2 * R, tn), lambda jn, i: (0, jn))],
        out_specs=pl.BlockSpec((tm, tn), lambda jn, i: (i, jn)),
        out_shape=jax.ShapeDtypeStruct((n_all, n_kv), BF16),
        scratch_shapes=[pltpu.VMEM((MLA_KV_RANK + 2 * R, tn), BF16)],
        compiler_params=_params(2, 4 * _nbytes((MLA_KV_RANK + 2 * R, tn), F32)
                                + 8 * _nbytes((tm, tn), F32)),
        name="mla_kv",
    )(proj, proj, kv_norm.reshape(1, MLA_KV_RANK), cs, w_kv)
    tq = 512
    per = SEQ // tq
    v0 = H * QK // V
    ctx0 = N_LAT // CTX_LEN
    est = (2 * _nbytes((tq, QK), BF16) + 2 * _nbytes((SEQ + CTX_LEN, QK + V), BF16)
           + 6 * _nbytes((tq // 4, SEQ + CTX_LEN), F32))
    att = pl.pallas_call(
        functools.partial(_attn_kernel, rows=tq // 4),
        grid=(BATCH, H, per),
        in_specs=[pl.BlockSpec((tq, QK), lambda b, h, i: (b * per + i, h)),
                  pl.BlockSpec((SEQ, QK), lambda b, h, i: (b, h)),
                  pl.BlockSpec((CTX_LEN, QK), lambda b, h, i: (ctx0 + b, h)),
                  pl.BlockSpec((SEQ, V), lambda b, h, i: (b, v0 + h)),
                  pl.BlockSpec((CTX_LEN, V), lambda b, h, i: (ctx0 + b, v0 + h))],
        out_specs=pl.BlockSpec((tq, V), lambda b, h, i: (b * per + i, h)),
        out_shape=jax.ShapeDtypeStruct((N_LAT, H * V), BF16),
        compiler_params=_params(3, est),
        name="mla_attention",
    )(q, kv, kv, kv, kv)
    return linear(att, w_o, e=j, tm=512, tn=1024, name="mla_out")


def dense_ffn(a, f, wg, wu, wd):
    p = swiglu_up(a, wg, wu, e=f, tm=512, tf=512, name="ffn_up")
    return linear(p, wd, e=f, tm=512, tn=512, name="ffn_down")


def _router_kernel(h_ref, sc_ref, sh_ref, whi_ref, wlo_ref, o_ref):
    a = h_ref[...] * (1.0 + sc_ref[...]) + sh_ref[...]
    a_hi, a_lo = _split_bf16(a)
    w_hi = whi_ref[...]
    o_ref[...] = (jnp.dot(a_hi, w_hi, preferred_element_type=F32)
                  + jnp.dot(a_lo, w_hi, preferred_element_type=F32)
                  + jnp.dot(a_hi, wlo_ref[...], preferred_element_type=F32))


def router_logits(h, n_rows, mods, w_router, *, tm=256):
    w = jnp.pad(w_router, ((0, 0), (0, LANE - N_EXPERTS)))
    w_hi, w_lo = _split_bf16(w)
    row = pl.BlockSpec((tm, D_MODEL), lambda i: (i, 0))
    wspec = pl.BlockSpec((D_MODEL, LANE), lambda i: (0, 0))
    out = pl.pallas_call(
        _router_kernel,
        grid=(n_rows // tm,),
        in_specs=[row, _mod_spec(4, tm), _mod_spec(3, tm), wspec, wspec],
        out_specs=pl.BlockSpec((tm, LANE), lambda i: (i, 0)),
        out_shape=jax.ShapeDtypeStruct((n_rows, LANE), F32),
        compiler_params=_params(1, 8 * _nbytes((tm, D_MODEL), F32)),
        name="moe_router",
    )(h, mods, mods, w_hi, w_lo)
    return out[:, :N_EXPERTS]


def moe_ffn(h, a_packed, n_rows, mods, f, w_router, wg, wu, wd, *, tm=512):
    logits = router_logits(h, n_rows, mods, w_router[f])
    top_v, top_i = lax.top_k(logits, TOP_K)
    gates = jax.nn.softmax(top_v, axis=-1)
    n_assign = n_rows * TOP_K
    n_tiles = n_assign // tm + N_EXPERTS
    flat_e = top_i.reshape(n_assign)
    onehot = (flat_e[:, None] == jnp.arange(N_EXPERTS)[None, :]).astype(jnp.int32)
    csum = jnp.cumsum(onehot, axis=0)
    rank = jnp.sum(csum * onehot, axis=1) - 1
    counts = csum[-1]
    padded = ((counts + tm - 1) // tm) * tm
    ends = jnp.cumsum(padded)
    starts = ends - padded
    pos = starts[flat_e] + rank
    src = jnp.zeros((n_tiles * tm,), jnp.int32).at[pos].set(
        jnp.arange(n_assign, dtype=jnp.int32) // TOP_K)
    tile_row0 = jnp.arange(n_tiles, dtype=jnp.int32) * tm
    te = jnp.minimum(jnp.searchsorted(ends, tile_row0, side="right"),
                     N_EXPERTS - 1).astype(jnp.int32)
    nv = (ends[-1] // tm).astype(jnp.int32).reshape(1)
    te = jnp.where(tile_row0 < ends[-1], te, te[jnp.maximum(nv[0] - 1, 0)]) + f * N_EXPERTS
    x_sorted = a_packed.at[src].get(mode="promise_in_bounds")
    stack = lambda w: w.reshape((-1,) + w.shape[2:])
    p = swiglu_up(x_sorted, stack(wg), stack(wu), tm=tm, tf=512, te=te, nv=nv, name="moe_up")
    out = linear(p, stack(wd), tm=tm, tn=512, te=te, nv=nv, name="moe_down")
    pos = pos.reshape(n_rows, TOP_K)
    ys = [out.at[pos[:, k]].get(mode="promise_in_bounds") for k in range(TOP_K)]
    return ys, [gates[:, k:k + 1] for k in range(TOP_K)]


def kernel(x, c, ctx, c_ctx, w_mod, b_mod, ln_g, ln_b, pool_w, pool_scale, hy_w_in, hy_b_in, hy_conv_w, hy_conv_b, hy_f_w1, hy_f_b1, hy_f_w2, hy_f_b2, hy_f_w3, hy_f_b3, hy_f_freq, hy_f_wout, hy_skip, hy_w_out, hy_b_out, mla_w_in, mla_q_norm, mla_kv_norm, mla_w_uq, mla_w_ukv, mla_w_o, ffn_w_gate, ffn_w_up, ffn_w_down, moe_w_router, moe_w_gate, moe_w_up, moe_w_down):
    attn_layers = [i for i in range(DEPTH) if i % N_MIXERS == 2]
    last_read = attn_layers[-1] if attn_layers else -1

    c8 = jnp.concatenate([c, c_ctx[None], jnp.zeros((MOD_ROWS - BATCH - 1, D_MODEL), F32)], axis=0)
    mods = modulation_table(c8, w_mod, b_mod)

    h = jnp.concatenate([x.reshape(N_LAT, D_MODEL), ctx.reshape(N_CTX, D_MODEL)], axis=0)
    a = None
    for i in range(DEPTH):
        kind, j = i % N_MIXERS, i // N_MIXERS
        ctx_out = i < last_read
        n_rows = N_LAT + N_CTX if ctx_out else N_LAT
        m = mods[i]
        moe = i % 2 == 1
        if kind == 0:
            tl = CTX_LEN if n_rows > N_LAT else 512
            h1, a2 = pool_layer(h, n_rows, m, pool_w[j].astype(BF16), pool_scale[j],
                                ln_g[i, 0], ln_b[i, 0], tl=tl, pack=moe)
        else:
            if kind == 1:
                filt = (hy_f_w1[j], hy_f_b1[j], hy_f_w2[j], hy_f_b2[j], hy_f_w3[j], hy_f_b3[j],
                        hy_f_freq[j], hy_f_wout[j])
                y = hyena_mix(a[:n_rows], n_rows, j, hy_w_in, hy_b_in, hy_conv_w[j], hy_conv_b[j],
                              filt, hy_skip[j], hy_w_out, hy_b_out)
            else:
                assert i <= last_read and not ctx_out and a.shape[0] == N_LAT + N_CTX
                y = mla_mix(a, j, mla_w_in[j], mla_q_norm[j], mla_kv_norm[j], mla_w_uq[j],
                            mla_w_ukv[j], mla_w_o)
            h1, a2 = resid_ln(h, [y], n_rows, m, 2, ln_g[i, 0], ln_b[i, 0], m, 4, 3, pack=moe)
        f = i // 2
        nm = mods[min(i + 1, DEPTH - 1)]
        if moe:
            ys, row_gates = moe_ffn(h1, a2, n_rows, m, f, moe_w_router, moe_w_gate, moe_w_up,
                                    moe_w_down)
        else:
            ys, row_gates = [dense_ffn(a2, f, ffn_w_gate, ffn_w_up, ffn_w_down)], None
        h, a = resid_ln(h1, ys, n_rows, m, 5, ln_g[i, 1], ln_b[i, 1], nm, 1, 0,
                        row_gates=row_gates)
    return h[:N_LAT].reshape(BATCH, SEQ, D_MODEL)
```

```python
import functools
import math

import jax
import jax.numpy as jnp
from jax import lax
from jax.experimental import pallas as pl
from jax.experimental.pallas import tpu as pltpu

F32 = jnp.float32
BF16 = jnp.bfloat16
U32 = jnp.uint32

D_MODEL = 2048
BATCH = 4
SEQ = 2048
DEPTH = 4
GRID_W = 64
CTX_LEN = 256
N_MIXERS = 3
POOL_WINDOWS = (2, 4, 8, 16)
POOL_GROUP = D_MODEL // len(POOL_WINDOWS)
HY_ORDER = 2
HY_SHORT = 3
HY_EMB = 33
HY_FILTER = 64
HY_DECAY_TARGET = 1e-2
HY_FAST = 0.3
HY_SLOW = 1.5
MLA_HEADS = D_MODEL // 128
MLA_Q_RANK = 512
MLA_KV_RANK = 512
MLA_NOPE = 128
MLA_ROPE = 64
MLA_V = 128
ROPE_BASE = 10000.0
D_FF = 5632
N_EXPERTS = 8
TOP_K = 2
EXPERT_FF = 2 * D_MODEL
LN_EPS = 1e-5
RMS_EPS = 1e-6
DN_ALPHA = (2 * DEPTH) ** 0.25

N_LAT = BATCH * SEQ
N_CTX = BATCH * CTX_LEN
MOD_ROWS = 8
CTX_MOD_ROW = BATCH
LANE = 128
SUBLANE = 8
VMEM_CAP = 60 * 1024 * 1024
MLA_QK_PAD = 256
MLA_IN = MLA_Q_RANK + MLA_KV_RANK + MLA_ROPE
MLA_IN_PAD = MLA_IN + MLA_ROPE


def _vmem(nbytes):
    return int(min(VMEM_CAP, max(16 * 1024 * 1024, nbytes * 3 // 2)))


def _nbytes(shape, dtype):
    return math.prod(shape) * jnp.dtype(dtype).itemsize


def _mod_row(i, tm):
    return jnp.minimum((i * tm) // SEQ, CTX_MOD_ROW)


def _mod_spec(which, tm):
    return pl.BlockSpec((None, 1, D_MODEL), lambda i: (_mod_row(i, tm) * 6 + which, 0, 0))


def _params(sem, est):
    return pltpu.CompilerParams(dimension_semantics=("arbitrary",) * sem,
                                vmem_limit_bytes=_vmem(est))


def _mod_kernel(c_ref, w_ref, b_ref, o_ref):
    c = c_ref[...]
    a = c / (1.0 + jnp.exp(-c))
    o_ref[...] = jnp.dot(a.astype(BF16), w_ref[...].astype(BF16),
                         preferred_element_type=F32) + b_ref[...]


def modulation_table(c8, w_mod, b_mod):
    tn = 1024
    n = 6 * D_MODEL
    out = pl.pallas_call(
        _mod_kernel,
        grid=(DEPTH, n // tn),
        in_specs=[pl.BlockSpec((MOD_ROWS, D_MODEL), lambda l, j: (0, 0)),
                  pl.BlockSpec((None, D_MODEL, tn), lambda l, j: (l, 0, j)),
                  pl.BlockSpec((None, 1, tn), lambda l, j: (l, 0, j))],
        out_specs=pl.BlockSpec((None, MOD_ROWS, tn), lambda l, j: (l, 0, j)),
        out_shape=jax.ShapeDtypeStruct((DEPTH, MOD_ROWS, n), F32),
        compiler_params=_params(2, 3 * _nbytes((D_MODEL, tn), F32)),
        name="modulation_table",
    )(c8, w_mod, b_mod.reshape(DEPTH, 1, n))
    return out.reshape(DEPTH, MOD_ROWS * 6, 1, D_MODEL)


def _pack_halves(a):
    half = a.shape[1] // 2
    hi = lax.bitcast_convert_type(a[:, :half].astype(BF16).astype(F32), U32)
    lo = lax.bitcast_convert_type(a[:, half:].astype(BF16).astype(F32), U32)
    return lax.bitcast_convert_type(hi | (lo >> 16), F32)


def _unpack_halves(p):
    p = lax.bitcast_convert_type(p, U32)
    hi = lax.bitcast_convert_type(p & jnp.uint32(0xFFFF0000), F32).astype(BF16)
    lo = lax.bitcast_convert_type(p << 16, F32).astype(BF16)
    return hi, lo


def _linear_kernel(xt_ref, te_ref, nv_ref, x_ref, w_ref, *rest, has_bias, cast_w):
    rest = list(rest)
    b_ref = rest.pop(0) if has_bias else None
    o_ref = rest.pop(0)
    wbf_ref = rest.pop(0) if cast_w else w_ref
    i = pl.program_id(1)

    if cast_w:
        prev = te_ref[jnp.maximum(i - 1, 0)]

        @pl.when(jnp.logical_or(i == 0, te_ref[i] != prev))
        def _():
            wbf_ref[...] = w_ref[...].astype(BF16)

    @pl.when(i < nv_ref[0])
    def _():
        acc = jnp.dot(x_ref[...].astype(BF16), wbf_ref[...], preferred_element_type=F32)
        if has_bias:
            acc = acc + b_ref[...]
        o_ref[...] = acc.astype(o_ref.dtype)

    @pl.when(i >= nv_ref[0])
    def _():
        o_ref[...] = jnp.zeros_like(o_ref)


def linear(x, w, *, tm, tn, e=0, n_out=None, n_tiles=None, xt=None, te=None, nv=None, bias=None,
           out_dtype=F32, w_col0=0, name="linear"):
    k = x.shape[1]
    if w.ndim == 2:
        w = w[None]
    if n_out is None:
        n_out = w.shape[2] - w_col0
    assert x.shape[0] % tm == 0 and n_out % tn == 0 and w_col0 % tn == 0 and w.shape[1] == k
    if n_tiles is None:
        n_tiles = x.shape[0] // tm
    if xt is None:
        xt = jnp.arange(n_tiles, dtype=jnp.int32)
    if te is None:
        te = jnp.full((n_tiles,), e, jnp.int32)
    if nv is None:
        nv = jnp.full((1,), n_tiles, jnp.int32)
    cast_w = w.dtype != BF16
    has_bias = bias is not None
    cb = w_col0 // tn
    in_specs = [pl.BlockSpec((tm, k), lambda j, i, xt, te, nv: (xt[i], 0)),
                pl.BlockSpec((None, k, tn), lambda j, i, xt, te, nv: (te[i], 0, j + cb))]
    args = [x, w]
    if has_bias:
        bias = bias.reshape(-1, 1, bias.shape[-1])
        in_specs.append(pl.BlockSpec((None, 1, tn), lambda j, i, xt, te, nv: (te[i], 0, j)))
        args.append(bias)
    scratch = [pltpu.VMEM((k, tn), BF16)] if cast_w else []
    est = (2 * _nbytes((tm, k), x.dtype) + 2 * _nbytes((k, tn), w.dtype)
           + (_nbytes((k, tn), BF16) if cast_w else 0) + _nbytes((tm, k), BF16)
           + 2 * _nbytes((tm, tn), out_dtype) + 2 * _nbytes((tm, tn), F32))
    return pl.pallas_call(
        functools.partial(_linear_kernel, has_bias=has_bias, cast_w=cast_w),
        grid_spec=pltpu.PrefetchScalarGridSpec(
            num_scalar_prefetch=3,
            grid=(n_out // tn, n_tiles),
            in_specs=in_specs,
            out_specs=pl.BlockSpec((tm, tn), lambda j, i, xt, te, nv: (i, j)),
            scratch_shapes=scratch),
        out_shape=jax.ShapeDtypeStruct((n_tiles * tm, n_out), out_dtype),
        compiler_params=_params(2, est),
        name=name,
    )(xt, te, nv, *args)


def _swiglu_kernel(te_ref, nv_ref, x_ref, wg_ref, wu_ref, o_ref, wgb_ref, wub_ref, *, packed):
    i = pl.program_id(1)
    prev = te_ref[jnp.maximum(i - 1, 0)]

    @pl.when(jnp.logical_or(i == 0, te_ref[i] != prev))
    def _():
        wgb_ref[...] = wg_ref[...].astype(BF16)
        wub_ref[...] = wu_ref[...].astype(BF16)

    @pl.when(i < nv_ref[0])
    def _():
        if packed:
            hi, lo = _unpack_halves(x_ref[...])
            half = hi.shape[1]
            g = (jnp.dot(hi, wgb_ref[0:half, :], preferred_element_type=F32)
                 + jnp.dot(lo, wgb_ref[half:2 * half, :], preferred_element_type=F32))
            u = (jnp.dot(hi, wub_ref[0:half, :], preferred_element_type=F32)
                 + jnp.dot(lo, wub_ref[half:2 * half, :], preferred_element_type=F32))
        else:
            x = x_ref[...]
            g = jnp.dot(x, wgb_ref[...], preferred_element_type=F32)
            u = jnp.dot(x, wub_ref[...], preferred_element_type=F32)
        o_ref[...] = ((g / (1.0 + jnp.exp(-g))) * u).astype(o_ref.dtype)

    @pl.when(i >= nv_ref[0])
    def _():
        o_ref[...] = jnp.zeros_like(o_ref)


def swiglu_up(x, wg, wu, *, tm, tf, e=0, te=None, nv=None, name="swiglu_up"):
    m = x.shape[0]
    k, f = wg.shape[1], wg.shape[2]
    packed = x.dtype == F32
    assert x.shape[1] == (k // 2 if packed else k)
    assert m % tm == 0 and f % tf == 0
    n_tiles = m // tm
    if te is None:
        te = jnp.full((n_tiles,), e, jnp.int32)
    if nv is None:
        nv = jnp.full((1,), n_tiles, jnp.int32)
    wspec = pl.BlockSpec((None, k, tf), lambda j, i, te, nv: (te[i], 0, j))
    est = (3 * _nbytes((tm, k), BF16) + 4 * _nbytes((k, tf), F32) + 2 * _nbytes((k, tf), BF16)
           + 2 * _nbytes((tm, tf), BF16) + 3 * _nbytes((tm, tf), F32))
    return pl.pallas_call(
        functools.partial(_swiglu_kernel, packed=packed),
        grid_spec=pltpu.PrefetchScalarGridSpec(
            num_scalar_prefetch=2,
            grid=(f // tf, n_tiles),
            in_specs=[pl.BlockSpec((tm, x.shape[1]), lambda j, i, te, nv: (i, 0)), wspec, wspec],
            out_specs=pl.BlockSpec((tm, tf), lambda j, i, te, nv: (i, j)),
            scratch_shapes=[pltpu.VMEM((k, tf), BF16), pltpu.VMEM((k, tf), BF16)]),
        out_shape=jax.ShapeDtypeStruct((m, f), BF16),
        compiler_params=_params(2, est),
        name=name,
    )(te, nv, x, wg, wu)


def _ln_mod(v, g, b, sc, sh, pack):
    mu = jnp.mean(v, axis=-1, keepdims=True)
    d = v - mu
    var = jnp.mean(d * d, axis=-1, keepdims=True)
    h = d * lax.rsqrt(var + LN_EPS) * g + b
    a = h * (1.0 + sc) + sh
    return h, (_pack_halves(a) if pack else a.astype(BF16))


def _resid_ln_kernel(*refs, n_y, pack):
    h_ref = refs[0]
    y_refs = refs[1:1 + n_y]
    rest = refs[1 + n_y:]
    if n_y == 2:
        g0_ref, g1_ref = rest[0], rest[1]
        rest = rest[2:]
        y = g0_ref[...] * y_refs[0][...] + g1_ref[...] * y_refs[1][...]
    else:
        y = y_refs[0][...]
    gate_ref, g_ref, b_ref, sc_ref, sh_ref, o_ref, a_ref = rest
    v = DN_ALPHA * h_ref[...] + gate_ref[...] * y
    h, a = _ln_mod(v, g_ref[...], b_ref[...], sc_ref[...], sh_ref[...], pack)
    o_ref[...] = h
    a_ref[...] = a


def resid_ln(h, ys, n_rows, mods, gate_idx, ln_g, ln_b, nmods, sc_idx, sh_idx, *, row_gates=None,
             pack=False, tm=256):
    assert n_rows % tm == 0
    row = pl.BlockSpec((tm, D_MODEL), lambda i: (i, 0))
    col = pl.BlockSpec((tm, 1), lambda i: (i, 0))
    vec = pl.BlockSpec((1, D_MODEL), lambda i: (0, 0))
    a_w = D_MODEL // 2 if pack else D_MODEL
    a_dt = F32 if pack else BF16
    est = 10 * _nbytes((tm, D_MODEL), F32)
    in_specs = [row] + [row] * len(ys) + ([col, col] if row_gates else []) + [
        _mod_spec(gate_idx, tm), vec, vec, _mod_spec(sc_idx, tm), _mod_spec(sh_idx, tm)]
    args = [h, *ys, *(row_gates or []), mods, ln_g.reshape(1, D_MODEL), ln_b.reshape(1, D_MODEL),
            nmods, nmods]
    return pl.pallas_call(
        functools.partial(_resid_ln_kernel, n_y=len(ys), pack=pack),
        grid=(n_rows // tm,),
        in_specs=in_specs,
        out_specs=[row, pl.BlockSpec((tm, a_w), lambda i: (i, 0))],
        out_shape=[jax.ShapeDtypeStruct((n_rows, D_MODEL), F32),
                   jax.ShapeDtypeStruct((n_rows, a_w), a_dt)],
        compiler_params=_params(1, est),
        name="resid_ln",
    )(*args)


def _seq_pos(i, tl):
    row0 = i * tl
    is_lat = row0 < N_LAT
    pos = jnp.where(is_lat, row0 % SEQ, (row0 - N_LAT) % CTX_LEN)
    seqlen = jnp.where(is_lat, SEQ, CTX_LEN)
    return pos, seqlen


def _halo_specs(tl, width, n_rows, col_index=None):
    per = tl // SUBLANE
    last = n_rows // SUBLANE - 1
    if col_index is None:
        prev = pl.BlockSpec((SUBLANE, width), lambda i: (jnp.maximum(i * per - 1, 0), 0))
        nxt = pl.BlockSpec((SUBLANE, width), lambda i: (jnp.minimum((i + 1) * per, last), 0))
    else:
        prev = pl.BlockSpec((SUBLANE, width),
                            lambda i, j: (jnp.maximum(i * per - 1, 0), col_index(j)))
        nxt = pl.BlockSpec((SUBLANE, width),
                           lambda i, j: (jnp.minimum((i + 1) * per, last), col_index(j)))
    return prev, nxt


def _shift_rows(x, s):
    n = x.shape[0]
    return pltpu.roll(x, s % n, 0)


def _pool_kernel(h_ref, hp_ref, hn_ref, sc_ref, sh_ref, gate_ref, w_ref, ps_ref, g_ref, b_ref,
                 nsc_ref, nsh_ref, o_ref, a_ref, *, tl, pack):
    i = pl.program_id(0)
    pos, seqlen = _seq_pos(i, tl)
    sc = 1.0 + sc_ref[...]
    sh = sh_ref[...]
    h = h_ref[...]
    a = h * sc + sh
    keep_prev = (pos > 0).astype(F32)
    keep_next = (pos + tl < seqlen).astype(F32)
    ext = jnp.concatenate([(hp_ref[...] * sc + sh) * keep_prev, a,
                           (hn_ref[...] * sc + sh) * keep_next], axis=0)
    t = pos + lax.broadcasted_iota(jnp.int32, (tl, 1), 0)
    n_ext = tl + 2 * SUBLANE
    parts = []
    for g, w in enumerate(POOL_WINDOWS):
        cols = slice(g * POOL_GROUP, (g + 1) * POOL_GROUP)
        s = ext[:, cols]
        s = s + _shift_rows(s, 1)
        r = 1
        while 2 * r < w:
            s = _shift_rows(s, n_ext - r) + _shift_rows(s, r)
            r *= 2
        s = s[SUBLANE:SUBLANE + tl]
        cnt = (jnp.minimum(t + w // 2, seqlen) - jnp.maximum(t - w // 2, 0)).astype(F32)
        d = s / cnt - a[:, cols]
        parts.append(jnp.dot(d.astype(BF16), w_ref[g], preferred_element_type=F32))
    y = jnp.concatenate(parts, axis=1) * ps_ref[...]
    v = DN_ALPHA * h + gate_ref[...] * y
    hn, an = _ln_mod(v, g_ref[...], b_ref[...], nsc_ref[...], nsh_ref[...], pack)
    o_ref[...] = hn
    a_ref[...] = an


def pool_layer(h, n_rows, mods, w_grp_bf16, pool_scale, ln_g, ln_b, *, tl, pack):
    row = pl.BlockSpec((tl, D_MODEL), lambda i: (i, 0))
    vec = pl.BlockSpec((1, D_MODEL), lambda i: (0, 0))
    prev, nxt = _halo_specs(tl, D_MODEL, n_rows)
    a_w = D_MODEL // 2 if pack else D_MODEL
    est = 12 * _nbytes((tl, D_MODEL), F32) + 2 * _nbytes(w_grp_bf16.shape, BF16)
    return pl.pallas_call(
        functools.partial(_pool_kernel, tl=tl, pack=pack),
        grid=(n_rows // tl,),
        in_specs=[row, prev, nxt, _mod_spec(1, tl), _mod_spec(0, tl), _mod_spec(2, tl),
                  pl.BlockSpec(w_grp_bf16.shape, lambda i: (0, 0, 0)), vec, vec, vec,
                  _mod_spec(4, tl), _mod_spec(3, tl)],
        out_specs=[row, pl.BlockSpec((tl, a_w), lambda i: (i, 0))],
        out_shape=[jax.ShapeDtypeStruct((n_rows, D_MODEL), F32),
                   jax.ShapeDtypeStruct((n_rows, a_w), F32 if pack else BF16)],
        compiler_params=_params(1, est),
        name="pool_layer",
    )(h, h, h, mods, mods, mods, w_grp_bf16, pool_scale.reshape(1, D_MODEL),
      ln_g.reshape(1, D_MODEL), ln_b.reshape(1, D_MODEL), mods, mods)


def _short_conv_kernel(u_ref, up_ref, un_ref, w_ref, b_ref, o_ref, *, tl):
    i = pl.program_id(0)
    pos, seqlen = _seq_pos(i, tl)
    keep_prev = (pos > 0).astype(F32)
    keep_next = (pos + tl < seqlen).astype(F32)
    ext = jnp.concatenate([up_ref[...] * keep_prev, u_ref[...], un_ref[...] * keep_next], axis=0)
    n_ext = tl + 2 * SUBLANE
    before = _shift_rows(ext, 1)[SUBLANE:SUBLANE + tl]
    after = _shift_rows(ext, n_ext - 1)[SUBLANE:SUBLANE + tl]
    w = w_ref[...]
    y = b_ref[...] + before * w[0:1]
    y = y + u_ref[...] * w[1:2]
    y = y + after * w[2:3]
    o_ref[...] = y


def short_conv(u, conv_w, conv_b, *, tl=256, tc=2048):
    n_rows, c = u.shape
    blk = pl.BlockSpec((tl, tc), lambda i, j: (i, j))
    prev, nxt = _halo_specs(tl, tc, n_rows, col_index=lambda j: j)
    return pl.pallas_call(
        functools.partial(_short_conv_kernel, tl=tl),
        grid=(n_rows // tl, c // tc),
        in_specs=[blk, prev, nxt,
                  pl.BlockSpec((HY_SHORT, tc), lambda i, j: (0, j)),
                  pl.BlockSpec((1, tc), lambda i, j: (0, j))],
        out_specs=blk,
        out_shape=jax.ShapeDtypeStruct((n_rows, c), F32),
        compiler_params=_params(2, 8 * _nbytes((tl, tc), F32)),
        name="short_conv",
    )(u, u, u, conv_w, conv_b.reshape(1, c))


def _split_bf16(x):
    hi = x.astype(BF16)
    lo = (x - hi.astype(F32)).astype(BF16)
    return hi, lo


def _dot3(a, b):
    a_hi, a_lo = _split_bf16(a)
    b_hi, b_lo = _split_bf16(b)
    return (jnp.dot(a_hi, b_hi, preferred_element_type=F32)
            + jnp.dot(a_lo, b_hi, preferred_element_type=F32)
            + jnp.dot(a_hi, b_lo, preferred_element_type=F32))


def _filter_kernel(z_ref, w1_ref, b1_ref, w2_ref, b2_ref, w3_ref, b3_ref, fr_ref, wo_ref,
                   dist_ref, delta_ref, o_ref, a_ref):
    @pl.when(pl.program_id(0) == 0)
    def _():
        fr = fr_ref[...]
        a = jnp.sin(fr * (_dot3(z_ref[...], w1_ref[...]) + b1_ref[...]))
        a = jnp.sin(fr * (_dot3(a, w2_ref[...]) + b2_ref[...]))
        a_ref[...] = jnp.sin(fr * (_dot3(a, w3_ref[...]) + b3_ref[...]))

    k = _dot3(a_ref[...], wo_ref[...])
    k = k * jnp.exp(-dist_ref[...] * delta_ref[...])
    o_ref[...] = k * lax.rsqrt(jnp.sum(k * k, axis=0, keepdims=True) + 1e-6)


def _pad2(x, rows, cols):
    return jnp.pad(x, ((0, rows - x.shape[0]), (0, cols - x.shape[1])))


def hyena_filters(L, f_w1, f_b1, f_w2, f_b2, f_w3, f_b3, f_freq, f_wout, *, tn=512):
    bands = (HY_EMB - 1) // 2
    t = jnp.linspace(0.0, 1.0, L, dtype=F32)[:, None]
    wpos = 2.0 * math.pi * jnp.arange(L, dtype=F32)[:, None] / L
    f = jnp.linspace(1e-4, bands - 1, bands, dtype=F32)[None, :]
    z = jnp.concatenate([t, jnp.cos(f * wpos), -jnp.sin(f * wpos)], axis=-1)
    dist = (jnp.abs(jnp.arange(L) - L // 2).astype(F32) / max(L // 2, 1))[:, None]
    deltas = jnp.linspace(math.log(HY_DECAY_TARGET) / HY_SLOW, math.log(HY_DECAY_TARGET) / HY_FAST,
                          D_MODEL, dtype=F32)
    absdelta = jnp.tile(jnp.abs(deltas), HY_ORDER)[None, :]
    P = LANE
    n = HY_ORDER * D_MODEL
    full = lambda shape: pl.BlockSpec(shape, lambda j: (0, 0))
    args = [_pad2(z, L, P), _pad2(f_w1, P, P), _pad2(f_b1[None], 1, P), _pad2(f_w2, P, P),
            _pad2(f_b2[None], 1, P), _pad2(f_w3, P, P), _pad2(f_b3[None], 1, P),
            _pad2(f_freq[None], 1, P), _pad2(f_wout, P, n), dist, absdelta]
    in_specs = [full((L, P)), full((P, P)), full((1, P)), full((P, P)), full((1, P)), full((P, P)),
                full((1, P)), full((1, P)), pl.BlockSpec((P, tn), lambda j: (0, j)),
                full((L, 1)), pl.BlockSpec((1, tn), lambda j: (0, j))]
    return pl.pallas_call(
        _filter_kernel,
        grid=(n // tn,),
        in_specs=in_specs,
        out_specs=pl.BlockSpec((L, tn), lambda j: (0, j)),
        out_shape=jax.ShapeDtypeStruct((L, n), F32),
        scratch_shapes=[pltpu.VMEM((L, P), F32)],
        compiler_params=_params(1, 8 * _nbytes((L, tn), F32)),
        name="hyena_filters",
    )(*args)


def dft_tables(L, tm):
    n_fft = 2 * L
    half = tm // 2
    r_idx = jnp.arange(2 * L, dtype=jnp.int32)
    odd = 2 * ((r_idx // tm) * half + (r_idx % half)) + 1
    is_sin = ((r_idx % tm) >= half)[:, None]

    def trig(k):
        ang = (k % (2 * n_fft)).astype(F32) * (math.pi / n_fft)
        return jnp.cos(ang), jnp.sin(ang)

    lo = 64
    s1 = jnp.arange(L // lo, dtype=jnp.int32) * lo
    s0 = jnp.arange(lo, dtype=jnp.int32)
    cb, sb = trig(odd[:, None] * s0[None, :])

    def combine(ca, sa):
        a1 = jnp.where(is_sin, sa, ca)
        a2 = jnp.where(is_sin, ca, -sa)
        return a1, a2

    a1, a2 = combine(*trig(odd[:, None] * s1[None, :]))
    fwd = (a1[:, :, None] * cb[:, None, :] + a2[:, :, None] * sb[:, None, :]).reshape(2 * L, L)
    a1, a2 = combine(*trig(odd[:, None] * (s1[None, :] + L // 2)))
    inv = (a1.T[:, None, :] * cb.T[None, :, :] + a2.T[:, None, :] * sb.T[None, :, :]).reshape(
        L, 2 * L)
    return fwd.astype(BF16), inv.astype(BF16)


def _dft_mul_kernel(x_ref, w_ref, k_ref, o_ref, wb_ref, *, inv_len):
    @pl.when(pl.program_id(1) == 0)
    def _():
        wb_ref[...] = w_ref[...].astype(BF16)

    s = jnp.dot(x_ref[...], wb_ref[...], preferred_element_type=F32)
    half = s.shape[0] // 2
    a, b = s[:half], s[half:]
    ka, kb = k_ref[0:half, :], k_ref[half:2 * half, :]
    o_ref[0:half, :] = ((a * ka - b * kb) * inv_len).astype(BF16)
    o_ref[half:2 * half, :] = ((a * kb + b * ka) * inv_len).astype(BF16)


def dft_mul(fwd, v_view, kspec, k_col0, *, L, nb, tm, tn):
    d = D_MODEL
    per_b = d // tn
    v, v_row0, v_col0 = v_view
    vr, vc, kc = v_row0 // L, v_col0 // tn, k_col0 // tn
    est = (2 * _nbytes((tm, L), BF16) + 2 * _nbytes((L, tn), F32) + _nbytes((L, tn), BF16)
           + 2 * _nbytes((tm, tn), F32) + 2 * _nbytes((tm, tn), BF16) + 3 * _nbytes((tm, tn), F32))
    return pl.pallas_call(
        functools.partial(_dft_mul_kernel, inv_len=1.0 / L),
        grid=(nb * per_b, (2 * L) // tm),
        in_specs=[pl.BlockSpec((tm, L), lambda j, i: (i, 0)),
                  pl.BlockSpec((L, tn), lambda j, i: (j // per_b + vr, j % per_b + vc)),
                  pl.BlockSpec((tm, tn), lambda j, i: (i, j % per_b + kc))],
        out_specs=pl.BlockSpec((tm, tn), lambda j, i: (i, j)),
        out_shape=jax.ShapeDtypeStruct((2 * L, nb * d), BF16),
        scratch_shapes=[pltpu.VMEM((L, tn), BF16)],
        compiler_params=_params(2, est),
        name="dft_mul",
    )(fwd, v, kspec)


def _idft_kernel(x_ref, y_ref, gate_ref, z_ref, skip_ref, o_ref):
    conv = jnp.dot(x_ref[...], y_ref[...], preferred_element_type=F32)
    o_ref[...] = (gate_ref[...] * (conv + z_ref[...] * skip_ref[...])).astype(o_ref.dtype)


def idft_gate(inv, y, gate_view, z_view, skip, *, L, nb, tm, tn, out_dtype):
    d = D_MODEL
    per_b = d // tn
    per = L // tm
    (gate, g_row0, g_col0), (z, z_row0, z_col0) = gate_view, z_view
    gr, gc, zr, zc = g_row0 // tm, g_col0 // tn, z_row0 // tm, z_col0 // tn
    est = (2 * _nbytes((tm, 2 * L), BF16) + 2 * _nbytes((2 * L, tn), BF16)
           + 8 * _nbytes((tm, tn), F32))
    return pl.pallas_call(
        _idft_kernel,
        grid=(nb * per_b, per),
        in_specs=[pl.BlockSpec((tm, 2 * L), lambda j, i: (i, 0)),
                  pl.BlockSpec((2 * L, tn), lambda j, i: (0, j)),
                  pl.BlockSpec((tm, tn),
                               lambda j, i: ((j // per_b) * per + i + gr, j % per_b + gc)),
                  pl.BlockSpec((tm, tn),
                               lambda j, i: ((j // per_b) * per + i + zr, j % per_b + zc)),
                  pl.BlockSpec((1, tn), lambda j, i: (0, j % per_b))],
        out_specs=pl.BlockSpec((tm, tn), lambda j, i: ((j // per_b) * per + i, j % per_b)),
        out_shape=jax.ShapeDtypeStruct((nb * L, d), out_dtype),
        compiler_params=_params(2, est),
        name="idft_gate",
    )(inv, y, gate, z, skip.reshape(1, d))


def hyena_stream(u, row0, L, nb, filt, skip):
    d = D_MODEL
    tms = min(1024, 2 * L)
    tmi = min(512, L)
    k = hyena_filters(L, *filt)
    fwd, inv = dft_tables(L, tms)
    kspec = linear(fwd, k, tm=tms, tn=1024, name="dft_filters")
    y1 = dft_mul(fwd, (u, row0, 0), kspec, 0, L=L, nb=nb, tm=tms, tn=1024)
    z1 = idft_gate(inv, y1, (u, row0, d), (u, row0, 0), skip[0], L=L, nb=nb, tm=tmi, tn=1024,
                   out_dtype=F32)
    y2 = dft_mul(fwd, (z1, 0, 0), kspec, d, L=L, nb=nb, tm=tms, tn=1024)
    return idft_gate(inv, y2, (u, row0, 2 * d), (z1, 0, 0), skip[1], L=L, nb=nb, tm=tmi, tn=1024,
                     out_dtype=BF16)


def hyena_mix(a, n_rows, j, w_in, b_in, conv_w, conv_b, filt, skip, w_out, b_out):
    u = linear(a, w_in, e=j, tm=1024, tn=1024, bias=b_in, name="hyena_in")
    u = short_conv(u, conv_w, conv_b)
    z = hyena_stream(u, 0, SEQ, BATCH, filt, skip)
    if n_rows > N_LAT:
        z_ctx = hyena_stream(u, N_LAT, CTX_LEN, BATCH, filt, skip)
        z = jnp.concatenate([z, z_ctx], axis=0)
    return linear(z, w_out, e=j, tm=512, tn=1024, bias=b_out, name="hyena_out")


def _rms(x, g):
    return x * lax.rsqrt(jnp.mean(x * x, axis=-1, keepdims=True) + RMS_EPS) * g


def _mla_q_kernel(p_ref, g_ref, w1_ref, w2_ref, c_ref, s_ref, o_ref, w1b_ref, w2b_ref):
    @pl.when(pl.program_id(1) == 0)
    def _():
        w1b_ref[...] = w1_ref[...].astype(BF16)
        w2b_ref[...] = w2_ref[...].astype(BF16)

    xn = _rms(p_ref[...], g_ref[...]).astype(BF16)
    q1 = jnp.dot(xn, w1b_ref[...], preferred_element_type=F32)
    q2 = jnp.dot(xn, w2b_ref[...], preferred_element_type=F32)
    reps = q1.shape[1] // MLA_QK_PAD
    c = jnp.concatenate([c_ref[...]] * reps, axis=1)
    s = jnp.concatenate([s_ref[...]] * reps, axis=1)
    o_ref[...] = (q1 * c + q2 * s).astype(BF16)


def _mla_kv_kernel(p_ref, r_ref, g_ref, cs_ref, w_ref, o_ref, wb_ref):
    @pl.when(pl.program_id(1) == 0)
    def _():
        wb_ref[...] = w_ref[...].astype(BF16)

    xn = _rms(p_ref[...], g_ref[...]).astype(BF16)
    rr = (r_ref[...] * cs_ref[...]).astype(BF16)
    acc = (jnp.dot(xn, wb_ref[0:MLA_KV_RANK, :], preferred_element_type=F32)
           + jnp.dot(rr, wb_ref[MLA_KV_RANK:MLA_KV_RANK + 2 * MLA_ROPE, :],
                     preferred_element_type=F32))
    o_ref[...] = acc.astype(BF16)


def _attn_kernel(q_ref, kl_ref, kc_ref, vl_ref, vc_ref, o_ref, *, kchunk):
    c = (MLA_NOPE + MLA_ROPE) ** -0.5 * math.log2(math.e)
    nt = (((1,), (1,)), ((), ()))
    q = q_ref[...]
    chunks = [(kl_ref, vl_ref, r) for r in range(kl_ref.shape[0] // kchunk)]
    chunks += [(kc_ref, vc_ref, r) for r in range(kc_ref.shape[0] // kchunk)]
    m = l = acc = None
    for k_ref, v_ref, r in chunks:
        rows = slice(r * kchunk, (r + 1) * kchunk)
        s = lax.dot_general(q, k_ref[rows, :], nt, preferred_element_type=F32)
        ms = jnp.max(s, axis=-1, keepdims=True)
        m_new = ms if m is None else jnp.maximum(m, ms)
        p = jnp.exp2((s - m_new) * c)
        ps = jnp.sum(p, axis=-1, keepdims=True)
        pv = jnp.dot(p.astype(BF16), v_ref[rows, :], preferred_element_type=F32)
        if m is None:
            l, acc = ps, pv
        else:
            alpha = jnp.exp2((m - m_new) * c)
            l = alpha * l + ps
            acc = alpha * acc + pv
        m = m_new
    o_ref[...] = (acc / l).astype(o_ref.dtype)


def _rope_tables():
    rows = SEQ // GRID_W
    row = jnp.repeat(jnp.arange(rows), GRID_W).astype(F32)
    col = jnp.tile(jnp.arange(GRID_W), rows).astype(F32)
    n = MLA_ROPE // 4
    inv = ROPE_BASE ** (-jnp.arange(n, dtype=F32) / n)
    ang_row, ang_col = row[:, None] * inv, col[:, None] * inv
    cos = jnp.concatenate([jnp.cos(ang_row)] * 2 + [jnp.cos(ang_col)] * 2, axis=1)
    sin = jnp.concatenate([jnp.sin(ang_row)] * 2 + [jnp.sin(ang_col)] * 2, axis=1)
    return cos, sin


def _rope_partner(w):
    n = MLA_ROPE // 4
    return jnp.concatenate([-w[..., n:2 * n], w[..., 0:n], -w[..., 3 * n:4 * n],
                            w[..., 2 * n:3 * n]], axis=-1)


def mla_mix(a, j, w_in, q_norm, kv_norm, w_uq, w_ukv, w_o):
    H, NP, R, V, QK = MLA_HEADS, MLA_NOPE, MLA_ROPE, MLA_V, MLA_QK_PAD
    n_all = N_LAT + N_CTX
    cos, sin = _rope_tables()
    w_in_x = jnp.concatenate([w_in, _rope_partner(w_in[:, MLA_IN - R:])], axis=1)
    proj = linear(a, w_in_x, tm=512, tn=MLA_IN_PAD, name="mla_in")
    wq = w_uq.reshape(MLA_Q_RANK, H, NP + R)
    zq = jnp.zeros((MLA_Q_RANK, H, QK - NP - R), F32)
    wq1 = jnp.concatenate([wq, zq], axis=2).reshape(MLA_Q_RANK, H * QK)
    wq2 = jnp.concatenate([jnp.zeros((MLA_Q_RANK, H, NP), F32), _rope_partner(wq[..., NP:]), zq],
                          axis=2).reshape(MLA_Q_RANK, H * QK)
    ones, zeros = jnp.ones((SEQ, NP), F32), jnp.zeros((SEQ, QK - NP - R), F32)
    cq = jnp.concatenate([ones, cos, zeros], axis=1)
    sq = jnp.concatenate([0.0 * ones, sin, zeros], axis=1)
    tm, tn = 512, 1024
    per = SEQ // tm
    q = pl.pallas_call(
        _mla_q_kernel,
        grid=(H * QK // tn, N_LAT // tm),
        in_specs=[pl.BlockSpec((tm, MLA_Q_RANK), lambda jn, i: (i, 0)),
                  pl.BlockSpec((1, MLA_Q_RANK), lambda jn, i: (0, 0)),
                  pl.BlockSpec((MLA_Q_RANK, tn), lambda jn, i: (0, jn)),
                  pl.BlockSpec((MLA_Q_RANK, tn), lambda jn, i: (0, jn)),
                  pl.BlockSpec((tm, QK), lambda jn, i: (i % per, 0)),
                  pl.BlockSpec((tm, QK), lambda jn, i: (i % per, 0))],
        out_specs=pl.BlockSpec((tm, tn), lambda jn, i: (i, jn)),
        out_shape=jax.ShapeDtypeStruct((N_LAT, H * QK), BF16),
        scratch_shapes=[pltpu.VMEM((MLA_Q_RANK, tn), BF16), pltpu.VMEM((MLA_Q_RANK, tn), BF16)],
        compiler_params=_params(2, 6 * _nbytes((MLA_Q_RANK, tn), F32) + 8 * _nbytes((tm, tn), F32)),
        name="mla_q",
    )(proj, q_norm.reshape(1, MLA_Q_RANK), wq1, wq2, cq, sq)
    wkv = w_ukv.reshape(MLA_KV_RANK, H, NP + V)
    wk = jnp.concatenate([wkv[..., :NP], jnp.zeros((MLA_KV_RANK, H, QK - NP), F32)],
                         axis=2).reshape(MLA_KV_RANK, H * QK)
    place = jnp.concatenate([jnp.zeros((R, NP), F32), jnp.eye(R, dtype=F32),
                             jnp.zeros((R, QK - NP - R), F32)], axis=1)
    place = jnp.tile(jnp.concatenate([place, place], axis=0), (1, H))
    w_kv = jnp.concatenate([
        jnp.concatenate([wk, wkv[..., NP:].reshape(MLA_KV_RANK, H * V)], axis=1),
        jnp.concatenate([place, jnp.zeros((2 * R, H * V), F32)], axis=1)], axis=0)
    cs = jnp.concatenate([
        jnp.concatenate([cos, sin], axis=1),
        jnp.concatenate([jnp.ones((CTX_LEN, R), F32), jnp.zeros((CTX_LEN, R), F32)], axis=1)],
        axis=0)
    tm, tn = CTX_LEN, 2048
    per = SEQ // tm
    n_kv = H * (QK + V)
    kv = pl.pallas_call(
        _mla_kv_kernel,
        grid=(n_kv // tn, n_all // tm),
        in_specs=[pl.BlockSpec((tm, MLA_KV_RANK), lambda jn, i: (i, MLA_Q_RANK // MLA_KV_RANK)),
                  pl.BlockSpec((tm, 2 * R), lambda jn, i: (i, (MLA_IN - R) // (2 * R))),
                  pl.BlockSpec((1, MLA_KV_RANK), lambda jn, i: (0, 0)),
                  pl.BlockSpec((tm, 2 * R),
                               lambda jn, i: (jnp.where(i * tm < N_LAT, i % per, per), 0)),
                  pl.BlockSpec((MLA_KV_RANK + 2 * R, tn), lambda jn, i: (0, jn))],
        out_specs=pl.BlockSpec((tm, tn), lambda jn, i: (i, jn)),
        out_shape=jax.ShapeDtypeStruct((n_all, n_kv), BF16),
        scratch_shapes=[pltpu.VMEM((MLA_KV_RANK + 2 * R, tn), BF16)],
        compiler_params=_params(2, 4 * _nbytes((MLA_KV_RANK + 2 * R, tn), F32)
                                + 8 * _nbytes((tm, tn), F32)),
        name="mla_kv",
    )(proj, proj, kv_norm.reshape(1, MLA_KV_RANK), cs, w_kv)
    tq = 512
    per = SEQ // tq
    v0 = H * QK // V
    ctx0 = N_LAT // CTX_LEN
    est = (2 * _nbytes((tq, QK), BF16) + 2 * _nbytes((SEQ + CTX_LEN, QK + V), BF16)
           + 6 * _nbytes((tq // 4, SEQ + CTX_LEN), F32))
    att = pl.pallas_call(
        functools.partial(_attn_kernel, kchunk=CTX_LEN),
        grid=(BATCH, H, per),
        in_specs=[pl.BlockSpec((tq, QK), lambda b, h, i: (b * per + i, h)),
                  pl.BlockSpec((SEQ, QK), lambda b, h, i: (b, h)),
                  pl.BlockSpec((CTX_LEN, QK), lambda b, h, i: (ctx0 + b, h)),
                  pl.BlockSpec((SEQ, V), lambda b, h, i: (b, v0 + h)),
                  pl.BlockSpec((CTX_LEN, V), lambda b, h, i: (ctx0 + b, v0 + h))],
        out_specs=pl.BlockSpec((tq, V), lambda b, h, i: (b * per + i, h)),
        out_shape=jax.ShapeDtypeStruct((N_LAT, H * V), BF16),
        compiler_params=_params(3, est),
        name="mla_attention",
    )(q, kv, kv, kv, kv)
    return linear(att, w_o, e=j, tm=512, tn=1024, name="mla_out")


def dense_ffn(a, f, wg, wu, wd):
    p = swiglu_up(a, wg, wu, e=f, tm=1024, tf=512, name="ffn_up")
    return linear(p, wd, e=f, tm=512, tn=512, name="ffn_down")


def _router_kernel(h_ref, sc_ref, sh_ref, whi_ref, wlo_ref, o_ref):
    a = h_ref[...] * (1.0 + sc_ref[...]) + sh_ref[...]
    a_hi, a_lo = _split_bf16(a)
    w_hi = whi_ref[...]
    o_ref[...] = (jnp.dot(a_hi, w_hi, preferred_element_type=F32)
                  + jnp.dot(a_lo, w_hi, preferred_element_type=F32)
                  + jnp.dot(a_hi, wlo_ref[...], preferred_element_type=F32))


def router_logits(h, n_rows, mods, w_router, *, tm=256):
    w = jnp.pad(w_router, ((0, 0), (0, LANE - N_EXPERTS)))
    w_hi, w_lo = _split_bf16(w)
    row = pl.BlockSpec((tm, D_MODEL), lambda i: (i, 0))
    wspec = pl.BlockSpec((D_MODEL, LANE), lambda i: (0, 0))
    out = pl.pallas_call(
        _router_kernel,
        grid=(n_rows // tm,),
        in_specs=[row, _mod_spec(4, tm), _mod_spec(3, tm), wspec, wspec],
        out_specs=pl.BlockSpec((tm, LANE), lambda i: (i, 0)),
        out_shape=jax.ShapeDtypeStruct((n_rows, LANE), F32),
        compiler_params=_params(1, 8 * _nbytes((tm, D_MODEL), F32)),
        name="moe_router",
    )(h, mods, mods, w_hi, w_lo)
    return out[:, :N_EXPERTS]


def moe_ffn(h, a_packed, n_rows, mods, f, w_router, wg, wu, wd, *, tm=512):
    logits = router_logits(h, n_rows, mods, w_router[f])
    top_v, top_i = lax.top_k(logits, TOP_K)
    gates = jax.nn.softmax(top_v, axis=-1)
    n_assign = n_rows * TOP_K
    n_tiles = n_assign // tm + N_EXPERTS
    flat_e = top_i.reshape(n_assign)
    onehot = (flat_e[:, None] == jnp.arange(N_EXPERTS)[None, :]).astype(jnp.int32)
    csum = jnp.cumsum(onehot, axis=0)
    rank = jnp.sum(csum * onehot, axis=1) - 1
    counts = csum[-1]
    padded = ((counts + tm - 1) // tm) * tm
    ends = jnp.cumsum(padded)
    starts = ends - padded
    pos = starts[flat_e] + rank
    src = jnp.zeros((n_tiles * tm,), jnp.int32).at[pos].set(
        jnp.arange(n_assign, dtype=jnp.int32) // TOP_K)
    tile_row0 = jnp.arange(n_tiles, dtype=jnp.int32) * tm
    te = jnp.minimum(jnp.searchsorted(ends, tile_row0, side="right"),
                     N_EXPERTS - 1).astype(jnp.int32)
    nv = (ends[-1] // tm).astype(jnp.int32).reshape(1)
    te = jnp.where(tile_row0 < ends[-1], te, te[jnp.maximum(nv[0] - 1, 0)]) + f * N_EXPERTS
    x_sorted = a_packed.at[src].get(mode="promise_in_bounds")
    stack = lambda w: w.reshape((-1,) + w.shape[2:])
    p = swiglu_up(x_sorted, stack(wg), stack(wu), tm=tm, tf=1024, te=te, nv=nv, name="moe_up")
    out = linear(p, stack(wd), tm=tm, tn=512, te=te, nv=nv, name="moe_down")
    pos = pos.reshape(n_rows, TOP_K)
    ys = [out.at[pos[:, k]].get(mode="promise_in_bounds") for k in range(TOP_K)]
    return ys, [gates[:, k:k + 1] for k in range(TOP_K)]


def kernel(x, c, ctx, c_ctx, w_mod, b_mod, ln_g, ln_b, pool_w, pool_scale, hy_w_in, hy_b_in, hy_conv_w, hy_conv_b, hy_f_w1, hy_f_b1, hy_f_w2, hy_f_b2, hy_f_w3, hy_f_b3, hy_f_freq, hy_f_wout, hy_skip, hy_w_out, hy_b_out, mla_w_in, mla_q_norm, mla_kv_norm, mla_w_uq, mla_w_ukv, mla_w_o, ffn_w_gate, ffn_w_up, ffn_w_down, moe_w_router, moe_w_gate, moe_w_up, moe_w_down):
    attn_layers = [i for i in range(DEPTH) if i % N_MIXERS == 2]
    last_read = attn_layers[-1] if attn_layers else -1

    c8 = jnp.concatenate([c, c_ctx[None], jnp.zeros((MOD_ROWS - BATCH - 1, D_MODEL), F32)], axis=0)
    mods = modulation_table(c8, w_mod, b_mod)

    h = jnp.concatenate([x.reshape(N_LAT, D_MODEL), ctx.reshape(N_CTX, D_MODEL)], axis=0)
    a = None
    for i in range(DEPTH):
        kind, j = i % N_MIXERS, i // N_MIXERS
        ctx_out = i < last_read
        n_rows = N_LAT + N_CTX if ctx_out else N_LAT
        m = mods[i]
        moe = i % 2 == 1
        if kind == 0:
            tl = CTX_LEN if n_rows > N_LAT else 512
            h1, a2 = pool_layer(h, n_rows, m, pool_w[j].astype(BF16), pool_scale[j],
                                ln_g[i, 0], ln_b[i, 0], tl=tl, pack=moe)
        else:
            if kind == 1:
                filt = (hy_f_w1[j], hy_f_b1[j], hy_f_w2[j], hy_f_b2[j], hy_f_w3[j], hy_f_b3[j],
                        hy_f_freq[j], hy_f_wout[j])
                y = hyena_mix(a[:n_rows], n_rows, j, hy_w_in, hy_b_in, hy_conv_w[j], hy_conv_b[j],
                              filt, hy_skip[j], hy_w_out, hy_b_out)
            else:
                assert i <= last_read and not ctx_out and a.shape[0] == N_LAT + N_CTX
                y = mla_mix(a, j, mla_w_in[j], mla_q_norm[j], mla_kv_norm[j], mla_w_uq[j],
                            mla_w_ukv[j], mla_w_o)
            h1, a2 = resid_ln(h, [y], n_rows, m, 2, ln_g[i, 0], ln_b[i, 0], m, 4, 3, pack=moe)
        f = i // 2
        nm = mods[min(i + 1, DEPTH - 1)]
        if moe:
            ys, row_gates = moe_ffn(h1, a2, n_rows, m, f, moe_w_router, moe_w_gate, moe_w_up,
                                    moe_w_down)
        else:
            ys, row_gates = [dense_ffn(a2, f, ffn_w_gate, ffn_w_up, ffn_w_down)], None
        h, a = resid_ln(h1, ys, n_rows, m, 5, ln_g[i, 1], ln_b[i, 1], nm, 1, 0,
                        row_gates=row_gates)
    return h[:N_LAT].reshape(BATCH, SEQ, D_MODEL)
```

```python
import functools
import math

import jax
import jax.numpy as jnp
from jax import lax
from jax.experimental import pallas as pl
from jax.experimental.pallas import tpu as pltpu

F32 = jnp.float32
BF16 = jnp.bfloat16
U32 = jnp.uint32

D_MODEL = 2048
BATCH = 4
SEQ = 2048
DEPTH = 4
GRID_W = 64
CTX_LEN = 256
N_MIXERS = 3
POOL_WINDOWS = (2, 4, 8, 16)
POOL_GROUP = D_MODEL // len(POOL_WINDOWS)
HY_ORDER = 2
HY_SHORT = 3
HY_EMB = 33
HY_FILTER = 64
HY_DECAY_TARGET = 1e-2
HY_FAST = 0.3
HY_SLOW = 1.5
MLA_HEADS = D_MODEL // 128
MLA_Q_RANK = 512
MLA_KV_RANK = 512
MLA_NOPE = 128
MLA_ROPE = 64
MLA_V = 128
ROPE_BASE = 10000.0
D_FF = 5632
N_EXPERTS = 8
TOP_K = 2
EXPERT_FF = 2 * D_MODEL
LN_EPS = 1e-5
RMS_EPS = 1e-6
DN_ALPHA = (2 * DEPTH) ** 0.25

N_LAT = BATCH * SEQ
N_CTX = BATCH * CTX_LEN
MOD_ROWS = 8
CTX_MOD_ROW = BATCH
LANE = 128
SUBLANE = 8
VMEM_CAP = 60 * 1024 * 1024
MLA_QK_PAD = 256
MLA_IN = MLA_Q_RANK + MLA_KV_RANK + MLA_ROPE
MLA_IN_PAD = MLA_IN + MLA_ROPE


def _vmem(nbytes):
    return int(min(VMEM_CAP, max(16 * 1024 * 1024, nbytes * 3 // 2)))


def _nbytes(shape, dtype):
    return math.prod(shape) * jnp.dtype(dtype).itemsize


def _mod_row(i, tm):
    return jnp.minimum((i * tm) // SEQ, CTX_MOD_ROW)


def _mod_spec(which, tm):
    return pl.BlockSpec((None, 1, D_MODEL), lambda i: (_mod_row(i, tm) * 6 + which, 0, 0))


def _params(sem, est):
    return pltpu.CompilerParams(dimension_semantics=("arbitrary",) * sem,
                                vmem_limit_bytes=_vmem(est))


def _mod_kernel(c_ref, w_ref, b_ref, o_ref):
    c = c_ref[...]
    a = c / (1.0 + jnp.exp(-c))
    o_ref[...] = jnp.dot(a.astype(BF16), w_ref[...].astype(BF16),
                         preferred_element_type=F32) + b_ref[...]


def modulation_table(c8, w_mod, b_mod):
    tn = 1024
    n = 6 * D_MODEL
    out = pl.pallas_call(
        _mod_kernel,
        grid=(DEPTH, n // tn),
        in_specs=[pl.BlockSpec((MOD_ROWS, D_MODEL), lambda l, j: (0, 0)),
                  pl.BlockSpec((None, D_MODEL, tn), lambda l, j: (l, 0, j)),
                  pl.BlockSpec((None, 1, tn), lambda l, j: (l, 0, j))],
        out_specs=pl.BlockSpec((None, MOD_ROWS, tn), lambda l, j: (l, 0, j)),
        out_shape=jax.ShapeDtypeStruct((DEPTH, MOD_ROWS, n), F32),
        compiler_params=_params(2, 3 * _nbytes((D_MODEL, tn), F32)),
        name="modulation_table",
    )(c8, w_mod, b_mod.reshape(DEPTH, 1, n))
    return out.reshape(DEPTH, MOD_ROWS * 6, 1, D_MODEL)


def _linear_kernel(xt_ref, te_ref, nv_ref, x_ref, w_ref, *rest, has_bias, cast_w):
    rest = list(rest)
    b_ref = rest.pop(0) if has_bias else None
    o_ref = rest.pop(0)
    wbf_ref = rest.pop(0) if cast_w else w_ref
    i = pl.program_id(1)

    if cast_w:
        prev = te_ref[jnp.maximum(i - 1, 0)]

        @pl.when(jnp.logical_or(i == 0, te_ref[i] != prev))
        def _():
            wbf_ref[...] = w_ref[...].astype(BF16)

    @pl.when(i < nv_ref[0])
    def _():
        acc = jnp.dot(x_ref[...].astype(BF16), wbf_ref[...], preferred_element_type=F32)
        if has_bias:
            acc = acc + b_ref[...]
        o_ref[...] = acc.astype(o_ref.dtype)

    @pl.when(i >= nv_ref[0])
    def _():
        o_ref[...] = jnp.zeros_like(o_ref)


def linear(x, w, *, tm, tn, e=0, n_out=None, n_tiles=None, xt=None, te=None, nv=None, bias=None,
           out_dtype=F32, w_col0=0, name="linear"):
    k = x.shape[1]
    if w.ndim == 2:
        w = w[None]
    if n_out is None:
        n_out = w.shape[2] - w_col0
    assert x.shape[0] % tm == 0 and n_out % tn == 0 and w_col0 % tn == 0 and w.shape[1] == k
    if n_tiles is None:
        n_tiles = x.shape[0] // tm
    if xt is None:
        xt = jnp.arange(n_tiles, dtype=jnp.int32)
    if te is None:
        te = jnp.full((n_tiles,), e, jnp.int32)
    if nv is None:
        nv = jnp.full((1,), n_tiles, jnp.int32)
    cast_w = w.dtype != BF16
    has_bias = bias is not None
    cb = w_col0 // tn
    in_specs = [pl.BlockSpec((tm, k), lambda j, i, xt, te, nv: (xt[i], 0)),
                pl.BlockSpec((None, k, tn), lambda j, i, xt, te, nv: (te[i], 0, j + cb))]
    args = [x, w]
    if has_bias:
        bias = bias.reshape(-1, 1, bias.shape[-1])
        in_specs.append(pl.BlockSpec((None, 1, tn), lambda j, i, xt, te, nv: (te[i], 0, j)))
        args.append(bias)
    scratch = [pltpu.VMEM((k, tn), BF16)] if cast_w else []
    est = (2 * _nbytes((tm, k), x.dtype) + 2 * _nbytes((k, tn), w.dtype)
           + (_nbytes((k, tn), BF16) if cast_w else 0) + _nbytes((tm, k), BF16)
           + 2 * _nbytes((tm, tn), out_dtype) + 2 * _nbytes((tm, tn), F32))
    return pl.pallas_call(
        functools.partial(_linear_kernel, has_bias=has_bias, cast_w=cast_w),
        grid_spec=pltpu.PrefetchScalarGridSpec(
            num_scalar_prefetch=3,
            grid=(n_out // tn, n_tiles),
            in_specs=in_specs,
            out_specs=pl.BlockSpec((tm, tn), lambda j, i, xt, te, nv: (i, j)),
            scratch_shapes=scratch),
        out_shape=jax.ShapeDtypeStruct((n_tiles * tm, n_out), out_dtype),
        compiler_params=_params(2, est),
        name=name,
    )(xt, te, nv, *args)


def _swiglu_kernel(te_ref, nv_ref, x_ref, wg_ref, wu_ref, o_ref, wgb_ref, wub_ref):
    i = pl.program_id(1)
    prev = te_ref[jnp.maximum(i - 1, 0)]

    @pl.when(jnp.logical_or(i == 0, te_ref[i] != prev))
    def _():
        wgb_ref[...] = wg_ref[...].astype(BF16)
        wub_ref[...] = wu_ref[...].astype(BF16)

    @pl.when(i < nv_ref[0])
    def _():
        x = x_ref[...].astype(BF16)
        g = jnp.dot(x, wgb_ref[...], preferred_element_type=F32)
        u = jnp.dot(x, wub_ref[...], preferred_element_type=F32)
        o_ref[...] = ((g / (1.0 + jnp.exp(-g))) * u).astype(o_ref.dtype)

    @pl.when(i >= nv_ref[0])
    def _():
        o_ref[...] = jnp.zeros_like(o_ref)


def swiglu_up(x, wg, wu, *, tm, tf, e=0, te=None, nv=None, name="swiglu_up"):
    m, k = x.shape
    f = wg.shape[2]
    assert wg.shape[1] == k and m % tm == 0 and f % tf == 0
    n_tiles = m // tm
    if te is None:
        te = jnp.full((n_tiles,), e, jnp.int32)
    if nv is None:
        nv = jnp.full((1,), n_tiles, jnp.int32)
    wspec = pl.BlockSpec((None, k, tf), lambda j, i, te, nv: (te[i], 0, j))
    est = (2 * _nbytes((tm, k), x.dtype) + _nbytes((tm, k), BF16) + 4 * _nbytes((k, tf), F32)
           + 2 * _nbytes((k, tf), BF16) + 2 * _nbytes((tm, tf), BF16) + 3 * _nbytes((tm, tf), F32))
    return pl.pallas_call(
        _swiglu_kernel,
        grid_spec=pltpu.PrefetchScalarGridSpec(
            num_scalar_prefetch=2,
            grid=(f // tf, n_tiles),
            in_specs=[pl.BlockSpec((tm, k), lambda j, i, te, nv: (i, 0)), wspec, wspec],
            out_specs=pl.BlockSpec((tm, tf), lambda j, i, te, nv: (i, j)),
            scratch_shapes=[pltpu.VMEM((k, tf), BF16), pltpu.VMEM((k, tf), BF16)]),
        out_shape=jax.ShapeDtypeStruct((m, f), BF16),
        compiler_params=_params(2, est),
        name=name,
    )(te, nv, x, wg, wu)


def _ln_mod(v, g, b, sc, sh, pack):
    mu = jnp.mean(v, axis=-1, keepdims=True)
    d = v - mu
    var = jnp.mean(d * d, axis=-1, keepdims=True)
    h = d * lax.rsqrt(var + LN_EPS) * g + b
    a = h * (1.0 + sc) + sh
    return h, (a if pack else a.astype(BF16))


def _resid_ln_kernel(*refs, n_y, pack):
    h_ref = refs[0]
    y_refs = refs[1:1 + n_y]
    rest = refs[1 + n_y:]
    if n_y == 2:
        g0_ref, g1_ref = rest[0], rest[1]
        rest = rest[2:]
        y = g0_ref[...] * y_refs[0][...] + g1_ref[...] * y_refs[1][...]
    else:
        y = y_refs[0][...]
    gate_ref, g_ref, b_ref, sc_ref, sh_ref, o_ref, a_ref = rest
    v = DN_ALPHA * h_ref[...] + gate_ref[...] * y
    h, a = _ln_mod(v, g_ref[...], b_ref[...], sc_ref[...], sh_ref[...], pack)
    o_ref[...] = h
    a_ref[...] = a


def resid_ln(h, ys, n_rows, mods, gate_idx, ln_g, ln_b, nmods, sc_idx, sh_idx, *, row_gates=None,
             pack=False, tm=256):
    assert n_rows % tm == 0
    row = pl.BlockSpec((tm, D_MODEL), lambda i: (i, 0))
    col = pl.BlockSpec((tm, 1), lambda i: (i, 0))
    vec = pl.BlockSpec((1, D_MODEL), lambda i: (0, 0))
    a_w = D_MODEL
    a_dt = F32 if pack else BF16
    est = 10 * _nbytes((tm, D_MODEL), F32)
    in_specs = [row] + [row] * len(ys) + ([col, col] if row_gates else []) + [
        _mod_spec(gate_idx, tm), vec, vec, _mod_spec(sc_idx, tm), _mod_spec(sh_idx, tm)]
    args = [h, *ys, *(row_gates or []), mods, ln_g.reshape(1, D_MODEL), ln_b.reshape(1, D_MODEL),
            nmods, nmods]
    return pl.pallas_call(
        functools.partial(_resid_ln_kernel, n_y=len(ys), pack=pack),
        grid=(n_rows // tm,),
        in_specs=in_specs,
        out_specs=[row, pl.BlockSpec((tm, a_w), lambda i: (i, 0))],
        out_shape=[jax.ShapeDtypeStruct((n_rows, D_MODEL), F32),
                   jax.ShapeDtypeStruct((n_rows, a_w), a_dt)],
        compiler_params=_params(1, est),
        name="resid_ln",
    )(*args)


def _linear_ln_kernel(x_ref, w_ref, *rest, has_bias, pack):
    rest = list(rest)
    bias_ref = rest.pop(0) if has_bias else None
    h_ref, gate_ref, g_ref, b_ref, sc_ref, sh_ref, o_ref, a_ref, wb_ref = rest

    @pl.when(pl.program_id(0) == 0)
    def _():
        wb_ref[...] = w_ref[...].astype(BF16)

    y = jnp.dot(x_ref[...], wb_ref[...], preferred_element_type=F32)
    if has_bias:
        y = y + bias_ref[...]
    v = DN_ALPHA * h_ref[...] + gate_ref[...] * y
    h, a = _ln_mod(v, g_ref[...], b_ref[...], sc_ref[...], sh_ref[...], pack)
    o_ref[...] = h
    a_ref[...] = a


def linear_resid_ln(x, w, e, bias, h, n_rows, mods, gate_idx, ln_g, ln_b, sc_idx, sh_idx, *,
                    pack=False, tm=256):
    k = x.shape[1]
    row = pl.BlockSpec((tm, D_MODEL), lambda i: (i, 0))
    vec = pl.BlockSpec((1, D_MODEL), lambda i: (0, 0))
    a_w = D_MODEL
    in_specs = [pl.BlockSpec((tm, k), lambda i: (i, 0)),
                pl.BlockSpec((None, k, D_MODEL), lambda i: (e, 0, 0),
                             pipeline_mode=pl.Buffered(1))]
    args = [x, w]
    if bias is not None:
        in_specs.append(pl.BlockSpec((None, 1, D_MODEL), lambda i: (e, 0, 0)))
        args.append(bias.reshape(-1, 1, D_MODEL))
    in_specs += [row, _mod_spec(gate_idx, tm), vec, vec, _mod_spec(sc_idx, tm),
                 _mod_spec(sh_idx, tm)]
    args += [h, mods, ln_g.reshape(1, D_MODEL), ln_b.reshape(1, D_MODEL), mods, mods]
    est = (_nbytes((k, D_MODEL), F32) + _nbytes((k, D_MODEL), BF16) + 2 * _nbytes((tm, k), BF16)
           + 10 * _nbytes((tm, D_MODEL), F32))
    return pl.pallas_call(
        functools.partial(_linear_ln_kernel, has_bias=bias is not None, pack=pack),
        grid=(n_rows // tm,),
        in_specs=in_specs,
        out_specs=[row, pl.BlockSpec((tm, a_w), lambda i: (i, 0))],
        out_shape=[jax.ShapeDtypeStruct((n_rows, D_MODEL), F32),
                   jax.ShapeDtypeStruct((n_rows, a_w), F32 if pack else BF16)],
        scratch_shapes=[pltpu.VMEM((k, D_MODEL), BF16)],
        compiler_params=_params(1, est),
        name="linear_resid_ln",
    )(*args)


def _seq_pos(i, tl):
    row0 = i * tl
    is_lat = row0 < N_LAT
    pos = jnp.where(is_lat, row0 % SEQ, (row0 - N_LAT) % CTX_LEN)
    seqlen = jnp.where(is_lat, SEQ, CTX_LEN)
    return pos, seqlen


def _halo_specs(tl, width, n_rows, col_index=None):
    per = tl // SUBLANE
    last = n_rows // SUBLANE - 1
    if col_index is None:
        prev = pl.BlockSpec((SUBLANE, width), lambda i: (jnp.maximum(i * per - 1, 0), 0))
        nxt = pl.BlockSpec((SUBLANE, width), lambda i: (jnp.minimum((i + 1) * per, last), 0))
    else:
        prev = pl.BlockSpec((SUBLANE, width),
                            lambda i, j: (jnp.maximum(i * per - 1, 0), col_index(j)))
        nxt = pl.BlockSpec((SUBLANE, width),
                           lambda i, j: (jnp.minimum((i + 1) * per, last), col_index(j)))
    return prev, nxt


def _shift_rows(x, s):
    n = x.shape[0]
    return pltpu.roll(x, s % n, 0)


def _pool_kernel(h_ref, hp_ref, hn_ref, sc_ref, sh_ref, gate_ref, w_ref, ps_ref, g_ref, b_ref,
                 nsc_ref, nsh_ref, o_ref, a_ref, *, tl, pack):
    i = pl.program_id(0)
    pos, seqlen = _seq_pos(i, tl)
    sc = 1.0 + sc_ref[...]
    sh = sh_ref[...]
    h = h_ref[...]
    a = h * sc + sh
    keep_prev = (pos > 0).astype(F32)
    keep_next = (pos + tl < seqlen).astype(F32)
    ext = jnp.concatenate([(hp_ref[...] * sc + sh) * keep_prev, a,
                           (hn_ref[...] * sc + sh) * keep_next], axis=0)
    t = pos + lax.broadcasted_iota(jnp.int32, (tl, 1), 0)
    n_ext = tl + 2 * SUBLANE
    parts = []
    for g, w in enumerate(POOL_WINDOWS):
        cols = slice(g * POOL_GROUP, (g + 1) * POOL_GROUP)
        s = ext[:, cols]
        s = s + _shift_rows(s, 1)
        r = 1
        while 2 * r < w:
            s = _shift_rows(s, n_ext - r) + _shift_rows(s, r)
            r *= 2
        s = s[SUBLANE:SUBLANE + tl]
        cnt = (jnp.minimum(t + w // 2, seqlen) - jnp.maximum(t - w // 2, 0)).astype(F32)
        d = s / cnt - a[:, cols]
        parts.append(jnp.dot(d.astype(BF16), w_ref[g], preferred_element_type=F32))
    y = jnp.concatenate(parts, axis=1) * ps_ref[...]
    v = DN_ALPHA * h + gate_ref[...] * y
    hn, an = _ln_mod(v, g_ref[...], b_ref[...], nsc_ref[...], nsh_ref[...], pack)
    o_ref[...] = hn
    a_ref[...] = an


def pool_layer(h, n_rows, mods, w_grp_bf16, pool_scale, ln_g, ln_b, *, tl, pack):
    row = pl.BlockSpec((tl, D_MODEL), lambda i: (i, 0))
    vec = pl.BlockSpec((1, D_MODEL), lambda i: (0, 0))
    prev, nxt = _halo_specs(tl, D_MODEL, n_rows)
    a_w = D_MODEL
    est = 12 * _nbytes((tl, D_MODEL), F32) + 2 * _nbytes(w_grp_bf16.shape, BF16)
    return pl.pallas_call(
        functools.partial(_pool_kernel, tl=tl, pack=pack),
        grid=(n_rows // tl,),
        in_specs=[row, prev, nxt, _mod_spec(1, tl), _mod_spec(0, tl), _mod_spec(2, tl),
                  pl.BlockSpec(w_grp_bf16.shape, lambda i: (0, 0, 0)), vec, vec, vec,
                  _mod_spec(4, tl), _mod_spec(3, tl)],
        out_specs=[row, pl.BlockSpec((tl, a_w), lambda i: (i, 0))],
        out_shape=[jax.ShapeDtypeStruct((n_rows, D_MODEL), F32),
                   jax.ShapeDtypeStruct((n_rows, a_w), F32 if pack else BF16)],
        compiler_params=_params(1, est),
        name="pool_layer",
    )(h, h, h, mods, mods, mods, w_grp_bf16, pool_scale.reshape(1, D_MODEL),
      ln_g.reshape(1, D_MODEL), ln_b.reshape(1, D_MODEL), mods, mods)


def _conv3_seq(u, w, b):
    n = u.shape[0]
    t = lax.broadcasted_iota(jnp.int32, (n, 1), 0)
    before = jnp.where(t == 0, 0.0, _shift_rows(u, 1))
    after = jnp.where(t == n - 1, 0.0, _shift_rows(u, n - 1))
    return b + before * w[0:1] + u * w[1:2] + after * w[2:3]


def _conv3_tile(u, prev, nxt, w, b, first, last):
    n = u.shape[0]
    ext = jnp.concatenate([prev * (1.0 - first.astype(F32)), u, nxt * (1.0 - last.astype(F32))],
                          axis=0)
    before = _shift_rows(ext, 1)[SUBLANE:SUBLANE + n]
    after = _shift_rows(ext, n + 2 * SUBLANE - 1)[SUBLANE:SUBLANE + n]
    return b + before * w[0:1] + u * w[1:2] + after * w[2:3]


def _split_bf16(x):
    hi = x.astype(BF16)
    lo = (x - hi.astype(F32)).astype(BF16)
    return hi, lo


def _dot3(a, b):
    a_hi, a_lo = _split_bf16(a)
    b_hi, b_lo = _split_bf16(b)
    return (jnp.dot(a_hi, b_hi, preferred_element_type=F32)
            + jnp.dot(a_lo, b_hi, preferred_element_type=F32)
            + jnp.dot(a_hi, b_lo, preferred_element_type=F32))


def _filter_kernel(z_ref, w1_ref, b1_ref, w2_ref, b2_ref, w3_ref, b3_ref, fr_ref, wo_ref,
                   dist_ref, delta_ref, o_ref, a_ref):
    @pl.when(pl.program_id(0) == 0)
    def _():
        fr = fr_ref[...]
        a = jnp.sin(fr * (_dot3(z_ref[...], w1_ref[...]) + b1_ref[...]))
        a = jnp.sin(fr * (_dot3(a, w2_ref[...]) + b2_ref[...]))
        a_ref[...] = jnp.sin(fr * (_dot3(a, w3_ref[...]) + b3_ref[...]))

    k = _dot3(a_ref[...], wo_ref[...])
    k = k * jnp.exp(-dist_ref[...] * delta_ref[...])
    o_ref[...] = k * lax.rsqrt(jnp.sum(k * k, axis=0, keepdims=True) + 1e-6)


def _pad2(x, rows, cols):
    return jnp.pad(x, ((0, rows - x.shape[0]), (0, cols - x.shape[1])))


def hyena_filters(L, f_w1, f_b1, f_w2, f_b2, f_w3, f_b3, f_freq, f_wout, *, tn=512):
    bands = (HY_EMB - 1) // 2
    t = jnp.linspace(0.0, 1.0, L, dtype=F32)[:, None]
    wpos = 2.0 * math.pi * jnp.arange(L, dtype=F32)[:, None] / L
    f = jnp.linspace(1e-4, bands - 1, bands, dtype=F32)[None, :]
    z = jnp.concatenate([t, jnp.cos(f * wpos), -jnp.sin(f * wpos)], axis=-1)
    dist = (jnp.abs(jnp.arange(L) - L // 2).astype(F32) / max(L // 2, 1))[:, None]
    deltas = jnp.linspace(math.log(HY_DECAY_TARGET) / HY_SLOW, math.log(HY_DECAY_TARGET) / HY_FAST,
                          D_MODEL, dtype=F32)
    absdelta = jnp.tile(jnp.abs(deltas), HY_ORDER)[None, :]
    P = LANE
    n = HY_ORDER * D_MODEL
    full = lambda shape: pl.BlockSpec(shape, lambda j: (0, 0))
    args = [_pad2(z, L, P), _pad2(f_w1, P, P), _pad2(f_b1[None], 1, P), _pad2(f_w2, P, P),
            _pad2(f_b2[None], 1, P), _pad2(f_w3, P, P), _pad2(f_b3[None], 1, P),
            _pad2(f_freq[None], 1, P), _pad2(f_wout, P, n), dist, absdelta]
    in_specs = [full((L, P)), full((P, P)), full((1, P)), full((P, P)), full((1, P)), full((P, P)),
                full((1, P)), full((1, P)), pl.BlockSpec((P, tn), lambda j: (0, j)),
                full((L, 1)), pl.BlockSpec((1, tn), lambda j: (0, j))]
    return pl.pallas_call(
        _filter_kernel,
        grid=(n // tn,),
        in_specs=in_specs,
        out_specs=pl.BlockSpec((L, tn), lambda j: (0, j)),
        out_shape=jax.ShapeDtypeStruct((L, n), F32),
        scratch_shapes=[pltpu.VMEM((L, P), F32)],
        compiler_params=_params(1, 8 * _nbytes((L, tn), F32)),
        name="hyena_filters",
    )(*args)


def dft_tables(L, tm):
    n_fft = 2 * L
    half = tm // 2
    r_idx = jnp.arange(2 * L, dtype=jnp.int32)
    odd = 2 * ((r_idx // tm) * half + (r_idx % half)) + 1
    is_sin = ((r_idx % tm) >= half)[:, None]

    def trig(k):
        ang = (k % (2 * n_fft)).astype(F32) * (math.pi / n_fft)
        return jnp.cos(ang), jnp.sin(ang)

    lo = 64
    s1 = jnp.arange(L // lo, dtype=jnp.int32) * lo
    s0 = jnp.arange(lo, dtype=jnp.int32)
    cb, sb = trig(odd[:, None] * s0[None, :])

    def combine(ca, sa):
        a1 = jnp.where(is_sin, sa, ca)
        a2 = jnp.where(is_sin, ca, -sa)
        return a1, a2

    a1, a2 = combine(*trig(odd[:, None] * s1[None, :]))
    fwd = (a1[:, :, None] * cb[:, None, :] + a2[:, :, None] * sb[:, None, :]).reshape(2 * L, L)
    a1, a2 = combine(*trig(odd[:, None] * (s1[None, :] + L // 2)))
    inv = (a1.T[:, None, :] * cb.T[None, :, :] + a2.T[:, None, :] * sb.T[None, :, :]).reshape(
        L, 2 * L)
    return fwd.astype(BF16), inv.astype(BF16)


def _dft_mul_kernel(x_ref, w_ref, *rest, inv_len, conv):
    if conv:
        cw_ref, cb_ref, k_ref, o_ref, wb_ref = rest
    else:
        k_ref, o_ref, wb_ref = rest

    @pl.when(pl.program_id(1) == 0)
    def _():
        w = w_ref[...]
        if conv:
            w = _conv3_seq(w, cw_ref[...], cb_ref[...])
        wb_ref[...] = w.astype(BF16)

    s = jnp.dot(x_ref[...], wb_ref[...], preferred_element_type=F32)
    half = s.shape[0] // 2
    a, b = s[:half], s[half:]
    ka, kb = k_ref[0:half, :], k_ref[half:2 * half, :]
    o_ref[0:half, :] = ((a * ka - b * kb) * inv_len).astype(BF16)
    o_ref[half:2 * half, :] = ((a * kb + b * ka) * inv_len).astype(BF16)


def dft_mul(fwd, v_view, kspec, k_col0, *, L, nb, tm, tn, conv=None):
    d = D_MODEL
    per_b = d // tn
    v, v_row0, v_col0 = v_view
    vr, vc, kc = v_row0 // L, v_col0 // tn, k_col0 // tn
    est = (2 * _nbytes((tm, L), BF16) + 2 * _nbytes((L, tn), F32) + _nbytes((L, tn), BF16)
           + 2 * _nbytes((tm, tn), F32) + 2 * _nbytes((tm, tn), BF16) + 3 * _nbytes((tm, tn), F32))
    in_specs = [pl.BlockSpec((tm, L), lambda j, i: (i, 0)),
                pl.BlockSpec((L, tn), lambda j, i: (j // per_b + vr, j % per_b + vc))]
    args = [fwd, v]
    if conv is not None:
        in_specs += [pl.BlockSpec((HY_SHORT, tn), lambda j, i: (0, j % per_b + vc)),
                     pl.BlockSpec((1, tn), lambda j, i: (0, j % per_b + vc))]
        args += list(conv)
    in_specs.append(pl.BlockSpec((tm, tn), lambda j, i: (i, j % per_b + kc)))
    args.append(kspec)
    return pl.pallas_call(
        functools.partial(_dft_mul_kernel, inv_len=1.0 / L, conv=conv is not None),
        grid=(nb * per_b, (2 * L) // tm),
        in_specs=in_specs,
        out_specs=pl.BlockSpec((tm, tn), lambda j, i: (i, j)),
        out_shape=jax.ShapeDtypeStruct((2 * L, nb * d), BF16),
        scratch_shapes=[pltpu.VMEM((L, tn), BF16)],
        compiler_params=_params(2, est),
        name="dft_mul",
    )(*args)


def _idft_kernel(x_ref, y_ref, *rest, z_conv, per):
    rest = list(rest)
    g_ref, gp_ref, gn_ref, gw_ref, gb_ref = rest[:5]
    rest = rest[5:]
    i = pl.program_id(1)
    first, last = i == 0, i == per - 1
    gate = _conv3_tile(g_ref[...], gp_ref[...], gn_ref[...], gw_ref[...], gb_ref[...], first, last)
    if z_conv:
        z_ref, zp_ref, zn_ref, zw_ref, zb_ref, skip_ref, o_ref = rest
        z = _conv3_tile(z_ref[...], zp_ref[...], zn_ref[...], zw_ref[...], zb_ref[...], first, last)
    else:
        z_ref, skip_ref, o_ref = rest
        z = z_ref[...]
    conv = jnp.dot(x_ref[...], y_ref[...], preferred_element_type=F32)
    o_ref[...] = (gate * (conv + z * skip_ref[...])).astype(o_ref.dtype)


def idft_gate(inv, y, gate_view, z_view, skip, conv, *, z_conv, L, nb, tm, tn, out_dtype):
    d = D_MODEL
    per_b = d // tn
    per = L // tm
    sub = tm // SUBLANE

    def tile_specs(view, with_conv):
        arr, row0, col0 = view
        r0, c0 = row0 // tm, col0 // tn
        last_blk = arr.shape[0] // SUBLANE - 1
        row = lambda j, i: (j // per_b) * per + i + r0
        col = lambda j, i: j % per_b + c0
        specs = [pl.BlockSpec((tm, tn), lambda j, i: (row(j, i), col(j, i)))]
        args = [arr]
        if with_conv:
            specs += [
                pl.BlockSpec((SUBLANE, tn),
                             lambda j, i: (jnp.maximum(row(j, i) * sub - 1, 0), col(j, i))),
                pl.BlockSpec((SUBLANE, tn),
                             lambda j, i: (jnp.minimum((row(j, i) + 1) * sub, last_blk), col(j, i))),
                pl.BlockSpec((HY_SHORT, tn), lambda j, i: (0, col(j, i))),
                pl.BlockSpec((1, tn), lambda j, i: (0, col(j, i)))]
            args += [arr, arr, conv[0], conv[1]]
        return specs, args

    g_specs, g_args = tile_specs(gate_view, True)
    z_specs, z_args = tile_specs(z_view, z_conv)
    est = (2 * _nbytes((tm, 2 * L), BF16) + 2 * _nbytes((2 * L, tn), BF16)
           + 12 * _nbytes((tm, tn), F32))
    return pl.pallas_call(
        functools.partial(_idft_kernel, z_conv=z_conv, per=per),
        grid=(nb * per_b, per),
        in_specs=[pl.BlockSpec((tm, 2 * L), lambda j, i: (i, 0)),
                  pl.BlockSpec((2 * L, tn), lambda j, i: (0, j)),
                  *g_specs, *z_specs,
                  pl.BlockSpec((1, tn), lambda j, i: (0, j % per_b))],
        out_specs=pl.BlockSpec((tm, tn), lambda j, i: ((j // per_b) * per + i, j % per_b)),
        out_shape=jax.ShapeDtypeStruct((nb * L, d), out_dtype),
        compiler_params=_params(2, est),
        name="idft_gate",
    )(inv, y, *g_args, *z_args, skip.reshape(1, d))


def hyena_stream(u, row0, L, nb, conv, filt, skip):
    d = D_MODEL
    tms = min(1024, 2 * L)
    tmi = min(512, L)
    k = hyena_filters(L, *filt)
    fwd, inv = dft_tables(L, tms)
    kspec = linear(fwd, k, tm=tms, tn=1024, name="dft_filters")
    y1 = dft_mul(fwd, (u, row0, 0), kspec, 0, L=L, nb=nb, tm=tms, tn=1024, conv=conv)
    z1 = idft_gate(inv, y1, (u, row0, d), (u, row0, 0), skip[0], conv, z_conv=True, L=L, nb=nb,
                   tm=tmi, tn=1024, out_dtype=F32)
    y2 = dft_mul(fwd, (z1, 0, 0), kspec, d, L=L, nb=nb, tm=tms, tn=1024)
    return idft_gate(inv, y2, (u, row0, 2 * d), (z1, 0, 0), skip[1], conv, z_conv=False, L=L,
                     nb=nb, tm=tmi, tn=1024, out_dtype=BF16)


def hyena_mix(a, n_rows, j, w_in, b_in, conv_w, conv_b, filt, skip):
    u = linear(a, w_in, e=j, tm=1024, tn=1024, bias=b_in, name="hyena_in")
    conv = (conv_w, conv_b.reshape(1, -1))
    z = hyena_stream(u, 0, SEQ, BATCH, conv, filt, skip)
    if n_rows > N_LAT:
        z_ctx = hyena_stream(u, N_LAT, CTX_LEN, BATCH, conv, filt, skip)
        z = jnp.concatenate([z, z_ctx], axis=0)
    return z


def _rms(x, g):
    return x * lax.rsqrt(jnp.mean(x * x, axis=-1, keepdims=True) + RMS_EPS) * g


def _mla_q_kernel(p_ref, g_ref, w1_ref, w2_ref, c_ref, s_ref, o_ref, w1b_ref, w2b_ref):
    @pl.when(pl.program_id(1) == 0)
    def _():
        w1b_ref[...] = w1_ref[...].astype(BF16)
        w2b_ref[...] = w2_ref[...].astype(BF16)

    xn = _rms(p_ref[...], g_ref[...]).astype(BF16)
    q1 = jnp.dot(xn, w1b_ref[...], preferred_element_type=F32)
    q2 = jnp.dot(xn, w2b_ref[...], preferred_element_type=F32)
    reps = q1.shape[1] // MLA_QK_PAD
    c = jnp.concatenate([c_ref[...]] * reps, axis=1)
    s = jnp.concatenate([s_ref[...]] * reps, axis=1)
    o_ref[...] = (q1 * c + q2 * s).astype(BF16)


def _mla_kv_kernel(p_ref, r_ref, g_ref, cs_ref, w_ref, o_ref, wb_ref):
    @pl.when(pl.program_id(1) == 0)
    def _():
        wb_ref[...] = w_ref[...].astype(BF16)

    xn = _rms(p_ref[...], g_ref[...]).astype(BF16)
    rr = (r_ref[...] * cs_ref[...]).astype(BF16)
    acc = (jnp.dot(xn, wb_ref[0:MLA_KV_RANK, :], preferred_element_type=F32)
           + jnp.dot(rr, wb_ref[MLA_KV_RANK:MLA_KV_RANK + 2 * MLA_ROPE, :],
                     preferred_element_type=F32))
    o_ref[...] = acc.astype(BF16)


def _attn_kernel(q_ref, kl_ref, kc_ref, vl_ref, vc_ref, o_ref, *, kchunk):
    c = (MLA_NOPE + MLA_ROPE) ** -0.5 * math.log2(math.e)
    nt = (((1,), (1,)), ((), ()))
    q = q_ref[...]
    chunks = [(kl_ref, vl_ref, r) for r in range(kl_ref.shape[0] // kchunk)]
    chunks += [(kc_ref, vc_ref, r) for r in range(kc_ref.shape[0] // kchunk)]
    m = l = acc = None
    for k_ref, v_ref, r in chunks:
        rows = slice(r * kchunk, (r + 1) * kchunk)
        s = lax.dot_general(q, k_ref[rows, :], nt, preferred_element_type=F32)
        ms = jnp.max(s, axis=-1, keepdims=True)
        m_new = ms if m is None else jnp.maximum(m, ms)
        p = jnp.exp2((s - m_new) * c)
        ps = jnp.sum(p, axis=-1, keepdims=True)
        pv = jnp.dot(p.astype(BF16), v_ref[rows, :], preferred_element_type=F32)
        if m is None:
            l, acc = ps, pv
        else:
            alpha = jnp.exp2((m - m_new) * c)
            l = alpha * l + ps
            acc = alpha * acc + pv
        m = m_new
    o_ref[...] = (acc / l).astype(o_ref.dtype)


def _rope_tables():
    rows = SEQ // GRID_W
    row = jnp.repeat(jnp.arange(rows), GRID_W).astype(F32)
    col = jnp.tile(jnp.arange(GRID_W), rows).astype(F32)
    n = MLA_ROPE // 4
    inv = ROPE_BASE ** (-jnp.arange(n, dtype=F32) / n)
    ang_row, ang_col = row[:, None] * inv, col[:, None] * inv
    cos = jnp.concatenate([jnp.cos(ang_row)] * 2 + [jnp.cos(ang_col)] * 2, axis=1)
    sin = jnp.concatenate([jnp.sin(ang_row)] * 2 + [jnp.sin(ang_col)] * 2, axis=1)
    return cos, sin


def _rope_partner(w):
    n = MLA_ROPE // 4
    return jnp.concatenate([-w[..., n:2 * n], w[..., 0:n], -w[..., 3 * n:4 * n],
                            w[..., 2 * n:3 * n]], axis=-1)


def mla_mix(a, w_in, q_norm, kv_norm, w_uq, w_ukv):
    H, NP, R, V, QK = MLA_HEADS, MLA_NOPE, MLA_ROPE, MLA_V, MLA_QK_PAD
    n_all = N_LAT + N_CTX
    cos, sin = _rope_tables()
    w_in_x = jnp.concatenate([w_in, _rope_partner(w_in[:, MLA_IN - R:])], axis=1)
    proj = linear(a, w_in_x, tm=512, tn=MLA_IN_PAD, name="mla_in")
    wq = w_uq.reshape(MLA_Q_RANK, H, NP + R)
    zq = jnp.zeros((MLA_Q_RANK, H, QK - NP - R), F32)
    wq1 = jnp.concatenate([wq, zq], axis=2).reshape(MLA_Q_RANK, H * QK)
    wq2 = jnp.concatenate([jnp.zeros((MLA_Q_RANK, H, NP), F32), _rope_partner(wq[..., NP:]), zq],
                          axis=2).reshape(MLA_Q_RANK, H * QK)
    ones, zeros = jnp.ones((SEQ, NP), F32), jnp.zeros((SEQ, QK - NP - R), F32)
    cq = jnp.concatenate([ones, cos, zeros], axis=1)
    sq = jnp.concatenate([0.0 * ones, sin, zeros], axis=1)
    tm, tn = 512, 1024
    per = SEQ // tm
    q = pl.pallas_call(
        _mla_q_kernel,
        grid=(H * QK // tn, N_LAT // tm),
        in_specs=[pl.BlockSpec((tm, MLA_Q_RANK), lambda jn, i: (i, 0)),
                  pl.BlockSpec((1, MLA_Q_RANK), lambda jn, i: (0, 0)),
                  pl.BlockSpec((MLA_Q_RANK, tn), lambda jn, i: (0, jn)),
                  pl.BlockSpec((MLA_Q_RANK, tn), lambda jn, i: (0, jn)),
                  pl.BlockSpec((tm, QK), lambda jn, i: (i % per, 0)),
                  pl.BlockSpec((tm, QK), lambda jn, i: (i % per, 0))],
        out_specs=pl.BlockSpec((tm, tn), lambda jn, i: (i, jn)),
        out_shape=jax.ShapeDtypeStruct((N_LAT, H * QK), BF16),
        scratch_shapes=[pltpu.VMEM((MLA_Q_RANK, tn), BF16), pltpu.VMEM((MLA_Q_RANK, tn), BF16)],
        compiler_params=_params(2, 6 * _nbytes((MLA_Q_RANK, tn), F32) + 8 * _nbytes((tm, tn), F32)),
        name="mla_q",
    )(proj, q_norm.reshape(1, MLA_Q_RANK), wq1, wq2, cq, sq)
    wkv = w_ukv.reshape(MLA_KV_RANK, H, NP + V)
    wk = jnp.concatenate([wkv[..., :NP], jnp.zeros((MLA_KV_RANK, H, QK - NP), F32)],
                         axis=2).reshape(MLA_KV_RANK, H * QK)
    place = jnp.concatenate([jnp.zeros((R, NP), F32), jnp.eye(R, dtype=F32),
                             jnp.zeros((R, QK - NP - R), F32)], axis=1)
    place = jnp.tile(jnp.concatenate([place, place], axis=0), (1, H))
    w_kv = jnp.concatenate([
        jnp.concatenate([wk, wkv[..., NP:].reshape(MLA_KV_RANK, H * V)], axis=1),
        jnp.concatenate([place, jnp.zeros((2 * R, H * V), F32)], axis=1)], axis=0)
    cs = jnp.concatenate([
        jnp.concatenate([cos, sin], axis=1),
        jnp.concatenate([jnp.ones((CTX_LEN, R), F32), jnp.zeros((CTX_LEN, R), F32)], axis=1)],
        axis=0)
    tm, tn = CTX_LEN, 2048
    per = SEQ // tm
    n_kv = H * (QK + V)
    kv = pl.pallas_call(
        _mla_kv_kernel,
        grid=(n_kv // tn, n_all // tm),
        in_specs=[pl.BlockSpec((tm, MLA_KV_RANK), lambda jn, i: (i, MLA_Q_RANK // MLA_KV_RANK)),
                  pl.BlockSpec((tm, 2 * R), lambda jn, i: (i, (MLA_IN - R) // (2 * R))),
                  pl.BlockSpec((1, MLA_KV_RANK), lambda jn, i: (0, 0)),
                  pl.BlockSpec((tm, 2 * R),
                               lambda jn, i: (jnp.where(i * tm < N_LAT, i % per, per), 0)),
                  pl.BlockSpec((MLA_KV_RANK + 2 * R, tn), lambda jn, i: (0, jn))],
        out_specs=pl.BlockSpec((tm, tn), lambda jn, i: (i, jn)),
        out_shape=jax.ShapeDtypeStruct((n_all, n_kv), BF16),
        scratch_shapes=[pltpu.VMEM((MLA_KV_RANK + 2 * R, tn), BF16)],
        compiler_params=_params(2, 4 * _nbytes((MLA_KV_RANK + 2 * R, tn), F32)
                                + 8 * _nbytes((tm, tn), F32)),
        name="mla_kv",
    )(proj, proj, kv_norm.reshape(1, MLA_KV_RANK), cs, w_kv)
    tq = 1024
    per = SEQ // tq
    v0 = H * QK // V
    ctx0 = N_LAT // CTX_LEN
    est = (2 * _nbytes((tq, QK), BF16) + 2 * _nbytes((SEQ + CTX_LEN, QK + V), BF16)
           + 16 * _nbytes((tq, CTX_LEN), F32))
    return pl.pallas_call(
        functools.partial(_attn_kernel, kchunk=CTX_LEN),
        grid=(BATCH, H, per),
        in_specs=[pl.BlockSpec((tq, QK), lambda b, h, i: (b * per + i, h)),
                  pl.BlockSpec((SEQ, QK), lambda b, h, i: (b, h)),
                  pl.BlockSpec((CTX_LEN, QK), lambda b, h, i: (ctx0 + b, h)),
                  pl.BlockSpec((SEQ, V), lambda b, h, i: (b, v0 + h)),
                  pl.BlockSpec((CTX_LEN, V), lambda b, h, i: (ctx0 + b, v0 + h))],
        out_specs=pl.BlockSpec((tq, V), lambda b, h, i: (b * per + i, h)),
        out_shape=jax.ShapeDtypeStruct((N_LAT, H * V), BF16),
        compiler_params=_params(3, est),
        name="mla_attention",
    )(q, kv, kv, kv, kv)


def dense_ffn(a, f, wg, wu, wd):
    p = swiglu_up(a, wg, wu, e=f, tm=1024, tf=512, name="ffn_up")
    return linear(p, wd, e=f, tm=512, tn=512, out_dtype=BF16, name="ffn_down")


def _router_kernel(h_ref, sc_ref, sh_ref, whi_ref, wlo_ref, o_ref):
    a = h_ref[...] * (1.0 + sc_ref[...]) + sh_ref[...]
    a_hi, a_lo = _split_bf16(a)
    w_hi = whi_ref[...]
    o_ref[...] = (jnp.dot(a_hi, w_hi, preferred_element_type=F32)
                  + jnp.dot(a_lo, w_hi, preferred_element_type=F32)
                  + jnp.dot(a_hi, wlo_ref[...], preferred_element_type=F32))


def router_logits(h, n_rows, mods, w_router, *, tm=256):
    w = jnp.pad(w_router, ((0, 0), (0, LANE - N_EXPERTS)))
    w_hi, w_lo = _split_bf16(w)
    row = pl.BlockSpec((tm, D_MODEL), lambda i: (i, 0))
    wspec = pl.BlockSpec((D_MODEL, LANE), lambda i: (0, 0))
    out = pl.pallas_call(
        _router_kernel,
        grid=(n_rows // tm,),
        in_specs=[row, _mod_spec(4, tm), _mod_spec(3, tm), wspec, wspec],
        out_specs=pl.BlockSpec((tm, LANE), lambda i: (i, 0)),
        out_shape=jax.ShapeDtypeStruct((n_rows, LANE), F32),
        compiler_params=_params(1, 8 * _nbytes((tm, D_MODEL), F32)),
        name="moe_router",
    )(h, mods, mods, w_hi, w_lo)
    return out[:, :N_EXPERTS]


def moe_ffn(h, a_packed, n_rows, mods, f, w_router, wg, wu, wd, *, tm=512):
    logits = router_logits(h, n_rows, mods, w_router[f])
    top_v, top_i = lax.top_k(logits, TOP_K)
    gates = jax.nn.softmax(top_v, axis=-1)
    n_assign = n_rows * TOP_K
    n_tiles = n_assign // tm + N_EXPERTS
    flat_e = top_i.reshape(n_assign)
    onehot = (flat_e[:, None] == jnp.arange(N_EXPERTS)[None, :]).astype(jnp.int32)
    csum = jnp.cumsum(onehot, axis=0)
    rank = jnp.sum(csum * onehot, axis=1) - 1
    counts = csum[-1]
    padded = ((counts + tm - 1) // tm) * tm
    ends = jnp.cumsum(padded)
    starts = ends - padded
    pos = starts[flat_e] + rank
    src = jnp.zeros((n_tiles * tm,), jnp.int32).at[pos].set(
        jnp.arange(n_assign, dtype=jnp.int32) // TOP_K)
    tile_row0 = jnp.arange(n_tiles, dtype=jnp.int32) * tm
    te = jnp.minimum(jnp.sum((tile_row0[:, None] >= ends[None, :]).astype(jnp.int32), axis=1),
                     N_EXPERTS - 1)
    nv = (ends[-1] // tm).astype(jnp.int32).reshape(1)
    te = jnp.where(tile_row0 < ends[-1], te, te[jnp.maximum(nv[0] - 1, 0)]) + f * N_EXPERTS
    x_sorted = a_packed.at[src].get(mode="promise_in_bounds")
    stack = lambda w: w.reshape((-1,) + w.shape[2:])
    p = swiglu_up(x_sorted, stack(wg), stack(wu), tm=tm, tf=1024, te=te, nv=nv, name="moe_up")
    out = linear(p, stack(wd), tm=tm, tn=1024, te=te, nv=nv, name="moe_down")
    pos = pos.reshape(n_rows, TOP_K)
    ys = [out.at[pos[:, k]].get(mode="promise_in_bounds") for k in range(TOP_K)]
    return ys, [gates[:, k:k + 1] for k in range(TOP_K)]


def kernel(x, c, ctx, c_ctx, w_mod, b_mod, ln_g, ln_b, pool_w, pool_scale, hy_w_in, hy_b_in, hy_conv_w, hy_conv_b, hy_f_w1, hy_f_b1, hy_f_w2, hy_f_b2, hy_f_w3, hy_f_b3, hy_f_freq, hy_f_wout, hy_skip, hy_w_out, hy_b_out, mla_w_in, mla_q_norm, mla_kv_norm, mla_w_uq, mla_w_ukv, mla_w_o, ffn_w_gate, ffn_w_up, ffn_w_down, moe_w_router, moe_w_gate, moe_w_up, moe_w_down):
    attn_layers = [i for i in range(DEPTH) if i % N_MIXERS == 2]
    last_read = attn_layers[-1] if attn_layers else -1

    c8 = jnp.concatenate([c, c_ctx[None], jnp.zeros((MOD_ROWS - BATCH - 1, D_MODEL), F32)], axis=0)
    mods = modulation_table(c8, w_mod, b_mod)

    h = jnp.concatenate([x.reshape(N_LAT, D_MODEL), ctx.reshape(N_CTX, D_MODEL)], axis=0)
    a = None
    for i in range(DEPTH):
        kind, j = i % N_MIXERS, i // N_MIXERS
        ctx_out = i < last_read
        n_rows = N_LAT + N_CTX if ctx_out else N_LAT
        m = mods[i]
        moe = i % 2 == 1
        if kind == 0:
            tl = CTX_LEN if n_rows > N_LAT else 512
            h1, a2 = pool_layer(h, n_rows, m, pool_w[j].astype(BF16), pool_scale[j],
                                ln_g[i, 0], ln_b[i, 0], tl=tl, pack=moe)
        else:
            if kind == 1:
                filt = (hy_f_w1[j], hy_f_b1[j], hy_f_w2[j], hy_f_b2[j], hy_f_w3[j], hy_f_b3[j],
                        hy_f_freq[j], hy_f_wout[j])
                z = hyena_mix(a[:n_rows], n_rows, j, hy_w_in, hy_b_in, hy_conv_w[j], hy_conv_b[j],
                              filt, hy_skip[j])
                w_o, b_o = hy_w_out, hy_b_out
            else:
                assert i <= last_read and not ctx_out and a.shape[0] == N_LAT + N_CTX
                z = mla_mix(a, mla_w_in[j], mla_q_norm[j], mla_kv_norm[j], mla_w_uq[j],
                            mla_w_ukv[j])
                w_o, b_o = mla_w_o, None
            h1, a2 = linear_resid_ln(z, w_o, j, b_o, h, n_rows, m, 2, ln_g[i, 0], ln_b[i, 0], 4, 3,
                                     pack=moe)
        f = i // 2
        nm = mods[min(i + 1, DEPTH - 1)]
        if moe:
            ys, row_gates = moe_ffn(h1, a2, n_rows, m, f, moe_w_router, moe_w_gate, moe_w_up,
                                    moe_w_down)
        else:
            ys, row_gates = [dense_ffn(a2, f, ffn_w_gate, ffn_w_up, ffn_w_down)], None
        h, a = resid_ln(h1, ys, n_rows, m, 5, ln_g[i, 1], ln_b[i, 1], nm, 1, 0,
                        row_gates=row_gates)
    return h[:N_LAT].reshape(BATCH, SEQ, D_MODEL)
```

```python
import functools
import math

import jax
import jax.numpy as jnp
from jax import lax
from jax.experimental import pallas as pl
from jax.experimental.pallas import tpu as pltpu

F32 = jnp.float32
BF16 = jnp.bfloat16
U32 = jnp.uint32

D_MODEL = 2048
BATCH = 4
SEQ = 2048
DEPTH = 4
GRID_W = 64
CTX_LEN = 256
N_MIXERS = 3
POOL_WINDOWS = (2, 4, 8, 16)
POOL_GROUP = D_MODEL // len(POOL_WINDOWS)
HY_ORDER = 2
HY_SHORT = 3
HY_EMB = 33
HY_FILTER = 64
HY_DECAY_TARGET = 1e-2
HY_FAST = 0.3
HY_SLOW = 1.5
MLA_HEADS = D_MODEL // 128
MLA_Q_RANK = 512
MLA_KV_RANK = 512
MLA_NOPE = 128
MLA_ROPE = 64
MLA_V = 128
ROPE_BASE = 10000.0
D_FF = 5632
N_EXPERTS = 8
TOP_K = 2
EXPERT_FF = 2 * D_MODEL
LN_EPS = 1e-5
RMS_EPS = 1e-6
DN_ALPHA = (2 * DEPTH) ** 0.25

N_LAT = BATCH * SEQ
N_CTX = BATCH * CTX_LEN
MOD_ROWS = 8
CTX_MOD_ROW = BATCH
LANE = 128
SUBLANE = 8
VMEM_CAP = 60 * 1024 * 1024
MLA_QK_PAD = 256
MLA_IN = MLA_Q_RANK + MLA_KV_RANK + MLA_ROPE
MLA_IN_PAD = MLA_IN + MLA_ROPE


def _vmem(nbytes):
    return int(min(VMEM_CAP, max(16 * 1024 * 1024, nbytes * 3 // 2)))


def _nbytes(shape, dtype):
    return math.prod(shape) * jnp.dtype(dtype).itemsize


def _mod_row(i, tm):
    return jnp.minimum((i * tm) // SEQ, CTX_MOD_ROW)


def _mod_spec(which, tm):
    return pl.BlockSpec((None, 1, D_MODEL), lambda i: (_mod_row(i, tm) * 6 + which, 0, 0))


def _params(sem, est):
    return pltpu.CompilerParams(dimension_semantics=("arbitrary",) * sem,
                                vmem_limit_bytes=_vmem(est))


def _mod_kernel(c_ref, w_ref, b_ref, o_ref):
    c = c_ref[...]
    a = c / (1.0 + jnp.exp(-c))
    o_ref[...] = jnp.dot(a.astype(BF16), w_ref[...].astype(BF16),
                         preferred_element_type=F32) + b_ref[...]


def modulation_table(c8, w_mod, b_mod):
    tn = 1024
    n = 6 * D_MODEL
    out = pl.pallas_call(
        _mod_kernel,
        grid=(DEPTH, n // tn),
        in_specs=[pl.BlockSpec((MOD_ROWS, D_MODEL), lambda l, j: (0, 0)),
                  pl.BlockSpec((None, D_MODEL, tn), lambda l, j: (l, 0, j)),
                  pl.BlockSpec((None, 1, tn), lambda l, j: (l, 0, j))],
        out_specs=pl.BlockSpec((None, MOD_ROWS, tn), lambda l, j: (l, 0, j)),
        out_shape=jax.ShapeDtypeStruct((DEPTH, MOD_ROWS, n), F32),
        compiler_params=_params(2, 3 * _nbytes((D_MODEL, tn), F32)),
        name="modulation_table",
    )(c8, w_mod, b_mod.reshape(DEPTH, 1, n))
    return out.reshape(DEPTH, MOD_ROWS * 6, 1, D_MODEL)


def _linear_kernel(xt_ref, te_ref, nv_ref, x_ref, w_ref, *rest, has_bias, cast_w):
    rest = list(rest)
    b_ref = rest.pop(0) if has_bias else None
    o_ref = rest.pop(0)
    wbf_ref = rest.pop(0) if cast_w else w_ref
    i = pl.program_id(1)

    if cast_w:
        prev = te_ref[jnp.maximum(i - 1, 0)]

        @pl.when(jnp.logical_or(i == 0, te_ref[i] != prev))
        def _():
            wbf_ref[...] = w_ref[...].astype(BF16)

    @pl.when(i < nv_ref[0])
    def _():
        acc = jnp.dot(x_ref[...].astype(BF16), wbf_ref[...], preferred_element_type=F32)
        if has_bias:
            acc = acc + b_ref[...]
        o_ref[...] = acc.astype(o_ref.dtype)

    @pl.when(i >= nv_ref[0])
    def _():
        o_ref[...] = jnp.zeros_like(o_ref)


def linear(x, w, *, tm, tn, e=0, n_out=None, n_tiles=None, xt=None, te=None, nv=None, bias=None,
           out_dtype=F32, w_col0=0, name="linear"):
    k = x.shape[1]
    if w.ndim == 2:
        w = w[None]
    if n_out is None:
        n_out = w.shape[2] - w_col0
    assert x.shape[0] % tm == 0 and n_out % tn == 0 and w_col0 % tn == 0 and w.shape[1] == k
    if n_tiles is None:
        n_tiles = x.shape[0] // tm
    if xt is None:
        xt = jnp.arange(n_tiles, dtype=jnp.int32)
    if te is None:
        te = jnp.full((n_tiles,), e, jnp.int32)
    if nv is None:
        nv = jnp.full((1,), n_tiles, jnp.int32)
    cast_w = w.dtype != BF16
    has_bias = bias is not None
    cb = w_col0 // tn
    in_specs = [pl.BlockSpec((tm, k), lambda j, i, xt, te, nv: (xt[i], 0)),
                pl.BlockSpec((None, k, tn), lambda j, i, xt, te, nv: (te[i], 0, j + cb))]
    args = [x, w]
    if has_bias:
        bias = bias.reshape(-1, 1, bias.shape[-1])
        in_specs.append(pl.BlockSpec((None, 1, tn), lambda j, i, xt, te, nv: (te[i], 0, j)))
        args.append(bias)
    scratch = [pltpu.VMEM((k, tn), BF16)] if cast_w else []
    est = (2 * _nbytes((tm, k), x.dtype) + 2 * _nbytes((k, tn), w.dtype)
           + (_nbytes((k, tn), BF16) if cast_w else 0) + _nbytes((tm, k), BF16)
           + 2 * _nbytes((tm, tn), out_dtype) + 2 * _nbytes((tm, tn), F32))
    return pl.pallas_call(
        functools.partial(_linear_kernel, has_bias=has_bias, cast_w=cast_w),
        grid_spec=pltpu.PrefetchScalarGridSpec(
            num_scalar_prefetch=3,
            grid=(n_out // tn, n_tiles),
            in_specs=in_specs,
            out_specs=pl.BlockSpec((tm, tn), lambda j, i, xt, te, nv: (i, j)),
            scratch_shapes=scratch),
        out_shape=jax.ShapeDtypeStruct((n_tiles * tm, n_out), out_dtype),
        compiler_params=_params(2, est),
        name=name,
    )(xt, te, nv, *args)


def _swiglu_kernel(te_ref, nv_ref, *refs, n_parts):
    x_refs = refs[:n_parts]
    wg_ref, wu_ref, o_ref, wgb_ref, wub_ref = refs[n_parts:]
    i = pl.program_id(1)
    prev = te_ref[jnp.maximum(i - 1, 0)]

    @pl.when(jnp.logical_or(i == 0, te_ref[i] != prev))
    def _():
        wgb_ref[...] = wg_ref[...].astype(BF16)
        wub_ref[...] = wu_ref[...].astype(BF16)

    @pl.when(i < nv_ref[0])
    def _():
        g = u = None
        k0 = 0
        for x_ref in x_refs:
            x = x_ref[...].astype(BF16)
            rows = slice(k0, k0 + x.shape[1])
            gp = jnp.dot(x, wgb_ref[rows, :], preferred_element_type=F32)
            up = jnp.dot(x, wub_ref[rows, :], preferred_element_type=F32)
            g, u = (gp, up) if g is None else (g + gp, u + up)
            k0 += x.shape[1]
        o_ref[...] = ((g / (1.0 + jnp.exp(-g))) * u).astype(o_ref.dtype)

    @pl.when(i >= nv_ref[0])
    def _():
        o_ref[...] = jnp.zeros_like(o_ref)


def swiglu_up(x_parts, wg, wu, *, tm, tf, e=0, te=None, nv=None, name="swiglu_up"):
    m = x_parts[0].shape[0]
    k, f = wg.shape[1], wg.shape[2]
    assert sum(x.shape[1] for x in x_parts) == k and m % tm == 0 and f % tf == 0
    n_tiles = m // tm
    if te is None:
        te = jnp.full((n_tiles,), e, jnp.int32)
    if nv is None:
        nv = jnp.full((1,), n_tiles, jnp.int32)
    wspec = pl.BlockSpec((None, k, tf), lambda j, i, te, nv: (te[i], 0, j))
    est = (2 * _nbytes((tm, k), x_parts[0].dtype) + _nbytes((tm, k), BF16)
           + 4 * _nbytes((k, tf), F32) + 2 * _nbytes((k, tf), BF16) + 2 * _nbytes((tm, tf), BF16)
           + 3 * _nbytes((tm, tf), F32))
    x_specs = [pl.BlockSpec((tm, x.shape[1]), lambda j, i, te, nv: (i, 0)) for x in x_parts]
    return pl.pallas_call(
        functools.partial(_swiglu_kernel, n_parts=len(x_parts)),
        grid_spec=pltpu.PrefetchScalarGridSpec(
            num_scalar_prefetch=2,
            grid=(f // tf, n_tiles),
            in_specs=[*x_specs, wspec, wspec],
            out_specs=pl.BlockSpec((tm, tf), lambda j, i, te, nv: (i, j)),
            scratch_shapes=[pltpu.VMEM((k, tf), BF16), pltpu.VMEM((k, tf), BF16)]),
        out_shape=jax.ShapeDtypeStruct((m, f), BF16),
        compiler_params=_params(2, est),
        name=name,
    )(te, nv, *x_parts, wg, wu)


def _ln_mod(v, g, b, sc, sh, pack):
    mu = jnp.mean(v, axis=-1, keepdims=True)
    d = v - mu
    var = jnp.mean(d * d, axis=-1, keepdims=True)
    h = d * lax.rsqrt(var + LN_EPS) * g + b
    a = h * (1.0 + sc) + sh
    return h, (a if pack else a.astype(BF16))


def _resid_ln_kernel(*refs, n_y, pack):
    h_ref = refs[0]
    y_refs = refs[1:1 + n_y]
    rest = refs[1 + n_y:]
    if n_y == 2:
        g0_ref, g1_ref = rest[0], rest[1]
        rest = rest[2:]
        y = g0_ref[...] * y_refs[0][...] + g1_ref[...] * y_refs[1][...]
    else:
        y = y_refs[0][...]
    gate_ref, g_ref, b_ref, sc_ref, sh_ref, o_ref, a_ref = rest
    v = DN_ALPHA * h_ref[...] + gate_ref[...] * y
    h, a = _ln_mod(v, g_ref[...], b_ref[...], sc_ref[...], sh_ref[...], pack)
    o_ref[...] = h
    a_ref[...] = a


def resid_ln(h, ys, n_rows, mods, gate_idx, ln_g, ln_b, nmods, sc_idx, sh_idx, *, row_gates=None,
             pack=False, tm=256):
    assert n_rows % tm == 0
    row = pl.BlockSpec((tm, D_MODEL), lambda i: (i, 0))
    col = pl.BlockSpec((tm, 1), lambda i: (i, 0))
    vec = pl.BlockSpec((1, D_MODEL), lambda i: (0, 0))
    a_w = D_MODEL
    a_dt = F32 if pack else BF16
    est = 10 * _nbytes((tm, D_MODEL), F32)
    in_specs = [row] + [row] * len(ys) + ([col, col] if row_gates else []) + [
        _mod_spec(gate_idx, tm), vec, vec, _mod_spec(sc_idx, tm), _mod_spec(sh_idx, tm)]
    args = [h, *ys, *(row_gates or []), mods, ln_g.reshape(1, D_MODEL), ln_b.reshape(1, D_MODEL),
            nmods, nmods]
    return pl.pallas_call(
        functools.partial(_resid_ln_kernel, n_y=len(ys), pack=pack),
        grid=(n_rows // tm,),
        in_specs=in_specs,
        out_specs=[row, pl.BlockSpec((tm, a_w), lambda i: (i, 0))],
        out_shape=[jax.ShapeDtypeStruct((n_rows, D_MODEL), F32),
                   jax.ShapeDtypeStruct((n_rows, a_w), a_dt)],
        compiler_params=_params(1, est),
        name="resid_ln",
    )(*args)


def _linear_ln_kernel(x_ref, w_ref, *rest, has_bias, pack):
    rest = list(rest)
    bias_ref = rest.pop(0) if has_bias else None
    h_ref, gate_ref, g_ref, b_ref, sc_ref, sh_ref, o_ref, a_ref, wb_ref = rest

    @pl.when(pl.program_id(0) == 0)
    def _():
        wb_ref[...] = w_ref[...].astype(BF16)

    y = jnp.dot(x_ref[...], wb_ref[...], preferred_element_type=F32)
    if has_bias:
        y = y + bias_ref[...]
    v = DN_ALPHA * h_ref[...] + gate_ref[...] * y
    h, a = _ln_mod(v, g_ref[...], b_ref[...], sc_ref[...], sh_ref[...], pack)
    o_ref[...] = h
    a_ref[...] = a


def linear_resid_ln(x, w, e, bias, h, n_rows, mods, gate_idx, ln_g, ln_b, sc_idx, sh_idx, *,
                    pack=False, tm=256):
    k = x.shape[1]
    row = pl.BlockSpec((tm, D_MODEL), lambda i: (i, 0))
    vec = pl.BlockSpec((1, D_MODEL), lambda i: (0, 0))
    a_w = D_MODEL
    in_specs = [pl.BlockSpec((tm, k), lambda i: (i, 0)),
                pl.BlockSpec((None, k, D_MODEL), lambda i: (e, 0, 0),
                             pipeline_mode=pl.Buffered(1))]
    args = [x, w]
    if bias is not None:
        in_specs.append(pl.BlockSpec((None, 1, D_MODEL), lambda i: (e, 0, 0)))
        args.append(bias.reshape(-1, 1, D_MODEL))
    in_specs += [row, _mod_spec(gate_idx, tm), vec, vec, _mod_spec(sc_idx, tm),
                 _mod_spec(sh_idx, tm)]
    args += [h, mods, ln_g.reshape(1, D_MODEL), ln_b.reshape(1, D_MODEL), mods, mods]
    est = (_nbytes((k, D_MODEL), F32) + _nbytes((k, D_MODEL), BF16) + 2 * _nbytes((tm, k), BF16)
           + 10 * _nbytes((tm, D_MODEL), F32))
    return pl.pallas_call(
        functools.partial(_linear_ln_kernel, has_bias=bias is not None, pack=pack),
        grid=(n_rows // tm,),
        in_specs=in_specs,
        out_specs=[row, pl.BlockSpec((tm, a_w), lambda i: (i, 0))],
        out_shape=[jax.ShapeDtypeStruct((n_rows, D_MODEL), F32),
                   jax.ShapeDtypeStruct((n_rows, a_w), F32 if pack else BF16)],
        scratch_shapes=[pltpu.VMEM((k, D_MODEL), BF16)],
        compiler_params=_params(1, est),
        name="linear_resid_ln",
    )(*args)


def _seq_pos(i, tl):
    row0 = i * tl
    is_lat = row0 < N_LAT
    pos = jnp.where(is_lat, row0 % SEQ, (row0 - N_LAT) % CTX_LEN)
    seqlen = jnp.where(is_lat, SEQ, CTX_LEN)
    return pos, seqlen


def _halo_specs(tl, width, n_rows, col_index=None):
    per = tl // SUBLANE
    last = n_rows // SUBLANE - 1
    if col_index is None:
        prev = pl.BlockSpec((SUBLANE, width), lambda i: (jnp.maximum(i * per - 1, 0), 0))
        nxt = pl.BlockSpec((SUBLANE, width), lambda i: (jnp.minimum((i + 1) * per, last), 0))
    else:
        prev = pl.BlockSpec((SUBLANE, width),
                            lambda i, j: (jnp.maximum(i * per - 1, 0), col_index(j)))
        nxt = pl.BlockSpec((SUBLANE, width),
                           lambda i, j: (jnp.minimum((i + 1) * per, last), col_index(j)))
    return prev, nxt


def _shift_rows(x, s):
    n = x.shape[0]
    return pltpu.roll(x, s % n, 0)


def _pool_kernel(h_ref, hp_ref, hn_ref, sc_ref, sh_ref, gate_ref, w_ref, ps_ref, g_ref, b_ref,
                 nsc_ref, nsh_ref, o_ref, a_ref, *, tl, pack):
    i = pl.program_id(0)
    pos, seqlen = _seq_pos(i, tl)
    sc = 1.0 + sc_ref[...]
    sh = sh_ref[...]
    h = h_ref[...]
    a = h * sc + sh
    keep_prev = (pos > 0).astype(F32)
    keep_next = (pos + tl < seqlen).astype(F32)
    ext = jnp.concatenate([(hp_ref[...] * sc + sh) * keep_prev, a,
                           (hn_ref[...] * sc + sh) * keep_next], axis=0)
    t = pos + lax.broadcasted_iota(jnp.int32, (tl, 1), 0)
    n_ext = tl + 2 * SUBLANE
    parts = []
    for g, w in enumerate(POOL_WINDOWS):
        cols = slice(g * POOL_GROUP, (g + 1) * POOL_GROUP)
        s = ext[:, cols]
        s = s + _shift_rows(s, 1)
        r = 1
        while 2 * r < w:
            s = _shift_rows(s, n_ext - r) + _shift_rows(s, r)
            r *= 2
        s = s[SUBLANE:SUBLANE + tl]
        cnt = (jnp.minimum(t + w // 2, seqlen) - jnp.maximum(t - w // 2, 0)).astype(F32)
        d = s / cnt - a[:, cols]
        parts.append(jnp.dot(d.astype(BF16), w_ref[g], preferred_element_type=F32))
    y = jnp.concatenate(parts, axis=1) * ps_ref[...]
    v = DN_ALPHA * h + gate_ref[...] * y
    hn, an = _ln_mod(v, g_ref[...], b_ref[...], nsc_ref[...], nsh_ref[...], pack)
    o_ref[...] = hn
    a_ref[...] = an


def pool_layer(h, n_rows, mods, w_grp_bf16, pool_scale, ln_g, ln_b, *, tl, pack):
    row = pl.BlockSpec((tl, D_MODEL), lambda i: (i, 0))
    vec = pl.BlockSpec((1, D_MODEL), lambda i: (0, 0))
    prev, nxt = _halo_specs(tl, D_MODEL, n_rows)
    a_w = D_MODEL
    est = 12 * _nbytes((tl, D_MODEL), F32) + 2 * _nbytes(w_grp_bf16.shape, BF16)
    return pl.pallas_call(
        functools.partial(_pool_kernel, tl=tl, pack=pack),
        grid=(n_rows // tl,),
        in_specs=[row, prev, nxt, _mod_spec(1, tl), _mod_spec(0, tl), _mod_spec(2, tl),
                  pl.BlockSpec(w_grp_bf16.shape, lambda i: (0, 0, 0)), vec, vec, vec,
                  _mod_spec(4, tl), _mod_spec(3, tl)],
        out_specs=[row, pl.BlockSpec((tl, a_w), lambda i: (i, 0))],
        out_shape=[jax.ShapeDtypeStruct((n_rows, D_MODEL), F32),
                   jax.ShapeDtypeStruct((n_rows, a_w), F32 if pack else BF16)],
        compiler_params=_params(1, est),
        name="pool_layer",
    )(h, h, h, mods, mods, mods, w_grp_bf16, pool_scale.reshape(1, D_MODEL),
      ln_g.reshape(1, D_MODEL), ln_b.reshape(1, D_MODEL), mods, mods)


def _conv3_seq(u, w, b):
    n = u.shape[0]
    t = lax.broadcasted_iota(jnp.int32, (n, 1), 0)
    before = jnp.where(t == 0, 0.0, _shift_rows(u, 1))
    after = jnp.where(t == n - 1, 0.0, _shift_rows(u, n - 1))
    return b + before * w[0:1] + u * w[1:2] + after * w[2:3]


def _conv3_tile(u, prev, nxt, w, b, first, last):
    n = u.shape[0]
    ext = jnp.concatenate([prev * (1.0 - first.astype(F32)), u, nxt * (1.0 - last.astype(F32))],
                          axis=0)
    before = _shift_rows(ext, 1)[SUBLANE:SUBLANE + n]
    after = _shift_rows(ext, n + 2 * SUBLANE - 1)[SUBLANE:SUBLANE + n]
    return b + before * w[0:1] + u * w[1:2] + after * w[2:3]


def _split_bf16(x):
    hi = x.astype(BF16)
    lo = (x - hi.astype(F32)).astype(BF16)
    return hi, lo


def _dot3(a, b):
    a_hi, a_lo = _split_bf16(a)
    b_hi, b_lo = _split_bf16(b)
    return (jnp.dot(a_hi, b_hi, preferred_element_type=F32)
            + jnp.dot(a_lo, b_hi, preferred_element_type=F32)
            + jnp.dot(a_hi, b_lo, preferred_element_type=F32))


def _filter_kernel(z_ref, w1_ref, b1_ref, w2_ref, b2_ref, w3_ref, b3_ref, fr_ref, wo_ref,
                   dist_ref, delta_ref, o_ref, a_ref):
    @pl.when(pl.program_id(0) == 0)
    def _():
        fr = fr_ref[...]
        a = jnp.sin(fr * (_dot3(z_ref[...], w1_ref[...]) + b1_ref[...]))
        a = jnp.sin(fr * (_dot3(a, w2_ref[...]) + b2_ref[...]))
        a_ref[...] = jnp.sin(fr * (_dot3(a, w3_ref[...]) + b3_ref[...]))

    k = _dot3(a_ref[...], wo_ref[...])
    k = k * jnp.exp(-dist_ref[...] * delta_ref[...])
    o_ref[...] = k * lax.rsqrt(jnp.sum(k * k, axis=0, keepdims=True) + 1e-6)


def _pad2(x, rows, cols):
    return jnp.pad(x, ((0, rows - x.shape[0]), (0, cols - x.shape[1])))


def hyena_filters(L, f_w1, f_b1, f_w2, f_b2, f_w3, f_b3, f_freq, f_wout, *, tn=512):
    bands = (HY_EMB - 1) // 2
    t = jnp.linspace(0.0, 1.0, L, dtype=F32)[:, None]
    wpos = 2.0 * math.pi * jnp.arange(L, dtype=F32)[:, None] / L
    f = jnp.linspace(1e-4, bands - 1, bands, dtype=F32)[None, :]
    z = jnp.concatenate([t, jnp.cos(f * wpos), -jnp.sin(f * wpos)], axis=-1)
    dist = (jnp.abs(jnp.arange(L) - L // 2).astype(F32) / max(L // 2, 1))[:, None]
    deltas = jnp.linspace(math.log(HY_DECAY_TARGET) / HY_SLOW, math.log(HY_DECAY_TARGET) / HY_FAST,
                          D_MODEL, dtype=F32)
    absdelta = jnp.tile(jnp.abs(deltas), HY_ORDER)[None, :]
    P = LANE
    n = HY_ORDER * D_MODEL
    full = lambda shape: pl.BlockSpec(shape, lambda j: (0, 0))
    args = [_pad2(z, L, P), _pad2(f_w1, P, P), _pad2(f_b1[None], 1, P), _pad2(f_w2, P, P),
            _pad2(f_b2[None], 1, P), _pad2(f_w3, P, P), _pad2(f_b3[None], 1, P),
            _pad2(f_freq[None], 1, P), _pad2(f_wout, P, n), dist, absdelta]
    in_specs = [full((L, P)), full((P, P)), full((1, P)), full((P, P)), full((1, P)), full((P, P)),
                full((1, P)), full((1, P)), pl.BlockSpec((P, tn), lambda j: (0, j)),
                full((L, 1)), pl.BlockSpec((1, tn), lambda j: (0, j))]
    return pl.pallas_call(
        _filter_kernel,
        grid=(n // tn,),
        in_specs=in_specs,
        out_specs=pl.BlockSpec((L, tn), lambda j: (0, j)),
        out_shape=jax.ShapeDtypeStruct((L, n), F32),
        scratch_shapes=[pltpu.VMEM((L, P), F32)],
        compiler_params=_params(1, 8 * _nbytes((L, tn), F32)),
        name="hyena_filters",
    )(*args)


def dft_tables(L, tm):
    n_fft = 2 * L
    half = tm // 2
    r_idx = jnp.arange(2 * L, dtype=jnp.int32)
    odd = 2 * ((r_idx // tm) * half + (r_idx % half)) + 1
    is_sin = ((r_idx % tm) >= half)[:, None]

    def trig(k):
        ang = (k % (2 * n_fft)).astype(F32) * (math.pi / n_fft)
        return jnp.cos(ang), jnp.sin(ang)

    lo = 64
    s1 = jnp.arange(L // lo, dtype=jnp.int32) * lo
    s0 = jnp.arange(lo, dtype=jnp.int32)
    cb, sb = trig(odd[:, None] * s0[None, :])

    def combine(ca, sa):
        a1 = jnp.where(is_sin, sa, ca)
        a2 = jnp.where(is_sin, ca, -sa)
        return a1, a2

    a1, a2 = combine(*trig(odd[:, None] * s1[None, :]))
    fwd = (a1[:, :, None] * cb[:, None, :] + a2[:, :, None] * sb[:, None, :]).reshape(2 * L, L)
    a1, a2 = combine(*trig(odd[:, None] * (s1[None, :] + L // 2)))
    inv = (a1.T[:, None, :] * cb.T[None, :, :] + a2.T[:, None, :] * sb.T[None, :, :]).reshape(
        L, 2 * L)
    return fwd.astype(BF16), inv.astype(BF16)


def _dft_mul_kernel(x_ref, w_ref, *rest, inv_len, conv):
    if conv:
        cw_ref, cb_ref, k_ref, o_ref, wb_ref = rest
    else:
        k_ref, o_ref, wb_ref = rest

    @pl.when(pl.program_id(1) == 0)
    def _():
        w = w_ref[...]
        if conv:
            w = _conv3_seq(w, cw_ref[...], cb_ref[...])
        wb_ref[...] = w.astype(BF16)

    s = jnp.dot(x_ref[...], wb_ref[...], preferred_element_type=F32)
    half = s.shape[0] // 2
    a, b = s[:half], s[half:]
    ka, kb = k_ref[0:half, :], k_ref[half:2 * half, :]
    o_ref[0:half, :] = ((a * ka - b * kb) * inv_len).astype(BF16)
    o_ref[half:2 * half, :] = ((a * kb + b * ka) * inv_len).astype(BF16)


def dft_mul(fwd, v_view, kspec, k_col0, *, L, nb, tm, tn, conv=None):
    d = D_MODEL
    per_b = d // tn
    v, v_row0, v_col0 = v_view
    vr, vc, kc = v_row0 // L, v_col0 // tn, k_col0 // tn
    est = (2 * _nbytes((tm, L), BF16) + 2 * _nbytes((L, tn), F32) + _nbytes((L, tn), BF16)
           + 2 * _nbytes((tm, tn), F32) + 2 * _nbytes((tm, tn), BF16) + 3 * _nbytes((tm, tn), F32))
    in_specs = [pl.BlockSpec((tm, L), lambda j, i: (i, 0)),
                pl.BlockSpec((L, tn), lambda j, i: (j // per_b + vr, j % per_b + vc))]
    args = [fwd, v]
    if conv is not None:
        in_specs += [pl.BlockSpec((HY_SHORT, tn), lambda j, i: (0, j % per_b + vc)),
                     pl.BlockSpec((1, tn), lambda j, i: (0, j % per_b + vc))]
        args += list(conv)
    in_specs.append(pl.BlockSpec((tm, tn), lambda j, i: (i, j % per_b + kc)))
    args.append(kspec)
    return pl.pallas_call(
        functools.partial(_dft_mul_kernel, inv_len=1.0 / L, conv=conv is not None),
        grid=(nb * per_b, (2 * L) // tm),
        in_specs=in_specs,
        out_specs=pl.BlockSpec((tm, tn), lambda j, i: (i, j)),
        out_shape=jax.ShapeDtypeStruct((2 * L, nb * d), BF16),
        scratch_shapes=[pltpu.VMEM((L, tn), BF16)],
        compiler_params=_params(2, est),
        name="dft_mul",
    )(*args)


def _idft_kernel(x_ref, y_ref, *rest, z_conv, per):
    rest = list(rest)
    g_ref, gp_ref, gn_ref, gw_ref, gb_ref = rest[:5]
    rest = rest[5:]
    i = pl.program_id(1)
    first, last = i == 0, i == per - 1
    gate = _conv3_tile(g_ref[...], gp_ref[...], gn_ref[...], gw_ref[...], gb_ref[...], first, last)
    if z_conv:
        z_ref, zp_ref, zn_ref, zw_ref, zb_ref, skip_ref, o_ref = rest
        z = _conv3_tile(z_ref[...], zp_ref[...], zn_ref[...], zw_ref[...], zb_ref[...], first, last)
    else:
        z_ref, skip_ref, o_ref = rest
        z = z_ref[...]
    conv = jnp.dot(x_ref[...], y_ref[...], preferred_element_type=F32)
    o_ref[...] = (gate * (conv + z * skip_ref[...])).astype(o_ref.dtype)


def idft_gate(inv, y, gate_view, z_view, skip, conv, *, z_conv, L, nb, tm, tn, out_dtype):
    d = D_MODEL
    per_b = d // tn
    per = L // tm
    sub = tm // SUBLANE

    def tile_specs(view, with_conv):
        arr, row0, col0 = view
        r0, c0 = row0 // tm, col0 // tn
        last_blk = arr.shape[0] // SUBLANE - 1
        row = lambda j, i: (j // per_b) * per + i + r0
        col = lambda j, i: j % per_b + c0
        specs = [pl.BlockSpec((tm, tn), lambda j, i: (row(j, i), col(j, i)))]
        args = [arr]
        if with_conv:
            specs += [
                pl.BlockSpec((SUBLANE, tn),
                             lambda j, i: (jnp.maximum(row(j, i) * sub - 1, 0), col(j, i))),
                pl.BlockSpec((SUBLANE, tn),
                             lambda j, i: (jnp.minimum((row(j, i) + 1) * sub, last_blk), col(j, i))),
                pl.BlockSpec((HY_SHORT, tn), lambda j, i: (0, col(j, i))),
                pl.BlockSpec((1, tn), lambda j, i: (0, col(j, i)))]
            args += [arr, arr, conv[0], conv[1]]
        return specs, args

    g_specs, g_args = tile_specs(gate_view, True)
    z_specs, z_args = tile_specs(z_view, z_conv)
    est = (2 * _nbytes((tm, 2 * L), BF16) + 2 * _nbytes((2 * L, tn), BF16)
           + 12 * _nbytes((tm, tn), F32))
    return pl.pallas_call(
        functools.partial(_idft_kernel, z_conv=z_conv, per=per),
        grid=(nb * per_b, per),
        in_specs=[pl.BlockSpec((tm, 2 * L), lambda j, i: (i, 0)),
                  pl.BlockSpec((2 * L, tn), lambda j, i: (0, j)),
                  *g_specs, *z_specs,
                  pl.BlockSpec((1, tn), lambda j, i: (0, j % per_b))],
        out_specs=pl.BlockSpec((tm, tn), lambda j, i: ((j // per_b) * per + i, j % per_b)),
        out_shape=jax.ShapeDtypeStruct((nb * L, d), out_dtype),
        compiler_params=_params(2, est),
        name="idft_gate",
    )(inv, y, *g_args, *z_args, skip.reshape(1, d))


def hyena_stream(u, row0, L, nb, conv, filt, skip):
    d = D_MODEL
    tms = min(1024, 2 * L)
    tmi = min(512, L)
    k = hyena_filters(L, *filt)
    fwd, inv = dft_tables(L, tms)
    kspec = linear(fwd, k, tm=tms, tn=1024, name="dft_filters")
    y1 = dft_mul(fwd, (u, row0, 0), kspec, 0, L=L, nb=nb, tm=tms, tn=1024, conv=conv)
    z1 = idft_gate(inv, y1, (u, row0, d), (u, row0, 0), skip[0], conv, z_conv=True, L=L, nb=nb,
                   tm=tmi, tn=1024, out_dtype=F32)
    y2 = dft_mul(fwd, (z1, 0, 0), kspec, d, L=L, nb=nb, tm=tms, tn=1024)
    return idft_gate(inv, y2, (u, row0, 2 * d), (z1, 0, 0), skip[1], conv, z_conv=False, L=L,
                     nb=nb, tm=tmi, tn=1024, out_dtype=BF16)


def hyena_mix(a, n_rows, j, w_in, b_in, conv_w, conv_b, filt, skip):
    u = linear(a, w_in, e=j, tm=1024, tn=1024, bias=b_in, name="hyena_in")
    conv = (conv_w, conv_b.reshape(1, -1))
    z = hyena_stream(u, 0, SEQ, BATCH, conv, filt, skip)
    if n_rows > N_LAT:
        z_ctx = hyena_stream(u, N_LAT, CTX_LEN, BATCH, conv, filt, skip)
        z = jnp.concatenate([z, z_ctx], axis=0)
    return z


def _rms(x, g):
    return x * lax.rsqrt(jnp.mean(x * x, axis=-1, keepdims=True) + RMS_EPS) * g


def _mla_q_kernel(p_ref, g_ref, w1_ref, w2_ref, c_ref, s_ref, o_ref, w1b_ref, w2b_ref):
    @pl.when(pl.program_id(1) == 0)
    def _():
        w1b_ref[...] = w1_ref[...].astype(BF16)
        w2b_ref[...] = w2_ref[...].astype(BF16)

    xn = _rms(p_ref[...], g_ref[...]).astype(BF16)
    q1 = jnp.dot(xn, w1b_ref[...], preferred_element_type=F32)
    q2 = jnp.dot(xn, w2b_ref[...], preferred_element_type=F32)
    reps = q1.shape[1] // MLA_QK_PAD
    c = jnp.concatenate([c_ref[...]] * reps, axis=1)
    s = jnp.concatenate([s_ref[...]] * reps, axis=1)
    o_ref[...] = (q1 * c + q2 * s).astype(BF16)


def _mla_kv_kernel(p_ref, r_ref, g_ref, cs_ref, w_ref, o_ref, wb_ref):
    @pl.when(pl.program_id(1) == 0)
    def _():
        wb_ref[...] = w_ref[...].astype(BF16)

    xn = _rms(p_ref[...], g_ref[...]).astype(BF16)
    rr = (r_ref[...] * cs_ref[...]).astype(BF16)
    acc = (jnp.dot(xn, wb_ref[0:MLA_KV_RANK, :], preferred_element_type=F32)
           + jnp.dot(rr, wb_ref[MLA_KV_RANK:MLA_KV_RANK + 2 * MLA_ROPE, :],
                     preferred_element_type=F32))
    o_ref[...] = acc.astype(BF16)


def _attn_kernel(q_ref, kl_ref, kc_ref, vl_ref, vc_ref, o_ref, *, kchunk):
    c = (MLA_NOPE + MLA_ROPE) ** -0.5 * math.log2(math.e)
    nt = (((1,), (1,)), ((), ()))
    q = q_ref[...]
    chunks = [(kl_ref, vl_ref, r) for r in range(kl_ref.shape[0] // kchunk)]
    chunks += [(kc_ref, vc_ref, r) for r in range(kc_ref.shape[0] // kchunk)]
    m = l = acc = None
    for k_ref, v_ref, r in chunks:
        rows = slice(r * kchunk, (r + 1) * kchunk)
        s = lax.dot_general(q, k_ref[rows, :], nt, preferred_element_type=F32)
        ms = jnp.max(s, axis=-1, keepdims=True)
        m_new = ms if m is None else jnp.maximum(m, ms)
        p = jnp.exp2((s - m_new) * c)
        ps = jnp.sum(p, axis=-1, keepdims=True)
        pv = jnp.dot(p.astype(BF16), v_ref[rows, :], preferred_element_type=F32)
        if m is None:
            l, acc = ps, pv
        else:
            alpha = jnp.exp2((m - m_new) * c)
            l = alpha * l + ps
            acc = alpha * acc + pv
        m = m_new
    o_ref[...] = (acc / l).astype(o_ref.dtype)


def _rope_tables():
    rows = SEQ // GRID_W
    row = jnp.repeat(jnp.arange(rows), GRID_W).astype(F32)
    col = jnp.tile(jnp.arange(GRID_W), rows).astype(F32)
    n = MLA_ROPE // 4
    inv = ROPE_BASE ** (-jnp.arange(n, dtype=F32) / n)
    ang_row, ang_col = row[:, None] * inv, col[:, None] * inv
    cos = jnp.concatenate([jnp.cos(ang_row)] * 2 + [jnp.cos(ang_col)] * 2, axis=1)
    sin = jnp.concatenate([jnp.sin(ang_row)] * 2 + [jnp.sin(ang_col)] * 2, axis=1)
    return cos, sin


def _rope_partner(w):
    n = MLA_ROPE // 4
    return jnp.concatenate([-w[..., n:2 * n], w[..., 0:n], -w[..., 3 * n:4 * n],
                            w[..., 2 * n:3 * n]], axis=-1)


def mla_mix(a, w_in, q_norm, kv_norm, w_uq, w_ukv):
    H, NP, R, V, QK = MLA_HEADS, MLA_NOPE, MLA_ROPE, MLA_V, MLA_QK_PAD
    n_all = N_LAT + N_CTX
    cos, sin = _rope_tables()
    w_in_x = jnp.concatenate([w_in, _rope_partner(w_in[:, MLA_IN - R:])], axis=1)
    proj = linear(a, w_in_x, tm=512, tn=MLA_IN_PAD, name="mla_in")
    wq = w_uq.reshape(MLA_Q_RANK, H, NP + R)
    zq = jnp.zeros((MLA_Q_RANK, H, QK - NP - R), F32)
    wq1 = jnp.concatenate([wq, zq], axis=2).reshape(MLA_Q_RANK, H * QK)
    wq2 = jnp.concatenate([jnp.zeros((MLA_Q_RANK, H, NP), F32), _rope_partner(wq[..., NP:]), zq],
                          axis=2).reshape(MLA_Q_RANK, H * QK)
    ones, zeros = jnp.ones((SEQ, NP), F32), jnp.zeros((SEQ, QK - NP - R), F32)
    cq = jnp.concatenate([ones, cos, zeros], axis=1)
    sq = jnp.concatenate([0.0 * ones, sin, zeros], axis=1)
    tm, tn = 512, 1024
    per = SEQ // tm
    q = pl.pallas_call(
        _mla_q_kernel,
        grid=(H * QK // tn, N_LAT // tm),
        in_specs=[pl.BlockSpec((tm, MLA_Q_RANK), lambda jn, i: (i, 0)),
                  pl.BlockSpec((1, MLA_Q_RANK), lambda jn, i: (0, 0)),
                  pl.BlockSpec((MLA_Q_RANK, tn), lambda jn, i: (0, jn)),
                  pl.BlockSpec((MLA_Q_RANK, tn), lambda jn, i: (0, jn)),
                  pl.BlockSpec((tm, QK), lambda jn, i: (i % per, 0)),
                  pl.BlockSpec((tm, QK), lambda jn, i: (i % per, 0))],
        out_specs=pl.BlockSpec((tm, tn), lambda jn, i: (i, jn)),
        out_shape=jax.ShapeDtypeStruct((N_LAT, H * QK), BF16),
        scratch_shapes=[pltpu.VMEM((MLA_Q_RANK, tn), BF16), pltpu.VMEM((MLA_Q_RANK, tn), BF16)],
        compiler_params=_params(2, 6 * _nbytes((MLA_Q_RANK, tn), F32) + 8 * _nbytes((tm, tn), F32)),
        name="mla_q",
    )(proj, q_norm.reshape(1, MLA_Q_RANK), wq1, wq2, cq, sq)
    wkv = w_ukv.reshape(MLA_KV_RANK, H, NP + V)
    wk = jnp.concatenate([wkv[..., :NP], jnp.zeros((MLA_KV_RANK, H, QK - NP), F32)],
                         axis=2).reshape(MLA_KV_RANK, H * QK)
    place = jnp.concatenate([jnp.zeros((R, NP), F32), jnp.eye(R, dtype=F32),
                             jnp.zeros((R, QK - NP - R), F32)], axis=1)
    place = jnp.tile(jnp.concatenate([place, place], axis=0), (1, H))
    w_kv = jnp.concatenate([
        jnp.concatenate([wk, wkv[..., NP:].reshape(MLA_KV_RANK, H * V)], axis=1),
        jnp.concatenate([place, jnp.zeros((2 * R, H * V), F32)], axis=1)], axis=0)
    cs = jnp.concatenate([
        jnp.concatenate([cos, sin], axis=1),
        jnp.concatenate([jnp.ones((CTX_LEN, R), F32), jnp.zeros((CTX_LEN, R), F32)], axis=1)],
        axis=0)
    tm, tn = CTX_LEN, 2048
    per = SEQ // tm
    n_kv = H * (QK + V)
    kv = pl.pallas_call(
        _mla_kv_kernel,
        grid=(n_kv // tn, n_all // tm),
        in_specs=[pl.BlockSpec((tm, MLA_KV_RANK), lambda jn, i: (i, MLA_Q_RANK // MLA_KV_RANK)),
                  pl.BlockSpec((tm, 2 * R), lambda jn, i: (i, (MLA_IN - R) // (2 * R))),
                  pl.BlockSpec((1, MLA_KV_RANK), lambda jn, i: (0, 0)),
                  pl.BlockSpec((tm, 2 * R),
                               lambda jn, i: (jnp.where(i * tm < N_LAT, i % per, per), 0)),
                  pl.BlockSpec((MLA_KV_RANK + 2 * R, tn), lambda jn, i: (0, jn))],
        out_specs=pl.BlockSpec((tm, tn), lambda jn, i: (i, jn)),
        out_shape=jax.ShapeDtypeStruct((n_all, n_kv), BF16),
        scratch_shapes=[pltpu.VMEM((MLA_KV_RANK + 2 * R, tn), BF16)],
        compiler_params=_params(2, 4 * _nbytes((MLA_KV_RANK + 2 * R, tn), F32)
                                + 8 * _nbytes((tm, tn), F32)),
        name="mla_kv",
    )(proj, proj, kv_norm.reshape(1, MLA_KV_RANK), cs, w_kv)
    tq = 1024
    per = SEQ // tq
    v0 = H * QK // V
    ctx0 = N_LAT // CTX_LEN
    est = (2 * _nbytes((tq, QK), BF16) + 2 * _nbytes((SEQ + CTX_LEN, QK + V), BF16)
           + 16 * _nbytes((tq, CTX_LEN), F32))
    return pl.pallas_call(
        functools.partial(_attn_kernel, kchunk=CTX_LEN),
        grid=(BATCH, H, per),
        in_specs=[pl.BlockSpec((tq, QK), lambda b, h, i: (b * per + i, h)),
                  pl.BlockSpec((SEQ, QK), lambda b, h, i: (b, h)),
                  pl.BlockSpec((CTX_LEN, QK), lambda b, h, i: (ctx0 + b, h)),
                  pl.BlockSpec((SEQ, V), lambda b, h, i: (b, v0 + h)),
                  pl.BlockSpec((CTX_LEN, V), lambda b, h, i: (ctx0 + b, v0 + h))],
        out_specs=pl.BlockSpec((tq, V), lambda b, h, i: (b * per + i, h)),
        out_shape=jax.ShapeDtypeStruct((N_LAT, H * V), BF16),
        compiler_params=_params(3, est),
        name="mla_attention",
    )(q, kv, kv, kv, kv)


def dense_ffn(a, f, wg, wu, wd):
    p = swiglu_up([a], wg, wu, e=f, tm=1024, tf=512, name="ffn_up")
    return linear(p, wd, e=f, tm=512, tn=512, out_dtype=BF16, name="ffn_down")


def _router_kernel(h_ref, sc_ref, sh_ref, whi_ref, wlo_ref, o_ref):
    a = h_ref[...] * (1.0 + sc_ref[...]) + sh_ref[...]
    a_hi, a_lo = _split_bf16(a)
    w_hi = whi_ref[...]
    o_ref[...] = (jnp.dot(a_hi, w_hi, preferred_element_type=F32)
                  + jnp.dot(a_lo, w_hi, preferred_element_type=F32)
                  + jnp.dot(a_hi, wlo_ref[...], preferred_element_type=F32))


def router_logits(h, n_rows, mods, w_router, *, tm=256):
    w = jnp.pad(w_router, ((0, 0), (0, LANE - N_EXPERTS)))
    w_hi, w_lo = _split_bf16(w)
    row = pl.BlockSpec((tm, D_MODEL), lambda i: (i, 0))
    wspec = pl.BlockSpec((D_MODEL, LANE), lambda i: (0, 0))
    out = pl.pallas_call(
        _router_kernel,
        grid=(n_rows // tm,),
        in_specs=[row, _mod_spec(4, tm), _mod_spec(3, tm), wspec, wspec],
        out_specs=pl.BlockSpec((tm, LANE), lambda i: (i, 0)),
        out_shape=jax.ShapeDtypeStruct((n_rows, LANE), F32),
        compiler_params=_params(1, 8 * _nbytes((tm, D_MODEL), F32)),
        name="moe_router",
    )(h, mods, mods, w_hi, w_lo)
    return out[:, :N_EXPERTS]


def moe_ffn(h, a_rows, n_rows, mods, f, w_router, wg, wu, wd, *, tm=512):
    logits = router_logits(h, n_rows, mods, w_router[f])
    top_v, top_i = lax.top_k(logits, TOP_K)
    gates = jax.nn.softmax(top_v, axis=-1)
    n_assign = n_rows * TOP_K
    n_tiles = n_assign // tm + N_EXPERTS
    flat_e = top_i.reshape(n_assign)
    onehot = (flat_e[:, None] == jnp.arange(N_EXPERTS)[None, :]).astype(jnp.int32)
    csum = jnp.cumsum(onehot, axis=0)
    rank = jnp.sum(csum * onehot, axis=1) - 1
    counts = csum[-1]
    padded = ((counts + tm - 1) // tm) * tm
    ends = jnp.cumsum(padded)
    starts = ends - padded
    pos = starts[flat_e] + rank
    src = (jnp.arange(n_tiles * tm, dtype=jnp.int32) % n_rows).at[pos].set(
        jnp.arange(n_assign, dtype=jnp.int32) // TOP_K)
    tile_row0 = jnp.arange(n_tiles, dtype=jnp.int32) * tm
    te = jnp.minimum(jnp.sum((tile_row0[:, None] >= ends[None, :]).astype(jnp.int32), axis=1),
                     N_EXPERTS - 1)
    nv = (ends[-1] // tm).astype(jnp.int32).reshape(1)
    te = jnp.where(tile_row0 < ends[-1], te, te[jnp.maximum(nv[0] - 1, 0)]) + f * N_EXPERTS
    if f == 0:
        x_parts = [a_rows.at[src].get(mode="promise_in_bounds")]
    else:
        half = D_MODEL // 2
        x_parts = [a_rows[:, :half].at[src].get(mode="promise_in_bounds"),
                   a_rows[:, half:].at[src].get(mode="promise_in_bounds")]
    stack = lambda w: w.reshape((-1,) + w.shape[2:])
    p = swiglu_up(x_parts, stack(wg), stack(wu), tm=tm, tf=1024, te=te, nv=nv, name="moe_up")
    out = linear(p, stack(wd), tm=tm, tn=1024, te=te, nv=nv, name="moe_down")
    pos = pos.reshape(n_rows, TOP_K)
    ys = [out.at[pos[:, k]].get(mode="promise_in_bounds") for k in range(TOP_K)]
    return ys, [gates[:, k:k + 1] for k in range(TOP_K)]


def kernel(x, c, ctx, c_ctx, w_mod, b_mod, ln_g, ln_b, pool_w, pool_scale, hy_w_in, hy_b_in, hy_conv_w, hy_conv_b, hy_f_w1, hy_f_b1, hy_f_w2, hy_f_b2, hy_f_w3, hy_f_b3, hy_f_freq, hy_f_wout, hy_skip, hy_w_out, hy_b_out, mla_w_in, mla_q_norm, mla_kv_norm, mla_w_uq, mla_w_ukv, mla_w_o, ffn_w_gate, ffn_w_up, ffn_w_down, moe_w_router, moe_w_gate, moe_w_up, moe_w_down):
    attn_layers = [i for i in range(DEPTH) if i % N_MIXERS == 2]
    last_read = attn_layers[-1] if attn_layers else -1

    c8 = jnp.concatenate([c, c_ctx[None], jnp.zeros((MOD_ROWS - BATCH - 1, D_MODEL), F32)], axis=0)
    mods = modulation_table(c8, w_mod, b_mod)

    h = jnp.concatenate([x.reshape(N_LAT, D_MODEL), ctx.reshape(N_CTX, D_MODEL)], axis=0)
    a = None
    for i in range(DEPTH):
        kind, j = i % N_MIXERS, i // N_MIXERS
        ctx_out = i < last_read
        n_rows = N_LAT + N_CTX if ctx_out else N_LAT
        m = mods[i]
        moe = i % 2 == 1
        if kind == 0:
            tl = CTX_LEN if n_rows > N_LAT else 512
            h1, a2 = pool_layer(h, n_rows, m, pool_w[j].astype(BF16), pool_scale[j],
                                ln_g[i, 0], ln_b[i, 0], tl=tl, pack=moe)
        else:
            if kind == 1:
                filt = (hy_f_w1[j], hy_f_b1[j], hy_f_w2[j], hy_f_b2[j], hy_f_w3[j], hy_f_b3[j],
                        hy_f_freq[j], hy_f_wout[j])
                z = hyena_mix(a[:n_rows], n_rows, j, hy_w_in, hy_b_in, hy_conv_w[j], hy_conv_b[j],
                              filt, hy_skip[j])
                w_o, b_o = hy_w_out, hy_b_out
            else:
                assert i <= last_read and not ctx_out and a.shape[0] == N_LAT + N_CTX
                z = mla_mix(a, mla_w_in[j], mla_q_norm[j], mla_kv_norm[j], mla_w_uq[j],
                            mla_w_ukv[j])
                w_o, b_o = mla_w_o, None
            h1, a2 = linear_resid_ln(z, w_o, j, b_o, h, n_rows, m, 2, ln_g[i, 0], ln_b[i, 0], 4, 3,
                                     pack=moe)
        f = i // 2
        nm = mods[min(i + 1, DEPTH - 1)]
        if moe:
            ys, row_gates = moe_ffn(h1, a2, n_rows, m, f, moe_w_router, moe_w_gate, moe_w_up,
                                    moe_w_down)
        else:
            ys, row_gates = [dense_ffn(a2, f, ffn_w_gate, ffn_w_up, ffn_w_down)], None
        h, a = resid_ln(h1, ys, n_rows, m, 5, ln_g[i, 1], ln_b[i, 1], nm, 1, 0,
                        row_gates=row_gates)
    return h[:N_LAT].reshape(BATCH, SEQ, D_MODEL)
```

```python
import functools
import math

import jax
import jax.numpy as jnp
from jax import lax
from jax.experimental import pallas as pl
from jax.experimental.pallas import tpu as pltpu

F32 = jnp.float32
BF16 = jnp.bfloat16
U32 = jnp.uint32

D_MODEL = 2048
BATCH = 4
SEQ = 2048
DEPTH = 4
GRID_W = 64
CTX_LEN = 256
N_MIXERS = 3
POOL_WINDOWS = (2, 4, 8, 16)
POOL_GROUP = D_MODEL // len(POOL_WINDOWS)
HY_ORDER = 2
HY_SHORT = 3
HY_EMB = 33
HY_FILTER = 64
HY_DECAY_TARGET = 1e-2
HY_FAST = 0.3
HY_SLOW = 1.5
MLA_HEADS = D_MODEL // 128
MLA_Q_RANK = 512
MLA_KV_RANK = 512
MLA_NOPE = 128
MLA_ROPE = 64
MLA_V = 128
ROPE_BASE = 10000.0
D_FF = 5632
N_EXPERTS = 8
TOP_K = 2
EXPERT_FF = 2 * D_MODEL
LN_EPS = 1e-5
RMS_EPS = 1e-6
DN_ALPHA = (2 * DEPTH) ** 0.25

N_LAT = BATCH * SEQ
N_CTX = BATCH * CTX_LEN
MOD_ROWS = 8
CTX_MOD_ROW = BATCH
LANE = 128
SUBLANE = 8
VMEM_CAP = 60 * 1024 * 1024
MLA_QK_PAD = 256
MLA_IN = MLA_Q_RANK + MLA_KV_RANK + MLA_ROPE
MLA_IN_PAD = MLA_IN + MLA_ROPE


def _vmem(nbytes):
    return int(min(VMEM_CAP, max(16 * 1024 * 1024, nbytes * 3 // 2)))


def _nbytes(shape, dtype):
    return math.prod(shape) * jnp.dtype(dtype).itemsize


def _mod_row(i, tm):
    return jnp.minimum((i * tm) // SEQ, CTX_MOD_ROW)


def _mod_spec(which, tm):
    return pl.BlockSpec((None, 1, D_MODEL), lambda i: (_mod_row(i, tm) * 6 + which, 0, 0))


def _params(sem, est):
    return pltpu.CompilerParams(dimension_semantics=("arbitrary",) * sem,
                                vmem_limit_bytes=_vmem(est))


def _mod_kernel(c_ref, w_ref, b_ref, o_ref):
    c = c_ref[...]
    a = c / (1.0 + jnp.exp(-c))
    o_ref[...] = jnp.dot(a.astype(BF16), w_ref[...].astype(BF16),
                         preferred_element_type=F32) + b_ref[...]


def modulation_table(c8, w_mod, b_mod):
    tn = 1024
    n = 6 * D_MODEL
    out = pl.pallas_call(
        _mod_kernel,
        grid=(DEPTH, n // tn),
        in_specs=[pl.BlockSpec((MOD_ROWS, D_MODEL), lambda l, j: (0, 0)),
                  pl.BlockSpec((None, D_MODEL, tn), lambda l, j: (l, 0, j)),
                  pl.BlockSpec((None, 1, tn), lambda l, j: (l, 0, j))],
        out_specs=pl.BlockSpec((None, MOD_ROWS, tn), lambda l, j: (l, 0, j)),
        out_shape=jax.ShapeDtypeStruct((DEPTH, MOD_ROWS, n), F32),
        compiler_params=_params(2, 3 * _nbytes((D_MODEL, tn), F32)),
        name="modulation_table",
    )(c8, w_mod, b_mod.reshape(DEPTH, 1, n))
    return out.reshape(DEPTH, MOD_ROWS * 6, 1, D_MODEL)


def _linear_kernel(xt_ref, te_ref, nv_ref, x_ref, w_ref, *rest, has_bias, cast_w):
    rest = list(rest)
    b_ref = rest.pop(0) if has_bias else None
    o_ref = rest.pop(0)
    wbf_ref = rest.pop(0) if cast_w else w_ref
    i = pl.program_id(1)

    if cast_w:
        prev = te_ref[jnp.maximum(i - 1, 0)]

        @pl.when(jnp.logical_or(i == 0, te_ref[i] != prev))
        def _():
            wbf_ref[...] = w_ref[...].astype(BF16)

    @pl.when(i < nv_ref[0])
    def _():
        acc = jnp.dot(x_ref[...].astype(BF16), wbf_ref[...], preferred_element_type=F32)
        if has_bias:
            acc = acc + b_ref[...]
        o_ref[...] = acc.astype(o_ref.dtype)

    @pl.when(i >= nv_ref[0])
    def _():
        o_ref[...] = jnp.zeros_like(o_ref)


def linear(x, w, *, tm, tn, e=0, n_out=None, n_tiles=None, xt=None, te=None, nv=None, bias=None,
           out_dtype=F32, w_col0=0, name="linear"):
    k = x.shape[1]
    if w.ndim == 2:
        w = w[None]
    if n_out is None:
        n_out = w.shape[2] - w_col0
    assert x.shape[0] % tm == 0 and n_out % tn == 0 and w_col0 % tn == 0 and w.shape[1] == k
    if n_tiles is None:
        n_tiles = x.shape[0] // tm
    if xt is None:
        xt = jnp.arange(n_tiles, dtype=jnp.int32)
    if te is None:
        te = jnp.full((n_tiles,), e, jnp.int32)
    if nv is None:
        nv = jnp.full((1,), n_tiles, jnp.int32)
    cast_w = w.dtype != BF16
    has_bias = bias is not None
    cb = w_col0 // tn
    in_specs = [pl.BlockSpec((tm, k), lambda j, i, xt, te, nv: (xt[i], 0)),
                pl.BlockSpec((None, k, tn), lambda j, i, xt, te, nv: (te[i], 0, j + cb))]
    args = [x, w]
    if has_bias:
        bias = bias.reshape(-1, 1, bias.shape[-1])
        in_specs.append(pl.BlockSpec((None, 1, tn), lambda j, i, xt, te, nv: (te[i], 0, j)))
        args.append(bias)
    scratch = [pltpu.VMEM((k, tn), BF16)] if cast_w else []
    est = (2 * _nbytes((tm, k), x.dtype) + 2 * _nbytes((k, tn), w.dtype)
           + (_nbytes((k, tn), BF16) if cast_w else 0) + _nbytes((tm, k), BF16)
           + 2 * _nbytes((tm, tn), out_dtype) + 2 * _nbytes((tm, tn), F32))
    return pl.pallas_call(
        functools.partial(_linear_kernel, has_bias=has_bias, cast_w=cast_w),
        grid_spec=pltpu.PrefetchScalarGridSpec(
            num_scalar_prefetch=3,
            grid=(n_out // tn, n_tiles),
            in_specs=in_specs,
            out_specs=pl.BlockSpec((tm, tn), lambda j, i, xt, te, nv: (i, j)),
            scratch_shapes=scratch),
        out_shape=jax.ShapeDtypeStruct((n_tiles * tm, n_out), out_dtype),
        compiler_params=_params(2, est),
        name=name,
    )(xt, te, nv, *args)


def _swiglu_kernel(te_ref, nv_ref, *refs, n_parts):
    x_refs = refs[:n_parts]
    wg_ref, wu_ref, o_ref, wgb_ref, wub_ref = refs[n_parts:]
    i = pl.program_id(1)
    prev = te_ref[jnp.maximum(i - 1, 0)]

    @pl.when(jnp.logical_or(i == 0, te_ref[i] != prev))
    def _():
        wgb_ref[...] = wg_ref[...].astype(BF16)
        wub_ref[...] = wu_ref[...].astype(BF16)

    @pl.when(i < nv_ref[0])
    def _():
        g = u = None
        k0 = 0
        for x_ref in x_refs:
            x = x_ref[...].astype(BF16)
            rows = slice(k0, k0 + x.shape[1])
            gp = jnp.dot(x, wgb_ref[rows, :], preferred_element_type=F32)
            up = jnp.dot(x, wub_ref[rows, :], preferred_element_type=F32)
            g, u = (gp, up) if g is None else (g + gp, u + up)
            k0 += x.shape[1]
        o_ref[...] = ((g / (1.0 + jnp.exp(-g))) * u).astype(o_ref.dtype)

    @pl.when(i >= nv_ref[0])
    def _():
        o_ref[...] = jnp.zeros_like(o_ref)


def swiglu_up(x_parts, wg, wu, *, tm, tf, e=0, te=None, nv=None, name="swiglu_up"):
    m = x_parts[0].shape[0]
    k, f = wg.shape[1], wg.shape[2]
    assert sum(x.shape[1] for x in x_parts) == k and m % tm == 0 and f % tf == 0
    n_tiles = m // tm
    if te is None:
        te = jnp.full((n_tiles,), e, jnp.int32)
    if nv is None:
        nv = jnp.full((1,), n_tiles, jnp.int32)
    wspec = pl.BlockSpec((None, k, tf), lambda j, i, te, nv: (te[i], 0, j))
    est = (2 * _nbytes((tm, k), x_parts[0].dtype) + _nbytes((tm, k), BF16)
           + 4 * _nbytes((k, tf), F32) + 2 * _nbytes((k, tf), BF16) + 2 * _nbytes((tm, tf), BF16)
           + 3 * _nbytes((tm, tf), F32))
    x_specs = [pl.BlockSpec((tm, x.shape[1]), lambda j, i, te, nv: (i, 0)) for x in x_parts]
    return pl.pallas_call(
        functools.partial(_swiglu_kernel, n_parts=len(x_parts)),
        grid_spec=pltpu.PrefetchScalarGridSpec(
            num_scalar_prefetch=2,
            grid=(f // tf, n_tiles),
            in_specs=[*x_specs, wspec, wspec],
            out_specs=pl.BlockSpec((tm, tf), lambda j, i, te, nv: (i, j)),
            scratch_shapes=[pltpu.VMEM((k, tf), BF16), pltpu.VMEM((k, tf), BF16)]),
        out_shape=jax.ShapeDtypeStruct((m, f), BF16),
        compiler_params=_params(2, est),
        name=name,
    )(te, nv, *x_parts, wg, wu)


def _ln_mod(v, g, b, sc, sh, pack):
    mu = jnp.mean(v, axis=-1, keepdims=True)
    d = v - mu
    var = jnp.mean(d * d, axis=-1, keepdims=True)
    h = d * lax.rsqrt(var + LN_EPS) * g + b
    a = h * (1.0 + sc) + sh
    return h, (a if pack else a.astype(BF16))


def _resid_ln_kernel(*refs, n_y, pack):
    h_ref = refs[0]
    y_refs = refs[1:1 + n_y]
    rest = refs[1 + n_y:]
    if n_y == 2:
        g0_ref, g1_ref = rest[0], rest[1]
        rest = rest[2:]
        y = g0_ref[...] * y_refs[0][...] + g1_ref[...] * y_refs[1][...]
    else:
        y = y_refs[0][...]
    gate_ref, g_ref, b_ref, sc_ref, sh_ref, o_ref, a_ref = rest
    v = DN_ALPHA * h_ref[...] + gate_ref[...] * y
    h, a = _ln_mod(v, g_ref[...], b_ref[...], sc_ref[...], sh_ref[...], pack)
    o_ref[...] = h
    a_ref[...] = a


def resid_ln(h, ys, n_rows, mods, gate_idx, ln_g, ln_b, nmods, sc_idx, sh_idx, *, row_gates=None,
             pack=False, tm=256):
    assert n_rows % tm == 0
    row = pl.BlockSpec((tm, D_MODEL), lambda i: (i, 0))
    col = pl.BlockSpec((tm, 1), lambda i: (i, 0))
    vec = pl.BlockSpec((1, D_MODEL), lambda i: (0, 0))
    a_w = D_MODEL
    a_dt = F32 if pack else BF16
    est = 10 * _nbytes((tm, D_MODEL), F32)
    in_specs = [row] + [row] * len(ys) + ([col, col] if row_gates else []) + [
        _mod_spec(gate_idx, tm), vec, vec, _mod_spec(sc_idx, tm), _mod_spec(sh_idx, tm)]
    args = [h, *ys, *(row_gates or []), mods, ln_g.reshape(1, D_MODEL), ln_b.reshape(1, D_MODEL),
            nmods, nmods]
    return pl.pallas_call(
        functools.partial(_resid_ln_kernel, n_y=len(ys), pack=pack),
        grid=(n_rows // tm,),
        in_specs=in_specs,
        out_specs=[row, pl.BlockSpec((tm, a_w), lambda i: (i, 0))],
        out_shape=[jax.ShapeDtypeStruct((n_rows, D_MODEL), F32),
                   jax.ShapeDtypeStruct((n_rows, a_w), a_dt)],
        compiler_params=_params(1, est),
        name="resid_ln",
    )(*args)


def _split_bf16(x):
    hi = x.astype(BF16)
    lo = (x - hi.astype(F32)).astype(BF16)
    return hi, lo


def _router_logits(a, whi_ref, wlo_ref):
    a_hi, a_lo = _split_bf16(a)
    w_hi = whi_ref[...]
    return (jnp.dot(a_hi, w_hi, preferred_element_type=F32)
            + jnp.dot(a_lo, w_hi, preferred_element_type=F32)
            + jnp.dot(a_hi, wlo_ref[...], preferred_element_type=F32))


def _router_operands(w_router):
    w = jnp.pad(w_router, ((0, 0), (0, LANE - N_EXPERTS)))
    return _split_bf16(w), pl.BlockSpec((D_MODEL, LANE), lambda i: (0, 0))


def _linear_ln_kernel(x_ref, w_ref, *rest, has_bias, moe):
    rest = list(rest)
    bias_ref = rest.pop(0) if has_bias else None
    h_ref, gate_ref, g_ref, b_ref, sc_ref, sh_ref = rest[:6]
    rest = rest[6:]
    if moe:
        whi_ref, wlo_ref, o_ref, a_ref, l_ref, wb_ref = rest
    else:
        o_ref, a_ref, wb_ref = rest

    @pl.when(pl.program_id(0) == 0)
    def _():
        wb_ref[...] = w_ref[...].astype(BF16)

    y = jnp.dot(x_ref[...], wb_ref[...], preferred_element_type=F32)
    if has_bias:
        y = y + bias_ref[...]
    v = DN_ALPHA * h_ref[...] + gate_ref[...] * y
    h, a = _ln_mod(v, g_ref[...], b_ref[...], sc_ref[...], sh_ref[...], moe)
    o_ref[...] = h
    a_ref[...] = a
    if moe:
        l_ref[...] = _router_logits(a, whi_ref, wlo_ref)


def linear_resid_ln(x, w, e, bias, h, n_rows, mods, gate_idx, ln_g, ln_b, sc_idx, sh_idx, *,
                    w_router=None, tm=256):
    k = x.shape[1]
    moe = w_router is not None
    row = pl.BlockSpec((tm, D_MODEL), lambda i: (i, 0))
    vec = pl.BlockSpec((1, D_MODEL), lambda i: (0, 0))
    in_specs = [pl.BlockSpec((tm, k), lambda i: (i, 0)),
                pl.BlockSpec((None, k, D_MODEL), lambda i: (e, 0, 0),
                             pipeline_mode=pl.Buffered(1))]
    args = [x, w]
    if bias is not None:
        in_specs.append(pl.BlockSpec((None, 1, D_MODEL), lambda i: (e, 0, 0)))
        args.append(bias.reshape(-1, 1, D_MODEL))
    in_specs += [row, _mod_spec(gate_idx, tm), vec, vec, _mod_spec(sc_idx, tm),
                 _mod_spec(sh_idx, tm)]
    args += [h, mods, ln_g.reshape(1, D_MODEL), ln_b.reshape(1, D_MODEL), mods, mods]
    out_specs = [row, row]
    out_shape = [jax.ShapeDtypeStruct((n_rows, D_MODEL), F32),
                 jax.ShapeDtypeStruct((n_rows, D_MODEL), F32 if moe else BF16)]
    if moe:
        (w_hi, w_lo), wspec = _router_operands(w_router)
        in_specs += [wspec, wspec]
        args += [w_hi, w_lo]
        out_specs.append(pl.BlockSpec((tm, LANE), lambda i: (i, 0)))
        out_shape.append(jax.ShapeDtypeStruct((n_rows, LANE), F32))
    est = (_nbytes((k, D_MODEL), F32) + _nbytes((k, D_MODEL), BF16) + 2 * _nbytes((tm, k), BF16)
           + 12 * _nbytes((tm, D_MODEL), F32))
    return pl.pallas_call(
        functools.partial(_linear_ln_kernel, has_bias=bias is not None, moe=moe),
        grid=(n_rows // tm,),
        in_specs=in_specs,
        out_specs=out_specs,
        out_shape=out_shape,
        scratch_shapes=[pltpu.VMEM((k, D_MODEL), BF16)],
        compiler_params=_params(1, est),
        name="linear_resid_ln",
    )(*args)


def _seq_pos(i, tl):
    row0 = i * tl
    is_lat = row0 < N_LAT
    pos = jnp.where(is_lat, row0 % SEQ, (row0 - N_LAT) % CTX_LEN)
    seqlen = jnp.where(is_lat, SEQ, CTX_LEN)
    return pos, seqlen


def _halo_specs(tl, width, n_rows, col_index=None):
    per = tl // SUBLANE
    last = n_rows // SUBLANE - 1
    if col_index is None:
        prev = pl.BlockSpec((SUBLANE, width), lambda i: (jnp.maximum(i * per - 1, 0), 0))
        nxt = pl.BlockSpec((SUBLANE, width), lambda i: (jnp.minimum((i + 1) * per, last), 0))
    else:
        prev = pl.BlockSpec((SUBLANE, width),
                            lambda i, j: (jnp.maximum(i * per - 1, 0), col_index(j)))
        nxt = pl.BlockSpec((SUBLANE, width),
                           lambda i, j: (jnp.minimum((i + 1) * per, last), col_index(j)))
    return prev, nxt


def _shift_rows(x, s):
    n = x.shape[0]
    return pltpu.roll(x, s % n, 0)


def _pool_kernel(h_ref, hp_ref, hn_ref, sc_ref, sh_ref, gate_ref, w_ref, ps_ref, g_ref, b_ref,
                 nsc_ref, nsh_ref, *rest, tl, moe):
    if moe:
        whi_ref, wlo_ref, o_ref, a_ref, l_ref = rest
    else:
        o_ref, a_ref = rest
    i = pl.program_id(0)
    pos, seqlen = _seq_pos(i, tl)
    sc = 1.0 + sc_ref[...]
    sh = sh_ref[...]
    h = h_ref[...]
    a = h * sc + sh
    keep_prev = (pos > 0).astype(F32)
    keep_next = (pos + tl < seqlen).astype(F32)
    ext = jnp.concatenate([(hp_ref[...] * sc + sh) * keep_prev, a,
                           (hn_ref[...] * sc + sh) * keep_next], axis=0)
    t = pos + lax.broadcasted_iota(jnp.int32, (tl, 1), 0)
    n_ext = tl + 2 * SUBLANE
    parts = []
    for g, w in enumerate(POOL_WINDOWS):
        cols = slice(g * POOL_GROUP, (g + 1) * POOL_GROUP)
        s = ext[:, cols]
        s = s + _shift_rows(s, 1)
        r = 1
        while 2 * r < w:
            s = _shift_rows(s, n_ext - r) + _shift_rows(s, r)
            r *= 2
        s = s[SUBLANE:SUBLANE + tl]
        cnt = (jnp.minimum(t + w // 2, seqlen) - jnp.maximum(t - w // 2, 0)).astype(F32)
        d = s / cnt - a[:, cols]
        parts.append(jnp.dot(d.astype(BF16), w_ref[g], preferred_element_type=F32))
    y = jnp.concatenate(parts, axis=1) * ps_ref[...]
    v = DN_ALPHA * h + gate_ref[...] * y
    hn, an = _ln_mod(v, g_ref[...], b_ref[...], nsc_ref[...], nsh_ref[...], moe)
    o_ref[...] = hn
    a_ref[...] = an
    if moe:
        l_ref[...] = _router_logits(an, whi_ref, wlo_ref)


def pool_layer(h, n_rows, mods, w_grp_bf16, pool_scale, ln_g, ln_b, *, tl, w_router=None):
    moe = w_router is not None
    row = pl.BlockSpec((tl, D_MODEL), lambda i: (i, 0))
    vec = pl.BlockSpec((1, D_MODEL), lambda i: (0, 0))
    prev, nxt = _halo_specs(tl, D_MODEL, n_rows)
    in_specs = [row, prev, nxt, _mod_spec(1, tl), _mod_spec(0, tl), _mod_spec(2, tl),
                pl.BlockSpec(w_grp_bf16.shape, lambda i: (0, 0, 0)), vec, vec, vec,
                _mod_spec(4, tl), _mod_spec(3, tl)]
    args = [h, h, h, mods, mods, mods, w_grp_bf16, pool_scale.reshape(1, D_MODEL),
            ln_g.reshape(1, D_MODEL), ln_b.reshape(1, D_MODEL), mods, mods]
    out_specs = [row, row]
    out_shape = [jax.ShapeDtypeStruct((n_rows, D_MODEL), F32),
                 jax.ShapeDtypeStruct((n_rows, D_MODEL), F32 if moe else BF16)]
    if moe:
        (w_hi, w_lo), wspec = _router_operands(w_router)
        in_specs += [wspec, wspec]
        args += [w_hi, w_lo]
        out_specs.append(pl.BlockSpec((tl, LANE), lambda i: (i, 0)))
        out_shape.append(jax.ShapeDtypeStruct((n_rows, LANE), F32))
    est = 14 * _nbytes((tl, D_MODEL), F32) + 2 * _nbytes(w_grp_bf16.shape, BF16)
    return pl.pallas_call(
        functools.partial(_pool_kernel, tl=tl, moe=moe),
        grid=(n_rows // tl,),
        in_specs=in_specs,
        out_specs=out_specs,
        out_shape=out_shape,
        compiler_params=_params(1, est),
        name="pool_layer",
    )(*args)


def _conv3_seq(u, w, b):
    n = u.shape[0]
    t = lax.broadcasted_iota(jnp.int32, (n, 1), 0)
    before = jnp.where(t == 0, 0.0, _shift_rows(u, 1))
    after = jnp.where(t == n - 1, 0.0, _shift_rows(u, n - 1))
    return b + before * w[0:1] + u * w[1:2] + after * w[2:3]


def _conv3_tile(u, prev, nxt, w, b, first, last):
    n = u.shape[0]
    ext = jnp.concatenate([prev * (1.0 - first.astype(F32)), u, nxt * (1.0 - last.astype(F32))],
                          axis=0)
    before = _shift_rows(ext, 1)[SUBLANE:SUBLANE + n]
    after = _shift_rows(ext, n + 2 * SUBLANE - 1)[SUBLANE:SUBLANE + n]
    return b + before * w[0:1] + u * w[1:2] + after * w[2:3]


def _dot3(a, b):
    a_hi, a_lo = _split_bf16(a)
    b_hi, b_lo = _split_bf16(b)
    return (jnp.dot(a_hi, b_hi, preferred_element_type=F32)
            + jnp.dot(a_lo, b_hi, preferred_element_type=F32)
            + jnp.dot(a_hi, b_lo, preferred_element_type=F32))


def _filter_kernel(z_ref, w1_ref, b1_ref, w2_ref, b2_ref, w3_ref, b3_ref, fr_ref, wo_ref,
                   dist_ref, delta_ref, o_ref, a_ref):
    @pl.when(pl.program_id(0) == 0)
    def _():
        fr = fr_ref[...]
        a = jnp.sin(fr * (_dot3(z_ref[...], w1_ref[...]) + b1_ref[...]))
        a = jnp.sin(fr * (_dot3(a, w2_ref[...]) + b2_ref[...]))
        a_ref[...] = jnp.sin(fr * (_dot3(a, w3_ref[...]) + b3_ref[...]))

    k = _dot3(a_ref[...], wo_ref[...])
    k = k * jnp.exp(-dist_ref[...] * delta_ref[...])
    o_ref[...] = k * lax.rsqrt(jnp.sum(k * k, axis=0, keepdims=True) + 1e-6)


def _pad2(x, rows, cols):
    return jnp.pad(x, ((0, rows - x.shape[0]), (0, cols - x.shape[1])))


def hyena_filters(L, f_w1, f_b1, f_w2, f_b2, f_w3, f_b3, f_freq, f_wout, *, tn=512):
    bands = (HY_EMB - 1) // 2
    t = jnp.linspace(0.0, 1.0, L, dtype=F32)[:, None]
    wpos = 2.0 * math.pi * jnp.arange(L, dtype=F32)[:, None] / L
    f = jnp.linspace(1e-4, bands - 1, bands, dtype=F32)[None, :]
    z = jnp.concatenate([t, jnp.cos(f * wpos), -jnp.sin(f * wpos)], axis=-1)
    dist = (jnp.abs(jnp.arange(L) - L // 2).astype(F32) / max(L // 2, 1))[:, None]
    deltas = jnp.linspace(math.log(HY_DECAY_TARGET) / HY_SLOW, math.log(HY_DECAY_TARGET) / HY_FAST,
                          D_MODEL, dtype=F32)
    absdelta = jnp.tile(jnp.abs(deltas), HY_ORDER)[None, :]
    P = LANE
    n = HY_ORDER * D_MODEL
    full = lambda shape: pl.BlockSpec(shape, lambda j: (0, 0))
    args = [_pad2(z, L, P), _pad2(f_w1, P, P), _pad2(f_b1[None], 1, P), _pad2(f_w2, P, P),
            _pad2(f_b2[None], 1, P), _pad2(f_w3, P, P), _pad2(f_b3[None], 1, P),
            _pad2(f_freq[None], 1, P), _pad2(f_wout, P, n), dist, absdelta]
    in_specs = [full((L, P)), full((P, P)), full((1, P)), full((P, P)), full((1, P)), full((P, P)),
                full((1, P)), full((1, P)), pl.BlockSpec((P, tn), lambda j: (0, j)),
                full((L, 1)), pl.BlockSpec((1, tn), lambda j: (0, j))]
    return pl.pallas_call(
        _filter_kernel,
        grid=(n // tn,),
        in_specs=in_specs,
        out_specs=pl.BlockSpec((L, tn), lambda j: (0, j)),
        out_shape=jax.ShapeDtypeStruct((L, n), F32),
        scratch_shapes=[pltpu.VMEM((L, P), F32)],
        compiler_params=_params(1, 8 * _nbytes((L, tn), F32)),
        name="hyena_filters",
    )(*args)


def dft_tables(L, tm):
    n_fft = 2 * L
    half = tm // 2
    r_idx = jnp.arange(2 * L, dtype=jnp.int32)
    odd = 2 * ((r_idx // tm) * half + (r_idx % half)) + 1
    is_sin = ((r_idx % tm) >= half)[:, None]

    def trig(k):
        ang = (k % (2 * n_fft)).astype(F32) * (math.pi / n_fft)
        return jnp.cos(ang), jnp.sin(ang)

    lo = 64
    s1 = jnp.arange(L // lo, dtype=jnp.int32) * lo
    s0 = jnp.arange(lo, dtype=jnp.int32)
    cb, sb = trig(odd[:, None] * s0[None, :])

    def combine(ca, sa):
        a1 = jnp.where(is_sin, sa, ca)
        a2 = jnp.where(is_sin, ca, -sa)
        return a1, a2

    a1, a2 = combine(*trig(odd[:, None] * s1[None, :]))
    fwd = (a1[:, :, None] * cb[:, None, :] + a2[:, :, None] * sb[:, None, :]).reshape(2 * L, L)
    a1, a2 = combine(*trig(odd[:, None] * (s1[None, :] + L // 2)))
    inv = (a1.T[:, None, :] * cb.T[None, :, :] + a2.T[:, None, :] * sb.T[None, :, :]).reshape(
        L, 2 * L)
    return fwd.astype(BF16), inv.astype(BF16)


def _dft_mul_kernel(x_ref, w_ref, *rest, inv_len, conv):
    if conv:
        cw_ref, cb_ref, k_ref, o_ref, wb_ref = rest
    else:
        k_ref, o_ref, wb_ref = rest

    @pl.when(pl.program_id(1) == 0)
    def _():
        w = w_ref[...]
        if conv:
            w = _conv3_seq(w, cw_ref[...], cb_ref[...])
        wb_ref[...] = w.astype(BF16)

    s = jnp.dot(x_ref[...], wb_ref[...], preferred_element_type=F32)
    half = s.shape[0] // 2
    a, b = s[:half], s[half:]
    ka, kb = k_ref[0:half, :], k_ref[half:2 * half, :]
    o_ref[0:half, :] = ((a * ka - b * kb) * inv_len).astype(BF16)
    o_ref[half:2 * half, :] = ((a * kb + b * ka) * inv_len).astype(BF16)


def dft_mul(fwd, v_view, kspec, k_col0, *, L, nb, tm, tn, conv=None):
    d = D_MODEL
    per_b = d // tn
    v, v_row0, v_col0 = v_view
    vr, vc, kc = v_row0 // L, v_col0 // tn, k_col0 // tn
    est = (2 * _nbytes((tm, L), BF16) + 2 * _nbytes((L, tn), F32) + _nbytes((L, tn), BF16)
           + 2 * _nbytes((tm, tn), F32) + 2 * _nbytes((tm, tn), BF16) + 3 * _nbytes((tm, tn), F32))
    in_specs = [pl.BlockSpec((tm, L), lambda j, i: (i, 0)),
                pl.BlockSpec((L, tn), lambda j, i: (j // per_b + vr, j % per_b + vc))]
    args = [fwd, v]
    if conv is not None:
        in_specs += [pl.BlockSpec((HY_SHORT, tn), lambda j, i: (0, j % per_b + vc)),
                     pl.BlockSpec((1, tn), lambda j, i: (0, j % per_b + vc))]
        args += list(conv)
    in_specs.append(pl.BlockSpec((tm, tn), lambda j, i: (i, j % per_b + kc)))
    args.append(kspec)
    return pl.pallas_call(
        functools.partial(_dft_mul_kernel, inv_len=1.0 / L, conv=conv is not None),
        grid=(nb * per_b, (2 * L) // tm),
        in_specs=in_specs,
        out_specs=pl.BlockSpec((tm, tn), lambda j, i: (i, j)),
        out_shape=jax.ShapeDtypeStruct((2 * L, nb * d), BF16),
        scratch_shapes=[pltpu.VMEM((L, tn), BF16)],
        compiler_params=_params(2, est),
        name="dft_mul",
    )(*args)


def _idft_kernel(x_ref, y_ref, *rest, z_conv, per):
    rest = list(rest)
    g_ref, gp_ref, gn_ref, gw_ref, gb_ref = rest[:5]
    rest = rest[5:]
    i = pl.program_id(1)
    first, last = i == 0, i == per - 1
    gate = _conv3_tile(g_ref[...], gp_ref[...], gn_ref[...], gw_ref[...], gb_ref[...], first, last)
    if z_conv:
        z_ref, zp_ref, zn_ref, zw_ref, zb_ref, skip_ref, o_ref = rest
        z = _conv3_tile(z_ref[...], zp_ref[...], zn_ref[...], zw_ref[...], zb_ref[...], first, last)
    else:
        z_ref, skip_ref, o_ref = rest
        z = z_ref[...]
    conv = jnp.dot(x_ref[...], y_ref[...], preferred_element_type=F32)
    o_ref[...] = (gate * (conv + z * skip_ref[...])).astype(o_ref.dtype)


def idft_gate(inv, y, gate_view, z_view, skip, conv, *, z_conv, L, nb, tm, tn, out_dtype):
    d = D_MODEL
    per_b = d // tn
    per = L // tm
    sub = tm // SUBLANE

    def tile_specs(view, with_conv):
        arr, row0, col0 = view
        r0, c0 = row0 // tm, col0 // tn
        last_blk = arr.shape[0] // SUBLANE - 1
        row = lambda j, i: (j // per_b) * per + i + r0
        col = lambda j, i: j % per_b + c0
        specs = [pl.BlockSpec((tm, tn), lambda j, i: (row(j, i), col(j, i)))]
        args = [arr]
        if with_conv:
            specs += [
                pl.BlockSpec((SUBLANE, tn),
                             lambda j, i: (jnp.maximum(row(j, i) * sub - 1, 0), col(j, i))),
                pl.BlockSpec((SUBLANE, tn),
                             lambda j, i: (jnp.minimum((row(j, i) + 1) * sub, last_blk), col(j, i))),
                pl.BlockSpec((HY_SHORT, tn), lambda j, i: (0, col(j, i))),
                pl.BlockSpec((1, tn), lambda j, i: (0, col(j, i)))]
            args += [arr, arr, conv[0], conv[1]]
        return specs, args

    g_specs, g_args = tile_specs(gate_view, True)
    z_specs, z_args = tile_specs(z_view, z_conv)
    est = (2 * _nbytes((tm, 2 * L), BF16) + 2 * _nbytes((2 * L, tn), BF16)
           + 12 * _nbytes((tm, tn), F32))
    return pl.pallas_call(
        functools.partial(_idft_kernel, z_conv=z_conv, per=per),
        grid=(nb * per_b, per),
        in_specs=[pl.BlockSpec((tm, 2 * L), lambda j, i: (i, 0)),
                  pl.BlockSpec((2 * L, tn), lambda j, i: (0, j)),
                  *g_specs, *z_specs,
                  pl.BlockSpec((1, tn), lambda j, i: (0, j % per_b))],
        out_specs=pl.BlockSpec((tm, tn), lambda j, i: ((j // per_b) * per + i, j % per_b)),
        out_shape=jax.ShapeDtypeStruct((nb * L, d), out_dtype),
        compiler_params=_params(2, est),
        name="idft_gate",
    )(inv, y, *g_args, *z_args, skip.reshape(1, d))


def hyena_stream(u, row0, L, nb, conv, filt, skip):
    d = D_MODEL
    tms = min(1024, 2 * L)
    tmi = min(512, L)
    k = hyena_filters(L, *filt)
    fwd, inv = dft_tables(L, tms)
    kspec = linear(fwd, k, tm=tms, tn=1024, name="dft_filters")
    y1 = dft_mul(fwd, (u, row0, 0), kspec, 0, L=L, nb=nb, tm=tms, tn=1024, conv=conv)
    z1 = idft_gate(inv, y1, (u, row0, d), (u, row0, 0), skip[0], conv, z_conv=True, L=L, nb=nb,
                   tm=tmi, tn=1024, out_dtype=F32)
    y2 = dft_mul(fwd, (z1, 0, 0), kspec, d, L=L, nb=nb, tm=tms, tn=1024)
    return idft_gate(inv, y2, (u, row0, 2 * d), (z1, 0, 0), skip[1], conv, z_conv=False, L=L,
                     nb=nb, tm=tmi, tn=1024, out_dtype=BF16)


def hyena_mix(a, n_rows, j, w_in, b_in, conv_w, conv_b, filt, skip):
    u = linear(a, w_in, e=j, tm=1024, tn=1024, bias=b_in, name="hyena_in")
    conv = (conv_w, conv_b.reshape(1, -1))
    z = hyena_stream(u, 0, SEQ, BATCH, conv, filt, skip)
    if n_rows > N_LAT:
        z_ctx = hyena_stream(u, N_LAT, CTX_LEN, BATCH, conv, filt, skip)
        z = jnp.concatenate([z, z_ctx], axis=0)
    return z


def _rms(x, g):
    return x * lax.rsqrt(jnp.mean(x * x, axis=-1, keepdims=True) + RMS_EPS) * g


def _mla_q_kernel(p_ref, g_ref, w1_ref, w2_ref, c_ref, s_ref, o_ref, w1b_ref, w2b_ref):
    @pl.when(pl.program_id(1) == 0)
    def _():
        w1b_ref[...] = w1_ref[...].astype(BF16)
        w2b_ref[...] = w2_ref[...].astype(BF16)

    xn = _rms(p_ref[...], g_ref[...]).astype(BF16)
    q1 = jnp.dot(xn, w1b_ref[...], preferred_element_type=F32)
    q2 = jnp.dot(xn, w2b_ref[...], preferred_element_type=F32)
    reps = q1.shape[1] // MLA_QK_PAD
    c = jnp.concatenate([c_ref[...]] * reps, axis=1)
    s = jnp.concatenate([s_ref[...]] * reps, axis=1)
    o_ref[...] = (q1 * c + q2 * s).astype(BF16)


def _mla_kv_kernel(p_ref, r_ref, g_ref, cs_ref, w_ref, o_ref, wb_ref):
    @pl.when(pl.program_id(1) == 0)
    def _():
        wb_ref[...] = w_ref[...].astype(BF16)

    xn = _rms(p_ref[...], g_ref[...]).astype(BF16)
    rr = (r_ref[...] * cs_ref[...]).astype(BF16)
    acc = (jnp.dot(xn, wb_ref[0:MLA_KV_RANK, :], preferred_element_type=F32)
           + jnp.dot(rr, wb_ref[MLA_KV_RANK:MLA_KV_RANK + 2 * MLA_ROPE, :],
                     preferred_element_type=F32))
    o_ref[...] = acc.astype(BF16)


def _attn_kernel(q_ref, kl_ref, kc_ref, vl_ref, vc_ref, o_ref, *, kchunk):
    c = (MLA_NOPE + MLA_ROPE) ** -0.5 * math.log2(math.e)
    nt = (((1,), (1,)), ((), ()))
    q = q_ref[...]
    chunks = [(kl_ref, vl_ref, r) for r in range(kl_ref.shape[0] // kchunk)]
    chunks += [(kc_ref, vc_ref, r) for r in range(kc_ref.shape[0] // kchunk)]
    m = l = acc = None
    for k_ref, v_ref, r in chunks:
        rows = slice(r * kchunk, (r + 1) * kchunk)
        s = lax.dot_general(q, k_ref[rows, :], nt, preferred_element_type=F32)
        ms = jnp.max(s, axis=-1, keepdims=True)
        m_new = ms if m is None else jnp.maximum(m, ms)
        p = jnp.exp2((s - m_new) * c)
        ps = jnp.sum(p, axis=-1, keepdims=True)
        pv = jnp.dot(p.astype(BF16), v_ref[rows, :], preferred_element_type=F32)
        if m is None:
            l, acc = ps, pv
        else:
            alpha = jnp.exp2((m - m_new) * c)
            l = alpha * l + ps
            acc = alpha * acc + pv
        m = m_new
    o_ref[...] = (acc / l).astype(o_ref.dtype)


def _rope_tables():
    rows = SEQ // GRID_W
    row = jnp.repeat(jnp.arange(rows), GRID_W).astype(F32)
    col = jnp.tile(jnp.arange(GRID_W), rows).astype(F32)
    n = MLA_ROPE // 4
    inv = ROPE_BASE ** (-jnp.arange(n, dtype=F32) / n)
    ang_row, ang_col = row[:, None] * inv, col[:, None] * inv
    cos = jnp.concatenate([jnp.cos(ang_row)] * 2 + [jnp.cos(ang_col)] * 2, axis=1)
    sin = jnp.concatenate([jnp.sin(ang_row)] * 2 + [jnp.sin(ang_col)] * 2, axis=1)
    return cos, sin


def _rope_partner(w):
    n = MLA_ROPE // 4
    return jnp.concatenate([-w[..., n:2 * n], w[..., 0:n], -w[..., 3 * n:4 * n],
                            w[..., 2 * n:3 * n]], axis=-1)


def mla_mix(a, w_in, q_norm, kv_norm, w_uq, w_ukv):
    H, NP, R, V, QK = MLA_HEADS, MLA_NOPE, MLA_ROPE, MLA_V, MLA_QK_PAD
    n_all = N_LAT + N_CTX
    cos, sin = _rope_tables()
    w_in_x = jnp.concatenate([w_in, _rope_partner(w_in[:, MLA_IN - R:])], axis=1)
    proj = linear(a, w_in_x, tm=512, tn=MLA_IN_PAD, name="mla_in")
    wq = w_uq.reshape(MLA_Q_RANK, H, NP + R)
    zq = jnp.zeros((MLA_Q_RANK, H, QK - NP - R), F32)
    wq1 = jnp.concatenate([wq, zq], axis=2).reshape(MLA_Q_RANK, H * QK)
    wq2 = jnp.concatenate([jnp.zeros((MLA_Q_RANK, H, NP), F32), _rope_partner(wq[..., NP:]), zq],
                          axis=2).reshape(MLA_Q_RANK, H * QK)
    ones, zeros = jnp.ones((SEQ, NP), F32), jnp.zeros((SEQ, QK - NP - R), F32)
    cq = jnp.concatenate([ones, cos, zeros], axis=1)
    sq = jnp.concatenate([0.0 * ones, sin, zeros], axis=1)
    tm, tn = 512, 1024
    per = SEQ // tm
    q = pl.pallas_call(
        _mla_q_kernel,
        grid=(H * QK // tn, N_LAT // tm),
        in_specs=[pl.BlockSpec((tm, MLA_Q_RANK), lambda jn, i: (i, 0)),
                  pl.BlockSpec((1, MLA_Q_RANK), lambda jn, i: (0, 0)),
                  pl.BlockSpec((MLA_Q_RANK, tn), lambda jn, i: (0, jn)),
                  pl.BlockSpec((MLA_Q_RANK, tn), lambda jn, i: (0, jn)),
                  pl.BlockSpec((tm, QK), lambda jn, i: (i % per, 0)),
                  pl.BlockSpec((tm, QK), lambda jn, i: (i % per, 0))],
        out_specs=pl.BlockSpec((tm, tn), lambda jn, i: (i, jn)),
        out_shape=jax.ShapeDtypeStruct((N_LAT, H * QK), BF16),
        scratch_shapes=[pltpu.VMEM((MLA_Q_RANK, tn), BF16), pltpu.VMEM((MLA_Q_RANK, tn), BF16)],
        compiler_params=_params(2, 6 * _nbytes((MLA_Q_RANK, tn), F32) + 8 * _nbytes((tm, tn), F32)),
        name="mla_q",
    )(proj, q_norm.reshape(1, MLA_Q_RANK), wq1, wq2, cq, sq)
    wkv = w_ukv.reshape(MLA_KV_RANK, H, NP + V)
    wk = jnp.concatenate([wkv[..., :NP], jnp.zeros((MLA_KV_RANK, H, QK - NP), F32)],
                         axis=2).reshape(MLA_KV_RANK, H * QK)
    place = jnp.concatenate([jnp.zeros((R, NP), F32), jnp.eye(R, dtype=F32),
                             jnp.zeros((R, QK - NP - R), F32)], axis=1)
    place = jnp.tile(jnp.concatenate([place, place], axis=0), (1, H))
    w_kv = jnp.concatenate([
        jnp.concatenate([wk, wkv[..., NP:].reshape(MLA_KV_RANK, H * V)], axis=1),
        jnp.concatenate([place, jnp.zeros((2 * R, H * V), F32)], axis=1)], axis=0)
    cs = jnp.concatenate([
        jnp.concatenate([cos, sin], axis=1),
        jnp.concatenate([jnp.ones((CTX_LEN, R), F32), jnp.zeros((CTX_LEN, R), F32)], axis=1)],
        axis=0)
    tm, tn = CTX_LEN, 2048
    per = SEQ // tm
    n_kv = H * (QK + V)
    kv = pl.pallas_call(
        _mla_kv_kernel,
        grid=(n_kv // tn, n_all // tm),
        in_specs=[pl.BlockSpec((tm, MLA_KV_RANK), lambda jn, i: (i, MLA_Q_RANK // MLA_KV_RANK)),
                  pl.BlockSpec((tm, 2 * R), lambda jn, i: (i, (MLA_IN - R) // (2 * R))),
                  pl.BlockSpec((1, MLA_KV_RANK), lambda jn, i: (0, 0)),
                  pl.BlockSpec((tm, 2 * R),
                               lambda jn, i: (jnp.where(i * tm < N_LAT, i % per, per), 0)),
                  pl.BlockSpec((MLA_KV_RANK + 2 * R, tn), lambda jn, i: (0, jn))],
        out_specs=pl.BlockSpec((tm, tn), lambda jn, i: (i, jn)),
        out_shape=jax.ShapeDtypeStruct((n_all, n_kv), BF16),
        scratch_shapes=[pltpu.VMEM((MLA_KV_RANK + 2 * R, tn), BF16)],
        compiler_params=_params(2, 4 * _nbytes((MLA_KV_RANK + 2 * R, tn), F32)
                                + 8 * _nbytes((tm, tn), F32)),
        name="mla_kv",
    )(proj, proj, kv_norm.reshape(1, MLA_KV_RANK), cs, w_kv)
    tq = 1024
    per = SEQ // tq
    v0 = H * QK // V
    ctx0 = N_LAT // CTX_LEN
    est = (2 * _nbytes((tq, QK), BF16) + 2 * _nbytes((SEQ + CTX_LEN, QK + V), BF16)
           + 16 * _nbytes((tq, CTX_LEN), F32))
    return pl.pallas_call(
        functools.partial(_attn_kernel, kchunk=CTX_LEN),
        grid=(BATCH, H, per),
        in_specs=[pl.BlockSpec((tq, QK), lambda b, h, i: (b * per + i, h)),
                  pl.BlockSpec((SEQ, QK), lambda b, h, i: (b, h)),
                  pl.BlockSpec((CTX_LEN, QK), lambda b, h, i: (ctx0 + b, h)),
                  pl.BlockSpec((SEQ, V), lambda b, h, i: (b, v0 + h)),
                  pl.BlockSpec((CTX_LEN, V), lambda b, h, i: (ctx0 + b, v0 + h))],
        out_specs=pl.BlockSpec((tq, V), lambda b, h, i: (b * per + i, h)),
        out_shape=jax.ShapeDtypeStruct((N_LAT, H * V), BF16),
        compiler_params=_params(3, est),
        name="mla_attention",
    )(q, kv, kv, kv, kv)


def dense_ffn(a, f, wg, wu, wd):
    p = swiglu_up([a], wg, wu, e=f, tm=1024, tf=512, name="ffn_up")
    return linear(p, wd, e=f, tm=512, tn=512, out_dtype=BF16, name="ffn_down")


def _row_cumsum(onehot):
    n, e = onehot.shape
    blk = 256
    x = onehot.reshape(n // blk, blk, e).astype(BF16)
    tri = jnp.tril(jnp.ones((blk, blk), BF16))
    inner = jnp.einsum("ij,bje->bie", tri, x, preferred_element_type=F32)
    totals = inner[:, -1, :]
    offs = jnp.cumsum(totals, axis=0) - totals
    return (inner + offs[:, None, :]).reshape(n, e).astype(jnp.int32)


def moe_ffn(logits, a_rows, n_rows, f, wg, wu, wd, *, tm=512):
    top_v, top_i = lax.top_k(logits[:, :N_EXPERTS], TOP_K)
    gates = jax.nn.softmax(top_v, axis=-1)
    n_assign = n_rows * TOP_K
    n_tiles = n_assign // tm + N_EXPERTS
    flat_e = top_i.reshape(n_assign)
    onehot = (flat_e[:, None] == jnp.arange(N_EXPERTS)[None, :]).astype(jnp.int32)
    csum = _row_cumsum(onehot)
    rank = jnp.sum(csum * onehot, axis=1) - 1
    counts = csum[-1]
    padded = ((counts + tm - 1) // tm) * tm
    ends = jnp.cumsum(padded)
    starts = ends - padded
    pos = starts[flat_e] + rank
    src = (jnp.arange(n_tiles * tm, dtype=jnp.int32) % n_rows).at[pos].set(
        jnp.arange(n_assign, dtype=jnp.int32) // TOP_K)
    tile_row0 = jnp.arange(n_tiles, dtype=jnp.int32) * tm
    te = jnp.minimum(jnp.sum((tile_row0[:, None] >= ends[None, :]).astype(jnp.int32), axis=1),
                     N_EXPERTS - 1)
    nv = (ends[-1] // tm).astype(jnp.int32).reshape(1)
    te = jnp.where(tile_row0 < ends[-1], te, te[jnp.maximum(nv[0] - 1, 0)]) + f * N_EXPERTS
    x_sorted = a_rows.at[src].get(mode="promise_in_bounds")
    stack = lambda w: w.reshape((-1,) + w.shape[2:])
    p = swiglu_up([x_sorted], stack(wg), stack(wu), tm=tm, tf=1024, te=te, nv=nv, name="moe_up")
    out = linear(p, stack(wd), tm=tm, tn=1024, te=te, nv=nv, name="moe_down")
    pos = pos.reshape(n_rows, TOP_K)
    ys = [out.at[pos[:, k]].get(mode="promise_in_bounds") for k in range(TOP_K)]
    return ys, [gates[:, k:k + 1] for k in range(TOP_K)]


def kernel(x, c, ctx, c_ctx, w_mod, b_mod, ln_g, ln_b, pool_w, pool_scale, hy_w_in, hy_b_in, hy_conv_w, hy_conv_b, hy_f_w1, hy_f_b1, hy_f_w2, hy_f_b2, hy_f_w3, hy_f_b3, hy_f_freq, hy_f_wout, hy_skip, hy_w_out, hy_b_out, mla_w_in, mla_q_norm, mla_kv_norm, mla_w_uq, mla_w_ukv, mla_w_o, ffn_w_gate, ffn_w_up, ffn_w_down, moe_w_router, moe_w_gate, moe_w_up, moe_w_down):
    attn_layers = [i for i in range(DEPTH) if i % N_MIXERS == 2]
    last_read = attn_layers[-1] if attn_layers else -1

    c8 = jnp.concatenate([c, c_ctx[None], jnp.zeros((MOD_ROWS - BATCH - 1, D_MODEL), F32)], axis=0)
    mods = modulation_table(c8, w_mod, b_mod)

    h = jnp.concatenate([x.reshape(N_LAT, D_MODEL), ctx.reshape(N_CTX, D_MODEL)], axis=0)
    a = None
    for i in range(DEPTH):
        kind, j = i % N_MIXERS, i // N_MIXERS
        ctx_out = i < last_read
        n_rows = N_LAT + N_CTX if ctx_out else N_LAT
        m = mods[i]
        f = i // 2
        w_router = moe_w_router[f] if i % 2 == 1 else None
        if kind == 0:
            tl = CTX_LEN if n_rows > N_LAT else 512
            h1, a2, *logits = pool_layer(h, n_rows, m, pool_w[j].astype(BF16), pool_scale[j],
                                         ln_g[i, 0], ln_b[i, 0], tl=tl, w_router=w_router)
        else:
            if kind == 1:
                filt = (hy_f_w1[j], hy_f_b1[j], hy_f_w2[j], hy_f_b2[j], hy_f_w3[j], hy_f_b3[j],
                        hy_f_freq[j], hy_f_wout[j])
                z = hyena_mix(a[:n_rows], n_rows, j, hy_w_in, hy_b_in, hy_conv_w[j], hy_conv_b[j],
                              filt, hy_skip[j])
                w_o, b_o = hy_w_out, hy_b_out
            else:
                assert i <= last_read and not ctx_out and a.shape[0] == N_LAT + N_CTX
                z = mla_mix(a, mla_w_in[j], mla_q_norm[j], mla_kv_norm[j], mla_w_uq[j],
                            mla_w_ukv[j])
                w_o, b_o = mla_w_o, None
            h1, a2, *logits = linear_resid_ln(z, w_o, j, b_o, h, n_rows, m, 2, ln_g[i, 0],
                                              ln_b[i, 0], 4, 3, w_router=w_router)
        nm = mods[min(i + 1, DEPTH - 1)]
        if w_router is not None:
            ys, row_gates = moe_ffn(logits[0], a2, n_rows, f, moe_w_gate, moe_w_up, moe_w_down)
        else:
            ys, row_gates = [dense_ffn(a2, f, ffn_w_gate, ffn_w_up, ffn_w_down)], None
        h, a = resid_ln(h1, ys, n_rows, m, 5, ln_g[i, 1], ln_b[i, 1], nm, 1, 0,
                        row_gates=row_gates)
    return h[:N_LAT].reshape(BATCH, SEQ, D_MODEL)
```

```python
import functools
import math

import jax
import jax.numpy as jnp
from jax import lax
from jax.experimental import pallas as pl
from jax.experimental.pallas import tpu as pltpu

F32 = jnp.float32
BF16 = jnp.bfloat16
U32 = jnp.uint32

D_MODEL = 2048
BATCH = 4
SEQ = 2048
DEPTH = 4
GRID_W = 64
CTX_LEN = 256
N_MIXERS = 3
POOL_WINDOWS = (2, 4, 8, 16)
POOL_GROUP = D_MODEL // len(POOL_WINDOWS)
HY_ORDER = 2
HY_SHORT = 3
HY_EMB = 33
HY_FILTER = 64
HY_DECAY_TARGET = 1e-2
HY_FAST = 0.3
HY_SLOW = 1.5
MLA_HEADS = D_MODEL // 128
MLA_Q_RANK = 512
MLA_KV_RANK = 512
MLA_NOPE = 128
MLA_ROPE = 64
MLA_V = 128
ROPE_BASE = 10000.0
D_FF = 5632
N_EXPERTS = 8
TOP_K = 2
EXPERT_FF = 2 * D_MODEL
LN_EPS = 1e-5
RMS_EPS = 1e-6
DN_ALPHA = (2 * DEPTH) ** 0.25

N_LAT = BATCH * SEQ
N_CTX = BATCH * CTX_LEN
MOD_ROWS = 8
CTX_MOD_ROW = BATCH
LANE = 128
SUBLANE = 8
VMEM_CAP = 60 * 1024 * 1024
MLA_QK_PAD = 256
MLA_IN = MLA_Q_RANK + MLA_KV_RANK + MLA_ROPE
MLA_IN_PAD = MLA_IN + MLA_ROPE


def _vmem(nbytes):
    return int(min(VMEM_CAP, max(16 * 1024 * 1024, nbytes * 3 // 2)))


def _nbytes(shape, dtype):
    return math.prod(shape) * jnp.dtype(dtype).itemsize


def _mod_row(i, tm):
    return jnp.minimum((i * tm) // SEQ, CTX_MOD_ROW)


def _mod_spec(which, tm):
    return pl.BlockSpec((None, 1, D_MODEL), lambda i: (_mod_row(i, tm) * 6 + which, 0, 0))


def _params(sem, est):
    return pltpu.CompilerParams(dimension_semantics=("arbitrary",) * sem,
                                vmem_limit_bytes=_vmem(est))


def _mod_kernel(c_ref, w_ref, b_ref, o_ref):
    c = c_ref[...]
    a = c / (1.0 + jnp.exp(-c))
    o_ref[...] = jnp.dot(a.astype(BF16), w_ref[...].astype(BF16),
                         preferred_element_type=F32) + b_ref[...]


def modulation_table(c8, w_mod, b_mod):
    tn = 1024
    n = 6 * D_MODEL
    out = pl.pallas_call(
        _mod_kernel,
        grid=(DEPTH, n // tn),
        in_specs=[pl.BlockSpec((MOD_ROWS, D_MODEL), lambda l, j: (0, 0)),
                  pl.BlockSpec((None, D_MODEL, tn), lambda l, j: (l, 0, j)),
                  pl.BlockSpec((None, 1, tn), lambda l, j: (l, 0, j))],
        out_specs=pl.BlockSpec((None, MOD_ROWS, tn), lambda l, j: (l, 0, j)),
        out_shape=jax.ShapeDtypeStruct((DEPTH, MOD_ROWS, n), F32),
        compiler_params=_params(2, 3 * _nbytes((D_MODEL, tn), F32)),
        name="modulation_table",
    )(c8, w_mod, b_mod.reshape(DEPTH, 1, n))
    return out.reshape(DEPTH, MOD_ROWS * 6, 1, D_MODEL)


def _group_schedule(te):
    n = te.shape[0]
    first = jnp.concatenate([jnp.ones((1,), jnp.int32), (te[1:] != te[:-1]).astype(jnp.int32)])
    gid = jnp.cumsum(first) - 1
    ge = jnp.zeros((n,), jnp.int32).at[gid].max(te)
    return first, gid, ge, gid[-1:] + 1


def _stream_weights(sched, w_hbms, stages, sem, dsts, col0, tn):
    first_ref, gid_ref, ge_ref, ng_ref = sched
    j, i = pl.program_id(0), pl.program_id(1)
    g, ng = gid_ref[i], ng_ref[0]

    def copies(e, col):
        cols = pl.ds(pl.multiple_of((col + col0) * tn, tn), tn)
        return [pltpu.make_async_copy(w.at[e, :, cols], st, sem.at[k])
                for k, (w, st) in enumerate(zip(w_hbms, stages))]

    @pl.when(first_ref[i] == 1)
    def _():
        @pl.when(jnp.logical_and(j == 0, g == 0))
        def _():
            for c in copies(ge_ref[0], 0):
                c.start()

        for c in copies(ge_ref[g], j):
            c.wait()
        for st, dst in zip(stages, dsts):
            dst[...] = st[...].astype(BF16)
        last = g + 1 >= ng
        next_e = ge_ref[jnp.where(last, 0, g + 1)]
        next_col = jnp.where(last, j + 1, j)

        @pl.when(next_col < pl.num_programs(0))
        def _():
            for c in copies(next_e, next_col):
                c.start()


def _linear_kernel(xt_ref, nv_ref, first_ref, gid_ref, ge_ref, ng_ref, x_ref, w_hbm, *rest,
                   has_bias, col0, tn):
    rest = list(rest)
    b_ref = rest.pop(0) if has_bias else None
    o_ref, stage_ref, wbf_ref, sem = rest
    i = pl.program_id(1)
    _stream_weights((first_ref, gid_ref, ge_ref, ng_ref), [w_hbm], [stage_ref], sem, [wbf_ref],
                    col0, tn)

    @pl.when(i < nv_ref[0])
    def _():
        acc = jnp.dot(x_ref[...].astype(BF16), wbf_ref[...], preferred_element_type=F32)
        if has_bias:
            acc = acc + b_ref[...]
        o_ref[...] = acc.astype(o_ref.dtype)

    @pl.when(i >= nv_ref[0])
    def _():
        o_ref[...] = jnp.zeros_like(o_ref)


def linear(x, w, *, tm, tn, e=0, n_out=None, n_tiles=None, xt=None, te=None, nv=None, bias=None,
           out_dtype=F32, w_col0=0, name="linear"):
    k = x.shape[1]
    if w.ndim == 2:
        w = w[None]
    if n_out is None:
        n_out = w.shape[2] - w_col0
    assert x.shape[0] % tm == 0 and n_out % tn == 0 and w_col0 % tn == 0 and w.shape[1] == k
    assert w.dtype == F32
    if n_tiles is None:
        n_tiles = x.shape[0] // tm
    if xt is None:
        xt = jnp.arange(n_tiles, dtype=jnp.int32)
    if te is None:
        te = jnp.full((n_tiles,), e, jnp.int32)
    if nv is None:
        nv = jnp.full((1,), n_tiles, jnp.int32)
    first, gid, ge, ng = _group_schedule(te)
    has_bias = bias is not None
    in_specs = [pl.BlockSpec((tm, k), lambda j, i, xt, *_: (xt[i], 0)),
                pl.BlockSpec(memory_space=pl.ANY)]
    args = [x, w]
    if has_bias:
        bias = bias.reshape(-1, 1, bias.shape[-1])
        in_specs.append(pl.BlockSpec((None, 1, tn),
                                     lambda j, i, xt, nv, first, gid, ge, ng: (ge[gid[i]], 0, j)))
        args.append(bias)
    est = (2 * _nbytes((tm, k), x.dtype) + _nbytes((k, tn), F32) + _nbytes((k, tn), BF16)
           + _nbytes((tm, k), BF16) + 2 * _nbytes((tm, tn), out_dtype) + 2 * _nbytes((tm, tn), F32))
    return pl.pallas_call(
        functools.partial(_linear_kernel, has_bias=has_bias, col0=w_col0 // tn, tn=tn),
        grid_spec=pltpu.PrefetchScalarGridSpec(
            num_scalar_prefetch=6,
            grid=(n_out // tn, n_tiles),
            in_specs=in_specs,
            out_specs=pl.BlockSpec((tm, tn), lambda j, i, *_: (i, j)),
            scratch_shapes=[pltpu.VMEM((k, tn), F32), pltpu.VMEM((k, tn), BF16),
                            pltpu.SemaphoreType.DMA((1,))]),
        out_shape=jax.ShapeDtypeStruct((n_tiles * tm, n_out), out_dtype),
        compiler_params=_params(2, est),
        name=name,
    )(xt, nv, first, gid, ge, ng, *args)


def _swiglu_kernel(nv_ref, first_ref, gid_ref, ge_ref, ng_ref, *refs, n_parts, tf):
    x_refs = refs[:n_parts]
    wg_hbm, wu_hbm, o_ref, sg_ref, su_ref, wgb_ref, wub_ref, sem = refs[n_parts:]
    i = pl.program_id(1)
    _stream_weights((first_ref, gid_ref, ge_ref, ng_ref), [wg_hbm, wu_hbm], [sg_ref, su_ref], sem,
                    [wgb_ref, wub_ref], 0, tf)

    @pl.when(i < nv_ref[0])
    def _():
        g = u = None
        k0 = 0
        for x_ref in x_refs:
            x = x_ref[...].astype(BF16)
            rows = slice(k0, k0 + x.shape[1])
            gp = jnp.dot(x, wgb_ref[rows, :], preferred_element_type=F32)
            up = jnp.dot(x, wub_ref[rows, :], preferred_element_type=F32)
            g, u = (gp, up) if g is None else (g + gp, u + up)
            k0 += x.shape[1]
        o_ref[...] = ((g / (1.0 + jnp.exp(-g))) * u).astype(o_ref.dtype)

    @pl.when(i >= nv_ref[0])
    def _():
        o_ref[...] = jnp.zeros_like(o_ref)


def swiglu_up(x_parts, wg, wu, *, tm, tf, e=0, te=None, nv=None, name="swiglu_up"):
    m = x_parts[0].shape[0]
    k, f = wg.shape[1], wg.shape[2]
    assert sum(x.shape[1] for x in x_parts) == k and m % tm == 0 and f % tf == 0
    n_tiles = m // tm
    if te is None:
        te = jnp.full((n_tiles,), e, jnp.int32)
    if nv is None:
        nv = jnp.full((1,), n_tiles, jnp.int32)
    first, gid, ge, ng = _group_schedule(te)
    wspec = pl.BlockSpec(memory_space=pl.ANY)
    est = (2 * _nbytes((tm, k), x_parts[0].dtype) + _nbytes((tm, k), BF16)
           + 2 * _nbytes((k, tf), F32) + 2 * _nbytes((k, tf), BF16) + 2 * _nbytes((tm, tf), BF16)
           + 3 * _nbytes((tm, tf), F32))
    x_specs = [pl.BlockSpec((tm, x.shape[1]), lambda j, i, *_: (i, 0)) for x in x_parts]
    return pl.pallas_call(
        functools.partial(_swiglu_kernel, n_parts=len(x_parts), tf=tf),
        grid_spec=pltpu.PrefetchScalarGridSpec(
            num_scalar_prefetch=5,
            grid=(f // tf, n_tiles),
            in_specs=[*x_specs, wspec, wspec],
            out_specs=pl.BlockSpec((tm, tf), lambda j, i, *_: (i, j)),
            scratch_shapes=[pltpu.VMEM((k, tf), F32), pltpu.VMEM((k, tf), F32),
                            pltpu.VMEM((k, tf), BF16), pltpu.VMEM((k, tf), BF16),
                            pltpu.SemaphoreType.DMA((2,))]),
        out_shape=jax.ShapeDtypeStruct((m, f), BF16),
        compiler_params=_params(2, est),
        name=name,
    )(nv, first, gid, ge, ng, *x_parts, wg, wu)


def _ln_mod(v, g, b, sc, sh, pack):
    mu = jnp.mean(v, axis=-1, keepdims=True)
    d = v - mu
    var = jnp.mean(d * d, axis=-1, keepdims=True)
    h = d * lax.rsqrt(var + LN_EPS) * g + b
    a = h * (1.0 + sc) + sh
    return h, (a if pack else a.astype(BF16))


def _resid_ln_kernel(*refs, n_y, pack):
    h_ref = refs[0]
    y_refs = refs[1:1 + n_y]
    rest = refs[1 + n_y:]
    if n_y == 2:
        g0_ref, g1_ref = rest[0], rest[1]
        rest = rest[2:]
        y = g0_ref[...] * y_refs[0][...] + g1_ref[...] * y_refs[1][...]
    else:
        y = y_refs[0][...]
    gate_ref, g_ref, b_ref, sc_ref, sh_ref, o_ref, a_ref = rest
    v = DN_ALPHA * h_ref[...] + gate_ref[...] * y
    h, a = _ln_mod(v, g_ref[...], b_ref[...], sc_ref[...], sh_ref[...], pack)
    o_ref[...] = h
    a_ref[...] = a


def resid_ln(h, ys, n_rows, mods, gate_idx, ln_g, ln_b, nmods, sc_idx, sh_idx, *, row_gates=None,
             pack=False, tm=256):
    assert n_rows % tm == 0
    row = pl.BlockSpec((tm, D_MODEL), lambda i: (i, 0))
    col = pl.BlockSpec((tm, 1), lambda i: (i, 0))
    vec = pl.BlockSpec((1, D_MODEL), lambda i: (0, 0))
    a_w = D_MODEL
    a_dt = F32 if pack else BF16
    est = 10 * _nbytes((tm, D_MODEL), F32)
    in_specs = [row] + [row] * len(ys) + ([col, col] if row_gates else []) + [
        _mod_spec(gate_idx, tm), vec, vec, _mod_spec(sc_idx, tm), _mod_spec(sh_idx, tm)]
    args = [h, *ys, *(row_gates or []), mods, ln_g.reshape(1, D_MODEL), ln_b.reshape(1, D_MODEL),
            nmods, nmods]
    return pl.pallas_call(
        functools.partial(_resid_ln_kernel, n_y=len(ys), pack=pack),
        grid=(n_rows // tm,),
        in_specs=in_specs,
        out_specs=[row, pl.BlockSpec((tm, a_w), lambda i: (i, 0))],
        out_shape=[jax.ShapeDtypeStruct((n_rows, D_MODEL), F32),
                   jax.ShapeDtypeStruct((n_rows, a_w), a_dt)],
        compiler_params=_params(1, est),
        name="resid_ln",
    )(*args)


def _split_bf16(x):
    hi = x.astype(BF16)
    lo = (x - hi.astype(F32)).astype(BF16)
    return hi, lo


def _router_logits(a, whi_ref, wlo_ref):
    a_hi, a_lo = _split_bf16(a)
    w_hi = whi_ref[...]
    return (jnp.dot(a_hi, w_hi, preferred_element_type=F32)
            + jnp.dot(a_lo, w_hi, preferred_element_type=F32)
            + jnp.dot(a_hi, wlo_ref[...], preferred_element_type=F32))


def _router_operands(w_router):
    w = jnp.pad(w_router, ((0, 0), (0, LANE - N_EXPERTS)))
    return _split_bf16(w), pl.BlockSpec((D_MODEL, LANE), lambda i: (0, 0))


def _linear_ln_kernel(x_ref, w_ref, *rest, has_bias, moe):
    rest = list(rest)
    bias_ref = rest.pop(0) if has_bias else None
    h_ref, gate_ref, g_ref, b_ref, sc_ref, sh_ref = rest[:6]
    rest = rest[6:]
    if moe:
        whi_ref, wlo_ref, o_ref, a_ref, l_ref, wb_ref = rest
    else:
        o_ref, a_ref, wb_ref = rest

    @pl.when(pl.program_id(0) == 0)
    def _():
        wb_ref[...] = w_ref[...].astype(BF16)

    y = jnp.dot(x_ref[...], wb_ref[...], preferred_element_type=F32)
    if has_bias:
        y = y + bias_ref[...]
    v = DN_ALPHA * h_ref[...] + gate_ref[...] * y
    h, a = _ln_mod(v, g_ref[...], b_ref[...], sc_ref[...], sh_ref[...], moe)
    o_ref[...] = h
    a_ref[...] = a
    if moe:
        l_ref[...] = _router_logits(a, whi_ref, wlo_ref)


def linear_resid_ln(x, w, e, bias, h, n_rows, mods, gate_idx, ln_g, ln_b, sc_idx, sh_idx, *,
                    w_router=None, tm=256):
    k = x.shape[1]
    moe = w_router is not None
    row = pl.BlockSpec((tm, D_MODEL), lambda i: (i, 0))
    vec = pl.BlockSpec((1, D_MODEL), lambda i: (0, 0))
    in_specs = [pl.BlockSpec((tm, k), lambda i: (i, 0)),
                pl.BlockSpec((None, k, D_MODEL), lambda i: (e, 0, 0),
                             pipeline_mode=pl.Buffered(1))]
    args = [x, w]
    if bias is not None:
        in_specs.append(pl.BlockSpec((None, 1, D_MODEL), lambda i: (e, 0, 0)))
        args.append(bias.reshape(-1, 1, D_MODEL))
    in_specs += [row, _mod_spec(gate_idx, tm), vec, vec, _mod_spec(sc_idx, tm),
                 _mod_spec(sh_idx, tm)]
    args += [h, mods, ln_g.reshape(1, D_MODEL), ln_b.reshape(1, D_MODEL), mods, mods]
    out_specs = [row, row]
    out_shape = [jax.ShapeDtypeStruct((n_rows, D_MODEL), F32),
                 jax.ShapeDtypeStruct((n_rows, D_MODEL), F32 if moe else BF16)]
    if moe:
        (w_hi, w_lo), wspec = _router_operands(w_router)
        in_specs += [wspec, wspec]
        args += [w_hi, w_lo]
        out_specs.append(pl.BlockSpec((tm, LANE), lambda i: (i, 0)))
        out_shape.append(jax.ShapeDtypeStruct((n_rows, LANE), F32))
    est = (_nbytes((k, D_MODEL), F32) + _nbytes((k, D_MODEL), BF16) + 2 * _nbytes((tm, k), BF16)
           + 12 * _nbytes((tm, D_MODEL), F32))
    return pl.pallas_call(
        functools.partial(_linear_ln_kernel, has_bias=bias is not None, moe=moe),
        grid=(n_rows // tm,),
        in_specs=in_specs,
        out_specs=out_specs,
        out_shape=out_shape,
        scratch_shapes=[pltpu.VMEM((k, D_MODEL), BF16)],
        compiler_params=_params(1, est),
        name="linear_resid_ln",
    )(*args)


def _seq_pos(i, tl):
    row0 = i * tl
    is_lat = row0 < N_LAT
    pos = jnp.where(is_lat, row0 % SEQ, (row0 - N_LAT) % CTX_LEN)
    seqlen = jnp.where(is_lat, SEQ, CTX_LEN)
    return pos, seqlen


def _halo_specs(tl, width, n_rows, col_index=None):
    per = tl // SUBLANE
    last = n_rows // SUBLANE - 1
    if col_index is None:
        prev = pl.BlockSpec((SUBLANE, width), lambda i: (jnp.maximum(i * per - 1, 0), 0))
        nxt = pl.BlockSpec((SUBLANE, width), lambda i: (jnp.minimum((i + 1) * per, last), 0))
    else:
        prev = pl.BlockSpec((SUBLANE, width),
                            lambda i, j: (jnp.maximum(i * per - 1, 0), col_index(j)))
        nxt = pl.BlockSpec((SUBLANE, width),
                           lambda i, j: (jnp.minimum((i + 1) * per, last), col_index(j)))
    return prev, nxt


def _shift_rows(x, s):
    n = x.shape[0]
    return pltpu.roll(x, s % n, 0)


def _pool_kernel(h_ref, hp_ref, hn_ref, sc_ref, sh_ref, gate_ref, w_ref, ps_ref, g_ref, b_ref,
                 nsc_ref, nsh_ref, *rest, tl, moe):
    if moe:
        whi_ref, wlo_ref, o_ref, a_ref, l_ref = rest
    else:
        o_ref, a_ref = rest
    i = pl.program_id(0)
    pos, seqlen = _seq_pos(i, tl)
    sc = 1.0 + sc_ref[...]
    sh = sh_ref[...]
    h = h_ref[...]
    a = h * sc + sh
    keep_prev = (pos > 0).astype(F32)
    keep_next = (pos + tl < seqlen).astype(F32)
    ext = jnp.concatenate([(hp_ref[...] * sc + sh) * keep_prev, a,
                           (hn_ref[...] * sc + sh) * keep_next], axis=0)
    t = pos + lax.broadcasted_iota(jnp.int32, (tl, 1), 0)
    n_ext = tl + 2 * SUBLANE
    parts = []
    for g, w in enumerate(POOL_WINDOWS):
        cols = slice(g * POOL_GROUP, (g + 1) * POOL_GROUP)
        s = ext[:, cols]
        s = s + _shift_rows(s, 1)
        r = 1
        while 2 * r < w:
            s = _shift_rows(s, n_ext - r) + _shift_rows(s, r)
            r *= 2
        s = s[SUBLANE:SUBLANE + tl]
        cnt = (jnp.minimum(t + w // 2, seqlen) - jnp.maximum(t - w // 2, 0)).astype(F32)
        d = s / cnt - a[:, cols]
        parts.append(jnp.dot(d.astype(BF16), w_ref[g], preferred_element_type=F32))
    y = jnp.concatenate(parts, axis=1) * ps_ref[...]
    v = DN_ALPHA * h + gate_ref[...] * y
    hn, an = _ln_mod(v, g_ref[...], b_ref[...], nsc_ref[...], nsh_ref[...], moe)
    o_ref[...] = hn
    a_ref[...] = an
    if moe:
        l_ref[...] = _router_logits(an, whi_ref, wlo_ref)


def pool_layer(h, n_rows, mods, w_grp_bf16, pool_scale, ln_g, ln_b, *, tl, w_router=None):
    moe = w_router is not None
    row = pl.BlockSpec((tl, D_MODEL), lambda i: (i, 0))
    vec = pl.BlockSpec((1, D_MODEL), lambda i: (0, 0))
    prev, nxt = _halo_specs(tl, D_MODEL, n_rows)
    in_specs = [row, prev, nxt, _mod_spec(1, tl), _mod_spec(0, tl), _mod_spec(2, tl),
                pl.BlockSpec(w_grp_bf16.shape, lambda i: (0, 0, 0)), vec, vec, vec,
                _mod_spec(4, tl), _mod_spec(3, tl)]
    args = [h, h, h, mods, mods, mods, w_grp_bf16, pool_scale.reshape(1, D_MODEL),
            ln_g.reshape(1, D_MODEL), ln_b.reshape(1, D_MODEL), mods, mods]
    out_specs = [row, row]
    out_shape = [jax.ShapeDtypeStruct((n_rows, D_MODEL), F32),
                 jax.ShapeDtypeStruct((n_rows, D_MODEL), F32 if moe else BF16)]
    if moe:
        (w_hi, w_lo), wspec = _router_operands(w_router)
        in_specs += [wspec, wspec]
        args += [w_hi, w_lo]
        out_specs.append(pl.BlockSpec((tl, LANE), lambda i: (i, 0)))
        out_shape.append(jax.ShapeDtypeStruct((n_rows, LANE), F32))
    est = 14 * _nbytes((tl, D_MODEL), F32) + 2 * _nbytes(w_grp_bf16.shape, BF16)
    return pl.pallas_call(
        functools.partial(_pool_kernel, tl=tl, moe=moe),
        grid=(n_rows // tl,),
        in_specs=in_specs,
        out_specs=out_specs,
        out_shape=out_shape,
        compiler_params=_params(1, est),
        name="pool_layer",
    )(*args)


def _conv3_seq(u, w, b):
    n = u.shape[0]
    t = lax.broadcasted_iota(jnp.int32, (n, 1), 0)
    before = jnp.where(t == 0, 0.0, _shift_rows(u, 1))
    after = jnp.where(t == n - 1, 0.0, _shift_rows(u, n - 1))
    return b + before * w[0:1] + u * w[1:2] + after * w[2:3]


def _conv3_tile(u, prev, nxt, w, b, first, last):
    n = u.shape[0]
    t = lax.broadcasted_iota(jnp.int32, (n, 1), 0)
    row_before = prev[SUBLANE - 1:SUBLANE, :] * (1.0 - first.astype(F32))
    row_after = nxt[0:1, :] * (1.0 - last.astype(F32))
    before = jnp.where(t == 0, row_before, _shift_rows(u, 1))
    after = jnp.where(t == n - 1, row_after, _shift_rows(u, n - 1))
    return b + before * w[0:1] + u * w[1:2] + after * w[2:3]


def _dot3(a, b):
    a_hi, a_lo = _split_bf16(a)
    b_hi, b_lo = _split_bf16(b)
    return (jnp.dot(a_hi, b_hi, preferred_element_type=F32)
            + jnp.dot(a_lo, b_hi, preferred_element_type=F32)
            + jnp.dot(a_hi, b_lo, preferred_element_type=F32))


def _filter_kernel(z_ref, w1_ref, b1_ref, w2_ref, b2_ref, w3_ref, b3_ref, fr_ref, wo_ref,
                   dist_ref, delta_ref, o_ref, a_ref):
    @pl.when(pl.program_id(0) == 0)
    def _():
        fr = fr_ref[...]
        a = jnp.sin(fr * (_dot3(z_ref[...], w1_ref[...]) + b1_ref[...]))
        a = jnp.sin(fr * (_dot3(a, w2_ref[...]) + b2_ref[...]))
        a_ref[...] = jnp.sin(fr * (_dot3(a, w3_ref[...]) + b3_ref[...]))

    k = _dot3(a_ref[...], wo_ref[...])
    k = k * jnp.exp(-dist_ref[...] * delta_ref[...])
    o_ref[...] = k * lax.rsqrt(jnp.sum(k * k, axis=0, keepdims=True) + 1e-6)


def _pad2(x, rows, cols):
    return jnp.pad(x, ((0, rows - x.shape[0]), (0, cols - x.shape[1])))


def hyena_filters(L, f_w1, f_b1, f_w2, f_b2, f_w3, f_b3, f_freq, f_wout, *, tn=512):
    bands = (HY_EMB - 1) // 2
    t = jnp.linspace(0.0, 1.0, L, dtype=F32)[:, None]
    wpos = 2.0 * math.pi * jnp.arange(L, dtype=F32)[:, None] / L
    f = jnp.linspace(1e-4, bands - 1, bands, dtype=F32)[None, :]
    z = jnp.concatenate([t, jnp.cos(f * wpos), -jnp.sin(f * wpos)], axis=-1)
    dist = (jnp.abs(jnp.arange(L) - L // 2).astype(F32) / max(L // 2, 1))[:, None]
    deltas = jnp.linspace(math.log(HY_DECAY_TARGET) / HY_SLOW, math.log(HY_DECAY_TARGET) / HY_FAST,
                          D_MODEL, dtype=F32)
    absdelta = jnp.tile(jnp.abs(deltas), HY_ORDER)[None, :]
    P = LANE
    n = HY_ORDER * D_MODEL
    full = lambda shape: pl.BlockSpec(shape, lambda j: (0, 0))
    args = [_pad2(z, L, P), _pad2(f_w1, P, P), _pad2(f_b1[None], 1, P), _pad2(f_w2, P, P),
            _pad2(f_b2[None], 1, P), _pad2(f_w3, P, P), _pad2(f_b3[None], 1, P),
            _pad2(f_freq[None], 1, P), _pad2(f_wout, P, n), dist, absdelta]
    in_specs = [full((L, P)), full((P, P)), full((1, P)), full((P, P)), full((1, P)), full((P, P)),
                full((1, P)), full((1, P)), pl.BlockSpec((P, tn), lambda j: (0, j)),
                full((L, 1)), pl.BlockSpec((1, tn), lambda j: (0, j))]
    return pl.pallas_call(
        _filter_kernel,
        grid=(n // tn,),
        in_specs=in_specs,
        out_specs=pl.BlockSpec((L, tn), lambda j: (0, j)),
        out_shape=jax.ShapeDtypeStruct((L, n), F32),
        scratch_shapes=[pltpu.VMEM((L, P), F32)],
        compiler_params=_params(1, 8 * _nbytes((L, tn), F32)),
        name="hyena_filters",
    )(*args)


def dft_tables(L, tm):
    n_fft = 2 * L
    half = tm // 2
    r_idx = jnp.arange(2 * L, dtype=jnp.int32)
    odd = 2 * ((r_idx // tm) * half + (r_idx % half)) + 1
    is_sin = ((r_idx % tm) >= half)[:, None]

    def trig(k):
        ang = (k % (2 * n_fft)).astype(F32) * (math.pi / n_fft)
        return jnp.cos(ang), jnp.sin(ang)

    lo = 64
    s1 = jnp.arange(L // lo, dtype=jnp.int32) * lo
    s0 = jnp.arange(lo, dtype=jnp.int32)
    cb, sb = trig(odd[:, None] * s0[None, :])

    def combine(ca, sa):
        a1 = jnp.where(is_sin, sa, ca)
        a2 = jnp.where(is_sin, ca, -sa)
        return a1, a2

    a1, a2 = combine(*trig(odd[:, None] * s1[None, :]))
    fwd = (a1[:, :, None] * cb[:, None, :] + a2[:, :, None] * sb[:, None, :]).reshape(2 * L, L)
    a1, a2 = combine(*trig(odd[:, None] * (s1[None, :] + L // 2)))
    inv = (a1.T[:, None, :] * cb.T[None, :, :] + a2.T[:, None, :] * sb.T[None, :, :]).reshape(
        L, 2 * L)
    return fwd.astype(BF16), inv.astype(BF16)


def _dft_mul_kernel(x_ref, w_ref, *rest, inv_len, conv):
    if conv:
        cw_ref, cb_ref, k_ref, o_ref, wb_ref = rest
    else:
        k_ref, o_ref, wb_ref = rest

    @pl.when(pl.program_id(1) == 0)
    def _():
        w = w_ref[...]
        if conv:
            w = _conv3_seq(w, cw_ref[...], cb_ref[...])
        wb_ref[...] = w.astype(BF16)

    s = jnp.dot(x_ref[...], wb_ref[...], preferred_element_type=F32)
    half = s.shape[0] // 2
    a, b = s[:half], s[half:]
    ka, kb = k_ref[0:half, :], k_ref[half:2 * half, :]
    o_ref[0:half, :] = ((a * ka - b * kb) * inv_len).astype(BF16)
    o_ref[half:2 * half, :] = ((a * kb + b * ka) * inv_len).astype(BF16)


def dft_mul(fwd, v_view, kspec, k_col0, *, L, nb, tm, tn, conv=None):
    d = D_MODEL
    per_b = d // tn
    v, v_row0, v_col0 = v_view
    vr, vc, kc = v_row0 // L, v_col0 // tn, k_col0 // tn
    est = (2 * _nbytes((tm, L), BF16) + 2 * _nbytes((L, tn), F32) + _nbytes((L, tn), BF16)
           + 2 * _nbytes((tm, tn), F32) + 2 * _nbytes((tm, tn), BF16) + 3 * _nbytes((tm, tn), F32))
    in_specs = [pl.BlockSpec((tm, L), lambda j, i: (i, 0)),
                pl.BlockSpec((L, tn), lambda j, i: (j // per_b + vr, j % per_b + vc))]
    args = [fwd, v]
    if conv is not None:
        in_specs += [pl.BlockSpec((HY_SHORT, tn), lambda j, i: (0, j % per_b + vc)),
                     pl.BlockSpec((1, tn), lambda j, i: (0, j % per_b + vc))]
        args += list(conv)
    in_specs.append(pl.BlockSpec((tm, tn), lambda j, i: (i, j % per_b + kc)))
    args.append(kspec)
    return pl.pallas_call(
        functools.partial(_dft_mul_kernel, inv_len=1.0 / L, conv=conv is not None),
        grid=(nb * per_b, (2 * L) // tm),
        in_specs=in_specs,
        out_specs=pl.BlockSpec((tm, tn), lambda j, i: (i, j)),
        out_shape=jax.ShapeDtypeStruct((2 * L, nb * d), BF16),
        scratch_shapes=[pltpu.VMEM((L, tn), BF16)],
        compiler_params=_params(2, est),
        name="dft_mul",
    )(*args)


def _idft_kernel(x_ref, y_ref, *rest, z_conv, per):
    rest = list(rest)
    g_ref, gp_ref, gn_ref, gw_ref, gb_ref = rest[:5]
    rest = rest[5:]
    i = pl.program_id(1)
    first, last = i == 0, i == per - 1
    gate = _conv3_tile(g_ref[...], gp_ref[...], gn_ref[...], gw_ref[...], gb_ref[...], first, last)
    if z_conv:
        z_ref, zp_ref, zn_ref, zw_ref, zb_ref, skip_ref, o_ref = rest
        z = _conv3_tile(z_ref[...], zp_ref[...], zn_ref[...], zw_ref[...], zb_ref[...], first, last)
    else:
        z_ref, skip_ref, o_ref = rest
        z = z_ref[...]
    conv = jnp.dot(x_ref[...], y_ref[...], preferred_element_type=F32)
    o_ref[...] = (gate * (conv + z * skip_ref[...])).astype(o_ref.dtype)


def idft_gate(inv, y, gate_view, z_view, skip, conv, *, z_conv, L, nb, tm, tn, out_dtype):
    d = D_MODEL
    per_b = d // tn
    per = L // tm
    sub = tm // SUBLANE

    def tile_specs(view, with_conv):
        arr, row0, col0 = view
        r0, c0 = row0 // tm, col0 // tn
        last_blk = arr.shape[0] // SUBLANE - 1
        row = lambda j, i: (j // per_b) * per + i + r0
        col = lambda j, i: j % per_b + c0
        specs = [pl.BlockSpec((tm, tn), lambda j, i: (row(j, i), col(j, i)))]
        args = [arr]
        if with_conv:
            specs += [
                pl.BlockSpec((SUBLANE, tn),
                             lambda j, i: (jnp.maximum(row(j, i) * sub - 1, 0), col(j, i))),
                pl.BlockSpec((SUBLANE, tn),
                             lambda j, i: (jnp.minimum((row(j, i) + 1) * sub, last_blk), col(j, i))),
                pl.BlockSpec((HY_SHORT, tn), lambda j, i: (0, col(j, i))),
                pl.BlockSpec((1, tn), lambda j, i: (0, col(j, i)))]
            args += [arr, arr, conv[0], conv[1]]
        return specs, args

    g_specs, g_args = tile_specs(gate_view, True)
    z_specs, z_args = tile_specs(z_view, z_conv)
    est = (2 * _nbytes((tm, 2 * L), BF16) + 2 * _nbytes((2 * L, tn), BF16)
           + 12 * _nbytes((tm, tn), F32))
    return pl.pallas_call(
        functools.partial(_idft_kernel, z_conv=z_conv, per=per),
        grid=(nb * per_b, per),
        in_specs=[pl.BlockSpec((tm, 2 * L), lambda j, i: (i, 0)),
                  pl.BlockSpec((2 * L, tn), lambda j, i: (0, j)),
                  *g_specs, *z_specs,
                  pl.BlockSpec((1, tn), lambda j, i: (0, j % per_b))],
        out_specs=pl.BlockSpec((tm, tn), lambda j, i: ((j // per_b) * per + i, j % per_b)),
        out_shape=jax.ShapeDtypeStruct((nb * L, d), out_dtype),
        compiler_params=_params(2, est),
        name="idft_gate",
    )(inv, y, *g_args, *z_args, skip.reshape(1, d))


def hyena_stream(u, row0, L, nb, conv, filt, skip):
    d = D_MODEL
    tms = min(1024, 2 * L)
    tmi = min(512, L)
    k = hyena_filters(L, *filt)
    fwd, inv = dft_tables(L, tms)
    kspec = linear(fwd, k, tm=tms, tn=1024, name="dft_filters")
    y1 = dft_mul(fwd, (u, row0, 0), kspec, 0, L=L, nb=nb, tm=tms, tn=1024, conv=conv)
    z1 = idft_gate(inv, y1, (u, row0, d), (u, row0, 0), skip[0], conv, z_conv=True, L=L, nb=nb,
                   tm=tmi, tn=1024, out_dtype=F32)
    y2 = dft_mul(fwd, (z1, 0, 0), kspec, d, L=L, nb=nb, tm=tms, tn=1024)
    return idft_gate(inv, y2, (u, row0, 2 * d), (z1, 0, 0), skip[1], conv, z_conv=False, L=L,
                     nb=nb, tm=tmi, tn=1024, out_dtype=BF16)


def hyena_mix(a, n_rows, j, w_in, b_in, conv_w, conv_b, filt, skip):
    u = linear(a, w_in, e=j, tm=1024, tn=1024, bias=b_in, name="hyena_in")
    conv = (conv_w, conv_b.reshape(1, -1))
    z = hyena_stream(u, 0, SEQ, BATCH, conv, filt, skip)
    if n_rows > N_LAT:
        z_ctx = hyena_stream(u, N_LAT, CTX_LEN, BATCH, conv, filt, skip)
        z = jnp.concatenate([z, z_ctx], axis=0)
    return z


def _rms(x, g):
    return x * lax.rsqrt(jnp.mean(x * x, axis=-1, keepdims=True) + RMS_EPS) * g


def _mla_q_kernel(p_ref, g_ref, w1_ref, w2_ref, c_ref, s_ref, o_ref, w1b_ref, w2b_ref):
    @pl.when(pl.program_id(1) == 0)
    def _():
        w1b_ref[...] = w1_ref[...].astype(BF16)
        w2b_ref[...] = w2_ref[...].astype(BF16)

    xn = _rms(p_ref[...], g_ref[...]).astype(BF16)
    q1 = jnp.dot(xn, w1b_ref[...], preferred_element_type=F32)
    q2 = jnp.dot(xn, w2b_ref[...], preferred_element_type=F32)
    reps = q1.shape[1] // MLA_QK_PAD
    c = jnp.concatenate([c_ref[...]] * reps, axis=1)
    s = jnp.concatenate([s_ref[...]] * reps, axis=1)
    o_ref[...] = (q1 * c + q2 * s).astype(BF16)


def _mla_kv_kernel(p_ref, r_ref, g_ref, cs_ref, w_ref, o_ref, wb_ref):
    @pl.when(pl.program_id(1) == 0)
    def _():
        wb_ref[...] = w_ref[...].astype(BF16)

    xn = _rms(p_ref[...], g_ref[...]).astype(BF16)
    rr = (r_ref[...] * cs_ref[...]).astype(BF16)
    acc = (jnp.dot(xn, wb_ref[0:MLA_KV_RANK, :], preferred_element_type=F32)
           + jnp.dot(rr, wb_ref[MLA_KV_RANK:MLA_KV_RANK + 2 * MLA_ROPE, :],
                     preferred_element_type=F32))
    o_ref[...] = acc.astype(BF16)


def _attn_kernel(q_ref, kl_ref, kc_ref, vl_ref, vc_ref, o_ref, *, kchunk):
    c = (MLA_NOPE + MLA_ROPE) ** -0.5 * math.log2(math.e)
    nt = (((1,), (1,)), ((), ()))
    q = q_ref[...]
    n_lat, n_ctx = kl_ref.shape[0], kc_ref.shape[0]
    assert (n_lat + n_ctx) % kchunk == 0 and n_lat % kchunk + n_ctx == kchunk
    m = l = acc = None
    for r in range((n_lat + n_ctx) // kchunk):
        lo, hi = r * kchunk, min((r + 1) * kchunk, n_lat)
        if (r + 1) * kchunk <= n_lat:
            k, v = kl_ref[lo:hi, :], vl_ref[lo:hi, :]
        else:
            k = jnp.concatenate([kl_ref[lo:hi, :], kc_ref[...]], axis=0)
            v = jnp.concatenate([vl_ref[lo:hi, :], vc_ref[...]], axis=0)
        s = lax.dot_general(q, k, nt, preferred_element_type=F32)
        ms = jnp.max(s, axis=-1, keepdims=True)
        m_new = ms if m is None else jnp.maximum(m, ms)
        p = jnp.exp2((s - m_new) * c)
        ps = jnp.sum(p, axis=-1, keepdims=True)
        pv = jnp.dot(p.astype(BF16), v, preferred_element_type=F32)
        if m is None:
            l, acc = ps, pv
        else:
            alpha = jnp.exp2((m - m_new) * c)
            l = alpha * l + ps
            acc = alpha * acc + pv
        m = m_new
    o_ref[...] = (acc / l).astype(o_ref.dtype)


def _rope_tables():
    rows = SEQ // GRID_W
    row = jnp.repeat(jnp.arange(rows), GRID_W).astype(F32)
    col = jnp.tile(jnp.arange(GRID_W), rows).astype(F32)
    n = MLA_ROPE // 4
    inv = ROPE_BASE ** (-jnp.arange(n, dtype=F32) / n)
    ang_row, ang_col = row[:, None] * inv, col[:, None] * inv
    cos = jnp.concatenate([jnp.cos(ang_row)] * 2 + [jnp.cos(ang_col)] * 2, axis=1)
    sin = jnp.concatenate([jnp.sin(ang_row)] * 2 + [jnp.sin(ang_col)] * 2, axis=1)
    return cos, sin


def _rope_partner(w):
    n = MLA_ROPE // 4
    return jnp.concatenate([-w[..., n:2 * n], w[..., 0:n], -w[..., 3 * n:4 * n],
                            w[..., 2 * n:3 * n]], axis=-1)


def mla_mix(a, w_in, q_norm, kv_norm, w_uq, w_ukv):
    H, NP, R, V, QK = MLA_HEADS, MLA_NOPE, MLA_ROPE, MLA_V, MLA_QK_PAD
    n_all = N_LAT + N_CTX
    cos, sin = _rope_tables()
    w_in_x = jnp.concatenate([w_in, _rope_partner(w_in[:, MLA_IN - R:])], axis=1)
    proj = linear(a, w_in_x, tm=512, tn=MLA_IN_PAD, name="mla_in")
    wq = w_uq.reshape(MLA_Q_RANK, H, NP + R)
    zq = jnp.zeros((MLA_Q_RANK, H, QK - NP - R), F32)
    wq1 = jnp.concatenate([wq, zq], axis=2).reshape(MLA_Q_RANK, H * QK)
    wq2 = jnp.concatenate([jnp.zeros((MLA_Q_RANK, H, NP), F32), _rope_partner(wq[..., NP:]), zq],
                          axis=2).reshape(MLA_Q_RANK, H * QK)
    ones, zeros = jnp.ones((SEQ, NP), F32), jnp.zeros((SEQ, QK - NP - R), F32)
    cq = jnp.concatenate([ones, cos, zeros], axis=1)
    sq = jnp.concatenate([0.0 * ones, sin, zeros], axis=1)
    tm, tn = 512, 1024
    per = SEQ // tm
    q = pl.pallas_call(
        _mla_q_kernel,
        grid=(H * QK // tn, N_LAT // tm),
        in_specs=[pl.BlockSpec((tm, MLA_Q_RANK), lambda jn, i: (i, 0)),
                  pl.BlockSpec((1, MLA_Q_RANK), lambda jn, i: (0, 0)),
                  pl.BlockSpec((MLA_Q_RANK, tn), lambda jn, i: (0, jn)),
                  pl.BlockSpec((MLA_Q_RANK, tn), lambda jn, i: (0, jn)),
                  pl.BlockSpec((tm, QK), lambda jn, i: (i % per, 0)),
                  pl.BlockSpec((tm, QK), lambda jn, i: (i % per, 0))],
        out_specs=pl.BlockSpec((tm, tn), lambda jn, i: (i, jn)),
        out_shape=jax.ShapeDtypeStruct((N_LAT, H * QK), BF16),
        scratch_shapes=[pltpu.VMEM((MLA_Q_RANK, tn), BF16), pltpu.VMEM((MLA_Q_RANK, tn), BF16)],
        compiler_params=_params(2, 6 * _nbytes((MLA_Q_RANK, tn), F32) + 8 * _nbytes((tm, tn), F32)),
        name="mla_q",
    )(proj, q_norm.reshape(1, MLA_Q_RANK), wq1, wq2, cq, sq)
    wkv = w_ukv.reshape(MLA_KV_RANK, H, NP + V)
    wk = jnp.concatenate([wkv[..., :NP], jnp.zeros((MLA_KV_RANK, H, QK - NP), F32)],
                         axis=2).reshape(MLA_KV_RANK, H * QK)
    place = jnp.concatenate([jnp.zeros((R, NP), F32), jnp.eye(R, dtype=F32),
                             jnp.zeros((R, QK - NP - R), F32)], axis=1)
    place = jnp.tile(jnp.concatenate([place, place], axis=0), (1, H))
    w_kv = jnp.concatenate([
        jnp.concatenate([wk, wkv[..., NP:].reshape(MLA_KV_RANK, H * V)], axis=1),
        jnp.concatenate([place, jnp.zeros((2 * R, H * V), F32)], axis=1)], axis=0)
    cs = jnp.concatenate([
        jnp.concatenate([cos, sin], axis=1),
        jnp.concatenate([jnp.ones((CTX_LEN, R), F32), jnp.zeros((CTX_LEN, R), F32)], axis=1)],
        axis=0)
    tm, tn = CTX_LEN, 2048
    per = SEQ // tm
    n_kv = H * (QK + V)
    kv = pl.pallas_call(
        _mla_kv_kernel,
        grid=(n_kv // tn, n_all // tm),
        in_specs=[pl.BlockSpec((tm, MLA_KV_RANK), lambda jn, i: (i, MLA_Q_RANK // MLA_KV_RANK)),
                  pl.BlockSpec((tm, 2 * R), lambda jn, i: (i, (MLA_IN - R) // (2 * R))),
                  pl.BlockSpec((1, MLA_KV_RANK), lambda jn, i: (0, 0)),
                  pl.BlockSpec((tm, 2 * R),
                               lambda jn, i: (jnp.where(i * tm < N_LAT, i % per, per), 0)),
                  pl.BlockSpec((MLA_KV_RANK + 2 * R, tn), lambda jn, i: (0, jn))],
        out_specs=pl.BlockSpec((tm, tn), lambda jn, i: (i, jn)),
        out_shape=jax.ShapeDtypeStruct((n_all, n_kv), BF16),
        scratch_shapes=[pltpu.VMEM((MLA_KV_RANK + 2 * R, tn), BF16)],
        compiler_params=_params(2, 4 * _nbytes((MLA_KV_RANK + 2 * R, tn), F32)
                                + 8 * _nbytes((tm, tn), F32)),
        name="mla_kv",
    )(proj, proj, kv_norm.reshape(1, MLA_KV_RANK), cs, w_kv)
    tq = 1024
    kchunk = (SEQ + CTX_LEN) // 3
    per = SEQ // tq
    v0 = H * QK // V
    ctx0 = N_LAT // CTX_LEN
    est = (2 * _nbytes((tq, QK), BF16) + 2 * _nbytes((SEQ + CTX_LEN, QK + V), BF16)
           + 8 * _nbytes((tq, kchunk), F32))
    return pl.pallas_call(
        functools.partial(_attn_kernel, kchunk=kchunk),
        grid=(BATCH, H, per),
        in_specs=[pl.BlockSpec((tq, QK), lambda b, h, i: (b * per + i, h)),
                  pl.BlockSpec((SEQ, QK), lambda b, h, i: (b, h)),
                  pl.BlockSpec((CTX_LEN, QK), lambda b, h, i: (ctx0 + b, h)),
                  pl.BlockSpec((SEQ, V), lambda b, h, i: (b, v0 + h)),
                  pl.BlockSpec((CTX_LEN, V), lambda b, h, i: (ctx0 + b, v0 + h))],
        out_specs=pl.BlockSpec((tq, V), lambda b, h, i: (b * per + i, h)),
        out_shape=jax.ShapeDtypeStruct((N_LAT, H * V), BF16),
        compiler_params=_params(3, est),
        name="mla_attention",
    )(q, kv, kv, kv, kv)


def dense_ffn(a, f, wg, wu, wd):
    p = swiglu_up([a], wg, wu, e=f, tm=1024, tf=512, name="ffn_up")
    return linear(p, wd, e=f, tm=512, tn=512, out_dtype=BF16, name="ffn_down")


def _row_cumsum(onehot):
    n, e = onehot.shape
    blk = 256
    x = onehot.reshape(n // blk, blk, e).astype(BF16)
    tri = jnp.tril(jnp.ones((blk, blk), BF16))
    inner = jnp.einsum("ij,bje->bie", tri, x, preferred_element_type=F32)
    totals = inner[:, -1, :]
    offs = jnp.cumsum(totals, axis=0) - totals
    return (inner + offs[:, None, :]).reshape(n, e).astype(jnp.int32)


def moe_ffn(logits, a_rows, n_rows, f, wg, wu, wd, *, tm=512):
    top_v, top_i = lax.top_k(logits[:, :N_EXPERTS], TOP_K)
    gates = jax.nn.softmax(top_v, axis=-1)
    n_assign = n_rows * TOP_K
    n_tiles = n_assign // tm + N_EXPERTS
    flat_e = top_i.reshape(n_assign)
    onehot = (flat_e[:, None] == jnp.arange(N_EXPERTS)[None, :]).astype(jnp.int32)
    csum = _row_cumsum(onehot)
    rank = jnp.sum(csum * onehot, axis=1) - 1
    counts = csum[-1]
    padded = ((counts + tm - 1) // tm) * tm
    ends = jnp.cumsum(padded)
    starts = ends - padded
    pos = starts[flat_e] + rank
    src = (jnp.arange(n_tiles * tm, dtype=jnp.int32) % n_rows).at[pos].set(
        jnp.arange(n_assign, dtype=jnp.int32) // TOP_K)
    tile_row0 = jnp.arange(n_tiles, dtype=jnp.int32) * tm
    te = jnp.minimum(jnp.sum((tile_row0[:, None] >= ends[None, :]).astype(jnp.int32), axis=1),
                     N_EXPERTS - 1)
    nv = (ends[-1] // tm).astype(jnp.int32).reshape(1)
    te = jnp.where(tile_row0 < ends[-1], te, te[jnp.maximum(nv[0] - 1, 0)]) + f * N_EXPERTS
    x_sorted = a_rows.at[src].get(mode="promise_in_bounds")
    stack = lambda w: w.reshape((-1,) + w.shape[2:])
    p = swiglu_up([x_sorted], stack(wg), stack(wu), tm=tm, tf=1024, te=te, nv=nv, name="moe_up")
    out = linear(p, stack(wd), tm=tm, tn=1024, te=te, nv=nv, name="moe_down")
    pos = pos.reshape(n_rows, TOP_K)
    ys = [out.at[pos[:, k]].get(mode="promise_in_bounds") for k in range(TOP_K)]
    return ys, [gates[:, k:k + 1] for k in range(TOP_K)]


def kernel(x, c, ctx, c_ctx, w_mod, b_mod, ln_g, ln_b, pool_w, pool_scale, hy_w_in, hy_b_in, hy_conv_w, hy_conv_b, hy_f_w1, hy_f_b1, hy_f_w2, hy_f_b2, hy_f_w3, hy_f_b3, hy_f_freq, hy_f_wout, hy_skip, hy_w_out, hy_b_out, mla_w_in, mla_q_norm, mla_kv_norm, mla_w_uq, mla_w_ukv, mla_w_o, ffn_w_gate, ffn_w_up, ffn_w_down, moe_w_router, moe_w_gate, moe_w_up, moe_w_down):
    attn_layers = [i for i in range(DEPTH) if i % N_MIXERS == 2]
    last_read = attn_layers[-1] if attn_layers else -1

    c8 = jnp.concatenate([c, c_ctx[None], jnp.zeros((MOD_ROWS - BATCH - 1, D_MODEL), F32)], axis=0)
    mods = modulation_table(c8, w_mod, b_mod)

    h = jnp.concatenate([x.reshape(N_LAT, D_MODEL), ctx.reshape(N_CTX, D_MODEL)], axis=0)
    a = None
    for i in range(DEPTH):
        kind, j = i % N_MIXERS, i // N_MIXERS
        ctx_out = i < last_read
        n_rows = N_LAT + N_CTX if ctx_out else N_LAT
        m = mods[i]
        f = i // 2
        w_router = moe_w_router[f] if i % 2 == 1 else None
        if kind == 0:
            tl = CTX_LEN if n_rows > N_LAT else 512
            h1, a2, *logits = pool_layer(h, n_rows, m, pool_w[j].astype(BF16), pool_scale[j],
                                         ln_g[i, 0], ln_b[i, 0], tl=tl, w_router=w_router)
        else:
            if kind == 1:
                filt = (hy_f_w1[j], hy_f_b1[j], hy_f_w2[j], hy_f_b2[j], hy_f_w3[j], hy_f_b3[j],
                        hy_f_freq[j], hy_f_wout[j])
                z = hyena_mix(a[:n_rows], n_rows, j, hy_w_in, hy_b_in, hy_conv_w[j], hy_conv_b[j],
                              filt, hy_skip[j])
                w_o, b_o = hy_w_out, hy_b_out
            else:
                assert i <= last_read and not ctx_out and a.shape[0] == N_LAT + N_CTX
                z = mla_mix(a, mla_w_in[j], mla_q_norm[j], mla_kv_norm[j], mla_w_uq[j],
                            mla_w_ukv[j])
                w_o, b_o = mla_w_o, None
            h1, a2, *logits = linear_resid_ln(z, w_o, j, b_o, h, n_rows, m, 2, ln_g[i, 0],
                                              ln_b[i, 0], 4, 3, w_router=w_router)
        nm = mods[min(i + 1, DEPTH - 1)]
        if w_router is not None:
            ys, row_gates = moe_ffn(logits[0], a2, n_rows, f, moe_w_gate, moe_w_up, moe_w_down)
        else:
            ys, row_gates = [dense_ffn(a2, f, ffn_w_gate, ffn_w_up, ffn_w_down)], None
        h, a = resid_ln(h1, ys, n_rows, m, 5, ln_g[i, 1], ln_b[i, 1], nm, 1, 0,
                        row_gates=row_gates)
    return h[:N_LAT].reshape(BATCH, SEQ, D_MODEL)
```

```python
import functools
import math

import jax
import jax.numpy as jnp
from jax import lax
from jax.experimental import pallas as pl
from jax.experimental.pallas import tpu as pltpu

F32 = jnp.float32
BF16 = jnp.bfloat16
U32 = jnp.uint32

D_MODEL = 2048
BATCH = 4
SEQ = 2048
DEPTH = 4
GRID_W = 64
CTX_LEN = 256
N_MIXERS = 3
POOL_WINDOWS = (2, 4, 8, 16)
POOL_GROUP = D_MODEL // len(POOL_WINDOWS)
HY_ORDER = 2
HY_SHORT = 3
HY_EMB = 33
HY_FILTER = 64
HY_DECAY_TARGET = 1e-2
HY_FAST = 0.3
HY_SLOW = 1.5
MLA_HEADS = D_MODEL // 128
MLA_Q_RANK = 512
MLA_KV_RANK = 512
MLA_NOPE = 128
MLA_ROPE = 64
MLA_V = 128
ROPE_BASE = 10000.0
D_FF = 5632
N_EXPERTS = 8
TOP_K = 2
EXPERT_FF = 2 * D_MODEL
LN_EPS = 1e-5
RMS_EPS = 1e-6
DN_ALPHA = (2 * DEPTH) ** 0.25

N_LAT = BATCH * SEQ
N_CTX = BATCH * CTX_LEN
MOD_ROWS = 8
CTX_MOD_ROW = BATCH
LANE = 128
SUBLANE = 8
VMEM_CAP = 60 * 1024 * 1024
MLA_QK_PAD = 256
MLA_IN = MLA_Q_RANK + MLA_KV_RANK + MLA_ROPE
MLA_IN_PAD = MLA_IN + MLA_ROPE


def _vmem(nbytes):
    return int(min(VMEM_CAP, max(16 * 1024 * 1024, nbytes * 3 // 2)))


def _nbytes(shape, dtype):
    return math.prod(shape) * jnp.dtype(dtype).itemsize


def _mod_row(i, tm):
    return jnp.minimum((i * tm) // SEQ, CTX_MOD_ROW)


def _mod_spec(which, tm):
    return pl.BlockSpec((None, 1, D_MODEL), lambda i: (_mod_row(i, tm) * 6 + which, 0, 0))


def _params(sem, est):
    return pltpu.CompilerParams(dimension_semantics=("arbitrary",) * sem,
                                vmem_limit_bytes=_vmem(est))


def _mod_kernel(c_ref, w_ref, b_ref, o_ref):
    c = c_ref[...]
    a = c / (1.0 + jnp.exp(-c))
    o_ref[...] = jnp.dot(a.astype(BF16), w_ref[...].astype(BF16),
                         preferred_element_type=F32) + b_ref[...]


def modulation_table(c8, w_mod, b_mod):
    tn = 1024
    n = 6 * D_MODEL
    out = pl.pallas_call(
        _mod_kernel,
        grid=(DEPTH, n // tn),
        in_specs=[pl.BlockSpec((MOD_ROWS, D_MODEL), lambda l, j: (0, 0)),
                  pl.BlockSpec((None, D_MODEL, tn), lambda l, j: (l, 0, j)),
                  pl.BlockSpec((None, 1, tn), lambda l, j: (l, 0, j))],
        out_specs=pl.BlockSpec((None, MOD_ROWS, tn), lambda l, j: (l, 0, j)),
        out_shape=jax.ShapeDtypeStruct((DEPTH, MOD_ROWS, n), F32),
        compiler_params=_params(2, 3 * _nbytes((D_MODEL, tn), F32)),
        name="modulation_table",
    )(c8, w_mod, b_mod.reshape(DEPTH, 1, n))
    return out.reshape(DEPTH, MOD_ROWS * 6, 1, D_MODEL)


def _group_schedule(te):
    n = te.shape[0]
    first = jnp.concatenate([jnp.ones((1,), jnp.int32), (te[1:] != te[:-1]).astype(jnp.int32)])
    gid = jnp.cumsum(first) - 1
    ge = jnp.zeros((n,), jnp.int32).at[gid].max(te)
    return first, gid, ge, gid[-1:] + 1


def _stream_weights(sched, w_hbms, stages, sem, dsts, col0, tn):
    first_ref, gid_ref, ge_ref, ng_ref = sched
    j, i = pl.program_id(0), pl.program_id(1)
    g, ng = gid_ref[i], ng_ref[0]

    def copies(e, col):
        cols = pl.ds(pl.multiple_of((col + col0) * tn, tn), tn)
        return [pltpu.make_async_copy(w.at[e, :, cols], st, sem.at[k])
                for k, (w, st) in enumerate(zip(w_hbms, stages))]

    @pl.when(first_ref[i] == 1)
    def _():
        @pl.when(jnp.logical_and(j == 0, g == 0))
        def _():
            for c in copies(ge_ref[0], 0):
                c.start()

        for c in copies(ge_ref[g], j):
            c.wait()
        for st, dst in zip(stages, dsts):
            dst[...] = st[...].astype(BF16)
        last = g + 1 >= ng
        next_e = ge_ref[jnp.where(last, 0, g + 1)]
        next_col = jnp.where(last, j + 1, j)

        @pl.when(next_col < pl.num_programs(0))
        def _():
            for c in copies(next_e, next_col):
                c.start()


def _linear_kernel(xt_ref, nv_ref, first_ref, gid_ref, ge_ref, ng_ref, x_ref, w_hbm, *rest,
                   has_bias, col0, tn):
    rest = list(rest)
    b_ref = rest.pop(0) if has_bias else None
    o_ref, stage_ref, wbf_ref, sem = rest
    i = pl.program_id(1)
    _stream_weights((first_ref, gid_ref, ge_ref, ng_ref), [w_hbm], [stage_ref], sem, [wbf_ref],
                    col0, tn)

    @pl.when(i < nv_ref[0])
    def _():
        acc = jnp.dot(x_ref[...].astype(BF16), wbf_ref[...], preferred_element_type=F32)
        if has_bias:
            acc = acc + b_ref[...]
        o_ref[...] = acc.astype(o_ref.dtype)

    @pl.when(i >= nv_ref[0])
    def _():
        o_ref[...] = jnp.zeros_like(o_ref)


def linear(x, w, *, tm, tn, e=0, n_out=None, n_tiles=None, xt=None, te=None, nv=None, bias=None,
           out_dtype=F32, w_col0=0, name="linear"):
    k = x.shape[1]
    if w.ndim == 2:
        w = w[None]
    if n_out is None:
        n_out = w.shape[2] - w_col0
    assert x.shape[0] % tm == 0 and n_out % tn == 0 and w_col0 % tn == 0 and w.shape[1] == k
    assert w.dtype == F32
    if n_tiles is None:
        n_tiles = x.shape[0] // tm
    if xt is None:
        xt = jnp.arange(n_tiles, dtype=jnp.int32)
    if te is None:
        te = jnp.full((n_tiles,), e, jnp.int32)
    if nv is None:
        nv = jnp.full((1,), n_tiles, jnp.int32)
    first, gid, ge, ng = _group_schedule(te)
    has_bias = bias is not None
    in_specs = [pl.BlockSpec((tm, k), lambda j, i, xt, *_: (xt[i], 0)),
                pl.BlockSpec(memory_space=pl.ANY)]
    args = [x, w]
    if has_bias:
        bias = bias.reshape(-1, 1, bias.shape[-1])
        in_specs.append(pl.BlockSpec((None, 1, tn),
                                     lambda j, i, xt, nv, first, gid, ge, ng: (ge[gid[i]], 0, j)))
        args.append(bias)
    est = (2 * _nbytes((tm, k), x.dtype) + _nbytes((k, tn), F32) + _nbytes((k, tn), BF16)
           + _nbytes((tm, k), BF16) + 2 * _nbytes((tm, tn), out_dtype) + 2 * _nbytes((tm, tn), F32))
    return pl.pallas_call(
        functools.partial(_linear_kernel, has_bias=has_bias, col0=w_col0 // tn, tn=tn),
        grid_spec=pltpu.PrefetchScalarGridSpec(
            num_scalar_prefetch=6,
            grid=(n_out // tn, n_tiles),
            in_specs=in_specs,
            out_specs=pl.BlockSpec((tm, tn), lambda j, i, *_: (i, j)),
            scratch_shapes=[pltpu.VMEM((k, tn), F32), pltpu.VMEM((k, tn), BF16),
                            pltpu.SemaphoreType.DMA((1,))]),
        out_shape=jax.ShapeDtypeStruct((n_tiles * tm, n_out), out_dtype),
        compiler_params=_params(2, est),
        name=name,
    )(xt, nv, first, gid, ge, ng, *args)


def _swiglu_kernel(nv_ref, first_ref, gid_ref, ge_ref, ng_ref, *refs, n_parts, tf):
    x_refs = refs[:n_parts]
    wg_hbm, wu_hbm, o_ref, sg_ref, su_ref, wgb_ref, wub_ref, sem = refs[n_parts:]
    i = pl.program_id(1)
    _stream_weights((first_ref, gid_ref, ge_ref, ng_ref), [wg_hbm, wu_hbm], [sg_ref, su_ref], sem,
                    [wgb_ref, wub_ref], 0, tf)

    @pl.when(i < nv_ref[0])
    def _():
        g = u = None
        k0 = 0
        for x_ref in x_refs:
            x = x_ref[...].astype(BF16)
            rows = slice(k0, k0 + x.shape[1])
            gp = jnp.dot(x, wgb_ref[rows, :], preferred_element_type=F32)
            up = jnp.dot(x, wub_ref[rows, :], preferred_element_type=F32)
            g, u = (gp, up) if g is None else (g + gp, u + up)
            k0 += x.shape[1]
        o_ref[...] = ((g / (1.0 + jnp.exp(-g))) * u).astype(o_ref.dtype)

    @pl.when(i >= nv_ref[0])
    def _():
        o_ref[...] = jnp.zeros_like(o_ref)


def swiglu_up(x_parts, wg, wu, *, tm, tf, e=0, te=None, nv=None, name="swiglu_up"):
    m = x_parts[0].shape[0]
    k, f = wg.shape[1], wg.shape[2]
    assert sum(x.shape[1] for x in x_parts) == k and m % tm == 0 and f % tf == 0
    n_tiles = m // tm
    if te is None:
        te = jnp.full((n_tiles,), e, jnp.int32)
    if nv is None:
        nv = jnp.full((1,), n_tiles, jnp.int32)
    first, gid, ge, ng = _group_schedule(te)
    wspec = pl.BlockSpec(memory_space=pl.ANY)
    est = (2 * _nbytes((tm, k), x_parts[0].dtype) + _nbytes((tm, k), BF16)
           + 2 * _nbytes((k, tf), F32) + 2 * _nbytes((k, tf), BF16) + 2 * _nbytes((tm, tf), BF16)
           + 3 * _nbytes((tm, tf), F32))
    x_specs = [pl.BlockSpec((tm, x.shape[1]), lambda j, i, *_: (i, 0)) for x in x_parts]
    return pl.pallas_call(
        functools.partial(_swiglu_kernel, n_parts=len(x_parts), tf=tf),
        grid_spec=pltpu.PrefetchScalarGridSpec(
            num_scalar_prefetch=5,
            grid=(f // tf, n_tiles),
            in_specs=[*x_specs, wspec, wspec],
            out_specs=pl.BlockSpec((tm, tf), lambda j, i, *_: (i, j)),
            scratch_shapes=[pltpu.VMEM((k, tf), F32), pltpu.VMEM((k, tf), F32),
                            pltpu.VMEM((k, tf), BF16), pltpu.VMEM((k, tf), BF16),
                            pltpu.SemaphoreType.DMA((2,))]),
        out_shape=jax.ShapeDtypeStruct((m, f), BF16),
        compiler_params=_params(2, est),
        name=name,
    )(nv, first, gid, ge, ng, *x_parts, wg, wu)


def _ln_mod(v, g, b, sc, sh, pack):
    mu = jnp.mean(v, axis=-1, keepdims=True)
    d = v - mu
    var = jnp.mean(d * d, axis=-1, keepdims=True)
    h = d * lax.rsqrt(var + LN_EPS) * g + b
    a = h * (1.0 + sc) + sh
    return h, (a if pack else a.astype(BF16))


def _resid_ln_kernel(*refs, n_y, pack):
    h_ref = refs[0]
    y_refs = refs[1:1 + n_y]
    rest = refs[1 + n_y:]
    if n_y == 2:
        g0_ref, g1_ref = rest[0], rest[1]
        rest = rest[2:]
        y = g0_ref[...] * y_refs[0][...] + g1_ref[...] * y_refs[1][...]
    else:
        y = y_refs[0][...]
    gate_ref, g_ref, b_ref, sc_ref, sh_ref, o_ref, a_ref = rest
    v = DN_ALPHA * h_ref[...] + gate_ref[...] * y
    h, a = _ln_mod(v, g_ref[...], b_ref[...], sc_ref[...], sh_ref[...], pack)
    o_ref[...] = h
    a_ref[...] = a


def resid_ln(h, ys, n_rows, mods, gate_idx, ln_g, ln_b, nmods, sc_idx, sh_idx, *, row_gates=None,
             pack=False, tm=256):
    assert n_rows % tm == 0
    row = pl.BlockSpec((tm, D_MODEL), lambda i: (i, 0))
    col = pl.BlockSpec((tm, 1), lambda i: (i, 0))
    vec = pl.BlockSpec((1, D_MODEL), lambda i: (0, 0))
    a_w = D_MODEL
    a_dt = F32 if pack else BF16
    est = 10 * _nbytes((tm, D_MODEL), F32)
    in_specs = [row] + [row] * len(ys) + ([col, col] if row_gates else []) + [
        _mod_spec(gate_idx, tm), vec, vec, _mod_spec(sc_idx, tm), _mod_spec(sh_idx, tm)]
    args = [h, *ys, *(row_gates or []), mods, ln_g.reshape(1, D_MODEL), ln_b.reshape(1, D_MODEL),
            nmods, nmods]
    return pl.pallas_call(
        functools.partial(_resid_ln_kernel, n_y=len(ys), pack=pack),
        grid=(n_rows // tm,),
        in_specs=in_specs,
        out_specs=[row, pl.BlockSpec((tm, a_w), lambda i: (i, 0))],
        out_shape=[jax.ShapeDtypeStruct((n_rows, D_MODEL), F32),
                   jax.ShapeDtypeStruct((n_rows, a_w), a_dt)],
        compiler_params=_params(1, est),
        name="resid_ln",
    )(*args)


def _split_bf16(x):
    hi = x.astype(BF16)
    lo = (x - hi.astype(F32)).astype(BF16)
    return hi, lo


def _router_logits(a, whi_ref, wlo_ref):
    a_hi, a_lo = _split_bf16(a)
    w_hi = whi_ref[...]
    return (jnp.dot(a_hi, w_hi, preferred_element_type=F32)
            + jnp.dot(a_lo, w_hi, preferred_element_type=F32)
            + jnp.dot(a_hi, wlo_ref[...], preferred_element_type=F32))


def _router_operands(w_router):
    w = jnp.pad(w_router, ((0, 0), (0, LANE - N_EXPERTS)))
    return _split_bf16(w), pl.BlockSpec((D_MODEL, LANE), lambda i: (0, 0))


def _linear_ln_kernel(*refs, has_bias, moe, split):
    n_src = 1 if split is None else 2
    x_refs, w_ref, rest = refs[:n_src], refs[n_src], list(refs[n_src + 1:])
    bias_ref = rest.pop(0) if has_bias else None
    h_ref, gate_ref, g_ref, b_ref, sc_ref, sh_ref = rest[:6]
    rest = rest[6:]
    if moe:
        whi_ref, wlo_ref, o_ref, a_ref, l_ref, wb_ref = rest
    else:
        o_ref, a_ref, wb_ref = rest

    @pl.when(pl.program_id(0) == 0)
    def _():
        wb_ref[...] = w_ref[...].astype(BF16)

    if split is None:
        x = x_refs[0][...]
    else:
        x = jnp.where(pl.program_id(0) < split, x_refs[0][...], x_refs[1][...])
    y = jnp.dot(x, wb_ref[...], preferred_element_type=F32)
    if has_bias:
        y = y + bias_ref[...]
    v = DN_ALPHA * h_ref[...] + gate_ref[...] * y
    h, a = _ln_mod(v, g_ref[...], b_ref[...], sc_ref[...], sh_ref[...], moe)
    o_ref[...] = h
    a_ref[...] = a
    if moe:
        l_ref[...] = _router_logits(a, whi_ref, wlo_ref)


def linear_resid_ln(xs, w, e, bias, h, n_rows, mods, gate_idx, ln_g, ln_b, sc_idx, sh_idx, *,
                    w_router=None, tm=256):
    k = xs[0].shape[1]
    moe = w_router is not None
    row = pl.BlockSpec((tm, D_MODEL), lambda i: (i, 0))
    vec = pl.BlockSpec((1, D_MODEL), lambda i: (0, 0))
    split = None if len(xs) == 1 else xs[0].shape[0] // tm
    in_specs, t0 = [], 0
    for x in xs:
        nt = x.shape[0] // tm
        in_specs.append(pl.BlockSpec((tm, k), lambda i, t0=t0, nt=nt: (jnp.clip(i - t0, 0, nt - 1), 0)))
        t0 += nt
    in_specs.append(pl.BlockSpec((None, k, D_MODEL), lambda i: (e, 0, 0),
                                 pipeline_mode=pl.Buffered(1)))
    args = [*xs, w]
    if bias is not None:
        in_specs.append(pl.BlockSpec((None, 1, D_MODEL), lambda i: (e, 0, 0)))
        args.append(bias.reshape(-1, 1, D_MODEL))
    in_specs += [row, _mod_spec(gate_idx, tm), vec, vec, _mod_spec(sc_idx, tm),
                 _mod_spec(sh_idx, tm)]
    args += [h, mods, ln_g.reshape(1, D_MODEL), ln_b.reshape(1, D_MODEL), mods, mods]
    out_specs = [row, row]
    out_shape = [jax.ShapeDtypeStruct((n_rows, D_MODEL), F32),
                 jax.ShapeDtypeStruct((n_rows, D_MODEL), F32 if moe else BF16)]
    if moe:
        (w_hi, w_lo), wspec = _router_operands(w_router)
        in_specs += [wspec, wspec]
        args += [w_hi, w_lo]
        out_specs.append(pl.BlockSpec((tm, LANE), lambda i: (i, 0)))
        out_shape.append(jax.ShapeDtypeStruct((n_rows, LANE), F32))
    est = (_nbytes((k, D_MODEL), F32) + _nbytes((k, D_MODEL), BF16) + 2 * _nbytes((tm, k), BF16)
           + 12 * _nbytes((tm, D_MODEL), F32))
    return pl.pallas_call(
        functools.partial(_linear_ln_kernel, has_bias=bias is not None, moe=moe, split=split),
        grid=(n_rows // tm,),
        in_specs=in_specs,
        out_specs=out_specs,
        out_shape=out_shape,
        scratch_shapes=[pltpu.VMEM((k, D_MODEL), BF16)],
        compiler_params=_params(1, est),
        name="linear_resid_ln",
    )(*args)


def _seq_pos(i, tl):
    row0 = i * tl
    is_lat = row0 < N_LAT
    pos = jnp.where(is_lat, row0 % SEQ, (row0 - N_LAT) % CTX_LEN)
    seqlen = jnp.where(is_lat, SEQ, CTX_LEN)
    return pos, seqlen


def _tile_halo_specs(tl, width, t0, nt):
    per = tl // SUBLANE
    local = lambda i: jnp.clip(i - t0, 0, nt - 1)
    main = pl.BlockSpec((tl, width), lambda i: (local(i), 0))
    prev = pl.BlockSpec((SUBLANE, width), lambda i: (jnp.maximum(local(i) * per - 1, 0), 0))
    nxt = pl.BlockSpec((SUBLANE, width),
                       lambda i: (jnp.minimum((local(i) + 1) * per, nt * per - 1), 0))
    return [main, prev, nxt]


def _shift_rows(x, s):
    n = x.shape[0]
    return pltpu.roll(x, s % n, 0)


def _pool_kernel(*refs, tl, moe, split):
    n_src = 1 if split is None else 2
    srcs, rest = refs[:3 * n_src], refs[3 * n_src:]
    (sc_ref, sh_ref, gate_ref, w_ref, ps_ref, g_ref, b_ref, nsc_ref, nsh_ref), rest = rest[:9], rest[9:]
    if moe:
        whi_ref, wlo_ref, o_ref, a_ref, l_ref = rest
    else:
        o_ref, a_ref = rest
    i = pl.program_id(0)
    pos, seqlen = _seq_pos(i, tl)
    if split is None:
        h, h_prev, h_next = (r[...] for r in srcs)
    else:
        h, h_prev, h_next = (jnp.where(i < split, r0[...], r1[...])
                             for r0, r1 in zip(srcs[:3], srcs[3:]))
    sc = 1.0 + sc_ref[...]
    sh = sh_ref[...]
    a = h * sc + sh
    keep_prev = (pos > 0).astype(F32)
    keep_next = (pos + tl < seqlen).astype(F32)
    ext = jnp.concatenate([(h_prev * sc + sh) * keep_prev, a,
                           (h_next * sc + sh) * keep_next], axis=0)
    t = pos + lax.broadcasted_iota(jnp.int32, (tl, 1), 0)
    n_ext = tl + 2 * SUBLANE
    parts = []
    for g, w in enumerate(POOL_WINDOWS):
        cols = slice(g * POOL_GROUP, (g + 1) * POOL_GROUP)
        s = ext[:, cols]
        s = s + _shift_rows(s, 1)
        r = 1
        while 2 * r < w:
            s = _shift_rows(s, n_ext - r) + _shift_rows(s, r)
            r *= 2
        s = s[SUBLANE:SUBLANE + tl]
        cnt = (jnp.minimum(t + w // 2, seqlen) - jnp.maximum(t - w // 2, 0)).astype(F32)
        d = s / cnt - a[:, cols]
        parts.append(jnp.dot(d.astype(BF16), w_ref[g], preferred_element_type=F32))
    y = jnp.concatenate(parts, axis=1) * ps_ref[...]
    v = DN_ALPHA * h + gate_ref[...] * y
    hn, an = _ln_mod(v, g_ref[...], b_ref[...], nsc_ref[...], nsh_ref[...], moe)
    o_ref[...] = hn
    a_ref[...] = an
    if moe:
        l_ref[...] = _router_logits(an, whi_ref, wlo_ref)


def pool_layer(hs, n_rows, mods, w_grp_bf16, pool_scale, ln_g, ln_b, *, tl, w_router=None):
    moe = w_router is not None
    row = pl.BlockSpec((tl, D_MODEL), lambda i: (i, 0))
    vec = pl.BlockSpec((1, D_MODEL), lambda i: (0, 0))
    split = None if len(hs) == 1 else hs[0].shape[0] // tl
    in_specs, args, t0 = [], [], 0
    for h in hs:
        nt = h.shape[0] // tl
        in_specs += _tile_halo_specs(tl, D_MODEL, t0, nt)
        args += [h, h, h]
        t0 += nt
    in_specs += [_mod_spec(1, tl), _mod_spec(0, tl), _mod_spec(2, tl),
                 pl.BlockSpec(w_grp_bf16.shape, lambda i: (0, 0, 0)), vec, vec, vec,
                 _mod_spec(4, tl), _mod_spec(3, tl)]
    args += [mods, mods, mods, w_grp_bf16, pool_scale.reshape(1, D_MODEL),
             ln_g.reshape(1, D_MODEL), ln_b.reshape(1, D_MODEL), mods, mods]
    out_specs = [row, row]
    out_shape = [jax.ShapeDtypeStruct((n_rows, D_MODEL), F32),
                 jax.ShapeDtypeStruct((n_rows, D_MODEL), F32 if moe else BF16)]
    if moe:
        (w_hi, w_lo), wspec = _router_operands(w_router)
        in_specs += [wspec, wspec]
        args += [w_hi, w_lo]
        out_specs.append(pl.BlockSpec((tl, LANE), lambda i: (i, 0)))
        out_shape.append(jax.ShapeDtypeStruct((n_rows, LANE), F32))
    est = 14 * _nbytes((tl, D_MODEL), F32) + 2 * _nbytes(w_grp_bf16.shape, BF16)
    return pl.pallas_call(
        functools.partial(_pool_kernel, tl=tl, moe=moe, split=split),
        grid=(n_rows // tl,),
        in_specs=in_specs,
        out_specs=out_specs,
        out_shape=out_shape,
        compiler_params=_params(1, est),
        name="pool_layer",
    )(*args)


def _conv3_seq(u, w, b):
    n = u.shape[0]
    t = lax.broadcasted_iota(jnp.int32, (n, 1), 0)
    before = jnp.where(t == 0, 0.0, _shift_rows(u, 1))
    after = jnp.where(t == n - 1, 0.0, _shift_rows(u, n - 1))
    return b + before * w[0:1] + u * w[1:2] + after * w[2:3]


def _conv3_tile(u, prev, nxt, w, b, first, last):
    n = u.shape[0]
    t = lax.broadcasted_iota(jnp.int32, (n, 1), 0)
    row_before = prev[SUBLANE - 1:SUBLANE, :] * (1.0 - first.astype(F32))
    row_after = nxt[0:1, :] * (1.0 - last.astype(F32))
    before = jnp.where(t == 0, row_before, _shift_rows(u, 1))
    after = jnp.where(t == n - 1, row_after, _shift_rows(u, n - 1))
    return b + before * w[0:1] + u * w[1:2] + after * w[2:3]


def _dot3(a, b):
    a_hi, a_lo = _split_bf16(a)
    b_hi, b_lo = _split_bf16(b)
    return (jnp.dot(a_hi, b_hi, preferred_element_type=F32)
            + jnp.dot(a_lo, b_hi, preferred_element_type=F32)
            + jnp.dot(a_hi, b_lo, preferred_element_type=F32))


def _filter_kernel(z_ref, w1_ref, b1_ref, w2_ref, b2_ref, w3_ref, b3_ref, fr_ref, wo_ref,
                   dist_ref, delta_ref, o_ref, a_ref):
    @pl.when(pl.program_id(0) == 0)
    def _():
        fr = fr_ref[...]
        a = jnp.sin(fr * (_dot3(z_ref[...], w1_ref[...]) + b1_ref[...]))
        a = jnp.sin(fr * (_dot3(a, w2_ref[...]) + b2_ref[...]))
        a_ref[...] = jnp.sin(fr * (_dot3(a, w3_ref[...]) + b3_ref[...]))

    k = _dot3(a_ref[...], wo_ref[...])
    k = k * jnp.exp(-dist_ref[...] * delta_ref[...])
    o_ref[...] = k * lax.rsqrt(jnp.sum(k * k, axis=0, keepdims=True) + 1e-6)


def _pad2(x, rows, cols):
    return jnp.pad(x, ((0, rows - x.shape[0]), (0, cols - x.shape[1])))


def hyena_filters(L, f_w1, f_b1, f_w2, f_b2, f_w3, f_b3, f_freq, f_wout, *, tn=512):
    bands = (HY_EMB - 1) // 2
    t = jnp.linspace(0.0, 1.0, L, dtype=F32)[:, None]
    wpos = 2.0 * math.pi * jnp.arange(L, dtype=F32)[:, None] / L
    f = jnp.linspace(1e-4, bands - 1, bands, dtype=F32)[None, :]
    z = jnp.concatenate([t, jnp.cos(f * wpos), -jnp.sin(f * wpos)], axis=-1)
    dist = (jnp.abs(jnp.arange(L) - L // 2).astype(F32) / max(L // 2, 1))[:, None]
    deltas = jnp.linspace(math.log(HY_DECAY_TARGET) / HY_SLOW, math.log(HY_DECAY_TARGET) / HY_FAST,
                          D_MODEL, dtype=F32)
    absdelta = jnp.tile(jnp.abs(deltas), HY_ORDER)[None, :]
    P = LANE
    n = HY_ORDER * D_MODEL
    full = lambda shape: pl.BlockSpec(shape, lambda j: (0, 0))
    args = [_pad2(z, L, P), _pad2(f_w1, P, P), _pad2(f_b1[None], 1, P), _pad2(f_w2, P, P),
            _pad2(f_b2[None], 1, P), _pad2(f_w3, P, P), _pad2(f_b3[None], 1, P),
            _pad2(f_freq[None], 1, P), _pad2(f_wout, P, n), dist, absdelta]
    in_specs = [full((L, P)), full((P, P)), full((1, P)), full((P, P)), full((1, P)), full((P, P)),
                full((1, P)), full((1, P)), pl.BlockSpec((P, tn), lambda j: (0, j)),
                full((L, 1)), pl.BlockSpec((1, tn), lambda j: (0, j))]
    return pl.pallas_call(
        _filter_kernel,
        grid=(n // tn,),
        in_specs=in_specs,
        out_specs=pl.BlockSpec((L, tn), lambda j: (0, j)),
        out_shape=jax.ShapeDtypeStruct((L, n), F32),
        scratch_shapes=[pltpu.VMEM((L, P), F32)],
        compiler_params=_params(1, 8 * _nbytes((L, tn), F32)),
        name="hyena_filters",
    )(*args)


def dft_tables(L, tm):
    n_fft = 2 * L
    half = tm // 2
    r_idx = jnp.arange(2 * L, dtype=jnp.int32)
    odd = 2 * ((r_idx // tm) * half + (r_idx % half)) + 1
    is_sin = ((r_idx % tm) >= half)[:, None]

    def trig(k):
        ang = (k % (2 * n_fft)).astype(F32) * (math.pi / n_fft)
        return jnp.cos(ang), jnp.sin(ang)

    lo = 64
    s1 = jnp.arange(L // lo, dtype=jnp.int32) * lo
    s0 = jnp.arange(lo, dtype=jnp.int32)
    cb, sb = trig(odd[:, None] * s0[None, :])

    def combine(ca, sa):
        a1 = jnp.where(is_sin, sa, ca)
        a2 = jnp.where(is_sin, ca, -sa)
        return a1, a2

    a1, a2 = combine(*trig(odd[:, None] * s1[None, :]))
    coarse = lambda t: jnp.repeat(t, lo, axis=1)
    fine = lambda t: jnp.tile(t, (1, L // lo))
    fwd = coarse(a1) * fine(cb) + coarse(a2) * fine(sb)
    a1, a2 = combine(*trig(odd[:, None] * (s1[None, :] + L // 2)))
    inv = (a1.T[:, None, :] * cb.T[None, :, :] + a2.T[:, None, :] * sb.T[None, :, :]).reshape(
        L, 2 * L)
    return fwd.astype(BF16), inv.astype(BF16)


def _dft_mul_kernel(x_ref, w_ref, *rest, inv_len, conv):
    if conv:
        cw_ref, cb_ref, k_ref, o_ref, wb_ref = rest
    else:
        k_ref, o_ref, wb_ref = rest

    @pl.when(pl.program_id(1) == 0)
    def _():
        w = w_ref[...]
        if conv:
            w = _conv3_seq(w, cw_ref[...], cb_ref[...])
        wb_ref[...] = w.astype(BF16)

    s = jnp.dot(x_ref[...], wb_ref[...], preferred_element_type=F32)
    half = s.shape[0] // 2
    a, b = s[:half], s[half:]
    ka, kb = k_ref[0:half, :], k_ref[half:2 * half, :]
    o_ref[0:half, :] = ((a * ka - b * kb) * inv_len).astype(BF16)
    o_ref[half:2 * half, :] = ((a * kb + b * ka) * inv_len).astype(BF16)


def dft_mul(fwd, v_view, kspec, k_col0, *, L, nb, tm, tn, conv=None):
    d = D_MODEL
    per_b = d // tn
    v, v_row0, v_col0 = v_view
    vr, vc, kc = v_row0 // L, v_col0 // tn, k_col0 // tn
    est = (2 * _nbytes((tm, L), BF16) + 2 * _nbytes((L, tn), F32) + _nbytes((L, tn), BF16)
           + 2 * _nbytes((tm, tn), F32) + 2 * _nbytes((tm, tn), BF16) + 3 * _nbytes((tm, tn), F32))
    in_specs = [pl.BlockSpec((tm, L), lambda j, i: (i, 0)),
                pl.BlockSpec((L, tn), lambda j, i: (j // per_b + vr, j % per_b + vc))]
    args = [fwd, v]
    if conv is not None:
        in_specs += [pl.BlockSpec((HY_SHORT, tn), lambda j, i: (0, j % per_b + vc)),
                     pl.BlockSpec((1, tn), lambda j, i: (0, j % per_b + vc))]
        args += list(conv)
    in_specs.append(pl.BlockSpec((tm, tn), lambda j, i: (i, j % per_b + kc)))
    args.append(kspec)
    return pl.pallas_call(
        functools.partial(_dft_mul_kernel, inv_len=1.0 / L, conv=conv is not None),
        grid=(nb * per_b, (2 * L) // tm),
        in_specs=in_specs,
        out_specs=pl.BlockSpec((tm, tn), lambda j, i: (i, j)),
        out_shape=jax.ShapeDtypeStruct((2 * L, nb * d), BF16),
        scratch_shapes=[pltpu.VMEM((L, tn), BF16)],
        compiler_params=_params(2, est),
        name="dft_mul",
    )(*args)


def _idft_kernel(x_ref, y_ref, *rest, z_conv, per):
    rest = list(rest)
    g_ref, gp_ref, gn_ref, gw_ref, gb_ref = rest[:5]
    rest = rest[5:]
    i = pl.program_id(1)
    first, last = i == 0, i == per - 1
    conv = jnp.dot(x_ref[...], y_ref[...], preferred_element_type=F32)
    gate = _conv3_tile(g_ref[...], gp_ref[...], gn_ref[...], gw_ref[...], gb_ref[...], first, last)
    if z_conv:
        z_ref, zp_ref, zn_ref, zw_ref, zb_ref, skip_ref, o_ref = rest
        z = _conv3_tile(z_ref[...], zp_ref[...], zn_ref[...], zw_ref[...], zb_ref[...], first, last)
    else:
        z_ref, skip_ref, o_ref = rest
        z = z_ref[...]
    o_ref[...] = (gate * (conv + z * skip_ref[...])).astype(o_ref.dtype)


def idft_gate(inv, y, gate_view, z_view, skip, conv, *, z_conv, L, nb, tm, tn, out_dtype):
    d = D_MODEL
    per_b = d // tn
    per = L // tm
    sub = tm // SUBLANE

    def tile_specs(view, with_conv):
        arr, row0, col0 = view
        r0, c0 = row0 // tm, col0 // tn
        last_blk = arr.shape[0] // SUBLANE - 1
        row = lambda j, i: (j // per_b) * per + i + r0
        col = lambda j, i: j % per_b + c0
        specs = [pl.BlockSpec((tm, tn), lambda j, i: (row(j, i), col(j, i)))]
        args = [arr]
        if with_conv:
            specs += [
                pl.BlockSpec((SUBLANE, tn),
                             lambda j, i: (jnp.maximum(row(j, i) * sub - 1, 0), col(j, i))),
                pl.BlockSpec((SUBLANE, tn),
                             lambda j, i: (jnp.minimum((row(j, i) + 1) * sub, last_blk), col(j, i))),
                pl.BlockSpec((HY_SHORT, tn), lambda j, i: (0, col(j, i))),
                pl.BlockSpec((1, tn), lambda j, i: (0, col(j, i)))]
            args += [arr, arr, conv[0], conv[1]]
        return specs, args

    g_specs, g_args = tile_specs(gate_view, True)
    z_specs, z_args = tile_specs(z_view, z_conv)
    est = (2 * _nbytes((tm, 2 * L), BF16) + 2 * _nbytes((2 * L, tn), BF16)
           + 12 * _nbytes((tm, tn), F32))
    return pl.pallas_call(
        functools.partial(_idft_kernel, z_conv=z_conv, per=per),
        grid=(nb * per_b, per),
        in_specs=[pl.BlockSpec((tm, 2 * L), lambda j, i: (i, 0)),
                  pl.BlockSpec((2 * L, tn), lambda j, i: (0, j)),
                  *g_specs, *z_specs,
                  pl.BlockSpec((1, tn), lambda j, i: (0, j % per_b))],
        out_specs=pl.BlockSpec((tm, tn), lambda j, i: ((j // per_b) * per + i, j % per_b)),
        out_shape=jax.ShapeDtypeStruct((nb * L, d), out_dtype),
        compiler_params=_params(2, est),
        name="idft_gate",
    )(inv, y, *g_args, *z_args, skip.reshape(1, d))


def hyena_stream(u, row0, L, nb, conv, filt, skip):
    d = D_MODEL
    tms = min(1024, 2 * L)
    tmi = min(512, L)
    k = hyena_filters(L, *filt)
    fwd, inv = dft_tables(L, tms)
    kspec = linear(fwd, k, tm=tms, tn=1024, name="dft_filters")
    y1 = dft_mul(fwd, (u, row0, 0), kspec, 0, L=L, nb=nb, tm=tms, tn=1024, conv=conv)
    z1 = idft_gate(inv, y1, (u, row0, d), (u, row0, 0), skip[0], conv, z_conv=True, L=L, nb=nb,
                   tm=tmi, tn=1024, out_dtype=F32)
    y2 = dft_mul(fwd, (z1, 0, 0), kspec, d, L=L, nb=nb, tm=tms, tn=1024)
    return idft_gate(inv, y2, (u, row0, 2 * d), (z1, 0, 0), skip[1], conv, z_conv=False, L=L,
                     nb=nb, tm=tmi, tn=1024, out_dtype=BF16)


def hyena_mix(a, n_rows, j, w_in, b_in, conv_w, conv_b, filt, skip):
    u = linear(a, w_in, e=j, tm=1024, tn=1024, bias=b_in, name="hyena_in")
    conv = (conv_w, conv_b.reshape(1, -1))
    zs = [hyena_stream(u, 0, SEQ, BATCH, conv, filt, skip)]
    if n_rows > N_LAT:
        zs.append(hyena_stream(u, N_LAT, CTX_LEN, BATCH, conv, filt, skip))
    return zs


def _rms(x, g):
    return x * lax.rsqrt(jnp.mean(x * x, axis=-1, keepdims=True) + RMS_EPS) * g


def _mla_q_kernel(p_ref, g_ref, w1_ref, w2_ref, c_ref, s_ref, o_ref, w1b_ref, w2b_ref):
    @pl.when(pl.program_id(1) == 0)
    def _():
        w1b_ref[...] = w1_ref[...].astype(BF16)
        w2b_ref[...] = w2_ref[...].astype(BF16)

    xn = _rms(p_ref[...], g_ref[...]).astype(BF16)
    q1 = jnp.dot(xn, w1b_ref[...], preferred_element_type=F32)
    q2 = jnp.dot(xn, w2b_ref[...], preferred_element_type=F32)
    reps = q1.shape[1] // MLA_QK_PAD
    c = jnp.concatenate([c_ref[...]] * reps, axis=1)
    s = jnp.concatenate([s_ref[...]] * reps, axis=1)
    o_ref[...] = (q1 * c + q2 * s).astype(BF16)


def _mla_kv_kernel(p_ref, r_ref, g_ref, cs_ref, w_ref, o_ref, wb_ref):
    @pl.when(pl.program_id(1) == 0)
    def _():
        wb_ref[...] = w_ref[...].astype(BF16)

    xn = _rms(p_ref[...], g_ref[...]).astype(BF16)
    rr = (r_ref[...] * cs_ref[...]).astype(BF16)
    acc = (jnp.dot(xn, wb_ref[0:MLA_KV_RANK, :], preferred_element_type=F32)
           + jnp.dot(rr, wb_ref[MLA_KV_RANK:MLA_KV_RANK + 2 * MLA_ROPE, :],
                     preferred_element_type=F32))
    o_ref[...] = acc.astype(BF16)


def _attn_kernel(q_ref, kl_ref, kc_ref, vl_ref, vc_ref, o_ref, *, kchunk):
    c = (MLA_NOPE + MLA_ROPE) ** -0.5 * math.log2(math.e)
    nt = (((1,), (1,)), ((), ()))
    q = q_ref[...]
    n_lat, n_ctx = kl_ref.shape[0], kc_ref.shape[0]
    assert (n_lat + n_ctx) % kchunk == 0 and n_lat % kchunk + n_ctx == kchunk
    m = l = acc = None
    for r in range((n_lat + n_ctx) // kchunk):
        lo, hi = r * kchunk, min((r + 1) * kchunk, n_lat)
        if (r + 1) * kchunk <= n_lat:
            k, v = kl_ref[lo:hi, :], vl_ref[lo:hi, :]
        else:
            k = jnp.concatenate([kl_ref[lo:hi, :], kc_ref[...]], axis=0)
            v = jnp.concatenate([vl_ref[lo:hi, :], vc_ref[...]], axis=0)
        s = lax.dot_general(q, k, nt, preferred_element_type=F32)
        ms = jnp.max(s, axis=-1, keepdims=True)
        m_new = ms if m is None else jnp.maximum(m, ms)
        p = jnp.exp2((s - m_new) * c)
        ps = jnp.sum(p, axis=-1, keepdims=True)
        pv = jnp.dot(p.astype(BF16), v, preferred_element_type=F32)
        if m is None:
            l, acc = ps, pv
        else:
            alpha = jnp.exp2((m - m_new) * c)
            l = alpha * l + ps
            acc = alpha * acc + pv
        m = m_new
    o_ref[...] = (acc / l).astype(o_ref.dtype)


def _rope_tables():
    rows = SEQ // GRID_W
    row = jnp.repeat(jnp.arange(rows), GRID_W).astype(F32)
    col = jnp.tile(jnp.arange(GRID_W), rows).astype(F32)
    n = MLA_ROPE // 4
    inv = ROPE_BASE ** (-jnp.arange(n, dtype=F32) / n)
    ang_row, ang_col = row[:, None] * inv, col[:, None] * inv
    cos = jnp.concatenate([jnp.cos(ang_row)] * 2 + [jnp.cos(ang_col)] * 2, axis=1)
    sin = jnp.concatenate([jnp.sin(ang_row)] * 2 + [jnp.sin(ang_col)] * 2, axis=1)
    return cos, sin


def _rope_partner(w):
    n = MLA_ROPE // 4
    pairs = w.reshape(w.shape[:-1] + (2, 2, n))
    sign = jnp.array([-1.0, 1.0], F32)[:, None]
    return (pairs[..., ::-1, :] * sign).reshape(w.shape)


def mla_mix(a, w_in, q_norm, kv_norm, w_uq, w_ukv):
    H, NP, R, V, QK = MLA_HEADS, MLA_NOPE, MLA_ROPE, MLA_V, MLA_QK_PAD
    n_all = N_LAT + N_CTX
    cos, sin = _rope_tables()
    w_in_x = jnp.concatenate([w_in, _rope_partner(w_in[:, MLA_IN - R:])], axis=1)
    proj = linear(a, w_in_x, tm=512, tn=MLA_IN_PAD, name="mla_in")
    wq = w_uq.reshape(MLA_Q_RANK, H, NP + R)
    tail = QK - NP - R
    wq1 = jnp.pad(wq, ((0, 0), (0, 0), (0, tail))).reshape(MLA_Q_RANK, H * QK)
    wq2 = jnp.pad(_rope_partner(wq[..., NP:]),
                  ((0, 0), (0, 0), (NP, tail))).reshape(MLA_Q_RANK, H * QK)
    ones, zeros = jnp.ones((SEQ, NP), F32), jnp.zeros((SEQ, QK - NP - R), F32)
    cq = jnp.concatenate([ones, cos, zeros], axis=1)
    sq = jnp.concatenate([0.0 * ones, sin, zeros], axis=1)
    tm, tn = 512, 1024
    per = SEQ // tm
    q = pl.pallas_call(
        _mla_q_kernel,
        grid=(H * QK // tn, N_LAT // tm),
        in_specs=[pl.BlockSpec((tm, MLA_Q_RANK), lambda jn, i: (i, 0)),
                  pl.BlockSpec((1, MLA_Q_RANK), lambda jn, i: (0, 0)),
                  pl.BlockSpec((MLA_Q_RANK, tn), lambda jn, i: (0, jn)),
                  pl.BlockSpec((MLA_Q_RANK, tn), lambda jn, i: (0, jn)),
                  pl.BlockSpec((tm, QK), lambda jn, i: (i % per, 0)),
                  pl.BlockSpec((tm, QK), lambda jn, i: (i % per, 0))],
        out_specs=pl.BlockSpec((tm, tn), lambda jn, i: (i, jn)),
        out_shape=jax.ShapeDtypeStruct((N_LAT, H * QK), BF16),
        scratch_shapes=[pltpu.VMEM((MLA_Q_RANK, tn), BF16), pltpu.VMEM((MLA_Q_RANK, tn), BF16)],
        compiler_params=_params(2, 6 * _nbytes((MLA_Q_RANK, tn), F32) + 8 * _nbytes((tm, tn), F32)),
        name="mla_q",
    )(proj, q_norm.reshape(1, MLA_Q_RANK), wq1, wq2, cq, sq)
    wkv = w_ukv.reshape(MLA_KV_RANK, H, NP + V)
    wk = jnp.concatenate([wkv[..., :NP], jnp.zeros((MLA_KV_RANK, H, QK - NP), F32)],
                         axis=2).reshape(MLA_KV_RANK, H * QK)
    place = jnp.concatenate([jnp.zeros((R, NP), F32), jnp.eye(R, dtype=F32),
                             jnp.zeros((R, QK - NP - R), F32)], axis=1)
    place = jnp.tile(jnp.concatenate([place, place], axis=0), (1, H))
    w_kv = jnp.concatenate([
        jnp.concatenate([wk, wkv[..., NP:].reshape(MLA_KV_RANK, H * V)], axis=1),
        jnp.concatenate([place, jnp.zeros((2 * R, H * V), F32)], axis=1)], axis=0)
    cs = jnp.concatenate([
        jnp.concatenate([cos, sin], axis=1),
        jnp.concatenate([jnp.ones((CTX_LEN, R), F32), jnp.zeros((CTX_LEN, R), F32)], axis=1)],
        axis=0)
    tm, tn = CTX_LEN, 2048
    per = SEQ // tm
    n_kv = H * (QK + V)
    kv = pl.pallas_call(
        _mla_kv_kernel,
        grid=(n_kv // tn, n_all // tm),
        in_specs=[pl.BlockSpec((tm, MLA_KV_RANK), lambda jn, i: (i, MLA_Q_RANK // MLA_KV_RANK)),
                  pl.BlockSpec((tm, 2 * R), lambda jn, i: (i, (MLA_IN - R) // (2 * R))),
                  pl.BlockSpec((1, MLA_KV_RANK), lambda jn, i: (0, 0)),
                  pl.BlockSpec((tm, 2 * R),
                               lambda jn, i: (jnp.where(i * tm < N_LAT, i % per, per), 0)),
                  pl.BlockSpec((MLA_KV_RANK + 2 * R, tn), lambda jn, i: (0, jn))],
        out_specs=pl.BlockSpec((tm, tn), lambda jn, i: (i, jn)),
        out_shape=jax.ShapeDtypeStruct((n_all, n_kv), BF16),
        scratch_shapes=[pltpu.VMEM((MLA_KV_RANK + 2 * R, tn), BF16)],
        compiler_params=_params(2, 4 * _nbytes((MLA_KV_RANK + 2 * R, tn), F32)
                                + 8 * _nbytes((tm, tn), F32)),
        name="mla_kv",
    )(proj, proj, kv_norm.reshape(1, MLA_KV_RANK), cs, w_kv)
    tq = 1024
    kchunk = (SEQ + CTX_LEN) // 3
    per = SEQ // tq
    v0 = H * QK // V
    ctx0 = N_LAT // CTX_LEN
    est = (2 * _nbytes((tq, QK), BF16) + 2 * _nbytes((SEQ + CTX_LEN, QK + V), BF16)
           + 8 * _nbytes((tq, kchunk), F32))
    return pl.pallas_call(
        functools.partial(_attn_kernel, kchunk=kchunk),
        grid=(BATCH, H, per),
        in_specs=[pl.BlockSpec((tq, QK), lambda b, h, i: (b * per + i, h)),
                  pl.BlockSpec((SEQ, QK), lambda b, h, i: (b, h)),
                  pl.BlockSpec((CTX_LEN, QK), lambda b, h, i: (ctx0 + b, h)),
                  pl.BlockSpec((SEQ, V), lambda b, h, i: (b, v0 + h)),
                  pl.BlockSpec((CTX_LEN, V), lambda b, h, i: (ctx0 + b, v0 + h))],
        out_specs=pl.BlockSpec((tq, V), lambda b, h, i: (b * per + i, h)),
        out_shape=jax.ShapeDtypeStruct((N_LAT, H * V), BF16),
        compiler_params=_params(3, est),
        name="mla_attention",
    )(q, kv, kv, kv, kv)


def dense_ffn(a, f, wg, wu, wd):
    p = swiglu_up([a], wg, wu, e=f, tm=1024, tf=512, name="ffn_up")
    return linear(p, wd, e=f, tm=512, tn=512, out_dtype=BF16, name="ffn_down")


def _row_cumsum(onehot):
    n, e = onehot.shape
    blk = 256
    x = onehot.reshape(n // blk, blk, e).astype(BF16)
    tri = jnp.tril(jnp.ones((blk, blk), BF16))
    inner = jnp.einsum("ij,bje->bie", tri, x, preferred_element_type=F32)
    totals = inner[:, -1, :]
    offs = jnp.cumsum(totals, axis=0) - totals
    return (inner + offs[:, None, :]).reshape(n, e).astype(jnp.int32)


def moe_ffn(logits, a_rows, n_rows, f, wg, wu, wd, *, tm=512):
    top_v, top_i = lax.top_k(logits[:, :N_EXPERTS], TOP_K)
    gates = jax.nn.softmax(top_v, axis=-1)
    n_assign = n_rows * TOP_K
    n_tiles = n_assign // tm + N_EXPERTS
    flat_e = top_i.reshape(n_assign)
    onehot = (flat_e[:, None] == jnp.arange(N_EXPERTS)[None, :]).astype(jnp.int32)
    csum = _row_cumsum(onehot)
    rank = jnp.sum(csum * onehot, axis=1) - 1
    counts = csum[-1]
    padded = ((counts + tm - 1) // tm) * tm
    ends = jnp.cumsum(padded)
    starts = ends - padded
    pos = starts[flat_e] + rank
    src = (jnp.arange(n_tiles * tm, dtype=jnp.int32) % n_rows).at[pos].set(
        jnp.arange(n_assign, dtype=jnp.int32) // TOP_K)
    tile_row0 = jnp.arange(n_tiles, dtype=jnp.int32) * tm
    te = jnp.minimum(jnp.sum((tile_row0[:, None] >= ends[None, :]).astype(jnp.int32), axis=1),
                     N_EXPERTS - 1)
    nv = (ends[-1] // tm).astype(jnp.int32).reshape(1)
    te = jnp.where(tile_row0 < ends[-1], te, te[jnp.maximum(nv[0] - 1, 0)]) + f * N_EXPERTS
    x_sorted = a_rows.at[src].get(mode="promise_in_bounds")
    stack = lambda w: w.reshape((-1,) + w.shape[2:])
    p = swiglu_up([x_sorted], stack(wg), stack(wu), tm=tm, tf=1024, te=te, nv=nv, name="moe_up")
    out = linear(p, stack(wd), tm=tm, tn=1024, te=te, nv=nv, name="moe_down")
    pos = pos.reshape(n_rows, TOP_K)
    ys = [out.at[pos[:, k]].get(mode="promise_in_bounds") for k in range(TOP_K)]
    return ys, [gates[:, k:k + 1] for k in range(TOP_K)]


def kernel(x, c, ctx, c_ctx, w_mod, b_mod, ln_g, ln_b, pool_w, pool_scale, hy_w_in, hy_b_in, hy_conv_w, hy_conv_b, hy_f_w1, hy_f_b1, hy_f_w2, hy_f_b2, hy_f_w3, hy_f_b3, hy_f_freq, hy_f_wout, hy_skip, hy_w_out, hy_b_out, mla_w_in, mla_q_norm, mla_kv_norm, mla_w_uq, mla_w_ukv, mla_w_o, ffn_w_gate, ffn_w_up, ffn_w_down, moe_w_router, moe_w_gate, moe_w_up, moe_w_down):
    attn_layers = [i for i in range(DEPTH) if i % N_MIXERS == 2]
    last_read = attn_layers[-1] if attn_layers else -1

    c8 = jnp.concatenate([c, c_ctx[None], jnp.zeros((MOD_ROWS - BATCH - 1, D_MODEL), F32)], axis=0)
    mods = modulation_table(c8, w_mod, b_mod)

    assert 0 < last_read and N_MIXERS > 1
    h = [x.reshape(N_LAT, D_MODEL), ctx.reshape(N_CTX, D_MODEL)]
    a = None
    for i in range(DEPTH):
        kind, j = i % N_MIXERS, i // N_MIXERS
        ctx_out = i < last_read
        n_rows = N_LAT + N_CTX if ctx_out else N_LAT
        m = mods[i]
        f = i // 2
        w_router = moe_w_router[f] if i % 2 == 1 else None
        if kind == 0:
            tl = CTX_LEN if n_rows > N_LAT else 512
            h1, a2, *logits = pool_layer(h if isinstance(h, list) else [h], n_rows, m,
                                         pool_w[j].astype(BF16), pool_scale[j],
                                         ln_g[i, 0], ln_b[i, 0], tl=tl, w_router=w_router)
        else:
            if kind == 1:
                filt = (hy_f_w1[j], hy_f_b1[j], hy_f_w2[j], hy_f_b2[j], hy_f_w3[j], hy_f_b3[j],
                        hy_f_freq[j], hy_f_wout[j])
                z = hyena_mix(a[:n_rows], n_rows, j, hy_w_in, hy_b_in, hy_conv_w[j], hy_conv_b[j],
                              filt, hy_skip[j])
                w_o, b_o = hy_w_out, hy_b_out
            else:
                assert i <= last_read and not ctx_out and a.shape[0] == N_LAT + N_CTX
                z = [mla_mix(a, mla_w_in[j], mla_q_norm[j], mla_kv_norm[j], mla_w_uq[j],
                             mla_w_ukv[j])]
                w_o, b_o = mla_w_o, None
            h1, a2, *logits = linear_resid_ln(z, w_o, j, b_o, h, n_rows, m, 2, ln_g[i, 0],
                                              ln_b[i, 0], 4, 3, w_router=w_router)
        nm = mods[min(i + 1, DEPTH - 1)]
        if w_router is not None:
            ys, row_gates = moe_ffn(logits[0], a2, n_rows, f, moe_w_gate, moe_w_up, moe_w_down)
        else:
            ys, row_gates = [dense_ffn(a2, f, ffn_w_gate, ffn_w_up, ffn_w_down)], None
        h, a = resid_ln(h1, ys, n_rows, m, 5, ln_g[i, 1], ln_b[i, 1], nm, 1, 0,
                        row_gates=row_gates)
    return h[:N_LAT].reshape(BATCH, SEQ, D_MODEL)
```

```python
import functools
import math

import jax
import jax.numpy as jnp
from jax import lax
from jax.experimental import pallas as pl
from jax.experimental.pallas import tpu as pltpu

F32 = jnp.float32
BF16 = jnp.bfloat16
U32 = jnp.uint32

D_MODEL = 2048
BATCH = 4
SEQ = 2048
DEPTH = 4
GRID_W = 64
CTX_LEN = 256
N_MIXERS = 3
POOL_WINDOWS = (2, 4, 8, 16)
POOL_GROUP = D_MODEL // len(POOL_WINDOWS)
HY_ORDER = 2
HY_SHORT = 3
HY_EMB = 33
HY_FILTER = 64
HY_DECAY_TARGET = 1e-2
HY_FAST = 0.3
HY_SLOW = 1.5
MLA_HEADS = D_MODEL // 128
MLA_Q_RANK = 512
MLA_KV_RANK = 512
MLA_NOPE = 128
MLA_ROPE = 64
MLA_V = 128
ROPE_BASE = 10000.0
D_FF = 5632
N_EXPERTS = 8
TOP_K = 2
EXPERT_FF = 2 * D_MODEL
LN_EPS = 1e-5
RMS_EPS = 1e-6
DN_ALPHA = (2 * DEPTH) ** 0.25

N_LAT = BATCH * SEQ
N_CTX = BATCH * CTX_LEN
MOD_ROWS = 8
CTX_MOD_ROW = BATCH
LANE = 128
SUBLANE = 8
VMEM_CAP = 60 * 1024 * 1024
MLA_QK_PAD = 256
MLA_IN = MLA_Q_RANK + MLA_KV_RANK + MLA_ROPE
MLA_IN_PAD = MLA_IN + MLA_ROPE


def _vmem(nbytes):
    return int(min(VMEM_CAP, max(16 * 1024 * 1024, nbytes * 3 // 2)))


def _nbytes(shape, dtype):
    return math.prod(shape) * jnp.dtype(dtype).itemsize


def _mod_row(i, tm):
    return jnp.minimum((i * tm) // SEQ, CTX_MOD_ROW)


def _mod_spec(which, tm):
    return pl.BlockSpec((None, 1, D_MODEL), lambda i: (_mod_row(i, tm) * 6 + which, 0, 0))


def _params(sem, est):
    return pltpu.CompilerParams(dimension_semantics=("arbitrary",) * sem,
                                vmem_limit_bytes=_vmem(est))


def _mod_kernel(c_ref, w_ref, b_ref, o_ref):
    c = c_ref[...]
    a = c / (1.0 + jnp.exp(-c))
    o_ref[...] = jnp.dot(a.astype(BF16), w_ref[...].astype(BF16),
                         preferred_element_type=F32) + b_ref[...]


def modulation_table(c8, w_mod, b_mod):
    tn = 1024
    n = 6 * D_MODEL
    out = pl.pallas_call(
        _mod_kernel,
        grid=(DEPTH, n // tn),
        in_specs=[pl.BlockSpec((MOD_ROWS, D_MODEL), lambda l, j: (0, 0)),
                  pl.BlockSpec((None, D_MODEL, tn), lambda l, j: (l, 0, j)),
                  pl.BlockSpec((None, 1, tn), lambda l, j: (l, 0, j))],
        out_specs=pl.BlockSpec((None, MOD_ROWS, tn), lambda l, j: (l, 0, j)),
        out_shape=jax.ShapeDtypeStruct((DEPTH, MOD_ROWS, n), F32),
        compiler_params=_params(2, 3 * _nbytes((D_MODEL, tn), F32)),
        name="modulation_table",
    )(c8, w_mod, b_mod.reshape(DEPTH, 1, n))
    return out.reshape(DEPTH, MOD_ROWS * 6, 1, D_MODEL)


def _group_schedule(te):
    n = te.shape[0]
    first = jnp.concatenate([jnp.ones((1,), jnp.int32), (te[1:] != te[:-1]).astype(jnp.int32)])
    gid = jnp.cumsum(first) - 1
    ge = jnp.zeros((n,), jnp.int32).at[gid].max(te)
    return first, gid, ge, gid[-1:] + 1


def _stream_weights(sched, w_hbms, stages, sem, dsts, col0, tn):
    first_ref, gid_ref, ge_ref, ng_ref = sched
    j, i = pl.program_id(0), pl.program_id(1)
    g, ng = gid_ref[i], ng_ref[0]

    def copies(e, col):
        cols = pl.ds(pl.multiple_of((col + col0) * tn, tn), tn)
        return [pltpu.make_async_copy(w.at[e, :, cols], st, sem.at[k])
                for k, (w, st) in enumerate(zip(w_hbms, stages))]

    @pl.when(first_ref[i] == 1)
    def _():
        @pl.when(jnp.logical_and(j == 0, g == 0))
        def _():
            for c in copies(ge_ref[0], 0):
                c.start()

        for c in copies(ge_ref[g], j):
            c.wait()
        for st, dst in zip(stages, dsts):
            dst[...] = st[...].astype(BF16)
        last = g + 1 >= ng
        next_e = ge_ref[jnp.where(last, 0, g + 1)]
        next_col = jnp.where(last, j + 1, j)

        @pl.when(next_col < pl.num_programs(0))
        def _():
            for c in copies(next_e, next_col):
                c.start()


def _linear_kernel(xt_ref, nv_ref, first_ref, gid_ref, ge_ref, ng_ref, x_ref, w_hbm, *rest,
                   has_bias, col0, tn):
    rest = list(rest)
    b_ref = rest.pop(0) if has_bias else None
    o_ref, stage_ref, wbf_ref, sem = rest
    i = pl.program_id(1)
    _stream_weights((first_ref, gid_ref, ge_ref, ng_ref), [w_hbm], [stage_ref], sem, [wbf_ref],
                    col0, tn)

    @pl.when(i < nv_ref[0])
    def _():
        acc = jnp.dot(x_ref[...].astype(BF16), wbf_ref[...], preferred_element_type=F32)
        if has_bias:
            acc = acc + b_ref[...]
        o_ref[...] = acc.astype(o_ref.dtype)

    @pl.when(i >= nv_ref[0])
    def _():
        o_ref[...] = jnp.zeros_like(o_ref)


def linear(x, w, *, tm, tn, e=0, n_out=None, n_tiles=None, xt=None, te=None, nv=None, bias=None,
           out_dtype=F32, w_col0=0, name="linear"):
    k = x.shape[1]
    if w.ndim == 2:
        w = w[None]
    if n_out is None:
        n_out = w.shape[2] - w_col0
    assert x.shape[0] % tm == 0 and n_out % tn == 0 and w_col0 % tn == 0 and w.shape[1] == k
    assert w.dtype == F32
    if n_tiles is None:
        n_tiles = x.shape[0] // tm
    if xt is None:
        xt = jnp.arange(n_tiles, dtype=jnp.int32)
    if te is None:
        te = jnp.full((n_tiles,), e, jnp.int32)
    if nv is None:
        nv = jnp.full((1,), n_tiles, jnp.int32)
    first, gid, ge, ng = _group_schedule(te)
    has_bias = bias is not None
    in_specs = [pl.BlockSpec((tm, k), lambda j, i, xt, *_: (xt[i], 0)),
                pl.BlockSpec(memory_space=pl.ANY)]
    args = [x, w]
    if has_bias:
        bias = bias.reshape(-1, 1, bias.shape[-1])
        in_specs.append(pl.BlockSpec((None, 1, tn),
                                     lambda j, i, xt, nv, first, gid, ge, ng: (ge[gid[i]], 0, j)))
        args.append(bias)
    est = (2 * _nbytes((tm, k), x.dtype) + _nbytes((k, tn), F32) + _nbytes((k, tn), BF16)
           + _nbytes((tm, k), BF16) + 2 * _nbytes((tm, tn), out_dtype) + 2 * _nbytes((tm, tn), F32))
    return pl.pallas_call(
        functools.partial(_linear_kernel, has_bias=has_bias, col0=w_col0 // tn, tn=tn),
        grid_spec=pltpu.PrefetchScalarGridSpec(
            num_scalar_prefetch=6,
            grid=(n_out // tn, n_tiles),
            in_specs=in_specs,
            out_specs=pl.BlockSpec((tm, tn), lambda j, i, *_: (i, j)),
            scratch_shapes=[pltpu.VMEM((k, tn), F32), pltpu.VMEM((k, tn), BF16),
                            pltpu.SemaphoreType.DMA((1,))]),
        out_shape=jax.ShapeDtypeStruct((n_tiles * tm, n_out), out_dtype),
        compiler_params=_params(2, est),
        name=name,
    )(xt, nv, first, gid, ge, ng, *args)


def _swiglu_kernel(nv_ref, first_ref, gid_ref, ge_ref, ng_ref, *refs, n_parts, tf):
    x_refs = refs[:n_parts]
    wg_hbm, wu_hbm, o_ref, sg_ref, su_ref, wgb_ref, wub_ref, sem = refs[n_parts:]
    i = pl.program_id(1)
    _stream_weights((first_ref, gid_ref, ge_ref, ng_ref), [wg_hbm, wu_hbm], [sg_ref, su_ref], sem,
                    [wgb_ref, wub_ref], 0, tf)

    @pl.when(i < nv_ref[0])
    def _():
        g = u = None
        k0 = 0
        for x_ref in x_refs:
            x = x_ref[...].astype(BF16)
            rows = slice(k0, k0 + x.shape[1])
            gp = jnp.dot(x, wgb_ref[rows, :], preferred_element_type=F32)
            up = jnp.dot(x, wub_ref[rows, :], preferred_element_type=F32)
            g, u = (gp, up) if g is None else (g + gp, u + up)
            k0 += x.shape[1]
        o_ref[...] = ((g / (1.0 + jnp.exp(-g))) * u).astype(o_ref.dtype)

    @pl.when(i >= nv_ref[0])
    def _():
        o_ref[...] = jnp.zeros_like(o_ref)


def swiglu_up(x_parts, wg, wu, *, tm, tf, e=0, te=None, nv=None, name="swiglu_up"):
    m = x_parts[0].shape[0]
    k, f = wg.shape[1], wg.shape[2]
    assert sum(x.shape[1] for x in x_parts) == k and m % tm == 0 and f % tf == 0
    n_tiles = m // tm
    if te is None:
        te = jnp.full((n_tiles,), e, jnp.int32)
    if nv is None:
        nv = jnp.full((1,), n_tiles, jnp.int32)
    first, gid, ge, ng = _group_schedule(te)
    wspec = pl.BlockSpec(memory_space=pl.ANY)
    est = (2 * _nbytes((tm, k), x_parts[0].dtype) + _nbytes((tm, k), BF16)
           + 2 * _nbytes((k, tf), F32) + 2 * _nbytes((k, tf), BF16) + 2 * _nbytes((tm, tf), BF16)
           + 3 * _nbytes((tm, tf), F32))
    x_specs = [pl.BlockSpec((tm, x.shape[1]), lambda j, i, *_: (i, 0)) for x in x_parts]
    return pl.pallas_call(
        functools.partial(_swiglu_kernel, n_parts=len(x_parts), tf=tf),
        grid_spec=pltpu.PrefetchScalarGridSpec(
            num_scalar_prefetch=5,
            grid=(f // tf, n_tiles),
            in_specs=[*x_specs, wspec, wspec],
            out_specs=pl.BlockSpec((tm, tf), lambda j, i, *_: (i, j)),
            scratch_shapes=[pltpu.VMEM((k, tf), F32), pltpu.VMEM((k, tf), F32),
                            pltpu.VMEM((k, tf), BF16), pltpu.VMEM((k, tf), BF16),
                            pltpu.SemaphoreType.DMA((2,))]),
        out_shape=jax.ShapeDtypeStruct((m, f), BF16),
        compiler_params=_params(2, est),
        name=name,
    )(nv, first, gid, ge, ng, *x_parts, wg, wu)


def _ln_mod(v, g, b, sc, sh, pack):
    mu = jnp.mean(v, axis=-1, keepdims=True)
    d = v - mu
    var = jnp.mean(d * d, axis=-1, keepdims=True)
    h = d * lax.rsqrt(var + LN_EPS) * g + b
    a = h * (1.0 + sc) + sh
    return h, (a if pack else a.astype(BF16))


def _resid_ln_kernel(*refs, n_y, pack):
    h_ref = refs[0]
    y_refs = refs[1:1 + n_y]
    rest = refs[1 + n_y:]
    if n_y == 2:
        g0_ref, g1_ref = rest[0], rest[1]
        rest = rest[2:]
        y = g0_ref[...] * y_refs[0][...] + g1_ref[...] * y_refs[1][...]
    else:
        y = y_refs[0][...]
    gate_ref, g_ref, b_ref, sc_ref, sh_ref, o_ref, a_ref = rest
    v = DN_ALPHA * h_ref[...] + gate_ref[...] * y
    h, a = _ln_mod(v, g_ref[...], b_ref[...], sc_ref[...], sh_ref[...], pack)
    o_ref[...] = h
    a_ref[...] = a


def resid_ln(h, ys, n_rows, mods, gate_idx, ln_g, ln_b, nmods, sc_idx, sh_idx, *, row_gates=None,
             pack=False, tm=256):
    assert n_rows % tm == 0
    row = pl.BlockSpec((tm, D_MODEL), lambda i: (i, 0))
    col = pl.BlockSpec((tm, 1), lambda i: (i, 0))
    vec = pl.BlockSpec((1, D_MODEL), lambda i: (0, 0))
    a_w = D_MODEL
    a_dt = F32 if pack else BF16
    est = 10 * _nbytes((tm, D_MODEL), F32)
    in_specs = [row] + [row] * len(ys) + ([col, col] if row_gates else []) + [
        _mod_spec(gate_idx, tm), vec, vec, _mod_spec(sc_idx, tm), _mod_spec(sh_idx, tm)]
    args = [h, *ys, *(row_gates or []), mods, ln_g.reshape(1, D_MODEL), ln_b.reshape(1, D_MODEL),
            nmods, nmods]
    return pl.pallas_call(
        functools.partial(_resid_ln_kernel, n_y=len(ys), pack=pack),
        grid=(n_rows // tm,),
        in_specs=in_specs,
        out_specs=[row, pl.BlockSpec((tm, a_w), lambda i: (i, 0))],
        out_shape=[jax.ShapeDtypeStruct((n_rows, D_MODEL), F32),
                   jax.ShapeDtypeStruct((n_rows, a_w), a_dt)],
        compiler_params=_params(1, est),
        name="resid_ln",
    )(*args)


def _split_bf16(x):
    hi = x.astype(BF16)
    lo = (x - hi.astype(F32)).astype(BF16)
    return hi, lo


def _router_logits(a, whi_ref, wlo_ref):
    a_hi, a_lo = _split_bf16(a)
    w_hi = whi_ref[...]
    return (jnp.dot(a_hi, w_hi, preferred_element_type=F32)
            + jnp.dot(a_lo, w_hi, preferred_element_type=F32)
            + jnp.dot(a_hi, wlo_ref[...], preferred_element_type=F32))


def _router_operands(w_router):
    w = jnp.pad(w_router, ((0, 0), (0, LANE - N_EXPERTS)))
    return _split_bf16(w), pl.BlockSpec((D_MODEL, LANE), lambda i: (0, 0))


def _linear_ln_kernel(*refs, has_bias, moe, split):
    n_src = 1 if split is None else 2
    x_refs, w_ref, rest = refs[:n_src], refs[n_src], list(refs[n_src + 1:])
    bias_ref = rest.pop(0) if has_bias else None
    h_ref, gate_ref, g_ref, b_ref, sc_ref, sh_ref = rest[:6]
    rest = rest[6:]
    if moe:
        whi_ref, wlo_ref, o_ref, a_ref, l_ref, wb_ref = rest
    else:
        o_ref, a_ref, wb_ref = rest

    @pl.when(pl.program_id(0) == 0)
    def _():
        wb_ref[...] = w_ref[...].astype(BF16)

    if split is None:
        x = x_refs[0][...]
    else:
        x = jnp.where(pl.program_id(0) < split, x_refs[0][...], x_refs[1][...])
    y = jnp.dot(x, wb_ref[...], preferred_element_type=F32)
    if has_bias:
        y = y + bias_ref[...]
    v = DN_ALPHA * h_ref[...] + gate_ref[...] * y
    h, a = _ln_mod(v, g_ref[...], b_ref[...], sc_ref[...], sh_ref[...], moe)
    o_ref[...] = h
    a_ref[...] = a
    if moe:
        l_ref[...] = _router_logits(a, whi_ref, wlo_ref)


def linear_resid_ln(xs, w, e, bias, h, n_rows, mods, gate_idx, ln_g, ln_b, sc_idx, sh_idx, *,
                    w_router=None, tm=256):
    k = xs[0].shape[1]
    moe = w_router is not None
    row = pl.BlockSpec((tm, D_MODEL), lambda i: (i, 0))
    vec = pl.BlockSpec((1, D_MODEL), lambda i: (0, 0))
    split = None if len(xs) == 1 else xs[0].shape[0] // tm
    in_specs, t0 = [], 0
    for x in xs:
        nt = x.shape[0] // tm
        in_specs.append(pl.BlockSpec((tm, k), lambda i, t0=t0, nt=nt: (jnp.clip(i - t0, 0, nt - 1), 0)))
        t0 += nt
    in_specs.append(pl.BlockSpec((None, k, D_MODEL), lambda i: (e, 0, 0),
                                 pipeline_mode=pl.Buffered(1)))
    args = [*xs, w]
    if bias is not None:
        in_specs.append(pl.BlockSpec((None, 1, D_MODEL), lambda i: (e, 0, 0)))
        args.append(bias.reshape(-1, 1, D_MODEL))
    in_specs += [row, _mod_spec(gate_idx, tm), vec, vec, _mod_spec(sc_idx, tm),
                 _mod_spec(sh_idx, tm)]
    args += [h, mods, ln_g.reshape(1, D_MODEL), ln_b.reshape(1, D_MODEL), mods, mods]
    out_specs = [row, row]
    out_shape = [jax.ShapeDtypeStruct((n_rows, D_MODEL), F32),
                 jax.ShapeDtypeStruct((n_rows, D_MODEL), F32 if moe else BF16)]
    if moe:
        (w_hi, w_lo), wspec = _router_operands(w_router)
        in_specs += [wspec, wspec]
        args += [w_hi, w_lo]
        out_specs.append(pl.BlockSpec((tm, LANE), lambda i: (i, 0)))
        out_shape.append(jax.ShapeDtypeStruct((n_rows, LANE), F32))
    est = (_nbytes((k, D_MODEL), F32) + _nbytes((k, D_MODEL), BF16) + 2 * _nbytes((tm, k), BF16)
           + 12 * _nbytes((tm, D_MODEL), F32))
    return pl.pallas_call(
        functools.partial(_linear_ln_kernel, has_bias=bias is not None, moe=moe, split=split),
        grid=(n_rows // tm,),
        in_specs=in_specs,
        out_specs=out_specs,
        out_shape=out_shape,
        scratch_shapes=[pltpu.VMEM((k, D_MODEL), BF16)],
        compiler_params=_params(1, est),
        name="linear_resid_ln",
    )(*args)


def _seq_pos(i, tl):
    row0 = i * tl
    is_lat = row0 < N_LAT
    pos = jnp.where(is_lat, row0 % SEQ, (row0 - N_LAT) % CTX_LEN)
    seqlen = jnp.where(is_lat, SEQ, CTX_LEN)
    return pos, seqlen


def _tile_halo_specs(tl, width, t0, nt):
    per = tl // SUBLANE
    local = lambda i: jnp.clip(i - t0, 0, nt - 1)
    main = pl.BlockSpec((tl, width), lambda i: (local(i), 0))
    prev = pl.BlockSpec((SUBLANE, width), lambda i: (jnp.maximum(local(i) * per - 1, 0), 0))
    nxt = pl.BlockSpec((SUBLANE, width),
                       lambda i: (jnp.minimum((local(i) + 1) * per, nt * per - 1), 0))
    return [main, prev, nxt]


def _shift_rows(x, s):
    n = x.shape[0]
    return pltpu.roll(x, s % n, 0)


def _pool_kernel(*refs, tl, moe, split):
    n_src = 1 if split is None else 2
    srcs, rest = refs[:3 * n_src], refs[3 * n_src:]
    (sc_ref, sh_ref, gate_ref, w_ref, ps_ref, g_ref, b_ref, nsc_ref, nsh_ref), rest = rest[:9], rest[9:]
    if moe:
        whi_ref, wlo_ref, o_ref, a_ref, l_ref = rest
    else:
        o_ref, a_ref = rest
    i = pl.program_id(0)
    pos, seqlen = _seq_pos(i, tl)
    if split is None:
        h, h_prev, h_next = (r[...] for r in srcs)
    else:
        h, h_prev, h_next = (jnp.where(i < split, r0[...], r1[...])
                             for r0, r1 in zip(srcs[:3], srcs[3:]))
    sc = 1.0 + sc_ref[...]
    sh = sh_ref[...]
    a = h * sc + sh
    keep_prev = (pos > 0).astype(F32)
    keep_next = (pos + tl < seqlen).astype(F32)
    ext = jnp.concatenate([(h_prev * sc + sh) * keep_prev, a,
                           (h_next * sc + sh) * keep_next], axis=0)
    t = pos + lax.broadcasted_iota(jnp.int32, (tl, 1), 0)
    n_ext = tl + 2 * SUBLANE
    parts = []
    for g, w in enumerate(POOL_WINDOWS):
        cols = slice(g * POOL_GROUP, (g + 1) * POOL_GROUP)
        s = ext[:, cols]
        s = s + _shift_rows(s, 1)
        r = 1
        while 2 * r < w:
            s = _shift_rows(s, n_ext - r) + _shift_rows(s, r)
            r *= 2
        s = s[SUBLANE:SUBLANE + tl]
        cnt = (jnp.minimum(t + w // 2, seqlen) - jnp.maximum(t - w // 2, 0)).astype(F32)
        d = s / cnt - a[:, cols]
        parts.append(jnp.dot(d.astype(BF16), w_ref[g], preferred_element_type=F32))
    y = jnp.concatenate(parts, axis=1) * ps_ref[...]
    v = DN_ALPHA * h + gate_ref[...] * y
    hn, an = _ln_mod(v, g_ref[...], b_ref[...], nsc_ref[...], nsh_ref[...], moe)
    o_ref[...] = hn
    a_ref[...] = an
    if moe:
        l_ref[...] = _router_logits(an, whi_ref, wlo_ref)


def pool_layer(hs, n_rows, mods, w_grp_bf16, pool_scale, ln_g, ln_b, *, tl, w_router=None):
    moe = w_router is not None
    row = pl.BlockSpec((tl, D_MODEL), lambda i: (i, 0))
    vec = pl.BlockSpec((1, D_MODEL), lambda i: (0, 0))
    split = None if len(hs) == 1 else hs[0].shape[0] // tl
    in_specs, args, t0 = [], [], 0
    for h in hs:
        nt = h.shape[0] // tl
        in_specs += _tile_halo_specs(tl, D_MODEL, t0, nt)
        args += [h, h, h]
        t0 += nt
    in_specs += [_mod_spec(1, tl), _mod_spec(0, tl), _mod_spec(2, tl),
                 pl.BlockSpec(w_grp_bf16.shape, lambda i: (0, 0, 0)), vec, vec, vec,
                 _mod_spec(4, tl), _mod_spec(3, tl)]
    args += [mods, mods, mods, w_grp_bf16, pool_scale.reshape(1, D_MODEL),
             ln_g.reshape(1, D_MODEL), ln_b.reshape(1, D_MODEL), mods, mods]
    out_specs = [row, row]
    out_shape = [jax.ShapeDtypeStruct((n_rows, D_MODEL), F32),
                 jax.ShapeDtypeStruct((n_rows, D_MODEL), F32 if moe else BF16)]
    if moe:
        (w_hi, w_lo), wspec = _router_operands(w_router)
        in_specs += [wspec, wspec]
        args += [w_hi, w_lo]
        out_specs.append(pl.BlockSpec((tl, LANE), lambda i: (i, 0)))
        out_shape.append(jax.ShapeDtypeStruct((n_rows, LANE), F32))
    est = 14 * _nbytes((tl, D_MODEL), F32) + 2 * _nbytes(w_grp_bf16.shape, BF16)
    return pl.pallas_call(
        functools.partial(_pool_kernel, tl=tl, moe=moe, split=split),
        grid=(n_rows // tl,),
        in_specs=in_specs,
        out_specs=out_specs,
        out_shape=out_shape,
        compiler_params=_params(1, est),
        name="pool_layer",
    )(*args)


def _conv3_seq(u, w, b):
    n = u.shape[0]
    t = lax.broadcasted_iota(jnp.int32, (n, 1), 0)
    before = jnp.where(t == 0, 0.0, _shift_rows(u, 1))
    after = jnp.where(t == n - 1, 0.0, _shift_rows(u, n - 1))
    return b + before * w[0:1] + u * w[1:2] + after * w[2:3]


def _conv3_tile(u, prev, nxt, w, b, first, last):
    n = u.shape[0]
    t = lax.broadcasted_iota(jnp.int32, (n, 1), 0)
    row_before = prev[SUBLANE - 1:SUBLANE, :] * (1.0 - first.astype(F32))
    row_after = nxt[0:1, :] * (1.0 - last.astype(F32))
    before = jnp.where(t == 0, row_before, _shift_rows(u, 1))
    after = jnp.where(t == n - 1, row_after, _shift_rows(u, n - 1))
    return b + before * w[0:1] + u * w[1:2] + after * w[2:3]


def _dot3(a, b):
    a_hi, a_lo = _split_bf16(a)
    b_hi, b_lo = _split_bf16(b)
    return (jnp.dot(a_hi, b_hi, preferred_element_type=F32)
            + jnp.dot(a_lo, b_hi, preferred_element_type=F32)
            + jnp.dot(a_hi, b_lo, preferred_element_type=F32))


def _filter_kernel(z_ref, w1_ref, b1_ref, w2_ref, b2_ref, w3_ref, b3_ref, fr_ref, wo_ref,
                   dist_ref, delta_ref, o_ref, a_ref):
    @pl.when(pl.program_id(0) == 0)
    def _():
        fr = fr_ref[...]
        a = jnp.sin(fr * (_dot3(z_ref[...], w1_ref[...]) + b1_ref[...]))
        a = jnp.sin(fr * (_dot3(a, w2_ref[...]) + b2_ref[...]))
        a_ref[...] = jnp.sin(fr * (_dot3(a, w3_ref[...]) + b3_ref[...]))

    k = _dot3(a_ref[...], wo_ref[...])
    k = k * jnp.exp(-dist_ref[...] * delta_ref[...])
    o_ref[...] = k * lax.rsqrt(jnp.sum(k * k, axis=0, keepdims=True) + 1e-6)


def _pad2(x, rows, cols):
    return jnp.pad(x, ((0, rows - x.shape[0]), (0, cols - x.shape[1])))


def hyena_filters(L, f_w1, f_b1, f_w2, f_b2, f_w3, f_b3, f_freq, f_wout, *, tn=512):
    bands = (HY_EMB - 1) // 2
    t = jnp.linspace(0.0, 1.0, L, dtype=F32)[:, None]
    wpos = 2.0 * math.pi * jnp.arange(L, dtype=F32)[:, None] / L
    f = jnp.linspace(1e-4, bands - 1, bands, dtype=F32)[None, :]
    z = jnp.concatenate([t, jnp.cos(f * wpos), -jnp.sin(f * wpos)], axis=-1)
    dist = (jnp.abs(jnp.arange(L) - L // 2).astype(F32) / max(L // 2, 1))[:, None]
    deltas = jnp.linspace(math.log(HY_DECAY_TARGET) / HY_SLOW, math.log(HY_DECAY_TARGET) / HY_FAST,
                          D_MODEL, dtype=F32)
    absdelta = jnp.tile(jnp.abs(deltas), HY_ORDER)[None, :]
    P = LANE
    n = HY_ORDER * D_MODEL
    full = lambda shape: pl.BlockSpec(shape, lambda j: (0, 0))
    args = [_pad2(z, L, P), _pad2(f_w1, P, P), _pad2(f_b1[None], 1, P), _pad2(f_w2, P, P),
            _pad2(f_b2[None], 1, P), _pad2(f_w3, P, P), _pad2(f_b3[None], 1, P),
            _pad2(f_freq[None], 1, P), _pad2(f_wout, P, n), dist, absdelta]
    in_specs = [full((L, P)), full((P, P)), full((1, P)), full((P, P)), full((1, P)), full((P, P)),
                full((1, P)), full((1, P)), pl.BlockSpec((P, tn), lambda j: (0, j)),
                full((L, 1)), pl.BlockSpec((1, tn), lambda j: (0, j))]
    return pl.pallas_call(
        _filter_kernel,
        grid=(n // tn,),
        in_specs=in_specs,
        out_specs=pl.BlockSpec((L, tn), lambda j: (0, j)),
        out_shape=jax.ShapeDtypeStruct((L, n), F32),
        scratch_shapes=[pltpu.VMEM((L, P), F32)],
        compiler_params=_params(1, 8 * _nbytes((L, tn), F32)),
        name="hyena_filters",
    )(*args)


def dft_tables(L, tm):
    n_fft = 2 * L
    half = tm // 2
    r_idx = jnp.arange(2 * L, dtype=jnp.int32)
    odd = 2 * ((r_idx // tm) * half + (r_idx % half)) + 1
    is_sin = ((r_idx % tm) >= half)[:, None]

    def trig(k):
        ang = (k % (2 * n_fft)).astype(F32) * (math.pi / n_fft)
        return jnp.cos(ang), jnp.sin(ang)

    lo = 64
    s1 = jnp.arange(L // lo, dtype=jnp.int32) * lo
    s0 = jnp.arange(lo, dtype=jnp.int32)
    cb, sb = trig(odd[:, None] * s0[None, :])

    def combine(ca, sa):
        a1 = jnp.where(is_sin, sa, ca)
        a2 = jnp.where(is_sin, ca, -sa)
        return a1, a2

    a1, a2 = combine(*trig(odd[:, None] * s1[None, :]))
    coarse = lambda t: jnp.repeat(t, lo, axis=1)
    fine = lambda t: jnp.tile(t, (1, L // lo))
    fwd = coarse(a1) * fine(cb) + coarse(a2) * fine(sb)
    a1, a2 = combine(*trig(odd[:, None] * (s1[None, :] + L // 2)))
    inv = (a1.T[:, None, :] * cb.T[None, :, :] + a2.T[:, None, :] * sb.T[None, :, :]).reshape(
        L, 2 * L)
    return fwd.astype(BF16), inv.astype(BF16)


def _dft_mul_kernel(x_ref, w_ref, *rest, inv_len, conv):
    if conv:
        cw_ref, cb_ref, k_ref, o_ref, wb_ref = rest
    else:
        k_ref, o_ref, wb_ref = rest

    @pl.when(pl.program_id(1) == 0)
    def _():
        w = w_ref[...]
        if conv:
            w = _conv3_seq(w, cw_ref[...], cb_ref[...])
        wb_ref[...] = w.astype(BF16)

    s = jnp.dot(x_ref[...], wb_ref[...], preferred_element_type=F32)
    half = s.shape[0] // 2
    a, b = s[:half], s[half:]
    ka, kb = k_ref[0:half, :], k_ref[half:2 * half, :]
    o_ref[0:half, :] = ((a * ka - b * kb) * inv_len).astype(BF16)
    o_ref[half:2 * half, :] = ((a * kb + b * ka) * inv_len).astype(BF16)


def dft_mul(fwd, v_view, kspec, k_col0, *, L, nb, tm, tn, conv=None):
    d = D_MODEL
    per_b = d // tn
    v, v_row0, v_col0 = v_view
    vr, vc, kc = v_row0 // L, v_col0 // tn, k_col0 // tn
    est = (2 * _nbytes((tm, L), BF16) + 2 * _nbytes((L, tn), F32) + _nbytes((L, tn), BF16)
           + 2 * _nbytes((tm, tn), F32) + 2 * _nbytes((tm, tn), BF16) + 3 * _nbytes((tm, tn), F32))
    in_specs = [pl.BlockSpec((tm, L), lambda j, i: (i, 0)),
                pl.BlockSpec((L, tn), lambda j, i: (j // per_b + vr, j % per_b + vc))]
    args = [fwd, v]
    if conv is not None:
        in_specs += [pl.BlockSpec((HY_SHORT, tn), lambda j, i: (0, j % per_b + vc)),
                     pl.BlockSpec((1, tn), lambda j, i: (0, j % per_b + vc))]
        args += list(conv)
    in_specs.append(pl.BlockSpec((tm, tn), lambda j, i: (i, j % per_b + kc)))
    args.append(kspec)
    return pl.pallas_call(
        functools.partial(_dft_mul_kernel, inv_len=1.0 / L, conv=conv is not None),
        grid=(nb * per_b, (2 * L) // tm),
        in_specs=in_specs,
        out_specs=pl.BlockSpec((tm, tn), lambda j, i: (i, j)),
        out_shape=jax.ShapeDtypeStruct((2 * L, nb * d), BF16),
        scratch_shapes=[pltpu.VMEM((L, tn), BF16)],
        compiler_params=_params(2, est),
        name="dft_mul",
    )(*args)


def _idft_kernel(x_ref, y_ref, *rest, z_conv, per):
    rest = list(rest)
    g_ref, gp_ref, gn_ref, gw_ref, gb_ref = rest[:5]
    rest = rest[5:]
    i = pl.program_id(1)
    first, last = i == 0, i == per - 1
    conv = jnp.dot(x_ref[...], y_ref[...], preferred_element_type=F32)
    gate = _conv3_tile(g_ref[...], gp_ref[...], gn_ref[...], gw_ref[...], gb_ref[...], first, last)
    if z_conv:
        z_ref, zp_ref, zn_ref, zw_ref, zb_ref, skip_ref, o_ref = rest
        z = _conv3_tile(z_ref[...], zp_ref[...], zn_ref[...], zw_ref[...], zb_ref[...], first, last)
    else:
        z_ref, skip_ref, o_ref = rest
        z = z_ref[...]
    o_ref[...] = (gate * (conv + z * skip_ref[...])).astype(o_ref.dtype)


def idft_gate(inv, y, gate_view, z_view, skip, conv, *, z_conv, L, nb, tm, tn, out_dtype):
    d = D_MODEL
    per_b = d // tn
    per = L // tm
    sub = tm // SUBLANE

    def tile_specs(view, with_conv):
        arr, row0, col0 = view
        r0, c0 = row0 // tm, col0 // tn
        last_blk = arr.shape[0] // SUBLANE - 1
        row = lambda j, i: (j // per_b) * per + i + r0
        col = lambda j, i: j % per_b + c0
        specs = [pl.BlockSpec((tm, tn), lambda j, i: (row(j, i), col(j, i)))]
        args = [arr]
        if with_conv:
            specs += [
                pl.BlockSpec((SUBLANE, tn),
                             lambda j, i: (jnp.maximum(row(j, i) * sub - 1, 0), col(j, i))),
                pl.BlockSpec((SUBLANE, tn),
                             lambda j, i: (jnp.minimum((row(j, i) + 1) * sub, last_blk), col(j, i))),
                pl.BlockSpec((HY_SHORT, tn), lambda j, i: (0, col(j, i))),
                pl.BlockSpec((1, tn), lambda j, i: (0, col(j, i)))]
            args += [arr, arr, conv[0], conv[1]]
        return specs, args

    g_specs, g_args = tile_specs(gate_view, True)
    z_specs, z_args = tile_specs(z_view, z_conv)
    est = (2 * _nbytes((tm, 2 * L), BF16) + 2 * _nbytes((2 * L, tn), BF16)
           + 12 * _nbytes((tm, tn), F32))
    return pl.pallas_call(
        functools.partial(_idft_kernel, z_conv=z_conv, per=per),
        grid=(nb * per_b, per),
        in_specs=[pl.BlockSpec((tm, 2 * L), lambda j, i: (i, 0)),
                  pl.BlockSpec((2 * L, tn), lambda j, i: (0, j)),
                  *g_specs, *z_specs,
                  pl.BlockSpec((1, tn), lambda j, i: (0, j % per_b))],
        out_specs=pl.BlockSpec((tm, tn), lambda j, i: ((j // per_b) * per + i, j % per_b)),
        out_shape=jax.ShapeDtypeStruct((nb * L, d), out_dtype),
        compiler_params=_params(2, est),
        name="idft_gate",
    )(inv, y, *g_args, *z_args, skip.reshape(1, d))


def hyena_stream(u, row0, L, nb, conv, filt, skip):
    d = D_MODEL
    tms = min(1024, 2 * L)
    tmi = min(512, L)
    k = hyena_filters(L, *filt)
    fwd, inv = dft_tables(L, tms)
    kspec = linear(fwd, k, tm=tms, tn=1024, name="dft_filters")
    y1 = dft_mul(fwd, (u, row0, 0), kspec, 0, L=L, nb=nb, tm=tms, tn=1024, conv=conv)
    z1 = idft_gate(inv, y1, (u, row0, d), (u, row0, 0), skip[0], conv, z_conv=True, L=L, nb=nb,
                   tm=tmi, tn=1024, out_dtype=F32)
    y2 = dft_mul(fwd, (z1, 0, 0), kspec, d, L=L, nb=nb, tm=tms, tn=1024)
    return idft_gate(inv, y2, (u, row0, 2 * d), (z1, 0, 0), skip[1], conv, z_conv=False, L=L,
                     nb=nb, tm=tmi, tn=1024, out_dtype=BF16)


def hyena_mix(a, n_rows, j, w_in, b_in, conv_w, conv_b, filt, skip):
    u = linear(a, w_in, e=j, tm=1024, tn=2048, bias=b_in, name="hyena_in")
    conv = (conv_w, conv_b.reshape(1, -1))
    zs = [hyena_stream(u, 0, SEQ, BATCH, conv, filt, skip)]
    if n_rows > N_LAT:
        zs.append(hyena_stream(u, N_LAT, CTX_LEN, BATCH, conv, filt, skip))
    return zs


def _rms(x, g):
    return x * lax.rsqrt(jnp.mean(x * x, axis=-1, keepdims=True) + RMS_EPS) * g


def _mla_q_kernel(p_ref, g_ref, w1_ref, w2_ref, c_ref, s_ref, o_ref, w1b_ref, w2b_ref):
    @pl.when(pl.program_id(1) == 0)
    def _():
        w1b_ref[...] = w1_ref[...].astype(BF16)
        w2b_ref[...] = w2_ref[...].astype(BF16)

    xn = _rms(p_ref[...], g_ref[...]).astype(BF16)
    q1 = jnp.dot(xn, w1b_ref[...], preferred_element_type=F32)
    q2 = jnp.dot(xn, w2b_ref[...], preferred_element_type=F32)
    reps = q1.shape[1] // MLA_QK_PAD
    c = jnp.concatenate([c_ref[...]] * reps, axis=1)
    s = jnp.concatenate([s_ref[...]] * reps, axis=1)
    o_ref[...] = (q1 * c + q2 * s).astype(BF16)


def _mla_kv_kernel(p_ref, r_ref, g_ref, cs_ref, w_ref, o_ref, wb_ref):
    @pl.when(pl.program_id(1) == 0)
    def _():
        wb_ref[...] = w_ref[...].astype(BF16)

    xn = _rms(p_ref[...], g_ref[...]).astype(BF16)
    rr = (r_ref[...] * cs_ref[...]).astype(BF16)
    acc = (jnp.dot(xn, wb_ref[0:MLA_KV_RANK, :], preferred_element_type=F32)
           + jnp.dot(rr, wb_ref[MLA_KV_RANK:MLA_KV_RANK + 2 * MLA_ROPE, :],
                     preferred_element_type=F32))
    o_ref[...] = acc.astype(BF16)


def _attn_kernel(q_ref, kl_ref, kc_ref, vl_ref, vc_ref, o_ref, *, kchunk):
    c = (MLA_NOPE + MLA_ROPE) ** -0.5 * math.log2(math.e)
    nt = (((1,), (1,)), ((), ()))
    q = q_ref[...]
    n_lat, n_ctx = kl_ref.shape[0], kc_ref.shape[0]
    assert (n_lat + n_ctx) % kchunk == 0 and n_lat % kchunk + n_ctx == kchunk
    m = l = acc = None
    for r in range((n_lat + n_ctx) // kchunk):
        lo, hi = r * kchunk, min((r + 1) * kchunk, n_lat)
        if (r + 1) * kchunk <= n_lat:
            k, v = kl_ref[lo:hi, :], vl_ref[lo:hi, :]
        else:
            k = jnp.concatenate([kl_ref[lo:hi, :], kc_ref[...]], axis=0)
            v = jnp.concatenate([vl_ref[lo:hi, :], vc_ref[...]], axis=0)
        s = lax.dot_general(q, k, nt, preferred_element_type=F32)
        ms = jnp.max(s, axis=-1, keepdims=True)
        m_new = ms if m is None else jnp.maximum(m, ms)
        p = jnp.exp2((s - m_new) * c)
        ps = jnp.sum(p, axis=-1, keepdims=True)
        pv = jnp.dot(p.astype(BF16), v, preferred_element_type=F32)
        if m is None:
            l, acc = ps, pv
        else:
            alpha = jnp.exp2((m - m_new) * c)
            l = alpha * l + ps
            acc = alpha * acc + pv
        m = m_new
    o_ref[...] = (acc / l).astype(o_ref.dtype)


def _rope_tables():
    rows = SEQ // GRID_W
    row = jnp.repeat(jnp.arange(rows), GRID_W).astype(F32)
    col = jnp.tile(jnp.arange(GRID_W), rows).astype(F32)
    n = MLA_ROPE // 4
    inv = ROPE_BASE ** (-jnp.arange(n, dtype=F32) / n)
    ang_row, ang_col = row[:, None] * inv, col[:, None] * inv
    cos = jnp.concatenate([jnp.cos(ang_row)] * 2 + [jnp.cos(ang_col)] * 2, axis=1)
    sin = jnp.concatenate([jnp.sin(ang_row)] * 2 + [jnp.sin(ang_col)] * 2, axis=1)
    return cos, sin


def _rope_partner(w):
    n = MLA_ROPE // 4
    pairs = w.reshape(w.shape[:-1] + (2, 2, n))
    sign = jnp.array([-1.0, 1.0], F32)[:, None]
    return (pairs[..., ::-1, :] * sign).reshape(w.shape)


def mla_mix(a, w_in, q_norm, kv_norm, w_uq, w_ukv):
    H, NP, R, V, QK = MLA_HEADS, MLA_NOPE, MLA_ROPE, MLA_V, MLA_QK_PAD
    n_all = N_LAT + N_CTX
    cos, sin = _rope_tables()
    w_in_x = jnp.concatenate([w_in, _rope_partner(w_in[:, MLA_IN - R:])], axis=1)
    proj = linear(a, w_in_x, tm=512, tn=MLA_IN_PAD, name="mla_in")
    wq = w_uq.reshape(MLA_Q_RANK, H, NP + R)
    tail = QK - NP - R
    wq1 = jnp.pad(wq, ((0, 0), (0, 0), (0, tail))).reshape(MLA_Q_RANK, H * QK)
    wq2 = jnp.pad(_rope_partner(wq[..., NP:]),
                  ((0, 0), (0, 0), (NP, tail))).reshape(MLA_Q_RANK, H * QK)
    ones, zeros = jnp.ones((SEQ, NP), F32), jnp.zeros((SEQ, QK - NP - R), F32)
    cq = jnp.concatenate([ones, cos, zeros], axis=1)
    sq = jnp.concatenate([0.0 * ones, sin, zeros], axis=1)
    tm, tn = 1024, 1024
    per = SEQ // tm
    q = pl.pallas_call(
        _mla_q_kernel,
        grid=(H * QK // tn, N_LAT // tm),
        in_specs=[pl.BlockSpec((tm, MLA_Q_RANK), lambda jn, i: (i, 0)),
                  pl.BlockSpec((1, MLA_Q_RANK), lambda jn, i: (0, 0)),
                  pl.BlockSpec((MLA_Q_RANK, tn), lambda jn, i: (0, jn)),
                  pl.BlockSpec((MLA_Q_RANK, tn), lambda jn, i: (0, jn)),
                  pl.BlockSpec((tm, QK), lambda jn, i: (i % per, 0)),
                  pl.BlockSpec((tm, QK), lambda jn, i: (i % per, 0))],
        out_specs=pl.BlockSpec((tm, tn), lambda jn, i: (i, jn)),
        out_shape=jax.ShapeDtypeStruct((N_LAT, H * QK), BF16),
        scratch_shapes=[pltpu.VMEM((MLA_Q_RANK, tn), BF16), pltpu.VMEM((MLA_Q_RANK, tn), BF16)],
        compiler_params=_params(2, 6 * _nbytes((MLA_Q_RANK, tn), F32) + 8 * _nbytes((tm, tn), F32)),
        name="mla_q",
    )(proj, q_norm.reshape(1, MLA_Q_RANK), wq1, wq2, cq, sq)
    wkv = w_ukv.reshape(MLA_KV_RANK, H, NP + V)
    wk = jnp.concatenate([wkv[..., :NP], jnp.zeros((MLA_KV_RANK, H, QK - NP), F32)],
                         axis=2).reshape(MLA_KV_RANK, H * QK)
    place = jnp.concatenate([jnp.zeros((R, NP), F32), jnp.eye(R, dtype=F32),
                             jnp.zeros((R, QK - NP - R), F32)], axis=1)
    place = jnp.tile(jnp.concatenate([place, place], axis=0), (1, H))
    w_kv = jnp.concatenate([
        jnp.concatenate([wk, wkv[..., NP:].reshape(MLA_KV_RANK, H * V)], axis=1),
        jnp.concatenate([place, jnp.zeros((2 * R, H * V), F32)], axis=1)], axis=0)
    tm, tn = 512, 2048
    cs = jnp.concatenate([
        jnp.concatenate([cos, sin], axis=1),
        jnp.concatenate([jnp.ones((tm, R), F32), jnp.zeros((tm, R), F32)], axis=1)],
        axis=0)
    per = SEQ // tm
    n_kv = H * (QK + V)
    kv = pl.pallas_call(
        _mla_kv_kernel,
        grid=(n_kv // tn, n_all // tm),
        in_specs=[pl.BlockSpec((tm, MLA_KV_RANK), lambda jn, i: (i, MLA_Q_RANK // MLA_KV_RANK)),
                  pl.BlockSpec((tm, 2 * R), lambda jn, i: (i, (MLA_IN - R) // (2 * R))),
                  pl.BlockSpec((1, MLA_KV_RANK), lambda jn, i: (0, 0)),
                  pl.BlockSpec((tm, 2 * R),
                               lambda jn, i: (jnp.where(i * tm < N_LAT, i % per, per), 0)),
                  pl.BlockSpec((MLA_KV_RANK + 2 * R, tn), lambda jn, i: (0, jn))],
        out_specs=pl.BlockSpec((tm, tn), lambda jn, i: (i, jn)),
        out_shape=jax.ShapeDtypeStruct((n_all, n_kv), BF16),
        scratch_shapes=[pltpu.VMEM((MLA_KV_RANK + 2 * R, tn), BF16)],
        compiler_params=_params(2, 4 * _nbytes((MLA_KV_RANK + 2 * R, tn), F32)
                                + 8 * _nbytes((tm, tn), F32)),
        name="mla_kv",
    )(proj, proj, kv_norm.reshape(1, MLA_KV_RANK), cs, w_kv)
    tq = 2048
    kchunk = (SEQ + CTX_LEN) // 3
    per = SEQ // tq
    v0 = H * QK // V
    ctx0 = N_LAT // CTX_LEN
    est = (2 * _nbytes((tq, QK), BF16) + 2 * _nbytes((SEQ + CTX_LEN, QK + V), BF16)
           + 8 * _nbytes((tq, kchunk), F32))
    return pl.pallas_call(
        functools.partial(_attn_kernel, kchunk=kchunk),
        grid=(BATCH, H, per),
        in_specs=[pl.BlockSpec((tq, QK), lambda b, h, i: (b * per + i, h)),
                  pl.BlockSpec((SEQ, QK), lambda b, h, i: (b, h)),
                  pl.BlockSpec((CTX_LEN, QK), lambda b, h, i: (ctx0 + b, h)),
                  pl.BlockSpec((SEQ, V), lambda b, h, i: (b, v0 + h)),
                  pl.BlockSpec((CTX_LEN, V), lambda b, h, i: (ctx0 + b, v0 + h))],
        out_specs=pl.BlockSpec((tq, V), lambda b, h, i: (b * per + i, h)),
        out_shape=jax.ShapeDtypeStruct((N_LAT, H * V), BF16),
        compiler_params=_params(3, est),
        name="mla_attention",
    )(q, kv, kv, kv, kv)


def dense_ffn(a, f, wg, wu, wd):
    p = swiglu_up([a], wg, wu, e=f, tm=1024, tf=512, name="ffn_up")
    return linear(p, wd, e=f, tm=512, tn=1024, out_dtype=BF16, name="ffn_down")


def _row_cumsum(onehot):
    n, e = onehot.shape
    blk = 256
    x = onehot.reshape(n // blk, blk, e).astype(BF16)
    tri = jnp.tril(jnp.ones((blk, blk), BF16))
    inner = jnp.einsum("ij,bje->bie", tri, x, preferred_element_type=F32)
    totals = inner[:, -1, :]
    offs = jnp.cumsum(totals, axis=0) - totals
    return (inner + offs[:, None, :]).reshape(n, e).astype(jnp.int32)


def moe_ffn(logits, a_rows, n_rows, f, wg, wu, wd, *, tm=512):
    top_v, top_i = lax.top_k(logits[:, :N_EXPERTS], TOP_K)
    gates = jax.nn.softmax(top_v, axis=-1)
    n_assign = n_rows * TOP_K
    n_tiles = n_assign // tm + N_EXPERTS
    flat_e = top_i.reshape(n_assign)
    onehot = (flat_e[:, None] == jnp.arange(N_EXPERTS)[None, :]).astype(jnp.int32)
    csum = _row_cumsum(onehot)
    rank = jnp.sum(csum * onehot, axis=1) - 1
    counts = csum[-1]
    padded = ((counts + tm - 1) // tm) * tm
    ends = jnp.cumsum(padded)
    starts = ends - padded
    pos = starts[flat_e] + rank
    src = (jnp.arange(n_tiles * tm, dtype=jnp.int32) % n_rows).at[pos].set(
        jnp.arange(n_assign, dtype=jnp.int32) // TOP_K)
    tile_row0 = jnp.arange(n_tiles, dtype=jnp.int32) * tm
    te = jnp.minimum(jnp.sum((tile_row0[:, None] >= ends[None, :]).astype(jnp.int32), axis=1),
                     N_EXPERTS - 1)
    nv = (ends[-1] // tm).astype(jnp.int32).reshape(1)
    te = jnp.where(tile_row0 < ends[-1], te, te[jnp.maximum(nv[0] - 1, 0)]) + f * N_EXPERTS
    x_sorted = a_rows.at[src].get(mode="promise_in_bounds")
    stack = lambda w: w.reshape((-1,) + w.shape[2:])
    p = swiglu_up([x_sorted], stack(wg), stack(wu), tm=tm, tf=1024, te=te, nv=nv, name="moe_up")
    out = linear(p, stack(wd), tm=tm, tn=1024, te=te, nv=nv, name="moe_down")
    pos = pos.reshape(n_rows, TOP_K)
    ys = [out.at[pos[:, k]].get(mode="promise_in_bounds") for k in range(TOP_K)]
    return ys, [gates[:, k:k + 1] for k in range(TOP_K)]


def kernel(x, c, ctx, c_ctx, w_mod, b_mod, ln_g, ln_b, pool_w, pool_scale, hy_w_in, hy_b_in, hy_conv_w, hy_conv_b, hy_f_w1, hy_f_b1, hy_f_w2, hy_f_b2, hy_f_w3, hy_f_b3, hy_f_freq, hy_f_wout, hy_skip, hy_w_out, hy_b_out, mla_w_in, mla_q_norm, mla_kv_norm, mla_w_uq, mla_w_ukv, mla_w_o, ffn_w_gate, ffn_w_up, ffn_w_down, moe_w_router, moe_w_gate, moe_w_up, moe_w_down):
    attn_layers = [i for i in range(DEPTH) if i % N_MIXERS == 2]
    last_read = attn_layers[-1] if attn_layers else -1

    c8 = jnp.concatenate([c, c_ctx[None], jnp.zeros((MOD_ROWS - BATCH - 1, D_MODEL), F32)], axis=0)
    mods = modulation_table(c8, w_mod, b_mod)

    assert 0 < last_read and N_MIXERS > 1
    h = [x.reshape(N_LAT, D_MODEL), ctx.reshape(N_CTX, D_MODEL)]
    a = None
    for i in range(DEPTH):
        kind, j = i % N_MIXERS, i // N_MIXERS
        ctx_out = i < last_read
        n_rows = N_LAT + N_CTX if ctx_out else N_LAT
        m = mods[i]
        f = i // 2
        w_router = moe_w_router[f] if i % 2 == 1 else None
        if kind == 0:
            tl = CTX_LEN if n_rows > N_LAT else 512
            h1, a2, *logits = pool_layer(h if isinstance(h, list) else [h], n_rows, m,
                                         pool_w[j].astype(BF16), pool_scale[j],
                                         ln_g[i, 0], ln_b[i, 0], tl=tl, w_router=w_router)
        else:
            if kind == 1:
                filt = (hy_f_w1[j], hy_f_b1[j], hy_f_w2[j], hy_f_b2[j], hy_f_w3[j], hy_f_b3[j],
                        hy_f_freq[j], hy_f_wout[j])
                z = hyena_mix(a[:n_rows], n_rows, j, hy_w_in, hy_b_in, hy_conv_w[j], hy_conv_b[j],
                              filt, hy_skip[j])
                w_o, b_o = hy_w_out, hy_b_out
            else:
                assert i <= last_read and not ctx_out and a.shape[0] == N_LAT + N_CTX
                z = [mla_mix(a, mla_w_in[j], mla_q_norm[j], mla_kv_norm[j], mla_w_uq[j],
                             mla_w_ukv[j])]
                w_o, b_o = mla_w_o, None
            h1, a2, *logits = linear_resid_ln(z, w_o, j, b_o, h, n_rows, m, 2, ln_g[i, 0],
                                              ln_b[i, 0], 4, 3, w_router=w_router)
        nm = mods[min(i + 1, DEPTH - 1)]
        if w_router is not None:
            ys, row_gates = moe_ffn(logits[0], a2, n_rows, f, moe_w_gate, moe_w_up, moe_w_down)
        else:
            ys, row_gates = [dense_ffn(a2, f, ffn_w_gate, ffn_w_up, ffn_w_down)], None
        h, a = resid_ln(h1, ys, n_rows, m, 5, ln_g[i, 1], ln_b[i, 1], nm, 1, 0,
                        row_gates=row_gates)
    return h[:N_LAT].reshape(BATCH, SEQ, D_MODEL)
```

```python
import functools
import math

import jax
import jax.numpy as jnp
from jax import lax
from jax.experimental import pallas as pl
from jax.experimental.pallas import tpu as pltpu

F32 = jnp.float32
BF16 = jnp.bfloat16
U32 = jnp.uint32

D_MODEL = 2048
BATCH = 4
SEQ = 2048
DEPTH = 4
GRID_W = 64
CTX_LEN = 256
N_MIXERS = 3
POOL_WINDOWS = (2, 4, 8, 16)
POOL_GROUP = D_MODEL // len(POOL_WINDOWS)
HY_ORDER = 2
HY_SHORT = 3
HY_EMB = 33
HY_FILTER = 64
HY_DECAY_TARGET = 1e-2
HY_FAST = 0.3
HY_SLOW = 1.5
MLA_HEADS = D_MODEL // 128
MLA_Q_RANK = 512
MLA_KV_RANK = 512
MLA_NOPE = 128
MLA_ROPE = 64
MLA_V = 128
ROPE_BASE = 10000.0
D_FF = 5632
N_EXPERTS = 8
TOP_K = 2
EXPERT_FF = 2 * D_MODEL
LN_EPS = 1e-5
RMS_EPS = 1e-6
DN_ALPHA = (2 * DEPTH) ** 0.25

N_LAT = BATCH * SEQ
N_CTX = BATCH * CTX_LEN
MOD_ROWS = 8
CTX_MOD_ROW = BATCH
LANE = 128
SUBLANE = 8
VMEM_CAP = 60 * 1024 * 1024
MLA_QK_PAD = 256
MLA_IN = MLA_Q_RANK + MLA_KV_RANK + MLA_ROPE
MLA_IN_PAD = MLA_IN + MLA_ROPE


def _vmem(nbytes):
    return int(min(VMEM_CAP, max(16 * 1024 * 1024, nbytes * 3 // 2)))


def _nbytes(shape, dtype):
    return math.prod(shape) * jnp.dtype(dtype).itemsize


def _mod_row(i, tm):
    return jnp.minimum((i * tm) // SEQ, CTX_MOD_ROW)


def _mod_spec(which, tm):
    return pl.BlockSpec((None, 1, D_MODEL), lambda i: (_mod_row(i, tm) * 6 + which, 0, 0))


def _params(sem, est):
    return pltpu.CompilerParams(dimension_semantics=("arbitrary",) * sem,
                                vmem_limit_bytes=_vmem(est))


def _mod_kernel(c_ref, w_ref, b_ref, o_ref):
    c = c_ref[...]
    a = c / (1.0 + jnp.exp(-c))
    o_ref[...] = jnp.dot(a.astype(BF16), w_ref[...].astype(BF16),
                         preferred_element_type=F32) + b_ref[...]


def modulation_table(c8, w_mod, b_mod):
    tn = 1024
    n = 6 * D_MODEL
    out = pl.pallas_call(
        _mod_kernel,
        grid=(DEPTH, n // tn),
        in_specs=[pl.BlockSpec((MOD_ROWS, D_MODEL), lambda l, j: (0, 0)),
                  pl.BlockSpec((None, D_MODEL, tn), lambda l, j: (l, 0, j)),
                  pl.BlockSpec((None, 1, tn), lambda l, j: (l, 0, j))],
        out_specs=pl.BlockSpec((None, MOD_ROWS, tn), lambda l, j: (l, 0, j)),
        out_shape=jax.ShapeDtypeStruct((DEPTH, MOD_ROWS, n), F32),
        compiler_params=_params(2, 3 * _nbytes((D_MODEL, tn), F32)),
        name="modulation_table",
    )(c8, w_mod, b_mod.reshape(DEPTH, 1, n))
    return out.reshape(DEPTH, MOD_ROWS * 6, 1, D_MODEL)


def _group_schedule(te):
    n = te.shape[0]
    first = jnp.concatenate([jnp.ones((1,), jnp.int32), (te[1:] != te[:-1]).astype(jnp.int32)])
    gid = jnp.cumsum(first) - 1
    ge = jnp.zeros((n,), jnp.int32).at[gid].max(te)
    return first, gid, ge, gid[-1:] + 1


def _stream_weights(sched, w_hbms, stages, sem, dsts, col0, tn):
    first_ref, gid_ref, ge_ref, ng_ref = sched
    j, i = pl.program_id(0), pl.program_id(1)
    g, ng = gid_ref[i], ng_ref[0]

    def copies(e, col):
        cols = pl.ds(pl.multiple_of((col + col0) * tn, tn), tn)
        return [pltpu.make_async_copy(w.at[e, :, cols], st, sem.at[k])
                for k, (w, st) in enumerate(zip(w_hbms, stages))]

    @pl.when(first_ref[i] == 1)
    def _():
        @pl.when(jnp.logical_and(j == 0, g == 0))
        def _():
            for c in copies(ge_ref[0], 0):
                c.start()

        for c in copies(ge_ref[g], j):
            c.wait()
        for st, dst in zip(stages, dsts):
            dst[...] = st[...].astype(BF16)
        last = g + 1 >= ng
        next_e = ge_ref[jnp.where(last, 0, g + 1)]
        next_col = jnp.where(last, j + 1, j)

        @pl.when(next_col < pl.num_programs(0))
        def _():
            for c in copies(next_e, next_col):
                c.start()


def _for_valid_rows(vr, row_step, o_ref, compute):
    tm = o_ref.shape[0]
    for r in range(row_step, tm + 1, row_step):
        @pl.when(vr == r)
        def _(r=r):
            o_ref[0:r, :] = compute(r).astype(o_ref.dtype)
            if r < tm:
                o_ref[r:tm, :] = jnp.zeros((tm - r, o_ref.shape[1]), o_ref.dtype)

    @pl.when(vr == 0)
    def _():
        o_ref[...] = jnp.zeros_like(o_ref)


def _linear_kernel(xt_ref, vr_ref, first_ref, gid_ref, ge_ref, ng_ref, x_ref, w_hbm, *rest,
                   has_bias, col0, tn, row_step):
    rest = list(rest)
    b_ref = rest.pop(0) if has_bias else None
    o_ref, stage_ref, wbf_ref, sem = rest
    i = pl.program_id(1)
    _stream_weights((first_ref, gid_ref, ge_ref, ng_ref), [w_hbm], [stage_ref], sem, [wbf_ref],
                    col0, tn)

    def compute(r):
        acc = jnp.dot(x_ref[0:r, :].astype(BF16), wbf_ref[...], preferred_element_type=F32)
        return acc + b_ref[...] if has_bias else acc

    _for_valid_rows(vr_ref[i], row_step, o_ref, compute)


ROW_STEP = 128


def linear(x, w, *, tm, tn, e=0, n_out=None, n_tiles=None, xt=None, te=None, vr=None, bias=None,
           out_dtype=F32, w_col0=0, name="linear"):
    k = x.shape[1]
    if w.ndim == 2:
        w = w[None]
    if n_out is None:
        n_out = w.shape[2] - w_col0
    assert x.shape[0] % tm == 0 and n_out % tn == 0 and w_col0 % tn == 0 and w.shape[1] == k
    assert w.dtype == F32
    if n_tiles is None:
        n_tiles = x.shape[0] // tm
    if xt is None:
        xt = jnp.arange(n_tiles, dtype=jnp.int32)
    if te is None:
        te = jnp.full((n_tiles,), e, jnp.int32)
    row_step = tm if vr is None else ROW_STEP
    if vr is None:
        vr = jnp.full((n_tiles,), tm, jnp.int32)
    first, gid, ge, ng = _group_schedule(te)
    has_bias = bias is not None
    in_specs = [pl.BlockSpec((tm, k), lambda j, i, xt, *_: (xt[i], 0)),
                pl.BlockSpec(memory_space=pl.ANY)]
    args = [x, w]
    if has_bias:
        bias = bias.reshape(-1, 1, bias.shape[-1])
        in_specs.append(pl.BlockSpec((None, 1, tn),
                                     lambda j, i, xt, vr, first, gid, ge, ng: (ge[gid[i]], 0, j)))
        args.append(bias)
    est = (2 * _nbytes((tm, k), x.dtype) + _nbytes((k, tn), F32) + _nbytes((k, tn), BF16)
           + _nbytes((tm, k), BF16) + 2 * _nbytes((tm, tn), out_dtype) + 2 * _nbytes((tm, tn), F32))
    return pl.pallas_call(
        functools.partial(_linear_kernel, has_bias=has_bias, col0=w_col0 // tn, tn=tn,
                          row_step=row_step),
        grid_spec=pltpu.PrefetchScalarGridSpec(
            num_scalar_prefetch=6,
            grid=(n_out // tn, n_tiles),
            in_specs=in_specs,
            out_specs=pl.BlockSpec((tm, tn), lambda j, i, *_: (i, j)),
            scratch_shapes=[pltpu.VMEM((k, tn), F32), pltpu.VMEM((k, tn), BF16),
                            pltpu.SemaphoreType.DMA((1,))]),
        out_shape=jax.ShapeDtypeStruct((n_tiles * tm, n_out), out_dtype),
        compiler_params=_params(2, est),
        name=name,
    )(xt, vr, first, gid, ge, ng, *args)


def _swiglu_kernel(vr_ref, first_ref, gid_ref, ge_ref, ng_ref, *refs, n_parts, tf, row_step):
    x_refs = refs[:n_parts]
    wg_hbm, wu_hbm, o_ref, sg_ref, su_ref, wgb_ref, wub_ref, sem = refs[n_parts:]
    i = pl.program_id(1)
    _stream_weights((first_ref, gid_ref, ge_ref, ng_ref), [wg_hbm, wu_hbm], [sg_ref, su_ref], sem,
                    [wgb_ref, wub_ref], 0, tf)

    def compute(r):
        g = u = None
        k0 = 0
        for x_ref in x_refs:
            x = x_ref[0:r, :].astype(BF16)
            rows = slice(k0, k0 + x.shape[1])
            gp = jnp.dot(x, wgb_ref[rows, :], preferred_element_type=F32)
            up = jnp.dot(x, wub_ref[rows, :], preferred_element_type=F32)
            g, u = (gp, up) if g is None else (g + gp, u + up)
            k0 += x.shape[1]
        return (g / (1.0 + jnp.exp(-g))) * u

    _for_valid_rows(vr_ref[i], row_step, o_ref, compute)


def swiglu_up(x_parts, wg, wu, *, tm, tf, e=0, te=None, vr=None, name="swiglu_up"):
    m = x_parts[0].shape[0]
    k, f = wg.shape[1], wg.shape[2]
    assert sum(x.shape[1] for x in x_parts) == k and m % tm == 0 and f % tf == 0
    n_tiles = m // tm
    if te is None:
        te = jnp.full((n_tiles,), e, jnp.int32)
    row_step = tm if vr is None else ROW_STEP
    if vr is None:
        vr = jnp.full((n_tiles,), tm, jnp.int32)
    first, gid, ge, ng = _group_schedule(te)
    wspec = pl.BlockSpec(memory_space=pl.ANY)
    est = (2 * _nbytes((tm, k), x_parts[0].dtype) + _nbytes((tm, k), BF16)
           + 2 * _nbytes((k, tf), F32) + 2 * _nbytes((k, tf), BF16) + 2 * _nbytes((tm, tf), BF16)
           + 3 * _nbytes((tm, tf), F32))
    x_specs = [pl.BlockSpec((tm, x.shape[1]), lambda j, i, *_: (i, 0)) for x in x_parts]
    return pl.pallas_call(
        functools.partial(_swiglu_kernel, n_parts=len(x_parts), tf=tf, row_step=row_step),
        grid_spec=pltpu.PrefetchScalarGridSpec(
            num_scalar_prefetch=5,
            grid=(f // tf, n_tiles),
            in_specs=[*x_specs, wspec, wspec],
            out_specs=pl.BlockSpec((tm, tf), lambda j, i, *_: (i, j)),
            scratch_shapes=[pltpu.VMEM((k, tf), F32), pltpu.VMEM((k, tf), F32),
                            pltpu.VMEM((k, tf), BF16), pltpu.VMEM((k, tf), BF16),
                            pltpu.SemaphoreType.DMA((2,))]),
        out_shape=jax.ShapeDtypeStruct((m, f), BF16),
        compiler_params=_params(2, est),
        name=name,
    )(vr, first, gid, ge, ng, *x_parts, wg, wu)


def _ln_mod(v, g, b, sc, sh, pack):
    mu = jnp.mean(v, axis=-1, keepdims=True)
    d = v - mu
    var = jnp.mean(d * d, axis=-1, keepdims=True)
    h = d * lax.rsqrt(var + LN_EPS) * g + b
    a = h * (1.0 + sc) + sh
    return h, (a if pack else a.astype(BF16))


def _resid_ln_kernel(*refs, n_y, pack):
    h_ref = refs[0]
    y_refs = refs[1:1 + n_y]
    rest = refs[1 + n_y:]
    if n_y == 2:
        g0_ref, g1_ref = rest[0], rest[1]
        rest = rest[2:]
        y = g0_ref[...] * y_refs[0][...] + g1_ref[...] * y_refs[1][...]
    else:
        y = y_refs[0][...]
    gate_ref, g_ref, b_ref, sc_ref, sh_ref, o_ref, a_ref = rest
    v = DN_ALPHA * h_ref[...] + gate_ref[...] * y
    h, a = _ln_mod(v, g_ref[...], b_ref[...], sc_ref[...], sh_ref[...], pack)
    o_ref[...] = h
    a_ref[...] = a


def resid_ln(h, ys, n_rows, mods, gate_idx, ln_g, ln_b, nmods, sc_idx, sh_idx, *, row_gates=None,
             pack=False, tm=256):
    assert n_rows % tm == 0
    row = pl.BlockSpec((tm, D_MODEL), lambda i: (i, 0))
    col = pl.BlockSpec((tm, 1), lambda i: (i, 0))
    vec = pl.BlockSpec((1, D_MODEL), lambda i: (0, 0))
    a_w = D_MODEL
    a_dt = F32 if pack else BF16
    est = 10 * _nbytes((tm, D_MODEL), F32)
    in_specs = [row] + [row] * len(ys) + ([col, col] if row_gates else []) + [
        _mod_spec(gate_idx, tm), vec, vec, _mod_spec(sc_idx, tm), _mod_spec(sh_idx, tm)]
    args = [h, *ys, *(row_gates or []), mods, ln_g.reshape(1, D_MODEL), ln_b.reshape(1, D_MODEL),
            nmods, nmods]
    return pl.pallas_call(
        functools.partial(_resid_ln_kernel, n_y=len(ys), pack=pack),
        grid=(n_rows // tm,),
        in_specs=in_specs,
        out_specs=[row, pl.BlockSpec((tm, a_w), lambda i: (i, 0))],
        out_shape=[jax.ShapeDtypeStruct((n_rows, D_MODEL), F32),
                   jax.ShapeDtypeStruct((n_rows, a_w), a_dt)],
        compiler_params=_params(1, est),
        name="resid_ln",
    )(*args)


def _split_bf16(x):
    hi = x.astype(BF16)
    lo = (x - hi.astype(F32)).astype(BF16)
    return hi, lo


def _router_logits(a, whi_ref, wlo_ref):
    a_hi, a_lo = _split_bf16(a)
    w_hi = whi_ref[...]
    return (jnp.dot(a_hi, w_hi, preferred_element_type=F32)
            + jnp.dot(a_lo, w_hi, preferred_element_type=F32)
            + jnp.dot(a_hi, wlo_ref[...], preferred_element_type=F32))


def _router_operands(w_router):
    w = jnp.pad(w_router, ((0, 0), (0, LANE - N_EXPERTS)))
    return _split_bf16(w), pl.BlockSpec((D_MODEL, LANE), lambda i: (0, 0))


def _linear_ln_kernel(*refs, has_bias, moe, split):
    n_src = 1 if split is None else 2
    x_refs, w_ref, rest = refs[:n_src], refs[n_src], list(refs[n_src + 1:])
    bias_ref = rest.pop(0) if has_bias else None
    h_ref, gate_ref, g_ref, b_ref, sc_ref, sh_ref = rest[:6]
    rest = rest[6:]
    if moe:
        whi_ref, wlo_ref, o_ref, a_ref, l_ref, wb_ref = rest
    else:
        o_ref, a_ref, wb_ref = rest

    @pl.when(pl.program_id(0) == 0)
    def _():
        wb_ref[...] = w_ref[...].astype(BF16)

    if split is None:
        x = x_refs[0][...]
    else:
        x = jnp.where(pl.program_id(0) < split, x_refs[0][...], x_refs[1][...])
    y = jnp.dot(x, wb_ref[...], preferred_element_type=F32)
    if has_bias:
        y = y + bias_ref[...]
    v = DN_ALPHA * h_ref[...] + gate_ref[...] * y
    h, a = _ln_mod(v, g_ref[...], b_ref[...], sc_ref[...], sh_ref[...], moe)
    o_ref[...] = h
    a_ref[...] = a
    if moe:
        l_ref[...] = _router_logits(a, whi_ref, wlo_ref)


def linear_resid_ln(xs, w, e, bias, h, n_rows, mods, gate_idx, ln_g, ln_b, sc_idx, sh_idx, *,
                    w_router=None, tm=256):
    k = xs[0].shape[1]
    moe = w_router is not None
    row = pl.BlockSpec((tm, D_MODEL), lambda i: (i, 0))
    vec = pl.BlockSpec((1, D_MODEL), lambda i: (0, 0))
    split = None if len(xs) == 1 else xs[0].shape[0] // tm
    in_specs, t0 = [], 0
    for x in xs:
        nt = x.shape[0] // tm
        in_specs.append(pl.BlockSpec((tm, k), lambda i, t0=t0, nt=nt: (jnp.clip(i - t0, 0, nt - 1), 0)))
        t0 += nt
    in_specs.append(pl.BlockSpec((None, k, D_MODEL), lambda i: (e, 0, 0),
                                 pipeline_mode=pl.Buffered(1)))
    args = [*xs, w]
    if bias is not None:
        in_specs.append(pl.BlockSpec((None, 1, D_MODEL), lambda i: (e, 0, 0)))
        args.append(bias.reshape(-1, 1, D_MODEL))
    in_specs += [row, _mod_spec(gate_idx, tm), vec, vec, _mod_spec(sc_idx, tm),
                 _mod_spec(sh_idx, tm)]
    args += [h, mods, ln_g.reshape(1, D_MODEL), ln_b.reshape(1, D_MODEL), mods, mods]
    out_specs = [row, row]
    out_shape = [jax.ShapeDtypeStruct((n_rows, D_MODEL), F32),
                 jax.ShapeDtypeStruct((n_rows, D_MODEL), F32 if moe else BF16)]
    if moe:
        (w_hi, w_lo), wspec = _router_operands(w_router)
        in_specs += [wspec, wspec]
        args += [w_hi, w_lo]
        out_specs.append(pl.BlockSpec((tm, LANE), lambda i: (i, 0)))
        out_shape.append(jax.ShapeDtypeStruct((n_rows, LANE), F32))
    est = (_nbytes((k, D_MODEL), F32) + _nbytes((k, D_MODEL), BF16) + 2 * _nbytes((tm, k), BF16)
           + 12 * _nbytes((tm, D_MODEL), F32))
    return pl.pallas_call(
        functools.partial(_linear_ln_kernel, has_bias=bias is not None, moe=moe, split=split),
        grid=(n_rows // tm,),
        in_specs=in_specs,
        out_specs=out_specs,
        out_shape=out_shape,
        scratch_shapes=[pltpu.VMEM((k, D_MODEL), BF16)],
        compiler_params=_params(1, est),
        name="linear_resid_ln",
    )(*args)


def _seq_pos(i, tl):
    row0 = i * tl
    is_lat = row0 < N_LAT
    pos = jnp.where(is_lat, row0 % SEQ, (row0 - N_LAT) % CTX_LEN)
    seqlen = jnp.where(is_lat, SEQ, CTX_LEN)
    return pos, seqlen


def _tile_halo_specs(tl, width, t0, nt):
    per = tl // SUBLANE
    local = lambda i: jnp.clip(i - t0, 0, nt - 1)
    main = pl.BlockSpec((tl, width), lambda i: (local(i), 0))
    prev = pl.BlockSpec((SUBLANE, width), lambda i: (jnp.maximum(local(i) * per - 1, 0), 0))
    nxt = pl.BlockSpec((SUBLANE, width),
                       lambda i: (jnp.minimum((local(i) + 1) * per, nt * per - 1), 0))
    return [main, prev, nxt]


def _shift_rows(x, s):
    n = x.shape[0]
    return pltpu.roll(x, s % n, 0)


def _pool_kernel(*refs, tl, moe, split):
    n_src = 1 if split is None else 2
    srcs, rest = refs[:3 * n_src], refs[3 * n_src:]
    (sc_ref, sh_ref, gate_ref, w_ref, ps_ref, g_ref, b_ref, nsc_ref, nsh_ref), rest = rest[:9], rest[9:]
    if moe:
        whi_ref, wlo_ref, o_ref, a_ref, l_ref = rest
    else:
        o_ref, a_ref = rest
    i = pl.program_id(0)
    pos, seqlen = _seq_pos(i, tl)
    if split is None:
        h, h_prev, h_next = (r[...] for r in srcs)
    else:
        h, h_prev, h_next = (jnp.where(i < split, r0[...], r1[...])
                             for r0, r1 in zip(srcs[:3], srcs[3:]))
    sc = 1.0 + sc_ref[...]
    sh = sh_ref[...]
    a = h * sc + sh
    keep_prev = (pos > 0).astype(F32)
    keep_next = (pos + tl < seqlen).astype(F32)
    ext = jnp.concatenate([(h_prev * sc + sh) * keep_prev, a,
                           (h_next * sc + sh) * keep_next], axis=0)
    t = pos + lax.broadcasted_iota(jnp.int32, (tl, 1), 0)
    n_ext = tl + 2 * SUBLANE
    parts = []
    for g, w in enumerate(POOL_WINDOWS):
        cols = slice(g * POOL_GROUP, (g + 1) * POOL_GROUP)
        s = ext[:, cols]
        s = s + _shift_rows(s, 1)
        r = 1
        while 2 * r < w:
            s = _shift_rows(s, n_ext - r) + _shift_rows(s, r)
            r *= 2
        s = s[SUBLANE:SUBLANE + tl]
        cnt = (jnp.minimum(t + w // 2, seqlen) - jnp.maximum(t - w // 2, 0)).astype(F32)
        d = s / cnt - a[:, cols]
        parts.append(jnp.dot(d.astype(BF16), w_ref[g], preferred_element_type=F32))
    y = jnp.concatenate(parts, axis=1) * ps_ref[...]
    v = DN_ALPHA * h + gate_ref[...] * y
    hn, an = _ln_mod(v, g_ref[...], b_ref[...], nsc_ref[...], nsh_ref[...], moe)
    o_ref[...] = hn
    a_ref[...] = an
    if moe:
        l_ref[...] = _router_logits(an, whi_ref, wlo_ref)


def pool_layer(hs, n_rows, mods, w_grp_bf16, pool_scale, ln_g, ln_b, *, tl, w_router=None):
    moe = w_router is not None
    row = pl.BlockSpec((tl, D_MODEL), lambda i: (i, 0))
    vec = pl.BlockSpec((1, D_MODEL), lambda i: (0, 0))
    split = None if len(hs) == 1 else hs[0].shape[0] // tl
    in_specs, args, t0 = [], [], 0
    for h in hs:
        nt = h.shape[0] // tl
        in_specs += _tile_halo_specs(tl, D_MODEL, t0, nt)
        args += [h, h, h]
        t0 += nt
    in_specs += [_mod_spec(1, tl), _mod_spec(0, tl), _mod_spec(2, tl),
                 pl.BlockSpec(w_grp_bf16.shape, lambda i: (0, 0, 0)), vec, vec, vec,
                 _mod_spec(4, tl), _mod_spec(3, tl)]
    args += [mods, mods, mods, w_grp_bf16, pool_scale.reshape(1, D_MODEL),
             ln_g.reshape(1, D_MODEL), ln_b.reshape(1, D_MODEL), mods, mods]
    out_specs = [row, row]
    out_shape = [jax.ShapeDtypeStruct((n_rows, D_MODEL), F32),
                 jax.ShapeDtypeStruct((n_rows, D_MODEL), F32 if moe else BF16)]
    if moe:
        (w_hi, w_lo), wspec = _router_operands(w_router)
        in_specs += [wspec, wspec]
        args += [w_hi, w_lo]
        out_specs.append(pl.BlockSpec((tl, LANE), lambda i: (i, 0)))
        out_shape.append(jax.ShapeDtypeStruct((n_rows, LANE), F32))
    est = 14 * _nbytes((tl, D_MODEL), F32) + 2 * _nbytes(w_grp_bf16.shape, BF16)
    return pl.pallas_call(
        functools.partial(_pool_kernel, tl=tl, moe=moe, split=split),
        grid=(n_rows // tl,),
        in_specs=in_specs,
        out_specs=out_specs,
        out_shape=out_shape,
        compiler_params=_params(1, est),
        name="pool_layer",
    )(*args)


def _conv3_seq(u, w, b):
    n = u.shape[0]
    t = lax.broadcasted_iota(jnp.int32, (n, 1), 0)
    before = jnp.where(t == 0, 0.0, _shift_rows(u, 1))
    after = jnp.where(t == n - 1, 0.0, _shift_rows(u, n - 1))
    return b + before * w[0:1] + u * w[1:2] + after * w[2:3]


def _conv3_tile(u, prev, nxt, w, b, first, last):
    n = u.shape[0]
    t = lax.broadcasted_iota(jnp.int32, (n, 1), 0)
    row_before = prev[SUBLANE - 1:SUBLANE, :] * (1.0 - first.astype(F32))
    row_after = nxt[0:1, :] * (1.0 - last.astype(F32))
    before = jnp.where(t == 0, row_before, _shift_rows(u, 1))
    after = jnp.where(t == n - 1, row_after, _shift_rows(u, n - 1))
    return b + before * w[0:1] + u * w[1:2] + after * w[2:3]


def _dot3(a, b):
    a_hi, a_lo = _split_bf16(a)
    b_hi, b_lo = _split_bf16(b)
    return (jnp.dot(a_hi, b_hi, preferred_element_type=F32)
            + jnp.dot(a_lo, b_hi, preferred_element_type=F32)
            + jnp.dot(a_hi, b_lo, preferred_element_type=F32))


def _filter_kernel(z_ref, w1_ref, b1_ref, w2_ref, b2_ref, w3_ref, b3_ref, fr_ref, wo_ref,
                   dist_ref, delta_ref, o_ref, a_ref):
    @pl.when(pl.program_id(0) == 0)
    def _():
        fr = fr_ref[...]
        a = jnp.sin(fr * (_dot3(z_ref[...], w1_ref[...]) + b1_ref[...]))
        a = jnp.sin(fr * (_dot3(a, w2_ref[...]) + b2_ref[...]))
        a_ref[...] = jnp.sin(fr * (_dot3(a, w3_ref[...]) + b3_ref[...]))

    k = _dot3(a_ref[...], wo_ref[...])
    k = k * jnp.exp(-dist_ref[...] * delta_ref[...])
    o_ref[...] = k * lax.rsqrt(jnp.sum(k * k, axis=0, keepdims=True) + 1e-6)


def _pad2(x, rows, cols):
    return jnp.pad(x, ((0, rows - x.shape[0]), (0, cols - x.shape[1])))


def hyena_filters(L, f_w1, f_b1, f_w2, f_b2, f_w3, f_b3, f_freq, f_wout, *, tn=512):
    bands = (HY_EMB - 1) // 2
    t = jnp.linspace(0.0, 1.0, L, dtype=F32)[:, None]
    wpos = 2.0 * math.pi * jnp.arange(L, dtype=F32)[:, None] / L
    f = jnp.linspace(1e-4, bands - 1, bands, dtype=F32)[None, :]
    z = jnp.concatenate([t, jnp.cos(f * wpos), -jnp.sin(f * wpos)], axis=-1)
    dist = (jnp.abs(jnp.arange(L) - L // 2).astype(F32) / max(L // 2, 1))[:, None]
    deltas = jnp.linspace(math.log(HY_DECAY_TARGET) / HY_SLOW, math.log(HY_DECAY_TARGET) / HY_FAST,
                          D_MODEL, dtype=F32)
    absdelta = jnp.tile(jnp.abs(deltas), HY_ORDER)[None, :]
    P = LANE
    n = HY_ORDER * D_MODEL
    full = lambda shape: pl.BlockSpec(shape, lambda j: (0, 0))
    args = [_pad2(z, L, P), _pad2(f_w1, P, P), _pad2(f_b1[None], 1, P), _pad2(f_w2, P, P),
            _pad2(f_b2[None], 1, P), _pad2(f_w3, P, P), _pad2(f_b3[None], 1, P),
            _pad2(f_freq[None], 1, P), _pad2(f_wout, P, n), dist, absdelta]
    in_specs = [full((L, P)), full((P, P)), full((1, P)), full((P, P)), full((1, P)), full((P, P)),
                full((1, P)), full((1, P)), pl.BlockSpec((P, tn), lambda j: (0, j)),
                full((L, 1)), pl.BlockSpec((1, tn), lambda j: (0, j))]
    return pl.pallas_call(
        _filter_kernel,
        grid=(n // tn,),
        in_specs=in_specs,
        out_specs=pl.BlockSpec((L, tn), lambda j: (0, j)),
        out_shape=jax.ShapeDtypeStruct((L, n), F32),
        scratch_shapes=[pltpu.VMEM((L, P), F32)],
        compiler_params=_params(1, 8 * _nbytes((L, tn), F32)),
        name="hyena_filters",
    )(*args)


def dft_tables(L, tm):
    n_fft = 2 * L
    half = tm // 2
    r_idx = jnp.arange(2 * L, dtype=jnp.int32)
    odd = 2 * ((r_idx // tm) * half + (r_idx % half)) + 1
    is_sin = ((r_idx % tm) >= half)[:, None]

    def trig(k):
        ang = (k % (2 * n_fft)).astype(F32) * (math.pi / n_fft)
        return jnp.cos(ang), jnp.sin(ang)

    lo = 64
    s1 = jnp.arange(L // lo, dtype=jnp.int32) * lo
    s0 = jnp.arange(lo, dtype=jnp.int32)
    cb, sb = trig(odd[:, None] * s0[None, :])

    def combine(ca, sa):
        a1 = jnp.where(is_sin, sa, ca)
        a2 = jnp.where(is_sin, ca, -sa)
        return a1, a2

    a1, a2 = combine(*trig(odd[:, None] * s1[None, :]))
    coarse = lambda t: jnp.repeat(t, lo, axis=1)
    fine = lambda t: jnp.tile(t, (1, L // lo))
    fwd = coarse(a1) * fine(cb) + coarse(a2) * fine(sb)
    a1, a2 = combine(*trig(odd[:, None] * (s1[None, :] + L // 2)))
    inv = (a1.T[:, None, :] * cb.T[None, :, :] + a2.T[:, None, :] * sb.T[None, :, :]).reshape(
        L, 2 * L)
    return fwd.astype(BF16), inv.astype(BF16)


def _dft_mul_kernel(x_ref, w_ref, *rest, inv_len, conv):
    if conv:
        cw_ref, cb_ref, k_ref, o_ref, wb_ref = rest
    else:
        k_ref, o_ref, wb_ref = rest

    @pl.when(pl.program_id(1) == 0)
    def _():
        w = w_ref[...]
        if conv:
            w = _conv3_seq(w, cw_ref[...], cb_ref[...])
        wb_ref[...] = w.astype(BF16)

    s = jnp.dot(x_ref[...], wb_ref[...], preferred_element_type=F32)
    half = s.shape[0] // 2
    a, b = s[:half], s[half:]
    ka, kb = k_ref[0:half, :], k_ref[half:2 * half, :]
    o_ref[0:half, :] = ((a * ka - b * kb) * inv_len).astype(BF16)
    o_ref[half:2 * half, :] = ((a * kb + b * ka) * inv_len).astype(BF16)


def dft_mul(fwd, v_view, kspec, k_col0, *, L, nb, tm, tn, conv=None):
    d = D_MODEL
    per_b = d // tn
    v, v_row0, v_col0 = v_view
    vr, vc, kc = v_row0 // L, v_col0 // tn, k_col0 // tn
    est = (2 * _nbytes((tm, L), BF16) + 2 * _nbytes((L, tn), F32) + _nbytes((L, tn), BF16)
           + 2 * _nbytes((tm, tn), F32) + 2 * _nbytes((tm, tn), BF16) + 3 * _nbytes((tm, tn), F32))
    in_specs = [pl.BlockSpec((tm, L), lambda j, i: (i, 0)),
                pl.BlockSpec((L, tn), lambda j, i: (j // per_b + vr, j % per_b + vc))]
    args = [fwd, v]
    if conv is not None:
        in_specs += [pl.BlockSpec((HY_SHORT, tn), lambda j, i: (0, j % per_b + vc)),
                     pl.BlockSpec((1, tn), lambda j, i: (0, j % per_b + vc))]
        args += list(conv)
    in_specs.append(pl.BlockSpec((tm, tn), lambda j, i: (i, j % per_b + kc)))
    args.append(kspec)
    return pl.pallas_call(
        functools.partial(_dft_mul_kernel, inv_len=1.0 / L, conv=conv is not None),
        grid=(nb * per_b, (2 * L) // tm),
        in_specs=in_specs,
        out_specs=pl.BlockSpec((tm, tn), lambda j, i: (i, j)),
        out_shape=jax.ShapeDtypeStruct((2 * L, nb * d), BF16),
        scratch_shapes=[pltpu.VMEM((L, tn), BF16)],
        compiler_params=_params(2, est),
        name="dft_mul",
    )(*args)


def _idft_kernel(x_ref, y_ref, *rest, z_conv, per):
    rest = list(rest)
    g_ref, gp_ref, gn_ref, gw_ref, gb_ref = rest[:5]
    rest = rest[5:]
    i = pl.program_id(1)
    first, last = i == 0, i == per - 1
    conv = jnp.dot(x_ref[...], y_ref[...], preferred_element_type=F32)
    gate = _conv3_tile(g_ref[...], gp_ref[...], gn_ref[...], gw_ref[...], gb_ref[...], first, last)
    if z_conv:
        z_ref, zp_ref, zn_ref, zw_ref, zb_ref, skip_ref, o_ref = rest
        z = _conv3_tile(z_ref[...], zp_ref[...], zn_ref[...], zw_ref[...], zb_ref[...], first, last)
    else:
        z_ref, skip_ref, o_ref = rest
        z = z_ref[...]
    o_ref[...] = (gate * (conv + z * skip_ref[...])).astype(o_ref.dtype)


def idft_gate(inv, y, gate_view, z_view, skip, conv, *, z_conv, L, nb, tm, tn, out_dtype):
    d = D_MODEL
    per_b = d // tn
    per = L // tm
    sub = tm // SUBLANE

    def tile_specs(view, with_conv):
        arr, row0, col0 = view
        r0, c0 = row0 // tm, col0 // tn
        last_blk = arr.shape[0] // SUBLANE - 1
        row = lambda j, i: (j // per_b) * per + i + r0
        col = lambda j, i: j % per_b + c0
        specs = [pl.BlockSpec((tm, tn), lambda j, i: (row(j, i), col(j, i)))]
        args = [arr]
        if with_conv:
            specs += [
                pl.BlockSpec((SUBLANE, tn),
                             lambda j, i: (jnp.maximum(row(j, i) * sub - 1, 0), col(j, i))),
                pl.BlockSpec((SUBLANE, tn),
                             lambda j, i: (jnp.minimum((row(j, i) + 1) * sub, last_blk), col(j, i))),
                pl.BlockSpec((HY_SHORT, tn), lambda j, i: (0, col(j, i))),
                pl.BlockSpec((1, tn), lambda j, i: (0, col(j, i)))]
            args += [arr, arr, conv[0], conv[1]]
        return specs, args

    g_specs, g_args = tile_specs(gate_view, True)
    z_specs, z_args = tile_specs(z_view, z_conv)
    est = (2 * _nbytes((tm, 2 * L), BF16) + 2 * _nbytes((2 * L, tn), BF16)
           + 12 * _nbytes((tm, tn), F32))
    return pl.pallas_call(
        functools.partial(_idft_kernel, z_conv=z_conv, per=per),
        grid=(nb * per_b, per),
        in_specs=[pl.BlockSpec((tm, 2 * L), lambda j, i: (i, 0)),
                  pl.BlockSpec((2 * L, tn), lambda j, i: (0, j)),
                  *g_specs, *z_specs,
                  pl.BlockSpec((1, tn), lambda j, i: (0, j % per_b))],
        out_specs=pl.BlockSpec((tm, tn), lambda j, i: ((j // per_b) * per + i, j % per_b)),
        out_shape=jax.ShapeDtypeStruct((nb * L, d), out_dtype),
        compiler_params=_params(2, est),
        name="idft_gate",
    )(inv, y, *g_args, *z_args, skip.reshape(1, d))


def hyena_stream(u, row0, L, nb, conv, filt, skip):
    d = D_MODEL
    tms = min(1024, 2 * L)
    tmi = min(512, L)
    k = hyena_filters(L, *filt)
    fwd, inv = dft_tables(L, tms)
    kspec = linear(fwd, k, tm=tms, tn=1024, name="dft_filters")
    y1 = dft_mul(fwd, (u, row0, 0), kspec, 0, L=L, nb=nb, tm=tms, tn=1024, conv=conv)
    z1 = idft_gate(inv, y1, (u, row0, d), (u, row0, 0), skip[0], conv, z_conv=True, L=L, nb=nb,
                   tm=tmi, tn=1024, out_dtype=F32)
    y2 = dft_mul(fwd, (z1, 0, 0), kspec, d, L=L, nb=nb, tm=tms, tn=1024)
    return idft_gate(inv, y2, (u, row0, 2 * d), (z1, 0, 0), skip[1], conv, z_conv=False, L=L,
                     nb=nb, tm=tmi, tn=1024, out_dtype=BF16)


def hyena_mix(a, n_rows, j, w_in, b_in, conv_w, conv_b, filt, skip):
    u = linear(a, w_in, e=j, tm=1024, tn=2048, bias=b_in, name="hyena_in")
    conv = (conv_w, conv_b.reshape(1, -1))
    zs = [hyena_stream(u, 0, SEQ, BATCH, conv, filt, skip)]
    if n_rows > N_LAT:
        zs.append(hyena_stream(u, N_LAT, CTX_LEN, BATCH, conv, filt, skip))
    return zs


def _rms(x, g):
    return x * lax.rsqrt(jnp.mean(x * x, axis=-1, keepdims=True) + RMS_EPS) * g


def _mla_q_kernel(p_ref, g_ref, w1_ref, w2_ref, c_ref, s_ref, o_ref, w1b_ref, w2b_ref):
    @pl.when(pl.program_id(1) == 0)
    def _():
        w1b_ref[...] = w1_ref[...].astype(BF16)
        w2b_ref[...] = w2_ref[...].astype(BF16)

    xn = _rms(p_ref[...], g_ref[...]).astype(BF16)
    q1 = jnp.dot(xn, w1b_ref[...], preferred_element_type=F32)
    q2 = jnp.dot(xn, w2b_ref[...], preferred_element_type=F32)
    reps = q1.shape[1] // MLA_QK_PAD
    c = jnp.concatenate([c_ref[...]] * reps, axis=1)
    s = jnp.concatenate([s_ref[...]] * reps, axis=1)
    o_ref[...] = (q1 * c + q2 * s).astype(BF16)


def _mla_kv_kernel(p_ref, r_ref, g_ref, cs_ref, w_ref, o_ref, wb_ref):
    @pl.when(pl.program_id(1) == 0)
    def _():
        wb_ref[...] = w_ref[...].astype(BF16)

    xn = _rms(p_ref[...], g_ref[...]).astype(BF16)
    rr = (r_ref[...] * cs_ref[...]).astype(BF16)
    acc = (jnp.dot(xn, wb_ref[0:MLA_KV_RANK, :], preferred_element_type=F32)
           + jnp.dot(rr, wb_ref[MLA_KV_RANK:MLA_KV_RANK + 2 * MLA_ROPE, :],
                     preferred_element_type=F32))
    o_ref[...] = acc.astype(BF16)


def _attn_kernel(q_ref, kl_ref, kc_ref, vl_ref, vc_ref, o_ref, *, kchunk):
    c = (MLA_NOPE + MLA_ROPE) ** -0.5 * math.log2(math.e)
    nt = (((1,), (1,)), ((), ()))
    q = q_ref[...]
    n_lat, n_ctx = kl_ref.shape[0], kc_ref.shape[0]
    assert (n_lat + n_ctx) % kchunk == 0 and n_lat % kchunk + n_ctx == kchunk
    m = l = acc = None
    for r in range((n_lat + n_ctx) // kchunk):
        lo, hi = r * kchunk, min((r + 1) * kchunk, n_lat)
        if (r + 1) * kchunk <= n_lat:
            k, v = kl_ref[lo:hi, :], vl_ref[lo:hi, :]
        else:
            k = jnp.concatenate([kl_ref[lo:hi, :], kc_ref[...]], axis=0)
            v = jnp.concatenate([vl_ref[lo:hi, :], vc_ref[...]], axis=0)
        s = lax.dot_general(q, k, nt, preferred_element_type=F32)
        ms = jnp.max(s, axis=-1, keepdims=True)
        m_new = ms if m is None else jnp.maximum(m, ms)
        p = jnp.exp2((s - m_new) * c)
        ps = jnp.sum(p, axis=-1, keepdims=True)
        pv = jnp.dot(p.astype(BF16), v, preferred_element_type=F32)
        if m is None:
            l, acc = ps, pv
        else:
            alpha = jnp.exp2((m - m_new) * c)
            l = alpha * l + ps
            acc = alpha * acc + pv
        m = m_new
    o_ref[...] = (acc / l).astype(o_ref.dtype)


def _rope_tables():
    rows = SEQ // GRID_W
    row = jnp.repeat(jnp.arange(rows), GRID_W).astype(F32)
    col = jnp.tile(jnp.arange(GRID_W), rows).astype(F32)
    n = MLA_ROPE // 4
    inv = ROPE_BASE ** (-jnp.arange(n, dtype=F32) / n)
    ang_row, ang_col = row[:, None] * inv, col[:, None] * inv
    cos = jnp.concatenate([jnp.cos(ang_row)] * 2 + [jnp.cos(ang_col)] * 2, axis=1)
    sin = jnp.concatenate([jnp.sin(ang_row)] * 2 + [jnp.sin(ang_col)] * 2, axis=1)
    return cos, sin


def _rope_partner(w):
    n = MLA_ROPE // 4
    pairs = w.reshape(w.shape[:-1] + (2, 2, n))
    sign = jnp.array([-1.0, 1.0], F32)[:, None]
    return (pairs[..., ::-1, :] * sign).reshape(w.shape)


def mla_mix(a, w_in, q_norm, kv_norm, w_uq, w_ukv):
    H, NP, R, V, QK = MLA_HEADS, MLA_NOPE, MLA_ROPE, MLA_V, MLA_QK_PAD
    n_all = N_LAT + N_CTX
    cos, sin = _rope_tables()
    w_in_x = jnp.concatenate([w_in, _rope_partner(w_in[:, MLA_IN - R:])], axis=1)
    proj = linear(a, w_in_x, tm=512, tn=MLA_IN_PAD, name="mla_in")
    wq = w_uq.reshape(MLA_Q_RANK, H, NP + R)
    tail = QK - NP - R
    wq1 = jnp.pad(wq, ((0, 0), (0, 0), (0, tail))).reshape(MLA_Q_RANK, H * QK)
    wq2 = jnp.pad(_rope_partner(wq[..., NP:]),
                  ((0, 0), (0, 0), (NP, tail))).reshape(MLA_Q_RANK, H * QK)
    ones, zeros = jnp.ones((SEQ, NP), F32), jnp.zeros((SEQ, QK - NP - R), F32)
    cq = jnp.concatenate([ones, cos, zeros], axis=1)
    sq = jnp.concatenate([0.0 * ones, sin, zeros], axis=1)
    tm, tn = 1024, 1024
    per = SEQ // tm
    q = pl.pallas_call(
        _mla_q_kernel,
        grid=(H * QK // tn, N_LAT // tm),
        in_specs=[pl.BlockSpec((tm, MLA_Q_RANK), lambda jn, i: (i, 0)),
                  pl.BlockSpec((1, MLA_Q_RANK), lambda jn, i: (0, 0)),
                  pl.BlockSpec((MLA_Q_RANK, tn), lambda jn, i: (0, jn)),
                  pl.BlockSpec((MLA_Q_RANK, tn), lambda jn, i: (0, jn)),
                  pl.BlockSpec((tm, QK), lambda jn, i: (i % per, 0)),
                  pl.BlockSpec((tm, QK), lambda jn, i: (i % per, 0))],
        out_specs=pl.BlockSpec((tm, tn), lambda jn, i: (i, jn)),
        out_shape=jax.ShapeDtypeStruct((N_LAT, H * QK), BF16),
        scratch_shapes=[pltpu.VMEM((MLA_Q_RANK, tn), BF16), pltpu.VMEM((MLA_Q_RANK, tn), BF16)],
        compiler_params=_params(2, 6 * _nbytes((MLA_Q_RANK, tn), F32) + 8 * _nbytes((tm, tn), F32)),
        name="mla_q",
    )(proj, q_norm.reshape(1, MLA_Q_RANK), wq1, wq2, cq, sq)
    wkv = w_ukv.reshape(MLA_KV_RANK, H, NP + V)
    wk = jnp.concatenate([wkv[..., :NP], jnp.zeros((MLA_KV_RANK, H, QK - NP), F32)],
                         axis=2).reshape(MLA_KV_RANK, H * QK)
    place = jnp.concatenate([jnp.zeros((R, NP), F32), jnp.eye(R, dtype=F32),
                             jnp.zeros((R, QK - NP - R), F32)], axis=1)
    place = jnp.tile(jnp.concatenate([place, place], axis=0), (1, H))
    w_kv = jnp.concatenate([
        jnp.concatenate([wk, wkv[..., NP:].reshape(MLA_KV_RANK, H * V)], axis=1),
        jnp.concatenate([place, jnp.zeros((2 * R, H * V), F32)], axis=1)], axis=0)
    tm, tn = 512, 2048
    cs = jnp.concatenate([
        jnp.concatenate([cos, sin], axis=1),
        jnp.concatenate([jnp.ones((tm, R), F32), jnp.zeros((tm, R), F32)], axis=1)],
        axis=0)
    per = SEQ // tm
    n_kv = H * (QK + V)
    kv = pl.pallas_call(
        _mla_kv_kernel,
        grid=(n_kv // tn, n_all // tm),
        in_specs=[pl.BlockSpec((tm, MLA_KV_RANK), lambda jn, i: (i, MLA_Q_RANK // MLA_KV_RANK)),
                  pl.BlockSpec((tm, 2 * R), lambda jn, i: (i, (MLA_IN - R) // (2 * R))),
                  pl.BlockSpec((1, MLA_KV_RANK), lambda jn, i: (0, 0)),
                  pl.BlockSpec((tm, 2 * R),
                               lambda jn, i: (jnp.where(i * tm < N_LAT, i % per, per), 0)),
                  pl.BlockSpec((MLA_KV_RANK + 2 * R, tn), lambda jn, i: (0, jn))],
        out_specs=pl.BlockSpec((tm, tn), lambda jn, i: (i, jn)),
        out_shape=jax.ShapeDtypeStruct((n_all, n_kv), BF16),
        scratch_shapes=[pltpu.VMEM((MLA_KV_RANK + 2 * R, tn), BF16)],
        compiler_params=_params(2, 4 * _nbytes((MLA_KV_RANK + 2 * R, tn), F32)
                                + 8 * _nbytes((tm, tn), F32)),
        name="mla_kv",
    )(proj, proj, kv_norm.reshape(1, MLA_KV_RANK), cs, w_kv)
    tq = 2048
    kchunk = (SEQ + CTX_LEN) // 3
    per = SEQ // tq
    v0 = H * QK // V
    ctx0 = N_LAT // CTX_LEN
    est = (2 * _nbytes((tq, QK), BF16) + 2 * _nbytes((SEQ + CTX_LEN, QK + V), BF16)
           + 8 * _nbytes((tq, kchunk), F32))
    return pl.pallas_call(
        functools.partial(_attn_kernel, kchunk=kchunk),
        grid=(BATCH, H, per),
        in_specs=[pl.BlockSpec((tq, QK), lambda b, h, i: (b * per + i, h)),
                  pl.BlockSpec((SEQ, QK), lambda b, h, i: (b, h)),
                  pl.BlockSpec((CTX_LEN, QK), lambda b, h, i: (ctx0 + b, h)),
                  pl.BlockSpec((SEQ, V), lambda b, h, i: (b, v0 + h)),
                  pl.BlockSpec((CTX_LEN, V), lambda b, h, i: (ctx0 + b, v0 + h))],
        out_specs=pl.BlockSpec((tq, V), lambda b, h, i: (b * per + i, h)),
        out_shape=jax.ShapeDtypeStruct((N_LAT, H * V), BF16),
        compiler_params=_params(3, est),
        name="mla_attention",
    )(q, kv, kv, kv, kv)


def dense_ffn(a, f, wg, wu, wd):
    p = swiglu_up([a], wg, wu, e=f, tm=1024, tf=512, name="ffn_up")
    return linear(p, wd, e=f, tm=512, tn=1024, out_dtype=BF16, name="ffn_down")


def _row_cumsum(onehot):
    n, e = onehot.shape
    blk = 256
    x = onehot.reshape(n // blk, blk, e).astype(BF16)
    tri = jnp.tril(jnp.ones((blk, blk), BF16))
    inner = jnp.einsum("ij,bje->bie", tri, x, preferred_element_type=F32)
    totals = inner[:, -1, :]
    offs = jnp.cumsum(totals, axis=0) - totals
    return (inner + offs[:, None, :]).reshape(n, e).astype(jnp.int32)


def moe_ffn(logits, a_rows, n_rows, f, wg, wu, wd, *, tm=512):
    top_v, top_i = lax.top_k(logits[:, :N_EXPERTS], TOP_K)
    gates = jax.nn.softmax(top_v, axis=-1)
    n_assign = n_rows * TOP_K
    n_tiles = n_assign // tm + N_EXPERTS
    flat_e = top_i.reshape(n_assign)
    onehot = (flat_e[:, None] == jnp.arange(N_EXPERTS)[None, :]).astype(jnp.int32)
    csum = _row_cumsum(onehot)
    rank = jnp.sum(csum * onehot, axis=1) - 1
    counts = csum[-1]
    padded = ((counts + tm - 1) // tm) * tm
    ends = jnp.cumsum(padded)
    starts = ends - padded
    pos = starts[flat_e] + rank
    src = (jnp.arange(n_tiles * tm, dtype=jnp.int32) % n_rows).at[pos].set(
        jnp.arange(n_assign, dtype=jnp.int32) // TOP_K)
    tile_row0 = jnp.arange(n_tiles, dtype=jnp.int32) * tm
    te = jnp.minimum(jnp.sum((tile_row0[:, None] >= ends[None, :]).astype(jnp.int32), axis=1),
                     N_EXPERTS - 1)
    used = jnp.clip(counts[te] - (tile_row0 - starts[te]), 0, tm)
    vr = jnp.where(tile_row0 < ends[-1], -(-used // ROW_STEP) * ROW_STEP, 0).astype(jnp.int32)
    n_used = ends[-1] // tm
    te = jnp.where(tile_row0 < ends[-1], te, te[jnp.maximum(n_used - 1, 0)]) + f * N_EXPERTS
    x_sorted = a_rows.at[src].get(mode="promise_in_bounds")
    stack = lambda w: w.reshape((-1,) + w.shape[2:])
    p = swiglu_up([x_sorted], stack(wg), stack(wu), tm=tm, tf=1024, te=te, vr=vr, name="moe_up")
    out = linear(p, stack(wd), tm=tm, tn=1024, te=te, vr=vr, name="moe_down")
    pos = pos.reshape(n_rows, TOP_K)
    ys = [out.at[pos[:, k]].get(mode="promise_in_bounds") for k in range(TOP_K)]
    return ys, [gates[:, k:k + 1] for k in range(TOP_K)]


def kernel(x, c, ctx, c_ctx, w_mod, b_mod, ln_g, ln_b, pool_w, pool_scale, hy_w_in, hy_b_in, hy_conv_w, hy_conv_b, hy_f_w1, hy_f_b1, hy_f_w2, hy_f_b2, hy_f_w3, hy_f_b3, hy_f_freq, hy_f_wout, hy_skip, hy_w_out, hy_b_out, mla_w_in, mla_q_norm, mla_kv_norm, mla_w_uq, mla_w_ukv, mla_w_o, ffn_w_gate, ffn_w_up, ffn_w_down, moe_w_router, moe_w_gate, moe_w_up, moe_w_down):
    attn_layers = [i for i in range(DEPTH) if i % N_MIXERS == 2]
    last_read = attn_layers[-1] if attn_layers else -1

    c8 = jnp.concatenate([c, c_ctx[None], jnp.zeros((MOD_ROWS - BATCH - 1, D_MODEL), F32)], axis=0)
    mods = modulation_table(c8, w_mod, b_mod)

    assert 0 < last_read and N_MIXERS > 1
    h = [x.reshape(N_LAT, D_MODEL), ctx.reshape(N_CTX, D_MODEL)]
    a = None
    for i in range(DEPTH):
        kind, j = i % N_MIXERS, i // N_MIXERS
        ctx_out = i < last_read
        n_rows = N_LAT + N_CTX if ctx_out else N_LAT
        m = mods[i]
        f = i // 2
        w_router = moe_w_router[f] if i % 2 == 1 else None
        if kind == 0:
            tl = CTX_LEN if n_rows > N_LAT else 512
            h1, a2, *logits = pool_layer(h if isinstance(h, list) else [h], n_rows, m,
                                         pool_w[j].astype(BF16), pool_scale[j],
                                         ln_g[i, 0], ln_b[i, 0], tl=tl, w_router=w_router)
        else:
            if kind == 1:
                filt = (hy_f_w1[j], hy_f_b1[j], hy_f_w2[j], hy_f_b2[j], hy_f_w3[j], hy_f_b3[j],
                        hy_f_freq[j], hy_f_wout[j])
                z = hyena_mix(a[:n_rows], n_rows, j, hy_w_in, hy_b_in, hy_conv_w[j], hy_conv_b[j],
                              filt, hy_skip[j])
                w_o, b_o = hy_w_out, hy_b_out
            else:
                assert i <= last_read and not ctx_out and a.shape[0] == N_LAT + N_CTX
                z = [mla_mix(a, mla_w_in[j], mla_q_norm[j], mla_kv_norm[j], mla_w_uq[j],
                             mla_w_ukv[j])]
                w_o, b_o = mla_w_o, None
            h1, a2, *logits = linear_resid_ln(z, w_o, j, b_o, h, n_rows, m, 2, ln_g[i, 0],
                                              ln_b[i, 0], 4, 3, w_router=w_router)
        nm = mods[min(i + 1, DEPTH - 1)]
        if w_router is not None:
            ys, row_gates = moe_ffn(logits[0], a2, n_rows, f, moe_w_gate, moe_w_up, moe_w_down)
        else:
            ys, row_gates = [dense_ffn(a2, f, ffn_w_gate, ffn_w_up, ffn_w_down)], None
        h, a = resid_ln(h1, ys, n_rows, m, 5, ln_g[i, 1], ln_b[i, 1], nm, 1, 0,
                        row_gates=row_gates)
    return h[:N_LAT].reshape(BATCH, SEQ, D_MODEL)
```

```python
import functools
import math

import jax
import jax.numpy as jnp
from jax import lax
from jax.experimental import pallas as pl
from jax.experimental.pallas import tpu as pltpu

F32 = jnp.float32
BF16 = jnp.bfloat16

D_MODEL = 2048
BATCH = 4
SEQ = 2048
DEPTH = 4
GRID_W = 64
CTX_LEN = 256
N_MIXERS = 3
POOL_WINDOWS = (2, 4, 8, 16)
POOL_GROUP = D_MODEL // len(POOL_WINDOWS)
HY_ORDER = 2
HY_SHORT = 3
HY_EMB = 33
HY_FILTER = 64
HY_DECAY_TARGET = 1e-2
HY_FAST = 0.3
HY_SLOW = 1.5
MLA_HEADS = D_MODEL // 128
MLA_Q_RANK = 512
MLA_KV_RANK = 512
MLA_NOPE = 128
MLA_ROPE = 64
MLA_V = 128
ROPE_BASE = 10000.0
D_FF = 5632
N_EXPERTS = 8
TOP_K = 2
EXPERT_FF = 2 * D_MODEL
LN_EPS = 1e-5
RMS_EPS = 1e-6
DN_ALPHA = (2 * DEPTH) ** 0.25

N_LAT = BATCH * SEQ
N_CTX = BATCH * CTX_LEN
MOD_ROWS = 8
CTX_MOD_ROW = BATCH
LANE = 128
SUBLANE = 8
VMEM_CAP = 60 * 1024 * 1024
MLA_QK_PAD = 256
SOFTMAX_SCALE_LOG2E = (MLA_NOPE + MLA_ROPE) ** -0.5 * math.log2(math.e)
MLA_IN = MLA_Q_RANK + MLA_KV_RANK + MLA_ROPE
MLA_IN_PAD = MLA_IN + MLA_ROPE


def _vmem(nbytes):
    return int(min(VMEM_CAP, max(16 * 1024 * 1024, nbytes * 3 // 2)))


def _nbytes(shape, dtype):
    return math.prod(shape) * jnp.dtype(dtype).itemsize


def _mod_row(i, tm):
    return jnp.minimum((i * tm) // SEQ, CTX_MOD_ROW)


def _mod_spec(which, tm):
    return pl.BlockSpec((None, 1, D_MODEL), lambda i: (_mod_row(i, tm) * 6 + which, 0, 0))


def _params(sem, est):
    return pltpu.CompilerParams(dimension_semantics=("arbitrary",) * sem,
                                vmem_limit_bytes=_vmem(est))


def _mod_kernel(c_ref, w_ref, b_ref, o_ref):
    c = c_ref[...]
    a = c / (1.0 + jnp.exp(-c))
    o_ref[...] = jnp.dot(a.astype(BF16), w_ref[...].astype(BF16),
                         preferred_element_type=F32) + b_ref[...]


def modulation_table(c8, w_mod, b_mod):
    tn = 1024
    n = 6 * D_MODEL
    out = pl.pallas_call(
        _mod_kernel,
        grid=(DEPTH, n // tn),
        in_specs=[pl.BlockSpec((MOD_ROWS, D_MODEL), lambda l, j: (0, 0)),
                  pl.BlockSpec((None, D_MODEL, tn), lambda l, j: (l, 0, j)),
                  pl.BlockSpec((None, 1, tn), lambda l, j: (l, 0, j))],
        out_specs=pl.BlockSpec((None, MOD_ROWS, tn), lambda l, j: (l, 0, j)),
        out_shape=jax.ShapeDtypeStruct((DEPTH, MOD_ROWS, n), F32),
        compiler_params=_params(2, 3 * _nbytes((D_MODEL, tn), F32)),
        name="modulation_table",
    )(c8, w_mod, b_mod.reshape(DEPTH, 1, n))
    return out.reshape(DEPTH, MOD_ROWS * 6, 1, D_MODEL)


def _group_schedule(te):
    n = te.shape[0]
    first = jnp.concatenate([jnp.ones((1,), jnp.int32), (te[1:] != te[:-1]).astype(jnp.int32)])
    gid = jnp.cumsum(first) - 1
    ge = jnp.zeros((n,), jnp.int32).at[gid].max(te)
    return first, gid, ge, gid[-1:] + 1


def _stream_weights(sched, w_hbms, stages, sem, dsts, col0, tn):
    first_ref, gid_ref, ge_ref, ng_ref = sched
    j, i = pl.program_id(0), pl.program_id(1)
    g, ng = gid_ref[i], ng_ref[0]

    def copies(e, col):
        cols = pl.ds(pl.multiple_of((col + col0) * tn, tn), tn)
        return [pltpu.make_async_copy(w.at[e, :, cols], st, sem.at[k])
                for k, (w, st) in enumerate(zip(w_hbms, stages))]

    @pl.when(first_ref[i] == 1)
    def _():
        @pl.when(jnp.logical_and(j == 0, g == 0))
        def _():
            for c in copies(ge_ref[0], 0):
                c.start()

        for c in copies(ge_ref[g], j):
            c.wait()
        for st, dst in zip(stages, dsts):
            dst[...] = st[...].astype(BF16)
        last = g + 1 >= ng
        next_e = ge_ref[jnp.where(last, 0, g + 1)]
        next_col = jnp.where(last, j + 1, j)

        @pl.when(next_col < pl.num_programs(0))
        def _():
            for c in copies(next_e, next_col):
                c.start()


def _for_valid_rows(vr, row_step, o_ref, compute):
    tm = o_ref.shape[0]
    for r in range(row_step, tm + 1, row_step):
        @pl.when(vr == r)
        def _(r=r):
            o_ref[0:r, :] = compute(r).astype(o_ref.dtype)
            if r < tm:
                o_ref[r:tm, :] = jnp.zeros((tm - r, o_ref.shape[1]), o_ref.dtype)

    @pl.when(vr == 0)
    def _():
        o_ref[...] = jnp.zeros_like(o_ref)


def _linear_kernel(xt_ref, vr_ref, first_ref, gid_ref, ge_ref, ng_ref, x_ref, w_hbm, *rest,
                   has_bias, col0, tn, row_step):
    rest = list(rest)
    b_ref = rest.pop(0) if has_bias else None
    o_ref, stage_ref, wbf_ref, sem = rest
    i = pl.program_id(1)
    _stream_weights((first_ref, gid_ref, ge_ref, ng_ref), [w_hbm], [stage_ref], sem, [wbf_ref],
                    col0, tn)

    def compute(r):
        acc = jnp.dot(x_ref[0:r, :].astype(BF16), wbf_ref[...], preferred_element_type=F32)
        return acc + b_ref[...] if has_bias else acc

    _for_valid_rows(vr_ref[i], row_step, o_ref, compute)


ROW_STEP = 64


def linear(x, w, *, tm, tn, e=0, n_out=None, n_tiles=None, xt=None, te=None, vr=None, bias=None,
           out_dtype=F32, w_col0=0, name="linear"):
    k = x.shape[1]
    if w.ndim == 2:
        w = w[None]
    if n_out is None:
        n_out = w.shape[2] - w_col0
    assert x.shape[0] % tm == 0 and n_out % tn == 0 and w_col0 % tn == 0 and w.shape[1] == k
    assert w.dtype == F32
    if n_tiles is None:
        n_tiles = x.shape[0] // tm
    if xt is None:
        xt = jnp.arange(n_tiles, dtype=jnp.int32)
    if te is None:
        te = jnp.full((n_tiles,), e, jnp.int32)
    row_step = tm if vr is None else ROW_STEP
    if vr is None:
        vr = jnp.full((n_tiles,), tm, jnp.int32)
    first, gid, ge, ng = _group_schedule(te)
    has_bias = bias is not None
    in_specs = [pl.BlockSpec((tm, k), lambda j, i, xt, *_: (xt[i], 0)),
                pl.BlockSpec(memory_space=pl.ANY)]
    args = [x, w]
    if has_bias:
        bias = bias.reshape(-1, 1, bias.shape[-1])
        in_specs.append(pl.BlockSpec((None, 1, tn),
                                     lambda j, i, xt, vr, first, gid, ge, ng: (ge[gid[i]], 0, j)))
        args.append(bias)
    est = (2 * _nbytes((tm, k), x.dtype) + _nbytes((k, tn), F32) + _nbytes((k, tn), BF16)
           + _nbytes((tm, k), BF16) + 2 * _nbytes((tm, tn), out_dtype) + 2 * _nbytes((tm, tn), F32))
    return pl.pallas_call(
        functools.partial(_linear_kernel, has_bias=has_bias, col0=w_col0 // tn, tn=tn,
                          row_step=row_step),
        grid_spec=pltpu.PrefetchScalarGridSpec(
            num_scalar_prefetch=6,
            grid=(n_out // tn, n_tiles),
            in_specs=in_specs,
            out_specs=pl.BlockSpec((tm, tn), lambda j, i, *_: (i, j)),
            scratch_shapes=[pltpu.VMEM((k, tn), F32), pltpu.VMEM((k, tn), BF16),
                            pltpu.SemaphoreType.DMA((1,))]),
        out_shape=jax.ShapeDtypeStruct((n_tiles * tm, n_out), out_dtype),
        compiler_params=_params(2, est),
        name=name,
    )(xt, vr, first, gid, ge, ng, *args)


def _swiglu_kernel(vr_ref, first_ref, gid_ref, ge_ref, ng_ref, *refs, n_parts, tf, row_step):
    x_refs = refs[:n_parts]
    wg_hbm, wu_hbm, o_ref, sg_ref, su_ref, wgb_ref, wub_ref, sem = refs[n_parts:]
    i = pl.program_id(1)
    _stream_weights((first_ref, gid_ref, ge_ref, ng_ref), [wg_hbm, wu_hbm], [sg_ref, su_ref], sem,
                    [wgb_ref, wub_ref], 0, tf)

    def compute(r):
        g = u = None
        k0 = 0
        for x_ref in x_refs:
            x = x_ref[0:r, :].astype(BF16)
            rows = slice(k0, k0 + x.shape[1])
            gp = jnp.dot(x, wgb_ref[rows, :], preferred_element_type=F32)
            up = jnp.dot(x, wub_ref[rows, :], preferred_element_type=F32)
            g, u = (gp, up) if g is None else (g + gp, u + up)
            k0 += x.shape[1]
        return (g / (1.0 + jnp.exp(-g))) * u

    _for_valid_rows(vr_ref[i], row_step, o_ref, compute)


def swiglu_up(x_parts, wg, wu, *, tm, tf, e=0, te=None, vr=None, name="swiglu_up"):
    m = x_parts[0].shape[0]
    k, f = wg.shape[1], wg.shape[2]
    assert sum(x.shape[1] for x in x_parts) == k and m % tm == 0 and f % tf == 0
    n_tiles = m // tm
    if te is None:
        te = jnp.full((n_tiles,), e, jnp.int32)
    row_step = tm if vr is None else ROW_STEP
    if vr is None:
        vr = jnp.full((n_tiles,), tm, jnp.int32)
    first, gid, ge, ng = _group_schedule(te)
    wspec = pl.BlockSpec(memory_space=pl.ANY)
    est = (2 * _nbytes((tm, k), x_parts[0].dtype) + _nbytes((tm, k), BF16)
           + 2 * _nbytes((k, tf), F32) + 2 * _nbytes((k, tf), BF16) + 2 * _nbytes((tm, tf), BF16)
           + 3 * _nbytes((tm, tf), F32))
    x_specs = [pl.BlockSpec((tm, x.shape[1]), lambda j, i, *_: (i, 0)) for x in x_parts]
    return pl.pallas_call(
        functools.partial(_swiglu_kernel, n_parts=len(x_parts), tf=tf, row_step=row_step),
        grid_spec=pltpu.PrefetchScalarGridSpec(
            num_scalar_prefetch=5,
            grid=(f // tf, n_tiles),
            in_specs=[*x_specs, wspec, wspec],
            out_specs=pl.BlockSpec((tm, tf), lambda j, i, *_: (i, j)),
            scratch_shapes=[pltpu.VMEM((k, tf), F32), pltpu.VMEM((k, tf), F32),
                            pltpu.VMEM((k, tf), BF16), pltpu.VMEM((k, tf), BF16),
                            pltpu.SemaphoreType.DMA((2,))]),
        out_shape=jax.ShapeDtypeStruct((m, f), BF16),
        compiler_params=_params(2, est),
        name=name,
    )(vr, first, gid, ge, ng, *x_parts, wg, wu)


def _ln_mod(v, g, b, sc, sh, pack):
    mu = jnp.mean(v, axis=-1, keepdims=True)
    d = v - mu
    var = jnp.mean(d * d, axis=-1, keepdims=True)
    h = d * lax.rsqrt(var + LN_EPS) * g + b
    a = h * (1.0 + sc) + sh
    return h, (a if pack else a.astype(BF16))


def _resid_ln_kernel(*refs, n_y, pack):
    h_ref = refs[0]
    y_refs = refs[1:1 + n_y]
    rest = refs[1 + n_y:]
    if n_y == 2:
        g0_ref, g1_ref = rest[0], rest[1]
        rest = rest[2:]
        y = g0_ref[...] * y_refs[0][...] + g1_ref[...] * y_refs[1][...]
    else:
        y = y_refs[0][...]
    gate_ref, g_ref, b_ref, sc_ref, sh_ref, o_ref, a_ref = rest
    v = DN_ALPHA * h_ref[...] + gate_ref[...] * y
    h, a = _ln_mod(v, g_ref[...], b_ref[...], sc_ref[...], sh_ref[...], pack)
    o_ref[...] = h
    a_ref[...] = a


def resid_ln(h, ys, n_rows, mods, gate_idx, ln_g, ln_b, nmods, sc_idx, sh_idx, *, row_gates=None,
             pack=False, tm=256):
    assert n_rows % tm == 0
    row = pl.BlockSpec((tm, D_MODEL), lambda i: (i, 0))
    col = pl.BlockSpec((tm, 1), lambda i: (i, 0))
    vec = pl.BlockSpec((1, D_MODEL), lambda i: (0, 0))
    a_w = D_MODEL
    a_dt = F32 if pack else BF16
    est = 10 * _nbytes((tm, D_MODEL), F32)
    in_specs = [row] + [row] * len(ys) + ([col, col] if row_gates else []) + [
        _mod_spec(gate_idx, tm), vec, vec, _mod_spec(sc_idx, tm), _mod_spec(sh_idx, tm)]
    args = [h, *ys, *(row_gates or []), mods, ln_g.reshape(1, D_MODEL), ln_b.reshape(1, D_MODEL),
            nmods, nmods]
    return pl.pallas_call(
        functools.partial(_resid_ln_kernel, n_y=len(ys), pack=pack),
        grid=(n_rows // tm,),
        in_specs=in_specs,
        out_specs=[row, pl.BlockSpec((tm, a_w), lambda i: (i, 0))],
        out_shape=[jax.ShapeDtypeStruct((n_rows, D_MODEL), F32),
                   jax.ShapeDtypeStruct((n_rows, a_w), a_dt)],
        compiler_params=_params(1, est),
        name="resid_ln",
    )(*args)


def _split_bf16(x):
    hi = x.astype(BF16)
    lo = (x - hi.astype(F32)).astype(BF16)
    return hi, lo


def _router_logits(a, whi_ref, wlo_ref):
    a_hi, a_lo = _split_bf16(a)
    w_hi = whi_ref[...]
    return (jnp.dot(a_hi, w_hi, preferred_element_type=F32)
            + jnp.dot(a_lo, w_hi, preferred_element_type=F32)
            + jnp.dot(a_hi, wlo_ref[...], preferred_element_type=F32))


def _router_operands(w_router):
    w = jnp.pad(w_router, ((0, 0), (0, LANE - N_EXPERTS)))
    return _split_bf16(w), pl.BlockSpec((D_MODEL, LANE), lambda i: (0, 0))


def _linear_ln_kernel(*refs, has_bias, moe, split):
    n_src = 1 if split is None else 2
    x_refs, w_ref, rest = refs[:n_src], refs[n_src], list(refs[n_src + 1:])
    bias_ref = rest.pop(0) if has_bias else None
    h_ref, gate_ref, g_ref, b_ref, sc_ref, sh_ref = rest[:6]
    rest = rest[6:]
    if moe:
        whi_ref, wlo_ref, o_ref, a_ref, l_ref, wb_ref = rest
    else:
        o_ref, a_ref, wb_ref = rest

    @pl.when(pl.program_id(0) == 0)
    def _():
        wb_ref[...] = w_ref[...].astype(BF16)

    if split is None:
        x = x_refs[0][...]
    else:
        x = jnp.where(pl.program_id(0) < split, x_refs[0][...], x_refs[1][...])
    y = jnp.dot(x, wb_ref[...], preferred_element_type=F32)
    if has_bias:
        y = y + bias_ref[...]
    v = DN_ALPHA * h_ref[...] + gate_ref[...] * y
    h, a = _ln_mod(v, g_ref[...], b_ref[...], sc_ref[...], sh_ref[...], moe)
    o_ref[...] = h
    a_ref[...] = a
    if moe:
        l_ref[...] = _router_logits(a, whi_ref, wlo_ref)


def linear_resid_ln(xs, w, e, bias, h, n_rows, mods, gate_idx, ln_g, ln_b, sc_idx, sh_idx, *,
                    w_router=None, tm=256):
    k = xs[0].shape[1]
    moe = w_router is not None
    row = pl.BlockSpec((tm, D_MODEL), lambda i: (i, 0))
    vec = pl.BlockSpec((1, D_MODEL), lambda i: (0, 0))
    split = None if len(xs) == 1 else xs[0].shape[0] // tm
    in_specs, t0 = [], 0
    for x in xs:
        nt = x.shape[0] // tm
        in_specs.append(pl.BlockSpec((tm, k), lambda i, t0=t0, nt=nt: (jnp.clip(i - t0, 0, nt - 1), 0)))
        t0 += nt
    in_specs.append(pl.BlockSpec((None, k, D_MODEL), lambda i: (e, 0, 0),
                                 pipeline_mode=pl.Buffered(1)))
    args = [*xs, w]
    if bias is not None:
        in_specs.append(pl.BlockSpec((None, 1, D_MODEL), lambda i: (e, 0, 0)))
        args.append(bias.reshape(-1, 1, D_MODEL))
    in_specs += [row, _mod_spec(gate_idx, tm), vec, vec, _mod_spec(sc_idx, tm),
                 _mod_spec(sh_idx, tm)]
    args += [h, mods, ln_g.reshape(1, D_MODEL), ln_b.reshape(1, D_MODEL), mods, mods]
    out_specs = [row, row]
    out_shape = [jax.ShapeDtypeStruct((n_rows, D_MODEL), F32),
                 jax.ShapeDtypeStruct((n_rows, D_MODEL), F32 if moe else BF16)]
    if moe:
        (w_hi, w_lo), wspec = _router_operands(w_router)
        in_specs += [wspec, wspec]
        args += [w_hi, w_lo]
        out_specs.append(pl.BlockSpec((tm, LANE), lambda i: (i, 0)))
        out_shape.append(jax.ShapeDtypeStruct((n_rows, LANE), F32))
    est = (_nbytes((k, D_MODEL), F32) + _nbytes((k, D_MODEL), BF16) + 2 * _nbytes((tm, k), BF16)
           + 12 * _nbytes((tm, D_MODEL), F32))
    return pl.pallas_call(
        functools.partial(_linear_ln_kernel, has_bias=bias is not None, moe=moe, split=split),
        grid=(n_rows // tm,),
        in_specs=in_specs,
        out_specs=out_specs,
        out_shape=out_shape,
        scratch_shapes=[pltpu.VMEM((k, D_MODEL), BF16)],
        compiler_params=_params(1, est),
        name="linear_resid_ln",
    )(*args)


def _seq_pos(i, tl):
    row0 = i * tl
    is_lat = row0 < N_LAT
    pos = jnp.where(is_lat, row0 % SEQ, (row0 - N_LAT) % CTX_LEN)
    seqlen = jnp.where(is_lat, SEQ, CTX_LEN)
    return pos, seqlen


def _tile_halo_specs(tl, width, t0, nt):
    per = tl // SUBLANE
    local = lambda i: jnp.clip(i - t0, 0, nt - 1)
    main = pl.BlockSpec((tl, width), lambda i: (local(i), 0))
    prev = pl.BlockSpec((SUBLANE, width), lambda i: (jnp.maximum(local(i) * per - 1, 0), 0))
    nxt = pl.BlockSpec((SUBLANE, width),
                       lambda i: (jnp.minimum((local(i) + 1) * per, nt * per - 1), 0))
    return [main, prev, nxt]


def _shift_rows(x, s):
    n = x.shape[0]
    return pltpu.roll(x, s % n, 0)


def _pool_kernel(*refs, tl, moe, split):
    n_src = 1 if split is None else 2
    srcs, rest = refs[:3 * n_src], refs[3 * n_src:]
    (sc_ref, sh_ref, gate_ref, w_ref, ps_ref, g_ref, b_ref, nsc_ref, nsh_ref), rest = rest[:9], rest[9:]
    if moe:
        whi_ref, wlo_ref, o_ref, a_ref, l_ref = rest
    else:
        o_ref, a_ref = rest
    i = pl.program_id(0)
    pos, seqlen = _seq_pos(i, tl)
    if split is None:
        h, h_prev, h_next = (r[...] for r in srcs)
    else:
        h, h_prev, h_next = (jnp.where(i < split, r0[...], r1[...])
                             for r0, r1 in zip(srcs[:3], srcs[3:]))
    sc = 1.0 + sc_ref[...]
    sh = sh_ref[...]
    a = h * sc + sh
    keep_prev = (pos > 0).astype(F32)
    keep_next = (pos + tl < seqlen).astype(F32)
    ext = jnp.concatenate([(h_prev * sc + sh) * keep_prev, a,
                           (h_next * sc + sh) * keep_next], axis=0)
    t = pos + lax.broadcasted_iota(jnp.int32, (tl, 1), 0)
    n_ext = tl + 2 * SUBLANE
    parts = []
    for g, w in enumerate(POOL_WINDOWS):
        cols = slice(g * POOL_GROUP, (g + 1) * POOL_GROUP)
        s = ext[:, cols]
        s = s + _shift_rows(s, 1)
        r = 1
        while 2 * r < w:
            s = _shift_rows(s, n_ext - r) + _shift_rows(s, r)
            r *= 2
        s = s[SUBLANE:SUBLANE + tl]
        cnt = (jnp.minimum(t + w // 2, seqlen) - jnp.maximum(t - w // 2, 0)).astype(F32)
        d = s / cnt - a[:, cols]
        parts.append(jnp.dot(d.astype(BF16), w_ref[g], preferred_element_type=F32))
    y = jnp.concatenate(parts, axis=1) * ps_ref[...]
    v = DN_ALPHA * h + gate_ref[...] * y
    hn, an = _ln_mod(v, g_ref[...], b_ref[...], nsc_ref[...], nsh_ref[...], moe)
    o_ref[...] = hn
    a_ref[...] = an
    if moe:
        l_ref[...] = _router_logits(an, whi_ref, wlo_ref)


def pool_layer(hs, n_rows, mods, w_grp_bf16, pool_scale, ln_g, ln_b, *, tl, w_router=None):
    moe = w_router is not None
    row = pl.BlockSpec((tl, D_MODEL), lambda i: (i, 0))
    vec = pl.BlockSpec((1, D_MODEL), lambda i: (0, 0))
    split = None if len(hs) == 1 else hs[0].shape[0] // tl
    in_specs, args, t0 = [], [], 0
    for h in hs:
        nt = h.shape[0] // tl
        in_specs += _tile_halo_specs(tl, D_MODEL, t0, nt)
        args += [h, h, h]
        t0 += nt
    in_specs += [_mod_spec(1, tl), _mod_spec(0, tl), _mod_spec(2, tl),
                 pl.BlockSpec(w_grp_bf16.shape, lambda i: (0, 0, 0)), vec, vec, vec,
                 _mod_spec(4, tl), _mod_spec(3, tl)]
    args += [mods, mods, mods, w_grp_bf16, pool_scale.reshape(1, D_MODEL),
             ln_g.reshape(1, D_MODEL), ln_b.reshape(1, D_MODEL), mods, mods]
    out_specs = [row, row]
    out_shape = [jax.ShapeDtypeStruct((n_rows, D_MODEL), F32),
                 jax.ShapeDtypeStruct((n_rows, D_MODEL), F32 if moe else BF16)]
    if moe:
        (w_hi, w_lo), wspec = _router_operands(w_router)
        in_specs += [wspec, wspec]
        args += [w_hi, w_lo]
        out_specs.append(pl.BlockSpec((tl, LANE), lambda i: (i, 0)))
        out_shape.append(jax.ShapeDtypeStruct((n_rows, LANE), F32))
    est = 14 * _nbytes((tl, D_MODEL), F32) + 2 * _nbytes(w_grp_bf16.shape, BF16)
    return pl.pallas_call(
        functools.partial(_pool_kernel, tl=tl, moe=moe, split=split),
        grid=(n_rows // tl,),
        in_specs=in_specs,
        out_specs=out_specs,
        out_shape=out_shape,
        compiler_params=_params(1, est),
        name="pool_layer",
    )(*args)


def _conv3_seq(u, w, b):
    n = u.shape[0]
    t = lax.broadcasted_iota(jnp.int32, (n, 1), 0)
    before = jnp.where(t == 0, 0.0, _shift_rows(u, 1))
    after = jnp.where(t == n - 1, 0.0, _shift_rows(u, n - 1))
    return b + before * w[0:1] + u * w[1:2] + after * w[2:3]


def _conv3_tile(u, prev, nxt, w, b, first, last):
    n = u.shape[0]
    t = lax.broadcasted_iota(jnp.int32, (n, 1), 0)
    row_before = prev[SUBLANE - 1:SUBLANE, :] * (1.0 - first.astype(F32))
    row_after = nxt[0:1, :] * (1.0 - last.astype(F32))
    before = jnp.where(t == 0, row_before, _shift_rows(u, 1))
    after = jnp.where(t == n - 1, row_after, _shift_rows(u, n - 1))
    return b + before * w[0:1] + u * w[1:2] + after * w[2:3]


def _dot3(a, b):
    a_hi, a_lo = _split_bf16(a)
    b_hi, b_lo = _split_bf16(b)
    return (jnp.dot(a_hi, b_hi, preferred_element_type=F32)
            + jnp.dot(a_lo, b_hi, preferred_element_type=F32)
            + jnp.dot(a_hi, b_lo, preferred_element_type=F32))


def _filter_kernel(z_ref, w1_ref, b1_ref, w2_ref, b2_ref, w3_ref, b3_ref, fr_ref, wo_ref,
                   dist_ref, delta_ref, o_ref, a_ref):
    @pl.when(pl.program_id(0) == 0)
    def _():
        fr = fr_ref[...]
        a = jnp.sin(fr * (_dot3(z_ref[...], w1_ref[...]) + b1_ref[...]))
        a = jnp.sin(fr * (_dot3(a, w2_ref[...]) + b2_ref[...]))
        a_ref[...] = jnp.sin(fr * (_dot3(a, w3_ref[...]) + b3_ref[...]))

    k = _dot3(a_ref[...], wo_ref[...])
    k = k * jnp.exp(-dist_ref[...] * delta_ref[...])
    o_ref[...] = k * lax.rsqrt(jnp.sum(k * k, axis=0, keepdims=True) + 1e-6)


def _pad2(x, rows, cols):
    return jnp.pad(x, ((0, rows - x.shape[0]), (0, cols - x.shape[1])))


def hyena_filters(L, f_w1, f_b1, f_w2, f_b2, f_w3, f_b3, f_freq, f_wout, *, tn=512):
    bands = (HY_EMB - 1) // 2
    t = jnp.linspace(0.0, 1.0, L, dtype=F32)[:, None]
    wpos = 2.0 * math.pi * jnp.arange(L, dtype=F32)[:, None] / L
    f = jnp.linspace(1e-4, bands - 1, bands, dtype=F32)[None, :]
    z = jnp.concatenate([t, jnp.cos(f * wpos), -jnp.sin(f * wpos)], axis=-1)
    dist = (jnp.abs(jnp.arange(L) - L // 2).astype(F32) / max(L // 2, 1))[:, None]
    deltas = jnp.linspace(math.log(HY_DECAY_TARGET) / HY_SLOW, math.log(HY_DECAY_TARGET) / HY_FAST,
                          D_MODEL, dtype=F32)
    absdelta = jnp.tile(jnp.abs(deltas), HY_ORDER)[None, :]
    P = LANE
    n = HY_ORDER * D_MODEL
    full = lambda shape: pl.BlockSpec(shape, lambda j: (0, 0))
    args = [_pad2(z, L, P), _pad2(f_w1, P, P), _pad2(f_b1[None], 1, P), _pad2(f_w2, P, P),
            _pad2(f_b2[None], 1, P), _pad2(f_w3, P, P), _pad2(f_b3[None], 1, P),
            _pad2(f_freq[None], 1, P), _pad2(f_wout, P, n), dist, absdelta]
    in_specs = [full((L, P)), full((P, P)), full((1, P)), full((P, P)), full((1, P)), full((P, P)),
                full((1, P)), full((1, P)), pl.BlockSpec((P, tn), lambda j: (0, j)),
                full((L, 1)), pl.BlockSpec((1, tn), lambda j: (0, j))]
    return pl.pallas_call(
        _filter_kernel,
        grid=(n // tn,),
        in_specs=in_specs,
        out_specs=pl.BlockSpec((L, tn), lambda j: (0, j)),
        out_shape=jax.ShapeDtypeStruct((L, n), F32),
        scratch_shapes=[pltpu.VMEM((L, P), F32)],
        compiler_params=_params(1, 8 * _nbytes((L, tn), F32)),
        name="hyena_filters",
    )(*args)


def dft_tables(L, tm):
    n_fft = 2 * L
    half = tm // 2
    r_idx = jnp.arange(2 * L, dtype=jnp.int32)
    odd = 2 * ((r_idx // tm) * half + (r_idx % half)) + 1
    is_sin = ((r_idx % tm) >= half)[:, None]

    def trig(k):
        ang = (k % (2 * n_fft)).astype(F32) * (math.pi / n_fft)
        return jnp.cos(ang), jnp.sin(ang)

    lo = 64
    s1 = jnp.arange(L // lo, dtype=jnp.int32) * lo
    s0 = jnp.arange(lo, dtype=jnp.int32)
    cb, sb = trig(odd[:, None] * s0[None, :])

    def combine(ca, sa):
        a1 = jnp.where(is_sin, sa, ca)
        a2 = jnp.where(is_sin, ca, -sa)
        return a1, a2

    a1, a2 = combine(*trig(odd[:, None] * s1[None, :]))
    coarse = lambda t: jnp.repeat(t, lo, axis=1)
    fine = lambda t: jnp.tile(t, (1, L // lo))
    fwd = coarse(a1) * fine(cb) + coarse(a2) * fine(sb)
    a1, a2 = combine(*trig(odd[:, None] * (s1[None, :] + L // 2)))
    inv = (a1.T[:, None, :] * cb.T[None, :, :] + a2.T[:, None, :] * sb.T[None, :, :]).reshape(
        L, 2 * L)
    return fwd.astype(BF16), inv.astype(BF16)


def _dft_mul_kernel(x_ref, w_ref, *rest, inv_len, conv):
    if conv:
        cw_ref, cb_ref, k_ref, o_ref, wb_ref = rest
    else:
        k_ref, o_ref, wb_ref = rest

    @pl.when(pl.program_id(1) == 0)
    def _():
        w = w_ref[...]
        if conv:
            w = _conv3_seq(w, cw_ref[...], cb_ref[...])
        wb_ref[...] = w.astype(BF16)

    s = jnp.dot(x_ref[...], wb_ref[...], preferred_element_type=F32)
    half = s.shape[0] // 2
    a, b = s[:half], s[half:]
    ka, kb = k_ref[0:half, :], k_ref[half:2 * half, :]
    o_ref[0:half, :] = ((a * ka - b * kb) * inv_len).astype(BF16)
    o_ref[half:2 * half, :] = ((a * kb + b * ka) * inv_len).astype(BF16)


def dft_mul(fwd, v_view, kspec, k_col0, *, L, nb, tm, tn, conv=None):
    d = D_MODEL
    per_b = d // tn
    v, v_row0, v_col0 = v_view
    vr, vc, kc = v_row0 // L, v_col0 // tn, k_col0 // tn
    est = (2 * _nbytes((tm, L), BF16) + 2 * _nbytes((L, tn), F32) + _nbytes((L, tn), BF16)
           + 2 * _nbytes((tm, tn), F32) + 2 * _nbytes((tm, tn), BF16) + 3 * _nbytes((tm, tn), F32))
    in_specs = [pl.BlockSpec((tm, L), lambda j, i: (i, 0)),
                pl.BlockSpec((L, tn), lambda j, i: (j // per_b + vr, j % per_b + vc))]
    args = [fwd, v]
    if conv is not None:
        in_specs += [pl.BlockSpec((HY_SHORT, tn), lambda j, i: (0, j % per_b + vc)),
                     pl.BlockSpec((1, tn), lambda j, i: (0, j % per_b + vc))]
        args += list(conv)
    in_specs.append(pl.BlockSpec((tm, tn), lambda j, i: (i, j % per_b + kc)))
    args.append(kspec)
    return pl.pallas_call(
        functools.partial(_dft_mul_kernel, inv_len=1.0 / L, conv=conv is not None),
        grid=(nb * per_b, (2 * L) // tm),
        in_specs=in_specs,
        out_specs=pl.BlockSpec((tm, tn), lambda j, i: (i, j)),
        out_shape=jax.ShapeDtypeStruct((2 * L, nb * d), BF16),
        scratch_shapes=[pltpu.VMEM((L, tn), BF16)],
        compiler_params=_params(2, est),
        name="dft_mul",
    )(*args)


def _idft_kernel(x_ref, y_ref, *rest, z_conv, per):
    rest = list(rest)
    g_ref, gp_ref, gn_ref, gw_ref, gb_ref = rest[:5]
    rest = rest[5:]
    i = pl.program_id(1)
    first, last = i == 0, i == per - 1
    conv = jnp.dot(x_ref[...], y_ref[...], preferred_element_type=F32)
    gate = _conv3_tile(g_ref[...], gp_ref[...], gn_ref[...], gw_ref[...], gb_ref[...], first, last)
    if z_conv:
        z_ref, zp_ref, zn_ref, zw_ref, zb_ref, skip_ref, o_ref = rest
        z = _conv3_tile(z_ref[...], zp_ref[...], zn_ref[...], zw_ref[...], zb_ref[...], first, last)
    else:
        z_ref, skip_ref, o_ref = rest
        z = z_ref[...]
    o_ref[...] = (gate * (conv + z * skip_ref[...])).astype(o_ref.dtype)


def idft_gate(inv, y, gate_view, z_view, skip, conv, *, z_conv, L, nb, tm, tn, out_dtype):
    d = D_MODEL
    per_b = d // tn
    per = L // tm
    sub = tm // SUBLANE

    def tile_specs(view, with_conv):
        arr, row0, col0 = view
        r0, c0 = row0 // tm, col0 // tn
        last_blk = arr.shape[0] // SUBLANE - 1
        row = lambda j, i: (j // per_b) * per + i + r0
        col = lambda j, i: j % per_b + c0
        specs = [pl.BlockSpec((tm, tn), lambda j, i: (row(j, i), col(j, i)))]
        args = [arr]
        if with_conv:
            specs += [
                pl.BlockSpec((SUBLANE, tn),
                             lambda j, i: (jnp.maximum(row(j, i) * sub - 1, 0), col(j, i))),
                pl.BlockSpec((SUBLANE, tn),
                             lambda j, i: (jnp.minimum((row(j, i) + 1) * sub, last_blk), col(j, i))),
                pl.BlockSpec((HY_SHORT, tn), lambda j, i: (0, col(j, i))),
                pl.BlockSpec((1, tn), lambda j, i: (0, col(j, i)))]
            args += [arr, arr, conv[0], conv[1]]
        return specs, args

    g_specs, g_args = tile_specs(gate_view, True)
    z_specs, z_args = tile_specs(z_view, z_conv)
    est = (2 * _nbytes((tm, 2 * L), BF16) + 2 * _nbytes((2 * L, tn), BF16)
           + 12 * _nbytes((tm, tn), F32))
    return pl.pallas_call(
        functools.partial(_idft_kernel, z_conv=z_conv, per=per),
        grid=(nb * per_b, per),
        in_specs=[pl.BlockSpec((tm, 2 * L), lambda j, i: (i, 0)),
                  pl.BlockSpec((2 * L, tn), lambda j, i: (0, j)),
                  *g_specs, *z_specs,
                  pl.BlockSpec((1, tn), lambda j, i: (0, j % per_b))],
        out_specs=pl.BlockSpec((tm, tn), lambda j, i: ((j // per_b) * per + i, j % per_b)),
        out_shape=jax.ShapeDtypeStruct((nb * L, d), out_dtype),
        compiler_params=_params(2, est),
        name="idft_gate",
    )(inv, y, *g_args, *z_args, skip.reshape(1, d))


def hyena_stream(u, row0, L, nb, conv, filt, skip):
    d = D_MODEL
    tms = min(1024, 2 * L)
    tmi = min(512, L)
    k = hyena_filters(L, *filt)
    fwd, inv = dft_tables(L, tms)
    kspec = linear(fwd, k, tm=tms, tn=1024, name="dft_filters")
    y1 = dft_mul(fwd, (u, row0, 0), kspec, 0, L=L, nb=nb, tm=tms, tn=1024, conv=conv)
    z1 = idft_gate(inv, y1, (u, row0, d), (u, row0, 0), skip[0], conv, z_conv=True, L=L, nb=nb,
                   tm=tmi, tn=1024, out_dtype=F32)
    y2 = dft_mul(fwd, (z1, 0, 0), kspec, d, L=L, nb=nb, tm=tms, tn=1024)
    return idft_gate(inv, y2, (u, row0, 2 * d), (z1, 0, 0), skip[1], conv, z_conv=False, L=L,
                     nb=nb, tm=tmi, tn=1024, out_dtype=BF16)


def hyena_mix(a, n_rows, j, w_in, b_in, conv_w, conv_b, filt, skip):
    u = linear(a, w_in, e=j, tm=1024, tn=2048, bias=b_in, name="hyena_in")
    conv = (conv_w, conv_b.reshape(1, -1))
    zs = [hyena_stream(u, 0, SEQ, BATCH, conv, filt, skip)]
    if n_rows > N_LAT:
        zs.append(hyena_stream(u, N_LAT, CTX_LEN, BATCH, conv, filt, skip))
    return zs


def _rms(x, g):
    return x * lax.rsqrt(jnp.mean(x * x, axis=-1, keepdims=True) + RMS_EPS) * g


def _mla_q_kernel(p_ref, g_ref, w1_ref, w2_ref, c_ref, s_ref, o_ref, w1b_ref, w2b_ref):
    @pl.when(pl.program_id(1) == 0)
    def _():
        w1b_ref[...] = w1_ref[...].astype(BF16)
        w2b_ref[...] = w2_ref[...].astype(BF16)

    xn = _rms(p_ref[...], g_ref[...]).astype(BF16)
    q1 = jnp.dot(xn, w1b_ref[...], preferred_element_type=F32)
    q2 = jnp.dot(xn, w2b_ref[...], preferred_element_type=F32)
    reps = q1.shape[1] // MLA_QK_PAD
    c = jnp.concatenate([c_ref[...]] * reps, axis=1)
    s = jnp.concatenate([s_ref[...]] * reps, axis=1)
    o_ref[...] = ((q1 * c + q2 * s) * SOFTMAX_SCALE_LOG2E).astype(BF16)


def _mla_kv_kernel(p_ref, r_ref, g_ref, cs_ref, w_ref, o_ref, wb_ref):
    @pl.when(pl.program_id(1) == 0)
    def _():
        wb_ref[...] = w_ref[...].astype(BF16)

    xn = _rms(p_ref[...], g_ref[...]).astype(BF16)
    rr = (r_ref[...] * cs_ref[...]).astype(BF16)
    acc = (jnp.dot(xn, wb_ref[0:MLA_KV_RANK, :], preferred_element_type=F32)
           + jnp.dot(rr, wb_ref[MLA_KV_RANK:MLA_KV_RANK + 2 * MLA_ROPE, :],
                     preferred_element_type=F32))
    o_ref[...] = acc.astype(BF16)


def _attn_kernel(q_ref, kl_ref, kc_ref, vl_ref, vc_ref, o_ref, *, kchunk):
    nt = (((1,), (1,)), ((), ()))
    q = q_ref[...]
    n_lat, n_ctx = kl_ref.shape[0], kc_ref.shape[0]
    assert (n_lat + n_ctx) % kchunk == 0 and n_lat % kchunk + n_ctx == kchunk
    m = l = acc = None
    for r in range((n_lat + n_ctx) // kchunk):
        lo, hi = r * kchunk, min((r + 1) * kchunk, n_lat)
        if (r + 1) * kchunk <= n_lat:
            k, v = kl_ref[lo:hi, :], vl_ref[lo:hi, :]
        else:
            k = jnp.concatenate([kl_ref[lo:hi, :], kc_ref[...]], axis=0)
            v = jnp.concatenate([vl_ref[lo:hi, :], vc_ref[...]], axis=0)
        s = lax.dot_general(q, k, nt, preferred_element_type=F32)
        ms = jnp.max(s, axis=-1, keepdims=True)
        m_new = ms if m is None else jnp.maximum(m, ms)
        p = jnp.exp2(s - m_new)
        ps = jnp.sum(p, axis=-1, keepdims=True)
        pv = jnp.dot(p.astype(BF16), v, preferred_element_type=F32)
        if m is None:
            l, acc = ps, pv
        else:
            alpha = jnp.exp2(m - m_new)
            l = alpha * l + ps
            acc = alpha * acc + pv
        m = m_new
    o_ref[...] = (acc / l).astype(o_ref.dtype)


def _rope_tables():
    rows = SEQ // GRID_W
    row = jnp.repeat(jnp.arange(rows), GRID_W).astype(F32)
    col = jnp.tile(jnp.arange(GRID_W), rows).astype(F32)
    n = MLA_ROPE // 4
    inv = ROPE_BASE ** (-jnp.arange(n, dtype=F32) / n)
    ang_row, ang_col = row[:, None] * inv, col[:, None] * inv
    cos = jnp.concatenate([jnp.cos(ang_row)] * 2 + [jnp.cos(ang_col)] * 2, axis=1)
    sin = jnp.concatenate([jnp.sin(ang_row)] * 2 + [jnp.sin(ang_col)] * 2, axis=1)
    return cos, sin


def _rope_partner(w):
    n = MLA_ROPE // 4
    pairs = w.reshape(w.shape[:-1] + (2, 2, n))
    sign = jnp.array([-1.0, 1.0], F32)[:, None]
    return (pairs[..., ::-1, :] * sign).reshape(w.shape)


def mla_mix(a, w_in, q_norm, kv_norm, w_uq, w_ukv):
    H, NP, R, V, QK = MLA_HEADS, MLA_NOPE, MLA_ROPE, MLA_V, MLA_QK_PAD
    n_all = N_LAT + N_CTX
    cos, sin = _rope_tables()
    w_in_x = jnp.concatenate([w_in, _rope_partner(w_in[:, MLA_IN - R:])], axis=1)
    proj = linear(a, w_in_x, tm=512, tn=MLA_IN_PAD, name="mla_in")
    wq = w_uq.reshape(MLA_Q_RANK, H, NP + R)
    tail = QK - NP - R
    wq1 = jnp.pad(wq, ((0, 0), (0, 0), (0, tail))).reshape(MLA_Q_RANK, H * QK)
    wq2 = jnp.pad(_rope_partner(wq[..., NP:]),
                  ((0, 0), (0, 0), (NP, tail))).reshape(MLA_Q_RANK, H * QK)
    ones, zeros = jnp.ones((SEQ, NP), F32), jnp.zeros((SEQ, QK - NP - R), F32)
    cq = jnp.concatenate([ones, cos, zeros], axis=1)
    sq = jnp.concatenate([0.0 * ones, sin, zeros], axis=1)
    tm, tn = 1024, 1024
    per = SEQ // tm
    q = pl.pallas_call(
        _mla_q_kernel,
        grid=(H * QK // tn, N_LAT // tm),
        in_specs=[pl.BlockSpec((tm, MLA_Q_RANK), lambda jn, i: (i, 0)),
                  pl.BlockSpec((1, MLA_Q_RANK), lambda jn, i: (0, 0)),
                  pl.BlockSpec((MLA_Q_RANK, tn), lambda jn, i: (0, jn)),
                  pl.BlockSpec((MLA_Q_RANK, tn), lambda jn, i: (0, jn)),
                  pl.BlockSpec((tm, QK), lambda jn, i: (i % per, 0)),
                  pl.BlockSpec((tm, QK), lambda jn, i: (i % per, 0))],
        out_specs=pl.BlockSpec((tm, tn), lambda jn, i: (i, jn)),
        out_shape=jax.ShapeDtypeStruct((N_LAT, H * QK), BF16),
        scratch_shapes=[pltpu.VMEM((MLA_Q_RANK, tn), BF16), pltpu.VMEM((MLA_Q_RANK, tn), BF16)],
        compiler_params=_params(2, 6 * _nbytes((MLA_Q_RANK, tn), F32) + 8 * _nbytes((tm, tn), F32)),
        name="mla_q",
    )(proj, q_norm.reshape(1, MLA_Q_RANK), wq1, wq2, cq, sq)
    wkv = w_ukv.reshape(MLA_KV_RANK, H, NP + V)
    wk = jnp.concatenate([wkv[..., :NP], jnp.zeros((MLA_KV_RANK, H, QK - NP), F32)],
                         axis=2).reshape(MLA_KV_RANK, H * QK)
    place = jnp.concatenate([jnp.zeros((R, NP), F32), jnp.eye(R, dtype=F32),
                             jnp.zeros((R, QK - NP - R), F32)], axis=1)
    place = jnp.tile(jnp.concatenate([place, place], axis=0), (1, H))
    w_kv = jnp.concatenate([
        jnp.concatenate([wk, wkv[..., NP:].reshape(MLA_KV_RANK, H * V)], axis=1),
        jnp.concatenate([place, jnp.zeros((2 * R, H * V), F32)], axis=1)], axis=0)
    tm, tn = 512, 2048
    cs = jnp.concatenate([
        jnp.concatenate([cos, sin], axis=1),
        jnp.concatenate([jnp.ones((tm, R), F32), jnp.zeros((tm, R), F32)], axis=1)],
        axis=0)
    per = SEQ // tm
    n_kv = H * (QK + V)
    kv = pl.pallas_call(
        _mla_kv_kernel,
        grid=(n_kv // tn, n_all // tm),
        in_specs=[pl.BlockSpec((tm, MLA_KV_RANK), lambda jn, i: (i, MLA_Q_RANK // MLA_KV_RANK)),
                  pl.BlockSpec((tm, 2 * R), lambda jn, i: (i, (MLA_IN - R) // (2 * R))),
                  pl.BlockSpec((1, MLA_KV_RANK), lambda jn, i: (0, 0)),
                  pl.BlockSpec((tm, 2 * R),
                               lambda jn, i: (jnp.where(i * tm < N_LAT, i % per, per), 0)),
                  pl.BlockSpec((MLA_KV_RANK + 2 * R, tn), lambda jn, i: (0, jn))],
        out_specs=pl.BlockSpec((tm, tn), lambda jn, i: (i, jn)),
        out_shape=jax.ShapeDtypeStruct((n_all, n_kv), BF16),
        scratch_shapes=[pltpu.VMEM((MLA_KV_RANK + 2 * R, tn), BF16)],
        compiler_params=_params(2, 4 * _nbytes((MLA_KV_RANK + 2 * R, tn), F32)
                                + 8 * _nbytes((tm, tn), F32)),
        name="mla_kv",
    )(proj, proj, kv_norm.reshape(1, MLA_KV_RANK), cs, w_kv)
    tq = 2048
    kchunk = (SEQ + CTX_LEN) // 3
    per = SEQ // tq
    v0 = H * QK // V
    ctx0 = N_LAT // CTX_LEN
    est = (2 * _nbytes((tq, QK), BF16) + 2 * _nbytes((SEQ + CTX_LEN, QK + V), BF16)
           + 8 * _nbytes((tq, kchunk), F32))
    return pl.pallas_call(
        functools.partial(_attn_kernel, kchunk=kchunk),
        grid=(BATCH, H, per),
        in_specs=[pl.BlockSpec((tq, QK), lambda b, h, i: (b * per + i, h)),
                  pl.BlockSpec((SEQ, QK), lambda b, h, i: (b, h)),
                  pl.BlockSpec((CTX_LEN, QK), lambda b, h, i: (ctx0 + b, h)),
                  pl.BlockSpec((SEQ, V), lambda b, h, i: (b, v0 + h)),
                  pl.BlockSpec((CTX_LEN, V), lambda b, h, i: (ctx0 + b, v0 + h))],
        out_specs=pl.BlockSpec((tq, V), lambda b, h, i: (b * per + i, h)),
        out_shape=jax.ShapeDtypeStruct((N_LAT, H * V), BF16),
        compiler_params=_params(3, est),
        name="mla_attention",
    )(q, kv, kv, kv, kv)


def dense_ffn(a, f, wg, wu, wd):
    p = swiglu_up([a], wg, wu, e=f, tm=1024, tf=512, name="ffn_up")
    return linear(p, wd, e=f, tm=512, tn=1024, out_dtype=BF16, name="ffn_down")


def _row_cumsum(onehot):
    n, e = onehot.shape
    blk = 256
    x = onehot.reshape(n // blk, blk, e).astype(BF16)
    tri = jnp.tril(jnp.ones((blk, blk), BF16))
    inner = jnp.einsum("ij,bje->bie", tri, x, preferred_element_type=F32)
    totals = inner[:, -1, :]
    offs = jnp.cumsum(totals, axis=0) - totals
    return (inner + offs[:, None, :]).reshape(n, e).astype(jnp.int32)


def moe_ffn(logits, a_rows, n_rows, f, wg, wu, wd, *, tm=512):
    top_v, top_i = lax.top_k(logits[:, :N_EXPERTS], TOP_K)
    gates = jax.nn.softmax(top_v, axis=-1)
    n_assign = n_rows * TOP_K
    n_tiles = n_assign // tm + N_EXPERTS
    flat_e = top_i.reshape(n_assign)
    onehot = (flat_e[:, None] == jnp.arange(N_EXPERTS)[None, :]).astype(jnp.int32)
    csum = _row_cumsum(onehot)
    rank = jnp.sum(csum * onehot, axis=1) - 1
    counts = csum[-1]
    padded = ((counts + tm - 1) // tm) * tm
    ends = jnp.cumsum(padded)
    starts = ends - padded
    pos = starts[flat_e] + rank
    src = (jnp.arange(n_tiles * tm, dtype=jnp.int32) % n_rows).at[pos].set(
        jnp.arange(n_assign, dtype=jnp.int32) // TOP_K)
    tile_row0 = jnp.arange(n_tiles, dtype=jnp.int32) * tm
    te = jnp.minimum(jnp.sum((tile_row0[:, None] >= ends[None, :]).astype(jnp.int32), axis=1),
                     N_EXPERTS - 1)
    used = jnp.clip(counts[te] - (tile_row0 - starts[te]), 0, tm)
    vr = jnp.where(tile_row0 < ends[-1], -(-used // ROW_STEP) * ROW_STEP, 0).astype(jnp.int32)
    n_used = ends[-1] // tm
    te = jnp.where(tile_row0 < ends[-1], te, te[jnp.maximum(n_used - 1, 0)]) + f * N_EXPERTS
    x_sorted = a_rows.at[src].get(mode="promise_in_bounds")
    stack = lambda w: w.reshape((-1,) + w.shape[2:])
    p = swiglu_up([x_sorted], stack(wg), stack(wu), tm=tm, tf=1024, te=te, vr=vr, name="moe_up")
    out = linear(p, stack(wd), tm=tm, tn=1024, te=te, vr=vr, name="moe_down")
    pos = pos.reshape(n_rows, TOP_K)
    ys = [out.at[pos[:, k]].get(mode="promise_in_bounds") for k in range(TOP_K)]
    return ys, [gates[:, k:k + 1] for k in range(TOP_K)]


def kernel(x, c, ctx, c_ctx, w_mod, b_mod, ln_g, ln_b, pool_w, pool_scale, hy_w_in, hy_b_in, hy_conv_w, hy_conv_b, hy_f_w1, hy_f_b1, hy_f_w2, hy_f_b2, hy_f_w3, hy_f_b3, hy_f_freq, hy_f_wout, hy_skip, hy_w_out, hy_b_out, mla_w_in, mla_q_norm, mla_kv_norm, mla_w_uq, mla_w_ukv, mla_w_o, ffn_w_gate, ffn_w_up, ffn_w_down, moe_w_router, moe_w_gate, moe_w_up, moe_w_down):
    attn_layers = [i for i in range(DEPTH) if i % N_MIXERS == 2]
    last_read = attn_layers[-1] if attn_layers else -1

    c8 = jnp.concatenate([c, c_ctx[None], jnp.zeros((MOD_ROWS - BATCH - 1, D_MODEL), F32)], axis=0)
    mods = modulation_table(c8, w_mod, b_mod)

    assert 0 < last_read and N_MIXERS > 1
    h = [x.reshape(N_LAT, D_MODEL), ctx.reshape(N_CTX, D_MODEL)]
    a = None
    for i in range(DEPTH):
        kind, j = i % N_MIXERS, i // N_MIXERS
        ctx_out = i < last_read
        n_rows = N_LAT + N_CTX if ctx_out else N_LAT
        m = mods[i]
        f = i // 2
        w_router = moe_w_router[f] if i % 2 == 1 else None
        if kind == 0:
            tl = CTX_LEN if n_rows > N_LAT else 512
            h1, a2, *logits = pool_layer(h if isinstance(h, list) else [h], n_rows, m,
                                         pool_w[j].astype(BF16), pool_scale[j],
                                         ln_g[i, 0], ln_b[i, 0], tl=tl, w_router=w_router)
        else:
            if kind == 1:
                filt = (hy_f_w1[j], hy_f_b1[j], hy_f_w2[j], hy_f_b2[j], hy_f_w3[j], hy_f_b3[j],
                        hy_f_freq[j], hy_f_wout[j])
                z = hyena_mix(a[:n_rows], n_rows, j, hy_w_in, hy_b_in, hy_conv_w[j], hy_conv_b[j],
                              filt, hy_skip[j])
                w_o, b_o = hy_w_out, hy_b_out
            else:
                assert i <= last_read and not ctx_out and a.shape[0] == N_LAT + N_CTX
                z = [mla_mix(a, mla_w_in[j], mla_q_norm[j], mla_kv_norm[j], mla_w_uq[j],
                             mla_w_ukv[j])]
                w_o, b_o = mla_w_o, None
            h1, a2, *logits = linear_resid_ln(z, w_o, j, b_o, h, n_rows, m, 2, ln_g[i, 0],
                                              ln_b[i, 0], 4, 3, w_router=w_router)
        nm = mods[min(i + 1, DEPTH - 1)]
        if w_router is not None:
            ys, row_gates = moe_ffn(logits[0], a2, n_rows, f, moe_w_gate, moe_w_up, moe_w_down)
        else:
            ys, row_gates = [dense_ffn(a2, f, ffn_w_gate, ffn_w_up, ffn_w_down)], None
        h, a = resid_ln(h1, ys, n_rows, m, 5, ln_g[i, 1], ln_b[i, 1], nm, 1, 0,
                        row_gates=row_gates)
    return h[:N_LAT].reshape(BATCH, SEQ, D_MODEL)
```

```python
import functools
import math

import jax
import jax.numpy as jnp
from jax import lax
from jax.experimental import pallas as pl
from jax.experimental.pallas import tpu as pltpu

F32 = jnp.float32
BF16 = jnp.bfloat16
U32 = jnp.uint32

D_MODEL = 2048
BATCH = 4
SEQ = 2048
DEPTH = 4
GRID_W = 64
CTX_LEN = 256
N_MIXERS = 3
POOL_WINDOWS = (2, 4, 8, 16)
POOL_GROUP = D_MODEL // len(POOL_WINDOWS)
HY_ORDER = 2
HY_SHORT = 3
HY_EMB = 33
HY_FILTER = 64
HY_DECAY_TARGET = 1e-2
HY_FAST = 0.3
HY_SLOW = 1.5
MLA_HEADS = D_MODEL // 128
MLA_Q_RANK = 512
MLA_KV_RANK = 512
MLA_NOPE = 128
MLA_ROPE = 64
MLA_V = 128
ROPE_BASE = 10000.0
D_FF = 5632
N_EXPERTS = 8
TOP_K = 2
EXPERT_FF = 2 * D_MODEL
LN_EPS = 1e-5
RMS_EPS = 1e-6
DN_ALPHA = (2 * DEPTH) ** 0.25

N_LAT = BATCH * SEQ
N_CTX = BATCH * CTX_LEN
MOD_ROWS = 8
CTX_MOD_ROW = BATCH
LANE = 128
SUBLANE = 8
VMEM_CAP = 60 * 1024 * 1024
MLA_QK_PAD = 256
MLA_IN = MLA_Q_RANK + MLA_KV_RANK + MLA_ROPE
MLA_IN_PAD = MLA_IN + MLA_ROPE


def _vmem(nbytes):
    return int(min(VMEM_CAP, max(16 * 1024 * 1024, nbytes * 3 // 2)))


def _nbytes(shape, dtype):
    return math.prod(shape) * jnp.dtype(dtype).itemsize


def _mod_row(i, tm):
    return jnp.minimum((i * tm) // SEQ, CTX_MOD_ROW)


def _mod_spec(which, tm):
    return pl.BlockSpec((None, 1, D_MODEL), lambda i: (_mod_row(i, tm) * 6 + which, 0, 0))


def _params(sem, est):
    return pltpu.CompilerParams(dimension_semantics=("arbitrary",) * sem,
                                vmem_limit_bytes=_vmem(est))


def _mod_kernel(c_ref, w_ref, b_ref, o_ref):
    c = c_ref[...]
    a = c / (1.0 + jnp.exp(-c))
    o_ref[...] = jnp.dot(a.astype(BF16), w_ref[...].astype(BF16),
                         preferred_element_type=F32) + b_ref[...]


def modulation_table(c8, w_mod, b_mod):
    tn = 1024
    n = 6 * D_MODEL
    out = pl.pallas_call(
        _mod_kernel,
        grid=(DEPTH, n // tn),
        in_specs=[pl.BlockSpec((MOD_ROWS, D_MODEL), lambda l, j: (0, 0)),
                  pl.BlockSpec((None, D_MODEL, tn), lambda l, j: (l, 0, j)),
                  pl.BlockSpec((None, 1, tn), lambda l, j: (l, 0, j))],
        out_specs=pl.BlockSpec((None, MOD_ROWS, tn), lambda l, j: (l, 0, j)),
        out_shape=jax.ShapeDtypeStruct((DEPTH, MOD_ROWS, n), F32),
        compiler_params=_params(2, 3 * _nbytes((D_MODEL, tn), F32)),
        name="modulation_table",
    )(c8, w_mod, b_mod.reshape(DEPTH, 1, n))
    return out.reshape(DEPTH, MOD_ROWS * 6, 1, D_MODEL)


def _group_schedule(te):
    n = te.shape[0]
    first = jnp.concatenate([jnp.ones((1,), jnp.int32), (te[1:] != te[:-1]).astype(jnp.int32)])
    gid = jnp.cumsum(first) - 1
    ge = jnp.zeros((n,), jnp.int32).at[gid].max(te)
    return first, gid, ge, gid[-1:] + 1


def _stream_weights(sched, w_hbms, stages, sem, dsts, col0, tn):
    first_ref, gid_ref, ge_ref, ng_ref = sched
    j, i = pl.program_id(0), pl.program_id(1)
    g, ng = gid_ref[i], ng_ref[0]

    def copies(e, col):
        cols = pl.ds(pl.multiple_of((col + col0) * tn, tn), tn)
        return [pltpu.make_async_copy(w.at[e, :, cols], st, sem.at[k])
                for k, (w, st) in enumerate(zip(w_hbms, stages))]

    @pl.when(first_ref[i] == 1)
    def _():
        @pl.when(jnp.logical_and(j == 0, g == 0))
        def _():
            for c in copies(ge_ref[0], 0):
                c.start()

        for c in copies(ge_ref[g], j):
            c.wait()
        for st, dst in zip(stages, dsts):
            dst[...] = st[...].astype(BF16)
        last = g + 1 >= ng
        next_e = ge_ref[jnp.where(last, 0, g + 1)]
        next_col = jnp.where(last, j + 1, j)

        @pl.when(next_col < pl.num_programs(0))
        def _():
            for c in copies(next_e, next_col):
                c.start()


def _for_valid_rows(vr, row_step, o_ref, compute):
    tm = o_ref.shape[0]
    for r in range(row_step, tm + 1, row_step):
        @pl.when(vr == r)
        def _(r=r):
            o_ref[0:r, :] = compute(r).astype(o_ref.dtype)
            if r < tm:
                o_ref[r:tm, :] = jnp.zeros((tm - r, o_ref.shape[1]), o_ref.dtype)

    @pl.when(vr == 0)
    def _():
        o_ref[...] = jnp.zeros_like(o_ref)


def _linear_kernel(xt_ref, vr_ref, first_ref, gid_ref, ge_ref, ng_ref, x_ref, w_hbm, *rest,
                   has_bias, col0, tn, row_step):
    rest = list(rest)
    b_ref = rest.pop(0) if has_bias else None
    o_ref, stage_ref, wbf_ref, sem = rest
    i = pl.program_id(1)
    _stream_weights((first_ref, gid_ref, ge_ref, ng_ref), [w_hbm], [stage_ref], sem, [wbf_ref],
                    col0, tn)

    def compute(r):
        acc = jnp.dot(x_ref[0:r, :].astype(BF16), wbf_ref[...], preferred_element_type=F32)
        return acc + b_ref[...] if has_bias else acc

    _for_valid_rows(vr_ref[i], row_step, o_ref, compute)


ROW_STEP = 128


def linear(x, w, *, tm, tn, e=0, n_out=None, n_tiles=None, xt=None, te=None, vr=None, bias=None,
           out_dtype=F32, w_col0=0, name="linear"):
    k = x.shape[1]
    if w.ndim == 2:
        w = w[None]
    if n_out is None:
        n_out = w.shape[2] - w_col0
    assert x.shape[0] % tm == 0 and n_out % tn == 0 and w_col0 % tn == 0 and w.shape[1] == k
    assert w.dtype == F32
    if n_tiles is None:
        n_tiles = x.shape[0] // tm
    if xt is None:
        xt = jnp.arange(n_tiles, dtype=jnp.int32)
    if te is None:
        te = jnp.full((n_tiles,), e, jnp.int32)
    row_step = tm if vr is None else ROW_STEP
    if vr is None:
        vr = jnp.full((n_tiles,), tm, jnp.int32)
    first, gid, ge, ng = _group_schedule(te)
    has_bias = bias is not None
    in_specs = [pl.BlockSpec((tm, k), lambda j, i, xt, *_: (xt[i], 0)),
                pl.BlockSpec(memory_space=pl.ANY)]
    args = [x, w]
    if has_bias:
        bias = bias.reshape(-1, 1, bias.shape[-1])
        in_specs.append(pl.BlockSpec((None, 1, tn),
                                     lambda j, i, xt, vr, first, gid, ge, ng: (ge[gid[i]], 0, j)))
        args.append(bias)
    est = (2 * _nbytes((tm, k), x.dtype) + _nbytes((k, tn), F32) + _nbytes((k, tn), BF16)
           + _nbytes((tm, k), BF16) + 2 * _nbytes((tm, tn), out_dtype) + 2 * _nbytes((tm, tn), F32))
    return pl.pallas_call(
        functools.partial(_linear_kernel, has_bias=has_bias, col0=w_col0 // tn, tn=tn,
                          row_step=row_step),
        grid_spec=pltpu.PrefetchScalarGridSpec(
            num_scalar_prefetch=6,
            grid=(n_out // tn, n_tiles),
            in_specs=in_specs,
            out_specs=pl.BlockSpec((tm, tn), lambda j, i, *_: (i, j)),
            scratch_shapes=[pltpu.VMEM((k, tn), F32), pltpu.VMEM((k, tn), BF16),
                            pltpu.SemaphoreType.DMA((1,))]),
        out_shape=jax.ShapeDtypeStruct((n_tiles * tm, n_out), out_dtype),
        compiler_params=_params(2, est),
        name=name,
    )(xt, vr, first, gid, ge, ng, *args)


def _swiglu_kernel(vr_ref, first_ref, gid_ref, ge_ref, ng_ref, *refs, n_parts, tf, row_step):
    x_refs = refs[:n_parts]
    wg_hbm, wu_hbm, o_ref, sg_ref, su_ref, wgb_ref, wub_ref, sem = refs[n_parts:]
    i = pl.program_id(1)
    _stream_weights((first_ref, gid_ref, ge_ref, ng_ref), [wg_hbm, wu_hbm], [sg_ref, su_ref], sem,
                    [wgb_ref, wub_ref], 0, tf)

    def compute(r):
        g = u = None
        k0 = 0
        for x_ref in x_refs:
            x = x_ref[0:r, :].astype(BF16)
            rows = slice(k0, k0 + x.shape[1])
            gp = jnp.dot(x, wgb_ref[rows, :], preferred_element_type=F32)
            up = jnp.dot(x, wub_ref[rows, :], preferred_element_type=F32)
            g, u = (gp, up) if g is None else (g + gp, u + up)
            k0 += x.shape[1]
        return (g / (1.0 + jnp.exp(-g))) * u

    _for_valid_rows(vr_ref[i], row_step, o_ref, compute)


def swiglu_up(x_parts, wg, wu, *, tm, tf, e=0, te=None, vr=None, name="swiglu_up"):
    m = x_parts[0].shape[0]
    k, f = wg.shape[1], wg.shape[2]
    assert sum(x.shape[1] for x in x_parts) == k and m % tm == 0 and f % tf == 0
    n_tiles = m // tm
    if te is None:
        te = jnp.full((n_tiles,), e, jnp.int32)
    row_step = tm if vr is None else ROW_STEP
    if vr is None:
        vr = jnp.full((n_tiles,), tm, jnp.int32)
    first, gid, ge, ng = _group_schedule(te)
    wspec = pl.BlockSpec(memory_space=pl.ANY)
    est = (2 * _nbytes((tm, k), x_parts[0].dtype) + _nbytes((tm, k), BF16)
           + 2 * _nbytes((k, tf), F32) + 2 * _nbytes((k, tf), BF16) + 2 * _nbytes((tm, tf), BF16)
           + 3 * _nbytes((tm, tf), F32))
    x_specs = [pl.BlockSpec((tm, x.shape[1]), lambda j, i, *_: (i, 0)) for x in x_parts]
    return pl.pallas_call(
        functools.partial(_swiglu_kernel, n_parts=len(x_parts), tf=tf, row_step=row_step),
        grid_spec=pltpu.PrefetchScalarGridSpec(
            num_scalar_prefetch=5,
            grid=(f // tf, n_tiles),
            in_specs=[*x_specs, wspec, wspec],
            out_specs=pl.BlockSpec((tm, tf), lambda j, i, *_: (i, j)),
            scratch_shapes=[pltpu.VMEM((k, tf), F32), pltpu.VMEM((k, tf), F32),
                            pltpu.VMEM((k, tf), BF16), pltpu.VMEM((k, tf), BF16),
                            pltpu.SemaphoreType.DMA((2,))]),
        out_shape=jax.ShapeDtypeStruct((m, f), BF16),
        compiler_params=_params(2, est),
        name=name,
    )(vr, first, gid, ge, ng, *x_parts, wg, wu)


def _ln_mod(v, g, b, sc, sh, pack):
    mu = jnp.mean(v, axis=-1, keepdims=True)
    d = v - mu
    var = jnp.mean(d * d, axis=-1, keepdims=True)
    h = d * lax.rsqrt(var + LN_EPS) * g + b
    a = h * (1.0 + sc) + sh
    return h, (a if pack else a.astype(BF16))


def _resid_ln_kernel(*refs, n_y, pack):
    h_ref = refs[0]
    y_refs = refs[1:1 + n_y]
    rest = refs[1 + n_y:]
    if n_y == 2:
        g0_ref, g1_ref = rest[0], rest[1]
        rest = rest[2:]
        y = g0_ref[...] * y_refs[0][...] + g1_ref[...] * y_refs[1][...]
    else:
        y = y_refs[0][...]
    gate_ref, g_ref, b_ref, sc_ref, sh_ref, o_ref, a_ref = rest
    v = DN_ALPHA * h_ref[...] + gate_ref[...] * y
    h, a = _ln_mod(v, g_ref[...], b_ref[...], sc_ref[...], sh_ref[...], pack)
    o_ref[...] = h
    a_ref[...] = a


def resid_ln(h, ys, n_rows, mods, gate_idx, ln_g, ln_b, nmods, sc_idx, sh_idx, *, row_gates=None,
             pack=False, tm=512):
    assert n_rows % tm == 0
    row = pl.BlockSpec((tm, D_MODEL), lambda i: (i, 0))
    col = pl.BlockSpec((tm, 1), lambda i: (i, 0))
    vec = pl.BlockSpec((1, D_MODEL), lambda i: (0, 0))
    a_w = D_MODEL
    a_dt = F32 if pack else BF16
    est = 12 * _nbytes((tm, D_MODEL), F32)
    in_specs = [row] + [row] * len(ys) + ([col, col] if row_gates else []) + [
        _mod_spec(gate_idx, tm), vec, vec, _mod_spec(sc_idx, tm), _mod_spec(sh_idx, tm)]
    args = [h, *ys, *(row_gates or []), mods, ln_g.reshape(1, D_MODEL), ln_b.reshape(1, D_MODEL),
            nmods, nmods]
    return pl.pallas_call(
        functools.partial(_resid_ln_kernel, n_y=len(ys), pack=pack),
        grid=(n_rows // tm,),
        in_specs=in_specs,
        out_specs=[row, pl.BlockSpec((tm, a_w), lambda i: (i, 0))],
        out_shape=[jax.ShapeDtypeStruct((n_rows, D_MODEL), F32),
                   jax.ShapeDtypeStruct((n_rows, a_w), a_dt)],
        compiler_params=_params(1, est),
        name="resid_ln",
    )(*args)


def _split_bf16(x):
    hi = x.astype(BF16)
    lo = (x - hi.astype(F32)).astype(BF16)
    return hi, lo


def _router_logits(a, whi_ref, wlo_ref):
    a_hi, a_lo = _split_bf16(a)
    w_hi = whi_ref[...]
    return (jnp.dot(a_hi, w_hi, preferred_element_type=F32)
            + jnp.dot(a_lo, w_hi, preferred_element_type=F32)
            + jnp.dot(a_hi, wlo_ref[...], preferred_element_type=F32))


def _router_operands(w_router):
    w = jnp.pad(w_router, ((0, 0), (0, LANE - N_EXPERTS)))
    return _split_bf16(w), pl.BlockSpec((D_MODEL, LANE), lambda i: (0, 0))


def _linear_ln_kernel(*refs, has_bias, moe, split):
    n_src = 1 if split is None else 2
    x_refs, w_ref, rest = refs[:n_src], refs[n_src], list(refs[n_src + 1:])
    bias_ref = rest.pop(0) if has_bias else None
    h_ref, gate_ref, g_ref, b_ref, sc_ref, sh_ref = rest[:6]
    rest = rest[6:]
    if moe:
        whi_ref, wlo_ref, o_ref, a_ref, l_ref, wb_ref = rest
    else:
        o_ref, a_ref, wb_ref = rest

    @pl.when(pl.program_id(0) == 0)
    def _():
        wb_ref[...] = w_ref[...].astype(BF16)

    if split is None:
        x = x_refs[0][...]
    else:
        x = jnp.where(pl.program_id(0) < split, x_refs[0][...], x_refs[1][...])
    y = jnp.dot(x, wb_ref[...], preferred_element_type=F32)
    if has_bias:
        y = y + bias_ref[...]
    v = DN_ALPHA * h_ref[...] + gate_ref[...] * y
    h, a = _ln_mod(v, g_ref[...], b_ref[...], sc_ref[...], sh_ref[...], moe)
    o_ref[...] = h
    a_ref[...] = a
    if moe:
        l_ref[...] = _router_logits(a, whi_ref, wlo_ref)


def linear_resid_ln(xs, w, e, bias, h, n_rows, mods, gate_idx, ln_g, ln_b, sc_idx, sh_idx, *,
                    w_router=None, tm=256):
    k = xs[0].shape[1]
    moe = w_router is not None
    row = pl.BlockSpec((tm, D_MODEL), lambda i: (i, 0))
    vec = pl.BlockSpec((1, D_MODEL), lambda i: (0, 0))
    split = None if len(xs) == 1 else xs[0].shape[0] // tm
    in_specs, t0 = [], 0
    for x in xs:
        nt = x.shape[0] // tm
        in_specs.append(pl.BlockSpec((tm, k), lambda i, t0=t0, nt=nt: (jnp.clip(i - t0, 0, nt - 1), 0)))
        t0 += nt
    in_specs.append(pl.BlockSpec((None, k, D_MODEL), lambda i: (e, 0, 0),
                                 pipeline_mode=pl.Buffered(1)))
    args = [*xs, w]
    if bias is not None:
        in_specs.append(pl.BlockSpec((None, 1, D_MODEL), lambda i: (e, 0, 0)))
        args.append(bias.reshape(-1, 1, D_MODEL))
    in_specs += [row, _mod_spec(gate_idx, tm), vec, vec, _mod_spec(sc_idx, tm),
                 _mod_spec(sh_idx, tm)]
    args += [h, mods, ln_g.reshape(1, D_MODEL), ln_b.reshape(1, D_MODEL), mods, mods]
    out_specs = [row, row]
    out_shape = [jax.ShapeDtypeStruct((n_rows, D_MODEL), F32),
                 jax.ShapeDtypeStruct((n_rows, D_MODEL), F32 if moe else BF16)]
    if moe:
        (w_hi, w_lo), wspec = _router_operands(w_router)
        in_specs += [wspec, wspec]
        args += [w_hi, w_lo]
        out_specs.append(pl.BlockSpec((tm, LANE), lambda i: (i, 0)))
        out_shape.append(jax.ShapeDtypeStruct((n_rows, LANE), F32))
    est = (_nbytes((k, D_MODEL), F32) + _nbytes((k, D_MODEL), BF16) + 2 * _nbytes((tm, k), BF16)
           + 12 * _nbytes((tm, D_MODEL), F32))
    return pl.pallas_call(
        functools.partial(_linear_ln_kernel, has_bias=bias is not None, moe=moe, split=split),
        grid=(n_rows // tm,),
        in_specs=in_specs,
        out_specs=out_specs,
        out_shape=out_shape,
        scratch_shapes=[pltpu.VMEM((k, D_MODEL), BF16)],
        compiler_params=_params(1, est),
        name="linear_resid_ln",
    )(*args)


def _seq_pos(i, tl):
    row0 = i * tl
    is_lat = row0 < N_LAT
    pos = jnp.where(is_lat, row0 % SEQ, (row0 - N_LAT) % CTX_LEN)
    seqlen = jnp.where(is_lat, SEQ, CTX_LEN)
    return pos, seqlen


def _tile_halo_specs(tl, width, t0, nt):
    per = tl // SUBLANE
    local = lambda i: jnp.clip(i - t0, 0, nt - 1)
    main = pl.BlockSpec((tl, width), lambda i: (local(i), 0))
    prev = pl.BlockSpec((SUBLANE, width), lambda i: (jnp.maximum(local(i) * per - 1, 0), 0))
    nxt = pl.BlockSpec((SUBLANE, width),
                       lambda i: (jnp.minimum((local(i) + 1) * per, nt * per - 1), 0))
    return [main, prev, nxt]


def _shift_rows(x, s):
    n = x.shape[0]
    return pltpu.roll(x, s % n, 0)


def _pool_kernel(*refs, tl, moe, split):
    n_src = 1 if split is None else 2
    srcs, rest = refs[:3 * n_src], refs[3 * n_src:]
    (sc_ref, sh_ref, gate_ref, w_ref, ps_ref, g_ref, b_ref, nsc_ref, nsh_ref), rest = rest[:9], rest[9:]
    if moe:
        whi_ref, wlo_ref, o_ref, a_ref, l_ref = rest
    else:
        o_ref, a_ref = rest
    i = pl.program_id(0)
    pos, seqlen = _seq_pos(i, tl)
    if split is None:
        h, h_prev, h_next = (r[...] for r in srcs)
    else:
        h, h_prev, h_next = (jnp.where(i < split, r0[...], r1[...])
                             for r0, r1 in zip(srcs[:3], srcs[3:]))
    sc = 1.0 + sc_ref[...]
    sh = sh_ref[...]
    a = h * sc + sh
    keep_prev = (pos > 0).astype(F32)
    keep_next = (pos + tl < seqlen).astype(F32)
    ext = jnp.concatenate([(h_prev * sc + sh) * keep_prev, a,
                           (h_next * sc + sh) * keep_next], axis=0)
    t = pos + lax.broadcasted_iota(jnp.int32, (tl, 1), 0)
    n_ext = tl + 2 * SUBLANE
    parts = []
    for g, w in enumerate(POOL_WINDOWS):
        cols = slice(g * POOL_GROUP, (g + 1) * POOL_GROUP)
        s = ext[:, cols]
        s = s + _shift_rows(s, 1)
        r = 1
        while 2 * r < w:
            s = _shift_rows(s, n_ext - r) + _shift_rows(s, r)
            r *= 2
        s = s[SUBLANE:SUBLANE + tl]
        cnt = (jnp.minimum(t + w // 2, seqlen) - jnp.maximum(t - w // 2, 0)).astype(F32)
        d = s / cnt - a[:, cols]
        parts.append(jnp.dot(d.astype(BF16), w_ref[g], preferred_element_type=F32))
    y = jnp.concatenate(parts, axis=1) * ps_ref[...]
    v = DN_ALPHA * h + gate_ref[...] * y
    hn, an = _ln_mod(v, g_ref[...], b_ref[...], nsc_ref[...], nsh_ref[...], moe)
    o_ref[...] = hn
    a_ref[...] = an
    if moe:
        l_ref[...] = _router_logits(an, whi_ref, wlo_ref)


def pool_layer(hs, n_rows, mods, w_grp_bf16, pool_scale, ln_g, ln_b, *, tl, w_router=None):
    moe = w_router is not None
    row = pl.BlockSpec((tl, D_MODEL), lambda i: (i, 0))
    vec = pl.BlockSpec((1, D_MODEL), lambda i: (0, 0))
    split = None if len(hs) == 1 else hs[0].shape[0] // tl
    in_specs, args, t0 = [], [], 0
    for h in hs:
        nt = h.shape[0] // tl
        in_specs += _tile_halo_specs(tl, D_MODEL, t0, nt)
        args += [h, h, h]
        t0 += nt
    in_specs += [_mod_spec(1, tl), _mod_spec(0, tl), _mod_spec(2, tl),
                 pl.BlockSpec(w_grp_bf16.shape, lambda i: (0, 0, 0)), vec, vec, vec,
                 _mod_spec(4, tl), _mod_spec(3, tl)]
    args += [mods, mods, mods, w_grp_bf16, pool_scale.reshape(1, D_MODEL),
             ln_g.reshape(1, D_MODEL), ln_b.reshape(1, D_MODEL), mods, mods]
    out_specs = [row, row]
    out_shape = [jax.ShapeDtypeStruct((n_rows, D_MODEL), F32),
                 jax.ShapeDtypeStruct((n_rows, D_MODEL), F32 if moe else BF16)]
    if moe:
        (w_hi, w_lo), wspec = _router_operands(w_router)
        in_specs += [wspec, wspec]
        args += [w_hi, w_lo]
        out_specs.append(pl.BlockSpec((tl, LANE), lambda i: (i, 0)))
        out_shape.append(jax.ShapeDtypeStruct((n_rows, LANE), F32))
    est = 14 * _nbytes((tl, D_MODEL), F32) + 2 * _nbytes(w_grp_bf16.shape, BF16)
    return pl.pallas_call(
        functools.partial(_pool_kernel, tl=tl, moe=moe, split=split),
        grid=(n_rows // tl,),
        in_specs=in_specs,
        out_specs=out_specs,
        out_shape=out_shape,
        compiler_params=_params(1, est),
        name="pool_layer",
    )(*args)


def _conv3_seq(u, w, b):
    n = u.shape[0]
    t = lax.broadcasted_iota(jnp.int32, (n, 1), 0)
    before = jnp.where(t == 0, 0.0, _shift_rows(u, 1))
    after = jnp.where(t == n - 1, 0.0, _shift_rows(u, n - 1))
    return b + before * w[0:1] + u * w[1:2] + after * w[2:3]


def _conv3_tile(u, prev, nxt, w, b, first, last):
    n = u.shape[0]
    t = lax.broadcasted_iota(jnp.int32, (n, 1), 0)
    row_before = prev[SUBLANE - 1:SUBLANE, :] * (1.0 - first.astype(F32))
    row_after = nxt[0:1, :] * (1.0 - last.astype(F32))
    before = jnp.where(t == 0, row_before, _shift_rows(u, 1))
    after = jnp.where(t == n - 1, row_after, _shift_rows(u, n - 1))
    return b + before * w[0:1] + u * w[1:2] + after * w[2:3]


def _dot3(a, b):
    a_hi, a_lo = _split_bf16(a)
    b_hi, b_lo = _split_bf16(b)
    return (jnp.dot(a_hi, b_hi, preferred_element_type=F32)
            + jnp.dot(a_lo, b_hi, preferred_element_type=F32)
            + jnp.dot(a_hi, b_lo, preferred_element_type=F32))


def _filter_kernel(z_ref, w1_ref, b1_ref, w2_ref, b2_ref, w3_ref, b3_ref, fr_ref, wo_ref,
                   dist_ref, delta_ref, o_ref, a_ref):
    @pl.when(pl.program_id(0) == 0)
    def _():
        fr = fr_ref[...]
        a = jnp.sin(fr * (_dot3(z_ref[...], w1_ref[...]) + b1_ref[...]))
        a = jnp.sin(fr * (_dot3(a, w2_ref[...]) + b2_ref[...]))
        a_ref[...] = jnp.sin(fr * (_dot3(a, w3_ref[...]) + b3_ref[...]))

    k = _dot3(a_ref[...], wo_ref[...])
    k = k * jnp.exp(-dist_ref[...] * delta_ref[...])
    o_ref[...] = k * lax.rsqrt(jnp.sum(k * k, axis=0, keepdims=True) + 1e-6)


def _pad2(x, rows, cols):
    return jnp.pad(x, ((0, rows - x.shape[0]), (0, cols - x.shape[1])))


def hyena_filters(L, f_w1, f_b1, f_w2, f_b2, f_w3, f_b3, f_freq, f_wout, *, tn=512):
    bands = (HY_EMB - 1) // 2
    t = jnp.linspace(0.0, 1.0, L, dtype=F32)[:, None]
    wpos = 2.0 * math.pi * jnp.arange(L, dtype=F32)[:, None] / L
    f = jnp.linspace(1e-4, bands - 1, bands, dtype=F32)[None, :]
    z = jnp.concatenate([t, jnp.cos(f * wpos), -jnp.sin(f * wpos)], axis=-1)
    dist = (jnp.abs(jnp.arange(L) - L // 2).astype(F32) / max(L // 2, 1))[:, None]
    deltas = jnp.linspace(math.log(HY_DECAY_TARGET) / HY_SLOW, math.log(HY_DECAY_TARGET) / HY_FAST,
                          D_MODEL, dtype=F32)
    absdelta = jnp.tile(jnp.abs(deltas), HY_ORDER)[None, :]
    P = LANE
    n = HY_ORDER * D_MODEL
    full = lambda shape: pl.BlockSpec(shape, lambda j: (0, 0))
    args = [_pad2(z, L, P), _pad2(f_w1, P, P), _pad2(f_b1[None], 1, P), _pad2(f_w2, P, P),
            _pad2(f_b2[None], 1, P), _pad2(f_w3, P, P), _pad2(f_b3[None], 1, P),
            _pad2(f_freq[None], 1, P), _pad2(f_wout, P, n), dist, absdelta]
    in_specs = [full((L, P)), full((P, P)), full((1, P)), full((P, P)), full((1, P)), full((P, P)),
                full((1, P)), full((1, P)), pl.BlockSpec((P, tn), lambda j: (0, j)),
                full((L, 1)), pl.BlockSpec((1, tn), lambda j: (0, j))]
    return pl.pallas_call(
        _filter_kernel,
        grid=(n // tn,),
        in_specs=in_specs,
        out_specs=pl.BlockSpec((L, tn), lambda j: (0, j)),
        out_shape=jax.ShapeDtypeStruct((L, n), F32),
        scratch_shapes=[pltpu.VMEM((L, P), F32)],
        compiler_params=_params(1, 8 * _nbytes((L, tn), F32)),
        name="hyena_filters",
    )(*args)


def dft_tables(L, tm):
    n_fft = 2 * L
    half = tm // 2
    r_idx = jnp.arange(2 * L, dtype=jnp.int32)
    odd = 2 * ((r_idx // tm) * half + (r_idx % half)) + 1
    is_sin = ((r_idx % tm) >= half)[:, None]

    def trig(k):
        ang = (k % (2 * n_fft)).astype(F32) * (math.pi / n_fft)
        return jnp.cos(ang), jnp.sin(ang)

    lo = 64
    s1 = jnp.arange(L // lo, dtype=jnp.int32) * lo
    s0 = jnp.arange(lo, dtype=jnp.int32)
    cb, sb = trig(odd[:, None] * s0[None, :])

    def combine(ca, sa):
        a1 = jnp.where(is_sin, sa, ca)
        a2 = jnp.where(is_sin, ca, -sa)
        return a1, a2

    a1, a2 = combine(*trig(odd[:, None] * s1[None, :]))
    coarse = lambda t: jnp.repeat(t, lo, axis=1)
    fine = lambda t: jnp.tile(t, (1, L // lo))
    fwd = coarse(a1) * fine(cb) + coarse(a2) * fine(sb)
    a1, a2 = combine(*trig(odd[:, None] * (s1[None, :] + L // 2)))
    inv = (a1.T[:, None, :] * cb.T[None, :, :] + a2.T[:, None, :] * sb.T[None, :, :]).reshape(
        L, 2 * L)
    return fwd.astype(BF16), inv.astype(BF16)


def _dft_mul_kernel(x_ref, w_ref, *rest, inv_len, conv):
    if conv:
        cw_ref, cb_ref, k_ref, o_ref, wb_ref = rest
    else:
        k_ref, o_ref, wb_ref = rest

    @pl.when(pl.program_id(1) == 0)
    def _():
        w = w_ref[...]
        if conv:
            w = _conv3_seq(w, cw_ref[...], cb_ref[...])
        wb_ref[...] = w.astype(BF16)

    s = jnp.dot(x_ref[...], wb_ref[...], preferred_element_type=F32)
    half = s.shape[0] // 2
    a, b = s[:half], s[half:]
    ka, kb = k_ref[0:half, :], k_ref[half:2 * half, :]
    o_ref[0:half, :] = ((a * ka - b * kb) * inv_len).astype(BF16)
    o_ref[half:2 * half, :] = ((a * kb + b * ka) * inv_len).astype(BF16)


def dft_mul(fwd, v_view, kspec, k_col0, *, L, nb, tm, tn, conv=None):
    d = D_MODEL
    per_b = d // tn
    v, v_row0, v_col0 = v_view
    vr, vc, kc = v_row0 // L, v_col0 // tn, k_col0 // tn
    est = (2 * _nbytes((tm, L), BF16) + 2 * _nbytes((L, tn), F32) + _nbytes((L, tn), BF16)
           + 2 * _nbytes((tm, tn), F32) + 2 * _nbytes((tm, tn), BF16) + 3 * _nbytes((tm, tn), F32))
    in_specs = [pl.BlockSpec((tm, L), lambda j, i: (i, 0)),
                pl.BlockSpec((L, tn), lambda j, i: (j // per_b + vr, j % per_b + vc))]
    args = [fwd, v]
    if conv is not None:
        in_specs += [pl.BlockSpec((HY_SHORT, tn), lambda j, i: (0, j % per_b + vc)),
                     pl.BlockSpec((1, tn), lambda j, i: (0, j % per_b + vc))]
        args += list(conv)
    in_specs.append(pl.BlockSpec((tm, tn), lambda j, i: (i, j % per_b + kc)))
    args.append(kspec)
    return pl.pallas_call(
        functools.partial(_dft_mul_kernel, inv_len=1.0 / L, conv=conv is not None),
        grid=(nb * per_b, (2 * L) // tm),
        in_specs=in_specs,
        out_specs=pl.BlockSpec((tm, tn), lambda j, i: (i, j)),
        out_shape=jax.ShapeDtypeStruct((2 * L, nb * d), BF16),
        scratch_shapes=[pltpu.VMEM((L, tn), BF16)],
        compiler_params=_params(2, est),
        name="dft_mul",
    )(*args)


def _idft_kernel(x_ref, y_ref, *rest, z_conv, per):
    rest = list(rest)
    g_ref, gp_ref, gn_ref, gw_ref, gb_ref = rest[:5]
    rest = rest[5:]
    i = pl.program_id(1)
    first, last = i == 0, i == per - 1
    conv = jnp.dot(x_ref[...], y_ref[...], preferred_element_type=F32)
    gate = _conv3_tile(g_ref[...], gp_ref[...], gn_ref[...], gw_ref[...], gb_ref[...], first, last)
    if z_conv:
        z_ref, zp_ref, zn_ref, zw_ref, zb_ref, skip_ref, o_ref = rest
        z = _conv3_tile(z_ref[...], zp_ref[...], zn_ref[...], zw_ref[...], zb_ref[...], first, last)
    else:
        z_ref, skip_ref, o_ref = rest
        z = z_ref[...]
    o_ref[...] = (gate * (conv + z * skip_ref[...])).astype(o_ref.dtype)


def idft_gate(inv, y, gate_view, z_view, skip, conv, *, z_conv, L, nb, tm, tn, out_dtype):
    d = D_MODEL
    per_b = d // tn
    per = L // tm
    sub = tm // SUBLANE

    def tile_specs(view, with_conv):
        arr, row0, col0 = view
        r0, c0 = row0 // tm, col0 // tn
        last_blk = arr.shape[0] // SUBLANE - 1
        row = lambda j, i: (j // per_b) * per + i + r0
        col = lambda j, i: j % per_b + c0
        specs = [pl.BlockSpec((tm, tn), lambda j, i: (row(j, i), col(j, i)))]
        args = [arr]
        if with_conv:
            specs += [
                pl.BlockSpec((SUBLANE, tn),
                             lambda j, i: (jnp.maximum(row(j, i) * sub - 1, 0), col(j, i))),
                pl.BlockSpec((SUBLANE, tn),
                             lambda j, i: (jnp.minimum((row(j, i) + 1) * sub, last_blk), col(j, i))),
                pl.BlockSpec((HY_SHORT, tn), lambda j, i: (0, col(j, i))),
                pl.BlockSpec((1, tn), lambda j, i: (0, col(j, i)))]
            args += [arr, arr, conv[0], conv[1]]
        return specs, args

    g_specs, g_args = tile_specs(gate_view, True)
    z_specs, z_args = tile_specs(z_view, z_conv)
    est = (2 * _nbytes((tm, 2 * L), BF16) + 2 * _nbytes((2 * L, tn), BF16)
           + 12 * _nbytes((tm, tn), F32))
    return pl.pallas_call(
        functools.partial(_idft_kernel, z_conv=z_conv, per=per),
        grid=(nb * per_b, per),
        in_specs=[pl.BlockSpec((tm, 2 * L), lambda j, i: (i, 0)),
                  pl.BlockSpec((2 * L, tn), lambda j, i: (0, j)),
                  *g_specs, *z_specs,
                  pl.BlockSpec((1, tn), lambda j, i: (0, j % per_b))],
        out_specs=pl.BlockSpec((tm, tn), lambda j, i: ((j // per_b) * per + i, j % per_b)),
        out_shape=jax.ShapeDtypeStruct((nb * L, d), out_dtype),
        compiler_params=_params(2, est),
        name="idft_gate",
    )(inv, y, *g_args, *z_args, skip.reshape(1, d))


def hyena_stream(u, row0, L, nb, conv, filt, skip):
    d = D_MODEL
    tms = min(1024, 2 * L)
    tmi = min(512, L)
    k = hyena_filters(L, *filt)
    fwd, inv = dft_tables(L, tms)
    kspec = linear(fwd, k, tm=tms, tn=1024, name="dft_filters")
    y1 = dft_mul(fwd, (u, row0, 0), kspec, 0, L=L, nb=nb, tm=tms, tn=1024, conv=conv)
    z1 = idft_gate(inv, y1, (u, row0, d), (u, row0, 0), skip[0], conv, z_conv=True, L=L, nb=nb,
                   tm=tmi, tn=1024, out_dtype=F32)
    y2 = dft_mul(fwd, (z1, 0, 0), kspec, d, L=L, nb=nb, tm=tms, tn=1024)
    return idft_gate(inv, y2, (u, row0, 2 * d), (z1, 0, 0), skip[1], conv, z_conv=False, L=L,
                     nb=nb, tm=tmi, tn=1024, out_dtype=BF16)


def hyena_mix(a, n_rows, j, w_in, b_in, conv_w, conv_b, filt, skip):
    u = linear(a, w_in, e=j, tm=1024, tn=2048, bias=b_in, name="hyena_in")
    conv = (conv_w, conv_b.reshape(1, -1))
    zs = [hyena_stream(u, 0, SEQ, BATCH, conv, filt, skip)]
    if n_rows > N_LAT:
        zs.append(hyena_stream(u, N_LAT, CTX_LEN, BATCH, conv, filt, skip))
    return zs


def _rms(x, g):
    return x * lax.rsqrt(jnp.mean(x * x, axis=-1, keepdims=True) + RMS_EPS) * g


def _mla_q_kernel(p_ref, g_ref, w1_ref, w2_ref, c_ref, s_ref, o_ref, w1b_ref, w2b_ref):
    @pl.when(pl.program_id(1) == 0)
    def _():
        w1b_ref[...] = w1_ref[...].astype(BF16)
        w2b_ref[...] = w2_ref[...].astype(BF16)

    xn = _rms(p_ref[...], g_ref[...]).astype(BF16)
    q1 = jnp.dot(xn, w1b_ref[...], preferred_element_type=F32)
    q2 = jnp.dot(xn, w2b_ref[...], preferred_element_type=F32)
    reps = q1.shape[1] // MLA_QK_PAD
    c = jnp.concatenate([c_ref[...]] * reps, axis=1)
    s = jnp.concatenate([s_ref[...]] * reps, axis=1)
    o_ref[...] = (q1 * c + q2 * s).astype(BF16)


def _mla_kv_kernel(p_ref, r_ref, g_ref, cs_ref, w_ref, o_ref, wb_ref):
    @pl.when(pl.program_id(1) == 0)
    def _():
        wb_ref[...] = w_ref[...].astype(BF16)

    xn = _rms(p_ref[...], g_ref[...]).astype(BF16)
    rr = (r_ref[...] * cs_ref[...]).astype(BF16)
    acc = (jnp.dot(xn, wb_ref[0:MLA_KV_RANK, :], preferred_element_type=F32)
           + jnp.dot(rr, wb_ref[MLA_KV_RANK:MLA_KV_RANK + 2 * MLA_ROPE, :],
                     preferred_element_type=F32))
    o_ref[...] = acc.astype(BF16)


def _attn_kernel(q_ref, kl_ref, kc_ref, vl_ref, vc_ref, o_ref, *, kchunk):
    c = (MLA_NOPE + MLA_ROPE) ** -0.5 * math.log2(math.e)
    nt = (((1,), (1,)), ((), ()))
    q = q_ref[...]
    n_lat, n_ctx = kl_ref.shape[0], kc_ref.shape[0]
    assert (n_lat + n_ctx) % kchunk == 0 and n_lat % kchunk + n_ctx == kchunk
    m = l = acc = None
    for r in range((n_lat + n_ctx) // kchunk):
        lo, hi = r * kchunk, min((r + 1) * kchunk, n_lat)
        if (r + 1) * kchunk <= n_lat:
            k, v = kl_ref[lo:hi, :], vl_ref[lo:hi, :]
        else:
            k = jnp.concatenate([kl_ref[lo:hi, :], kc_ref[...]], axis=0)
            v = jnp.concatenate([vl_ref[lo:hi, :], vc_ref[...]], axis=0)
        s = lax.dot_general(q, k, nt, preferred_element_type=F32)
        ms = jnp.max(s, axis=-1, keepdims=True)
        m_new = ms if m is None else jnp.maximum(m, ms)
        p = jnp.exp2((s - m_new) * c)
        ps = jnp.sum(p, axis=-1, keepdims=True)
        pv = jnp.dot(p.astype(BF16), v, preferred_element_type=F32)
        if m is None:
            l, acc = ps, pv
        else:
            alpha = jnp.exp2((m - m_new) * c)
            l = alpha * l + ps
            acc = alpha * acc + pv
        m = m_new
    o_ref[...] = (acc / l).astype(o_ref.dtype)


def _rope_tables():
    rows = SEQ // GRID_W
    row = jnp.repeat(jnp.arange(rows), GRID_W).astype(F32)
    col = jnp.tile(jnp.arange(GRID_W), rows).astype(F32)
    n = MLA_ROPE // 4
    inv = ROPE_BASE ** (-jnp.arange(n, dtype=F32) / n)
    ang_row, ang_col = row[:, None] * inv, col[:, None] * inv
    cos = jnp.concatenate([jnp.cos(ang_row)] * 2 + [jnp.cos(ang_col)] * 2, axis=1)
    sin = jnp.concatenate([jnp.sin(ang_row)] * 2 + [jnp.sin(ang_col)] * 2, axis=1)
    return cos, sin


def _rope_partner(w):
    n = MLA_ROPE // 4
    pairs = w.reshape(w.shape[:-1] + (2, 2, n))
    sign = jnp.array([-1.0, 1.0], F32)[:, None]
    return (pairs[..., ::-1, :] * sign).reshape(w.shape)


def mla_mix(a, w_in, q_norm, kv_norm, w_uq, w_ukv):
    H, NP, R, V, QK = MLA_HEADS, MLA_NOPE, MLA_ROPE, MLA_V, MLA_QK_PAD
    n_all = N_LAT + N_CTX
    cos, sin = _rope_tables()
    w_in_x = jnp.concatenate([w_in, _rope_partner(w_in[:, MLA_IN - R:])], axis=1)
    proj = linear(a, w_in_x, tm=512, tn=MLA_IN_PAD, name="mla_in")
    wq = w_uq.reshape(MLA_Q_RANK, H, NP + R)
    tail = QK - NP - R
    wq1 = jnp.pad(wq, ((0, 0), (0, 0), (0, tail))).reshape(MLA_Q_RANK, H * QK)
    wq2 = jnp.pad(_rope_partner(wq[..., NP:]),
                  ((0, 0), (0, 0), (NP, tail))).reshape(MLA_Q_RANK, H * QK)
    ones, zeros = jnp.ones((SEQ, NP), F32), jnp.zeros((SEQ, QK - NP - R), F32)
    cq = jnp.concatenate([ones, cos, zeros], axis=1)
    sq = jnp.concatenate([0.0 * ones, sin, zeros], axis=1)
    tm, tn = 1024, 1024
    per = SEQ // tm
    q = pl.pallas_call(
        _mla_q_kernel,
        grid=(H * QK // tn, N_LAT // tm),
        in_specs=[pl.BlockSpec((tm, MLA_Q_RANK), lambda jn, i: (i, 0)),
                  pl.BlockSpec((1, MLA_Q_RANK), lambda jn, i: (0, 0)),
                  pl.BlockSpec((MLA_Q_RANK, tn), lambda jn, i: (0, jn)),
                  pl.BlockSpec((MLA_Q_RANK, tn), lambda jn, i: (0, jn)),
                  pl.BlockSpec((tm, QK), lambda jn, i: (i % per, 0)),
                  pl.BlockSpec((tm, QK), lambda jn, i: (i % per, 0))],
        out_specs=pl.BlockSpec((tm, tn), lambda jn, i: (i, jn)),
        out_shape=jax.ShapeDtypeStruct((N_LAT, H * QK), BF16),
        scratch_shapes=[pltpu.VMEM((MLA_Q_RANK, tn), BF16), pltpu.VMEM((MLA_Q_RANK, tn), BF16)],
        compiler_params=_params(2, 6 * _nbytes((MLA_Q_RANK, tn), F32) + 8 * _nbytes((tm, tn), F32)),
        name="mla_q",
    )(proj, q_norm.reshape(1, MLA_Q_RANK), wq1, wq2, cq, sq)
    wkv = w_ukv.reshape(MLA_KV_RANK, H, NP + V)
    wk = jnp.concatenate([wkv[..., :NP], jnp.zeros((MLA_KV_RANK, H, QK - NP), F32)],
                         axis=2).reshape(MLA_KV_RANK, H * QK)
    place = jnp.concatenate([jnp.zeros((R, NP), F32), jnp.eye(R, dtype=F32),
                             jnp.zeros((R, QK - NP - R), F32)], axis=1)
    place = jnp.tile(jnp.concatenate([place, place], axis=0), (1, H))
    w_kv = jnp.concatenate([
        jnp.concatenate([wk, wkv[..., NP:].reshape(MLA_KV_RANK, H * V)], axis=1),
        jnp.concatenate([place, jnp.zeros((2 * R, H * V), F32)], axis=1)], axis=0)
    tm, tn = 512, 2048
    cs = jnp.concatenate([
        jnp.concatenate([cos, sin], axis=1),
        jnp.concatenate([jnp.ones((tm, R), F32), jnp.zeros((tm, R), F32)], axis=1)],
        axis=0)
    per = SEQ // tm
    n_kv = H * (QK + V)
    kv = pl.pallas_call(
        _mla_kv_kernel,
        grid=(n_kv // tn, n_all // tm),
        in_specs=[pl.BlockSpec((tm, MLA_KV_RANK), lambda jn, i: (i, MLA_Q_RANK // MLA_KV_RANK)),
                  pl.BlockSpec((tm, 2 * R), lambda jn, i: (i, (MLA_IN - R) // (2 * R))),
                  pl.BlockSpec((1, MLA_KV_RANK), lambda jn, i: (0, 0)),
                  pl.BlockSpec((tm, 2 * R),
                               lambda jn, i: (jnp.where(i * tm < N_LAT, i % per, per), 0)),
                  pl.BlockSpec((MLA_KV_RANK + 2 * R, tn), lambda jn, i: (0, jn))],
        out_specs=pl.BlockSpec((tm, tn), lambda jn, i: (i, jn)),
        out_shape=jax.ShapeDtypeStruct((n_all, n_kv), BF16),
        scratch_shapes=[pltpu.VMEM((MLA_KV_RANK + 2 * R, tn), BF16)],
        compiler_params=_params(2, 4 * _nbytes((MLA_KV_RANK + 2 * R, tn), F32)
                                + 8 * _nbytes((tm, tn), F32)),
        name="mla_kv",
    )(proj, proj, kv_norm.reshape(1, MLA_KV_RANK), cs, w_kv)
    tq = 2048
    kchunk = (SEQ + CTX_LEN) // 3
    per = SEQ // tq
    v0 = H * QK // V
    ctx0 = N_LAT // CTX_LEN
    est = (2 * _nbytes((tq, QK), BF16) + 2 * _nbytes((SEQ + CTX_LEN, QK + V), BF16)
           + 8 * _nbytes((tq, kchunk), F32))
    return pl.pallas_call(
        functools.partial(_attn_kernel, kchunk=kchunk),
        grid=(BATCH, H, per),
        in_specs=[pl.BlockSpec((tq, QK), lambda b, h, i: (b * per + i, h)),
                  pl.BlockSpec((SEQ, QK), lambda b, h, i: (b, h)),
                  pl.BlockSpec((CTX_LEN, QK), lambda b, h, i: (ctx0 + b, h)),
                  pl.BlockSpec((SEQ, V), lambda b, h, i: (b, v0 + h)),
                  pl.BlockSpec((CTX_LEN, V), lambda b, h, i: (ctx0 + b, v0 + h))],
        out_specs=pl.BlockSpec((tq, V), lambda b, h, i: (b * per + i, h)),
        out_shape=jax.ShapeDtypeStruct((N_LAT, H * V), BF16),
        compiler_params=_params(3, est),
        name="mla_attention",
    )(q, kv, kv, kv, kv)


def dense_ffn(a, f, wg, wu, wd):
    p = swiglu_up([a], wg, wu, e=f, tm=1024, tf=512, name="ffn_up")
    return linear(p, wd, e=f, tm=512, tn=1024, out_dtype=BF16, name="ffn_down")


def _row_cumsum(onehot):
    n, e = onehot.shape
    blk = 256
    x = onehot.reshape(n // blk, blk, e).astype(BF16)
    tri = jnp.tril(jnp.ones((blk, blk), BF16))
    inner = jnp.einsum("ij,bje->bie", tri, x, preferred_element_type=F32)
    totals = inner[:, -1, :]
    offs = jnp.cumsum(totals, axis=0) - totals
    return (inner + offs[:, None, :]).reshape(n, e).astype(jnp.int32)


def moe_ffn(logits, a_rows, n_rows, f, wg, wu, wd, *, tm=512):
    top_v, top_i = lax.top_k(logits[:, :N_EXPERTS], TOP_K)
    gates = jax.nn.softmax(top_v, axis=-1)
    n_assign = n_rows * TOP_K
    n_tiles = n_assign // tm + N_EXPERTS
    flat_e = top_i.reshape(n_assign)
    onehot = (flat_e[:, None] == jnp.arange(N_EXPERTS)[None, :]).astype(jnp.int32)
    csum = _row_cumsum(onehot)
    rank = jnp.sum(csum * onehot, axis=1) - 1
    counts = csum[-1]
    padded = ((counts + tm - 1) // tm) * tm
    ends = jnp.cumsum(padded)
    starts = ends - padded
    pos = starts[flat_e] + rank
    src = (jnp.arange(n_tiles * tm, dtype=jnp.int32) % n_rows).at[pos].set(
        jnp.arange(n_assign, dtype=jnp.int32) // TOP_K)
    tile_row0 = jnp.arange(n_tiles, dtype=jnp.int32) * tm
    te = jnp.minimum(jnp.sum((tile_row0[:, None] >= ends[None, :]).astype(jnp.int32), axis=1),
                     N_EXPERTS - 1)
    used = jnp.clip(counts[te] - (tile_row0 - starts[te]), 0, tm)
    vr = jnp.where(tile_row0 < ends[-1], -(-used // ROW_STEP) * ROW_STEP, 0).astype(jnp.int32)
    n_used = ends[-1] // tm
    te = jnp.where(tile_row0 < ends[-1], te, te[jnp.maximum(n_used - 1, 0)]) + f * N_EXPERTS
    x_sorted = a_rows.at[src].get(mode="promise_in_bounds")
    stack = lambda w: w.reshape((-1,) + w.shape[2:])
    p = swiglu_up([x_sorted], stack(wg), stack(wu), tm=tm, tf=1024, te=te, vr=vr, name="moe_up")
    out = linear(p, stack(wd), tm=tm, tn=1024, te=te, vr=vr, name="moe_down")
    pos = pos.reshape(n_rows, TOP_K)
    ys = [out.at[pos[:, k]].get(mode="promise_in_bounds") for k in range(TOP_K)]
    return ys, [gates[:, k:k + 1] for k in range(TOP_K)]


def kernel(x, c, ctx, c_ctx, w_mod, b_mod, ln_g, ln_b, pool_w, pool_scale, hy_w_in, hy_b_in, hy_conv_w, hy_conv_b, hy_f_w1, hy_f_b1, hy_f_w2, hy_f_b2, hy_f_w3, hy_f_b3, hy_f_freq, hy_f_wout, hy_skip, hy_w_out, hy_b_out, mla_w_in, mla_q_norm, mla_kv_norm, mla_w_uq, mla_w_ukv, mla_w_o, ffn_w_gate, ffn_w_up, ffn_w_down, moe_w_router, moe_w_gate, moe_w_up, moe_w_down):
    attn_layers = [i for i in range(DEPTH) if i % N_MIXERS == 2]
    last_read = attn_layers[-1] if attn_layers else -1

    c8 = jnp.concatenate([c, c_ctx[None], jnp.zeros((MOD_ROWS - BATCH - 1, D_MODEL), F32)], axis=0)
    mods = modulation_table(c8, w_mod, b_mod)

    assert 0 < last_read and N_MIXERS > 1
    h = [x.reshape(N_LAT, D_MODEL), ctx.reshape(N_CTX, D_MODEL)]
    a = None
    for i in range(DEPTH):
        kind, j = i % N_MIXERS, i // N_MIXERS
        ctx_out = i < last_read
        n_rows = N_LAT + N_CTX if ctx_out else N_LAT
        m = mods[i]
        f = i // 2
        w_router = moe_w_router[f] if i % 2 == 1 else None
        if kind == 0:
            tl = CTX_LEN if n_rows > N_LAT else 512
            h1, a2, *logits = pool_layer(h if isinstance(h, list) else [h], n_rows, m,
                                         pool_w[j].astype(BF16), pool_scale[j],
                                         ln_g[i, 0], ln_b[i, 0], tl=tl, w_router=w_router)
        else:
            if kind == 1:
                filt = (hy_f_w1[j], hy_f_b1[j], hy_f_w2[j], hy_f_b2[j], hy_f_w3[j], hy_f_b3[j],
                        hy_f_freq[j], hy_f_wout[j])
                z = hyena_mix(a[:n_rows], n_rows, j, hy_w_in, hy_b_in, hy_conv_w[j], hy_conv_b[j],
                              filt, hy_skip[j])
                w_o, b_o = hy_w_out, hy_b_out
            else:
                assert i <= last_read and not ctx_out and a.shape[0] == N_LAT + N_CTX
                z = [mla_mix(a, mla_w_in[j], mla_q_norm[j], mla_kv_norm[j], mla_w_uq[j],
                             mla_w_ukv[j])]
                w_o, b_o = mla_w_o, None
            h1, a2, *logits = linear_resid_ln(z, w_o, j, b_o, h, n_rows, m, 2, ln_g[i, 0],
                                              ln_b[i, 0], 4, 3, w_router=w_router)
        nm = mods[min(i + 1, DEPTH - 1)]
        if w_router is not None:
            ys, row_gates = moe_ffn(logits[0], a2, n_rows, f, moe_w_gate, moe_w_up, moe_w_down)
        else:
            ys, row_gates = [dense_ffn(a2, f, ffn_w_gate, ffn_w_up, ffn_w_down)], None
        h, a = resid_ln(h1, ys, n_rows, m, 5, ln_g[i, 1], ln_b[i, 1], nm, 1, 0,
                        row_gates=row_gates)
    return h[:N_LAT].reshape(BATCH, SEQ, D_MODEL)
```
